```python
import math
import jax, jax.numpy as jnp
from jax import lax
import numpy as np

D_MODEL = 1024
BATCH = 2
SEQ = 16384
DEPTH = 2

GRID_W = 64
CTX_LEN = 256
N_EVEN = (DEPTH + 1) // 2
N_ODD = DEPTH // 2
N_MOD = 6
EPS = 1e-6
Q_BLOCK = 128
ROPE_BASE = 10000.0

POOL_WINDOWS = (2, 4, 8, 16)
POOL_GROUP = D_MODEL // 16
POOL_WIDTH = POOL_GROUP * len(POOL_WINDOWS)
DIFF_HEAD_DIM = 64
DIFF_V_DIM = 2 * DIFF_HEAD_DIM
DIFF_HEADS = (D_MODEL - POOL_WIDTH) // DIFF_V_DIM
DIFF_QK_WIDTH = DIFF_HEADS * 2 * DIFF_HEAD_DIM
DIFF_V_WIDTH = DIFF_HEADS * DIFF_V_DIM
EVEN_IN_WIDTH = POOL_WIDTH + 2 * DIFF_QK_WIDTH + DIFF_V_WIDTH
EVEN_MIX_WIDTH = POOL_WIDTH + DIFF_V_WIDTH
MLA_HEADS = D_MODEL // 128
MLA_NOPE = 128
MLA_ROPE = 64
MLA_V = 128
MLA_Q_RANK = D_MODEL // 2
MLA_KV_RANK = D_MODEL // 4
ODD_IN_WIDTH = MLA_Q_RANK + MLA_KV_RANK + MLA_ROPE
MLA_MIX_WIDTH = MLA_HEADS * MLA_V
FFN_HIDDEN = ((8 * D_MODEL // 3 + 255) // 256) * 256
N_EXPERTS = 8
TOP_K = 2

kernel_name = "hybrid_pool_diffattn_mla_moe_prefix_dit"

F32 = jnp.float32


def rmsnorm(h, g=None):
    hf = h.astype(F32)
    y = hf * lax.rsqrt(jnp.mean(hf * hf, axis=-1, keepdims=True) + EPS)
    if g is not None:
        y = y * g.astype(F32)
    return y.astype(h.dtype)


def adaln(cond, w, b):
    return jax.nn.silu(cond) @ w + b


def modulate(h, shift, scale):
    return h * (1.0 + scale) + shift


def axial_rope_tables(rows, dim):
    t = jnp.arange(rows * GRID_W, dtype=jnp.int32)
    row = (t // GRID_W).astype(F32)
    col = (t % GRID_W).astype(F32)
    quarter = dim // 4
    inv = ROPE_BASE ** (-jnp.arange(quarter, dtype=F32) / quarter)
    ar = row[:, None] * inv[None, :]
    ac = col[:, None] * inv[None, :]
    cos = jnp.concatenate([jnp.cos(ar), jnp.cos(ar), jnp.cos(ac), jnp.cos(ac)], axis=-1)
    sin = jnp.concatenate([jnp.sin(ar), jnp.sin(ar), jnp.sin(ac), jnp.sin(ac)], axis=-1)
    return cos, sin


def apply_axial_rope(u, cos, sin):
    u1, u2, u3, u4 = jnp.split(u, 4, axis=-1)
    rot = jnp.concatenate([-u2, u1, -u4, u3], axis=-1)
    return (u.astype(F32) * cos + rot.astype(F32) * sin).astype(u.dtype)


def sweep_query_blocks(fn, qs):
    b, l = qs[0].shape[:2]
    nb = l // Q_BLOCK
    blocked = tuple(jnp.moveaxis(q.reshape((b, nb, Q_BLOCK) + q.shape[2:]), 1, 0) for q in qs)
    out = lax.map(lambda qb: fn(*qb), blocked)
    return jnp.moveaxis(out, 0, 1).reshape((b, l) + out.shape[3:])


def softmax_attn_block(q, k, v, scale):
    s = jnp.einsum('bqhd,bkhd->bhqk', q, k).astype(F32) * scale
    p = jax.nn.softmax(s, axis=-1)
    return jnp.einsum('bhqk,bkhe->bqhe', p.astype(v.dtype), v)


def diff_attn_block(q1, q2, k1, k2, v, lam):
    scale = DIFF_HEAD_DIM ** -0.5
    s1 = jnp.einsum('bqhd,bkhd->bhqk', q1, k1).astype(F32) * scale
    s2 = jnp.einsum('bqhd,bkhd->bhqk', q2, k2).astype(F32) * scale
    a = jax.nn.softmax(s1, axis=-1) - lam * jax.nn.softmax(s2, axis=-1)
    return jnp.einsum('bhqk,bkhe->bqhe', a.astype(v.dtype), v)


def multiscale_pool(u, pool_w, pool_scale):
    b, l, _ = u.shape
    uf = u.astype(F32)
    cs = jnp.concatenate([jnp.zeros((b, 1, POOL_WIDTH), F32), jnp.cumsum(uf, axis=1)], axis=1)
    t = jnp.arange(l)
    outs = []
    for g, w in enumerate(POOL_WINDOWS):
        lo = jnp.clip(t - w // 2, 0, l)
        hi = jnp.clip(t + w // 2, 0, l)
        sl = slice(g * POOL_GROUP, (g + 1) * POOL_GROUP)
        csg = cs[:, :, sl]
        s = jnp.take(csg, hi, axis=1) - jnp.take(csg, lo, axis=1)
        cnt = (hi - lo).astype(F32)[None, :, None]
        outs.append(s / cnt - uf[:, :, sl])
    p = jnp.stack(outs, axis=2).astype(u.dtype)
    y = jnp.einsum('blgc,gcd->blgd', p, pool_w).reshape(b, l, POOL_WIDTH)
    return y * pool_scale


def split_even(z):
    n = z.shape[:2]
    o = POOL_WIDTH
    p = z[..., :o]
    q = z[..., o:o + DIFF_QK_WIDTH].reshape(n + (DIFF_HEADS, 2, DIFF_HEAD_DIM))
    k = z[..., o + DIFF_QK_WIDTH:o + 2 * DIFF_QK_WIDTH].reshape(n + (DIFF_HEADS, 2, DIFF_HEAD_DIM))
    v = z[..., o + 2 * DIFF_QK_WIDTH:].reshape(n + (DIFF_HEADS, DIFF_V_DIM))
    return p, q, k, v


def even_merge(p, attn, pool_w, pool_scale, w_out, lam_init):
    b, l = p.shape[:2]
    pooled = multiscale_pool(p, pool_w, pool_scale)
    heads = (rmsnorm(attn) * (1.0 - lam_init)).reshape(b, l, DIFF_V_WIDTH)
    return jnp.concatenate([pooled, heads], axis=-1) @ w_out


def even_mixer(a_lat, a_ctx, w_in, pool_w, pool_scale, lq1, lk1, lq2, lk2, w_out,
               cos, sin, lam_init, need_ctx):
    p_lat, q_lat, k_lat, v_lat = split_even(a_lat @ w_in)
    p_ctx, q_ctx, k_ctx, v_ctx = split_even(a_ctx @ w_in)
    cs, sn = cos[None, :, None, None, :], sin[None, :, None, None, :]
    q_lat = apply_axial_rope(q_lat, cs, sn)
    k_lat = apply_axial_rope(k_lat, cs, sn)
    lam = (jnp.exp(jnp.sum(lq1.astype(F32) * lk1.astype(F32)))
           - jnp.exp(jnp.sum(lq2.astype(F32) * lk2.astype(F32))) + lam_init)
    k_all = jnp.concatenate([k_lat, k_ctx], axis=1)
    v_all = jnp.concatenate([v_lat, v_ctx], axis=1)
    k1, k2 = k_all[..., 0, :], k_all[..., 1, :]
    attn_lat = sweep_query_blocks(
        lambda b1, b2: diff_attn_block(b1, b2, k1, k2, v_all, lam),
        (q_lat[..., 0, :], q_lat[..., 1, :]))
    y_lat = even_merge(p_lat, attn_lat, pool_w, pool_scale, w_out, lam_init)
    y_ctx = None
    if need_ctx:
        attn_ctx = diff_attn_block(q_ctx[..., 0, :], q_ctx[..., 1, :],
                                   k_ctx[..., 0, :], k_ctx[..., 1, :], v_ctx, lam)
        y_ctx = even_merge(p_ctx, attn_ctx, pool_w, pool_scale, w_out, lam_init)
    return y_lat, y_ctx


def odd_mixer(a_lat, a_ctx, w_in, q_norm_g, kv_norm_g, w_uq, w_ukv, w_out, cos, sin, need_ctx):
    z_lat = a_lat @ w_in
    z_ctx = a_ctx @ w_in

    def queries(z):
        cq = rmsnorm(z[..., :MLA_Q_RANK], q_norm_g)
        return (cq @ w_uq).reshape(z.shape[:2] + (MLA_HEADS, MLA_NOPE + MLA_ROPE))

    def keys_values(z):
        ckv = rmsnorm(z[..., MLA_Q_RANK:MLA_Q_RANK + MLA_KV_RANK], kv_norm_g)
        kv = (ckv @ w_ukv).reshape(z.shape[:2] + (MLA_HEADS, MLA_NOPE + MLA_V))
        return kv[..., :MLA_NOPE], kv[..., MLA_NOPE:], z[..., MLA_Q_RANK + MLA_KV_RANK:]

    def full_keys(k_nope, k_rope):
        shared = jnp.broadcast_to(k_rope[:, :, None, :], k_nope.shape[:3] + (MLA_ROPE,))
        return jnp.concatenate([k_nope, shared], axis=-1)

    q_lat = queries(z_lat)
    q_lat = jnp.concatenate(
        [q_lat[..., :MLA_NOPE],
         apply_axial_rope(q_lat[..., MLA_NOPE:], cos[None, :, None, :], sin[None, :, None, :])],
        axis=-1)
    kn_lat, v_lat, kr_lat = keys_values(z_lat)
    kr_lat = apply_axial_rope(kr_lat, cos[None], sin[None])
    kn_ctx, v_ctx, kr_ctx = keys_values(z_ctx)
    k_ctx = full_keys(kn_ctx, kr_ctx)
    k_all = jnp.concatenate([full_keys(kn_lat, kr_lat), k_ctx], axis=1)
    v_all = jnp.concatenate([v_lat, v_ctx], axis=1)
    scale = (MLA_NOPE + MLA_ROPE) ** -0.5
    b, l = a_lat.shape[:2]
    o_lat = sweep_query_blocks(lambda qb: softmax_attn_block(qb, k_all, v_all, scale), (q_lat,))
    y_lat = o_lat.reshape(b, l, MLA_MIX_WIDTH) @ w_out
    y_ctx = None
    if need_ctx:
        o_ctx = softmax_attn_block(queries(z_ctx), k_ctx, v_ctx, scale)
        y_ctx = o_ctx.reshape(a_ctx.shape[:2] + (MLA_MIX_WIDTH,)) @ w_out
    return y_lat, y_ctx


def swiglu(h, w_gate, w_up, w_down):
    return (jax.nn.silu(h @ w_gate) * (h @ w_up)) @ w_down


def moe_swiglu(h, router_w, w_gate, w_up, w_down):
    b, l, d = h.shape
    tok = h.reshape(-1, d)
    n = tok.shape[0]
    probs = jax.nn.softmax((tok @ router_w).astype(F32), axis=-1)
    top_p, top_e = lax.top_k(probs, TOP_K)
    top_p = top_p / jnp.sum(top_p, axis=-1, keepdims=True)
    flat_e = top_e.reshape(-1)
    order = jnp.argsort(flat_e)
    tok_idx = order // TOP_K
    xs = tok[tok_idx]
    group_sizes = jnp.bincount(flat_e, length=N_EXPERTS).astype(jnp.int32)
    hg = lax.ragged_dot(xs, w_gate, group_sizes)
    hu = lax.ragged_dot(xs, w_up, group_sizes)
    ys = lax.ragged_dot(jax.nn.silu(hg) * hu, w_down, group_sizes)
    wts = top_p.reshape(-1)[order].astype(ys.dtype)
    out = jax.ops.segment_sum(ys * wts[:, None], tok_idx, num_segments=n)
    return out.reshape(b, l, d)


def setup_inputs(seed: int = 0) -> dict:
    key = jax.random.key(seed)
    ks = iter(jax.random.split(key, 48))
    D = D_MODEL
    F = FFN_HIDDEN

    def nrm(shape, s):
        return jax.random.normal(next(ks), shape, jnp.float32) * s

    return {
        "x": nrm((BATCH, SEQ, D), 1.0),
        "c": nrm((BATCH, D), 1.0),
        "ctx": nrm((BATCH, CTX_LEN, D), 1.0),
        "c_ctx": nrm((D,), 1.0),
        "norm1_g": 1.0 + nrm((DEPTH, D), 0.1),
        "norm2_g": 1.0 + nrm((DEPTH, D), 0.1),
        "mod_w": nrm((DEPTH, D, N_MOD * D), 0.5 * D ** -0.5),
        "mod_b": nrm((DEPTH, N_MOD * D), 0.01),
        "even_w_in": nrm((N_EVEN, D, EVEN_IN_WIDTH), D ** -0.5),
        "pool_w": nrm((N_EVEN, len(POOL_WINDOWS), POOL_GROUP, POOL_GROUP), POOL_GROUP ** -0.5),
        "pool_scale": 1.0 + nrm((N_EVEN, POOL_WIDTH), 0.1),
        "lambda_q1": nrm((N_EVEN, DIFF_HEAD_DIM), 0.1),
        "lambda_k1": nrm((N_EVEN, DIFF_HEAD_DIM), 0.1),
        "lambda_q2": nrm((N_EVEN, DIFF_HEAD_DIM), 0.1),
        "lambda_k2": nrm((N_EVEN, DIFF_HEAD_DIM), 0.1),
        "even_w_out": nrm((N_EVEN, EVEN_MIX_WIDTH, D), EVEN_MIX_WIDTH ** -0.5),
        "ffn_w_gate": nrm((N_EVEN, D, F), D ** -0.5),
        "ffn_w_up": nrm((N_EVEN, D, F), D ** -0.5),
        "ffn_w_down": nrm((N_EVEN, F, D), F ** -0.5),
        "odd_w_in": nrm((N_ODD, D, ODD_IN_WIDTH), D ** -0.5),
        "q_norm_g": 1.0 + nrm((N_ODD, MLA_Q_RANK), 0.1),
        "kv_norm_g": 1.0 + nrm((N_ODD, MLA_KV_RANK), 0.1),
        "w_uq": nrm((N_ODD, MLA_Q_RANK, MLA_HEADS * (MLA_NOPE + MLA_ROPE)), MLA_Q_RANK ** -0.5),
        "w_ukv": nrm((N_ODD, MLA_KV_RANK, MLA_HEADS * (MLA_NOPE + MLA_V)), MLA_KV_RANK ** -0.5),
        "odd_w_out": nrm((N_ODD, MLA_MIX_WIDTH, D), MLA_MIX_WIDTH ** -0.5),
        "router_w": nrm((N_ODD, D, N_EXPERTS), D ** -0.5),
        "moe_w_gate": nrm((N_ODD, N_EXPERTS, D, F), D ** -0.5),
        "moe_w_up": nrm((N_ODD, N_EXPERTS, D, F), D ** -0.5),
        "moe_w_down": nrm((N_ODD, N_EXPERTS, F, D), F ** -0.5),
        "final_norm_g": 1.0 + nrm((D,), 0.1),
    }


def reference(x, c, ctx, c_ctx, norm1_g, norm2_g, mod_w, mod_b, even_w_in, pool_w, pool_scale,
              lambda_q1, lambda_k1, lambda_q2, lambda_k2, even_w_out, ffn_w_gate, ffn_w_up,
              ffn_w_down, odd_w_in, q_norm_g, kv_norm_g, w_uq, w_ukv, odd_w_out, router_w,
              moe_w_gate, moe_w_up, moe_w_down, final_norm_g):
    rows = x.shape[1] // GRID_W
    cos_d, sin_d = axial_rope_tables(rows, DIFF_HEAD_DIM)
    cos_m, sin_m = axial_rope_tables(rows, MLA_ROPE)
    h_lat, h_ctx = x, ctx
    for i in range(DEPTH):
        need_ctx = i < DEPTH - 1
        j = i // 2
        mod_lat = adaln(c, mod_w[i], mod_b[i])[:, None, :]
        mod_ctx = adaln(c_ctx, mod_w[i], mod_b[i])[None, None, :]
        sh1, sc1, g1, sh2, sc2, g2 = jnp.split(mod_lat, N_MOD, axis=-1)
        csh1, csc1, cg1, csh2, csc2, cg2 = jnp.split(mod_ctx, N_MOD, axis=-1)
        a_lat = modulate(rmsnorm(h_lat, norm1_g[i]), sh1, sc1)
        a_ctx = modulate(rmsnorm(h_ctx, norm1_g[i]), csh1, csc1)
        if i % 2 == 0:
            lam_init = 0.8 - 0.6 * math.exp(-0.3 * i)
            y_lat, y_ctx = even_mixer(a_lat, a_ctx, even_w_in[j], pool_w[j], pool_scale[j],
                                      lambda_q1[j], lambda_k1[j], lambda_q2[j], lambda_k2[j],
                                      even_w_out[j], cos_d, sin_d, lam_init, need_ctx)
            ffn = lambda h: swiglu(h, ffn_w_gate[j], ffn_w_up[j], ffn_w_down[j])
        else:
            y_lat, y_ctx = odd_mixer(a_lat, a_ctx, odd_w_in[j], q_norm_g[j], kv_norm_g[j],
                                     w_uq[j], w_ukv[j], odd_w_out[j], cos_m, sin_m, need_ctx)
            ffn = lambda h: moe_swiglu(h, router_w[j], moe_w_gate[j], moe_w_up[j], moe_w_down[j])
        h_lat = h_lat + g1 * y_lat
        h_lat = h_lat + g2 * ffn(modulate(rmsnorm(h_lat, norm2_g[i]), sh2, sc2))
        if need_ctx:
            h_ctx = h_ctx + cg1 * y_ctx
            h_ctx = h_ctx + cg2 * ffn(modulate(rmsnorm(h_ctx, norm2_g[i]), csh2, csc2))
    return rmsnorm(h_lat, final_norm_g)
```

```python
import functools
import math

import jax
import jax.numpy as jnp
from jax import lax
from jax.experimental import pallas as pl
from jax.experimental.pallas import tpu as pltpu

F32 = jnp.float32
BF16 = jnp.bfloat16

EPS = 1e-6
GRID_W = 64
ROPE_BASE = 10000.0
N_MOD = 6
POOL_WINDOWS = (2, 4, 8, 16)
POOL_GROUP = 64
POOL_WIDTH = POOL_GROUP * len(POOL_WINDOWS)
POOL_HALO = max(POOL_WINDOWS) // 2
DIFF_HEAD_DIM = 64
DIFF_V_DIM = 128
MLA_NOPE = 128
MLA_ROPE = 64
MLA_V = 128
MLA_QK_PAD = 256
N_EXPERTS = 8
ROPE_QUARTER = 16

LANES = 128
ROW_TILE = 512
KV_TILE = 512
Q_TILE = 512
FFN_CHUNKS = (512, 512, 512, 512, 512, 256)
VMEM_LIMIT = 56 * 1024 * 1024

NT_DIMS = (((1,), (1,)), ((), ()))


def _params(sem):
    return pltpu.CompilerParams(dimension_semantics=sem, vmem_limit_bytes=VMEM_LIMIT)


def _dot(a, b):
    return jnp.dot(a, b, preferred_element_type=F32)


def _dot_nt(a, b):
    return lax.dot_general(a, b, NT_DIMS, preferred_element_type=F32)


def _rms(x):
    return x * lax.rsqrt(jnp.mean(x * x, axis=-1, keepdims=True) + EPS)


def _silu(x):
    return x * (1.0 / (1.0 + jnp.exp(-x)))


def _rope(x, cos, sin):
    lane = lax.broadcasted_iota(jnp.int32, x.shape, 1)
    even = (lane // ROPE_QUARTER) % 2 == 0
    rot = jnp.where(even, -pltpu.roll(x, LANES - ROPE_QUARTER, 1), pltpu.roll(x, ROPE_QUARTER, 1))
    return x * cos + rot * sin


def _mod_kernel(cond_ref, w_ref, b_ref, o_ref):
    s = _silu(cond_ref[...])
    o_ref[0] = jnp.dot(s, w_ref[0], preferred_element_type=F32,
                       precision=lax.Precision.HIGHEST) + b_ref[0]


def _modulation(cond, mod_w, mod_b):
    depth, d, n = mod_w.shape
    tn = n // 4
    return pl.pallas_call(
        _mod_kernel,
        grid=(depth, n // tn),
        in_specs=[pl.BlockSpec((8, d), lambda i, j: (0, 0)),
                  pl.BlockSpec((1, d, tn), lambda i, j: (i, 0, j)),
                  pl.BlockSpec((1, 1, tn), lambda i, j: (i, 0, j))],
        out_specs=pl.BlockSpec((1, 8, tn), lambda i, j: (i, 0, j)),
        out_shape=jax.ShapeDtypeStruct((depth, 8, n), F32),
        compiler_params=_params(("arbitrary", "arbitrary")),
        name="modulation",
    )(cond, mod_w, mod_b.reshape(depth, 1, n))


def _bvec_spec(arr, d):
    if arr.shape[0] == 1:
        return pl.BlockSpec((1, 1, d), lambda b, i: (0, 0, 0))
    return pl.BlockSpec((1, 1, d), lambda b, i: (b, 0, 0))


def _norm_mod(h, g, sh, sc):
    a = _rms(h) * g
    return a * (1.0 + sc) + sh


def _pre0_kernel(h_ref, g_ref, sh_ref, sc_ref, w_ref, cos_ref, sin_ref,
                 p_ref, q_ref, k_ref, vt_ref, *, use_rope, n_heads):
    a = _norm_mod(h_ref[0], g_ref[...], sh_ref[0], sc_ref[0])
    z = _dot(a.astype(BF16), w_ref[...])
    p_ref[0] = z[:, :POOL_WIDTH]
    qk_w = n_heads * 2 * DIFF_HEAD_DIM
    q_scale = DIFF_HEAD_DIM ** -0.5
    cos = cos_ref[...]
    sin = sin_ref[...]
    for j in range(qk_w // LANES):
        xq = z[:, POOL_WIDTH + j * LANES:POOL_WIDTH + (j + 1) * LANES]
        xk = z[:, POOL_WIDTH + qk_w + j * LANES:POOL_WIDTH + qk_w + (j + 1) * LANES]
        if use_rope:
            xq = _rope(xq, cos, sin)
            xk = _rope(xk, cos, sin)
        q_ref[0, :, j * LANES:(j + 1) * LANES] = (xq * q_scale).astype(BF16)
        k_ref[0, :, j * LANES:(j + 1) * LANES] = xk.astype(BF16)
    v = z[:, POOL_WIDTH + 2 * qk_w:]
    vt_ref[0, 0] = v.T.astype(BF16)


def _pre0(h, g, sh, sc, w_in, cos, sin, *, use_rope, tm):
    b, l, d = h.shape
    n_in = w_in.shape[1]
    v_w = (n_in - POOL_WIDTH) // 3
    n_heads = v_w // DIFF_V_DIM
    kern = functools.partial(_pre0_kernel, use_rope=use_rope, n_heads=n_heads)
    return pl.pallas_call(
        kern,
        grid=(b, l // tm),
        in_specs=[pl.BlockSpec((1, tm, d), lambda b_, i: (b_, i, 0)),
                  pl.BlockSpec((1, d), lambda b_, i: (0, 0)),
                  _bvec_spec(sh, d), _bvec_spec(sc, d),
                  pl.BlockSpec((d, n_in), lambda b_, i: (0, 0)),
                  pl.BlockSpec((tm, LANES), lambda b_, i: (i, 0)),
                  pl.BlockSpec((tm, LANES), lambda b_, i: (i, 0))],
        out_specs=[pl.BlockSpec((1, tm, POOL_WIDTH), lambda b_, i: (b_, i, 0)),
                   pl.BlockSpec((1, tm, v_w), lambda b_, i: (b_, i, 0)),
                   pl.BlockSpec((1, tm, v_w), lambda b_, i: (b_, i, 0)),
                   pl.BlockSpec((1, 1, v_w, tm), lambda b_, i: (b_, i, 0, 0))],
        out_shape=[jax.ShapeDtypeStruct((b, l, POOL_WIDTH), F32),
                   jax.ShapeDtypeStruct((b, l, v_w), BF16),
                   jax.ShapeDtypeStruct((b, l, v_w), BF16),
                   jax.ShapeDtypeStruct((b, l // tm, v_w, tm), BF16)],
        compiler_params=_params(("parallel", "parallel")),
        name="pre0_rope" if use_rope else "pre0_ctx",
    )(h, g, sh, sc, w_in, cos, sin)


def _softmax_step(s, vt, m, l, acc_ref):
    m_new = jnp.maximum(m, jnp.max(s, axis=0, keepdims=True))
    alpha = jnp.exp(m - m_new)
    p = jnp.exp(s - m_new)
    l_new = alpha * l + jnp.sum(p, axis=0, keepdims=True)
    acc_ref[...] = alpha * acc_ref[...] + _dot(vt, p.astype(BF16))
    return m_new, l_new


def _sweep_segments(kv_refs, seg_blocks, step, carry):
    for s, (nblk, tk) in enumerate(seg_blocks):
        k_ref, vt_ref = kv_refs[2 * s], kv_refs[2 * s + 1]
        if nblk == 1:
            carry = step(k_ref[0], vt_ref[0, 0], carry)
        else:
            def body(j, c, k_ref=k_ref, vt_ref=vt_ref, tk=tk):
                start = pl.multiple_of(j * tk, tk)
                return step(k_ref[0, pl.ds(start, tk), :], vt_ref[0, j], c)
            carry = lax.fori_loop(0, nblk, body, carry)
    return carry


def _diff_attn_kernel(lq_ref, lk_ref, q_ref, *rest, seg_blocks, lam_init):
    nseg = len(seg_blocks)
    kv_refs = rest[:2 * nseg]
    o_ref = rest[2 * nseg]
    acc1_ref, acc2_ref = rest[2 * nseg + 1:]
    q = q_ref[0]
    tq = q.shape[0]
    lane = lax.broadcasted_iota(jnp.int32, q.shape, 1)
    zero = jnp.zeros_like(q)
    q1 = jnp.where(lane < DIFF_HEAD_DIM, q, zero)
    q2 = jnp.where(lane >= DIFF_HEAD_DIM, q, zero)
    acc1_ref[...] = jnp.zeros_like(acc1_ref)
    acc2_ref[...] = jnp.zeros_like(acc2_ref)

    def step(k, vt, carry):
        m1, l1, m2, l2 = carry
        s1 = _dot_nt(k, q1)
        m1, l1 = _softmax_step(s1, vt, m1, l1, acc1_ref)
        s2 = _dot_nt(k, q2)
        m2, l2 = _softmax_step(s2, vt, m2, l2, acc2_ref)
        return m1, l1, m2, l2

    neg = jnp.full((1, tq), -jnp.inf, F32)
    zer = jnp.zeros((1, tq), F32)
    _, l1, _, l2 = _sweep_segments(kv_refs, seg_blocks, step, (neg, zer, neg, zer))

    e = jnp.exp(jnp.sum(lq_ref[...] * lk_ref[...], axis=-1, keepdims=True))
    lam = e[0:1] - e[1:2] + lam_init
    o = acc1_ref[...] * (1.0 / l1) - lam * (acc2_ref[...] * (1.0 / l2))
    ms = jnp.mean(o * o, axis=0, keepdims=True)
    y = o * (lax.rsqrt(ms + EPS) * (1.0 - lam_init))
    o_ref[0] = y.T.astype(BF16)


def _mla_attn_kernel(q_ref, *rest, seg_blocks):
    nseg = len(seg_blocks)
    kv_refs = rest[:2 * nseg]
    o_ref = rest[2 * nseg]
    acc_ref = rest[2 * nseg + 1]
    q = q_ref[0]
    tq = q.shape[0]
    acc_ref[...] = jnp.zeros_like(acc_ref)

    def step(k, vt, carry):
        m, l = carry
        s = _dot_nt(k, q)
        return _softmax_step(s, vt, m, l, acc_ref)

    neg = jnp.full((1, tq), -jnp.inf, F32)
    zer = jnp.zeros((1, tq), F32)
    _, l = _sweep_segments(kv_refs, seg_blocks, step, (neg, zer))
    o_ref[0] = (acc_ref[...] * (1.0 / l)).T.astype(BF16)


def _attention(kern, n_acc, q, segments, qk_w, extra=(), *, tq, name):
    b, lq, hw = q.shape
    n_heads = hw // qk_w
    in_specs = [pl.BlockSpec(x.shape, lambda b_, h, i: (0, 0)) for x in extra]
    in_specs.append(pl.BlockSpec((1, tq, qk_w), lambda b_, h, i: (b_, i, h)))
    args = list(extra) + [q]
    seg_blocks = []
    for k, vt in segments:
        ls = k.shape[1]
        nblk, tk = vt.shape[1], vt.shape[3]
        seg_blocks.append((nblk, tk))
        in_specs.append(pl.BlockSpec((1, ls, qk_w), lambda b_, h, i: (b_, 0, h)))
        in_specs.append(pl.BlockSpec((1, nblk, DIFF_V_DIM, tk), lambda b_, h, i: (b_, 0, h, 0)))
        args += [k, vt]
    return pl.pallas_call(
        functools.partial(kern, seg_blocks=tuple(seg_blocks)),
        grid=(b, n_heads, lq // tq),
        in_specs=in_specs,
        out_specs=pl.BlockSpec((1, tq, DIFF_V_DIM), lambda b_, h, i: (b_, i, h)),
        out_shape=jax.ShapeDtypeStruct((b, lq, n_heads * DIFF_V_DIM), BF16),
        scratch_shapes=[pltpu.VMEM((DIFF_V_DIM, tq), F32) for _ in range(n_acc)],
        compiler_params=_params(("parallel", "parallel", "arbitrary")),
        name=name,
    )(*args)


def _post0_kernel(attn_ref, p_ref, pprev_ref, pnext_ref, poolw_ref, pscale_ref, wout_ref,
                  h_ref, g1_ref, n2g_ref, sh2_ref, sc2_ref, h1_ref, a2_ref, *, seq_len):
    i = pl.program_id(1)
    u = p_ref[0]
    tm = u.shape[0]
    ext = jnp.concatenate([pprev_ref[0], u, pnext_ref[0]], axis=0)
    row = i * tm - POOL_HALO + lax.broadcasted_iota(jnp.int32, (tm + 2 * POOL_HALO, 1), 0)
    ext = jnp.where((row >= 0) & (row < seq_len), ext, 0.0)

    def shifted(j):
        return ext[POOL_HALO + j:POOL_HALO + j + tm]

    t = i * tm + lax.broadcasted_iota(jnp.int32, (tm, 1), 0)
    lane = lax.broadcasted_iota(jnp.int32, (tm, POOL_WIDTH), 1)
    win_sum = shifted(-1) + shifted(0)
    pooled = jnp.zeros_like(u)
    for g, w in enumerate(POOL_WINDOWS):
        half = w // 2
        if g > 0:
            prev_half = POOL_WINDOWS[g - 1] // 2
            for j in range(prev_half, half):
                win_sum = win_sum + shifted(-j - 1) + shifted(j)
        cnt = (jnp.minimum(t + half, seq_len) - jnp.maximum(t - half, 0)).astype(F32)
        in_group = (lane >= g * POOL_GROUP) & (lane < (g + 1) * POOL_GROUP)
        pooled = jnp.where(in_group, win_sum / cnt, pooled)
    pooled = pooled - u
    y_pool = _dot(pooled.astype(BF16), poolw_ref[...]) * pscale_ref[...]
    y = _dot(y_pool.astype(BF16), wout_ref[:POOL_WIDTH, :]) + _dot(attn_ref[0], wout_ref[POOL_WIDTH:, :])
    h1 = h_ref[0] + g1_ref[0] * y
    h1_ref[0] = h1
    a2_ref[0] = _norm_mod(h1, n2g_ref[...], sh2_ref[0], sc2_ref[0]).astype(BF16)


def _post0(attn, p, poolw_bd, pscale, w_out, h, g1, n2g, sh2, sc2, *, tm):
    b, l, d = h.shape
    hb = tm // POOL_HALO
    nhb = l // POOL_HALO
    return pl.pallas_call(
        functools.partial(_post0_kernel, seq_len=l),
        grid=(b, l // tm),
        in_specs=[pl.BlockSpec((1, tm, attn.shape[2]), lambda b_, i: (b_, i, 0)),
                  pl.BlockSpec((1, tm, POOL_WIDTH), lambda b_, i: (b_, i, 0)),
                  pl.BlockSpec((1, POOL_HALO, POOL_WIDTH),
                               lambda b_, i: (b_, jnp.maximum(i * hb - 1, 0), 0)),
                  pl.BlockSpec((1, POOL_HALO, POOL_WIDTH),
                               lambda b_, i: (b_, jnp.minimum((i + 1) * hb, nhb - 1), 0)),
                  pl.BlockSpec(poolw_bd.shape, lambda b_, i: (0, 0)),
                  pl.BlockSpec((1, POOL_WIDTH), lambda b_, i: (0, 0)),
                  pl.BlockSpec(w_out.shape, lambda b_, i: (0, 0)),
                  pl.BlockSpec((1, tm, d), lambda b_, i: (b_, i, 0)),
                  _bvec_spec(g1, d),
                  pl.BlockSpec((1, d), lambda b_, i: (0, 0)),
                  _bvec_spec(sh2, d), _bvec_spec(sc2, d)],
        out_specs=[pl.BlockSpec((1, tm, d), lambda b_, i: (b_, i, 0)),
                   pl.BlockSpec((1, tm, d), lambda b_, i: (b_, i, 0))],
        out_shape=[jax.ShapeDtypeStruct((b, l, d), F32),
                   jax.ShapeDtypeStruct((b, l, d), BF16)],
        compiler_params=_params(("parallel", "parallel")),
        name="post0",
    )(attn, p, p, p, poolw_bd, pscale, w_out, h, g1, n2g, sh2, sc2)


def _swiglu_tile(x, wg_ref, wu_ref, wd_ref, lead=()):
    acc = None
    c0 = 0
    for cw in FFN_CHUNKS:
        hg = _dot(x, wg_ref[lead + (slice(None), slice(c0, c0 + cw))])
        hu = _dot(x, wu_ref[lead + (slice(None), slice(c0, c0 + cw))])
        act = (_silu(hg) * hu).astype(BF16)
        part = _dot(act, wd_ref[lead + (slice(c0, c0 + cw), slice(None))])
        acc = part if acc is None else acc + part
        c0 += cw
    return acc


def _ffn_kernel(a2_ref, h1_ref, g2_ref, wg_ref, wu_ref, wd_ref, o_ref):
    y = _swiglu_tile(a2_ref[0], wg_ref, wu_ref, wd_ref)
    o_ref[0] = h1_ref[0] + g2_ref[0] * y


def _ffn(a2, h1, g2, wg, wu, wd, *, tm):
    b, l, d = h1.shape
    f = wg.shape[1]
    assert sum(FFN_CHUNKS) == f
    return pl.pallas_call(
        _ffn_kernel,
        grid=(b, l // tm),
        in_specs=[pl.BlockSpec((1, tm, d), lambda b_, i: (b_, i, 0)),
                  pl.BlockSpec((1, tm, d), lambda b_, i: (b_, i, 0)),
                  _bvec_spec(g2, d),
                  pl.BlockSpec((d, f), lambda b_, i: (0, 0)),
                  pl.BlockSpec((d, f), lambda b_, i: (0, 0)),
                  pl.BlockSpec((f, d), lambda b_, i: (0, 0))],
        out_specs=pl.BlockSpec((1, tm, d), lambda b_, i: (b_, i, 0)),
        out_shape=jax.ShapeDtypeStruct((b, l, d), F32),
        compiler_params=_params(("parallel", "parallel")),
        name="ffn_dense",
    )(a2, h1, g2, wg, wu, wd)


def _pre1_kernel(h_ref, g_ref, sh_ref, sc_ref, win_ref, qg_ref, kvg_ref, wuq_ref, wuk_ref, wuv_ref,
                 cos_ref, sin_ref, *out_refs, use_rope, want_q, q_rank, kv_rank, n_heads):
    a = _norm_mod(h_ref[0], g_ref[...], sh_ref[0], sc_ref[0])
    z = _dot(a.astype(BF16), win_ref[...])
    cos = cos_ref[...]
    sin = sin_ref[...]
    if want_q:
        q_ref, k_ref, vt_ref = out_refs
        cq = _rms(z[:, :q_rank]) * qg_ref[...]
        qf = _dot(cq.astype(BF16), wuq_ref[...])
        q_scale = (MLA_NOPE + MLA_ROPE) ** -0.5
        for h in range(n_heads):
            c0 = h * MLA_QK_PAD
            q_ref[0, :, c0:c0 + MLA_NOPE] = (qf[:, c0:c0 + MLA_NOPE] * q_scale).astype(BF16)
            qr = _rope(qf[:, c0 + MLA_NOPE:c0 + MLA_QK_PAD], cos, sin)
            q_ref[0, :, c0 + MLA_NOPE:c0 + MLA_QK_PAD] = (qr * q_scale).astype(BF16)
    else:
        k_ref, vt_ref = out_refs
    ckv = (_rms(z[:, q_rank:q_rank + kv_rank]) * kvg_ref[...]).astype(BF16)
    kn = _dot(ckv, wuk_ref[...])
    vv = _dot(ckv, wuv_ref[...])
    kr = z[:, q_rank + kv_rank:]
    if use_rope:
        kr = _rope(kr, cos, sin)
    kr = kr.astype(BF16)
    for h in range(n_heads):
        c0 = h * MLA_QK_PAD
        k_ref[0, :, c0:c0 + MLA_NOPE] = kn[:, h * MLA_NOPE:(h + 1) * MLA_NOPE].astype(BF16)
        k_ref[0, :, c0 + MLA_NOPE:c0 + MLA_QK_PAD] = kr
    vt_ref[0, 0] = vv.T.astype(BF16)


def _pre1(h, g, sh, sc, w_in, qg, kvg, wuq, wuk, wuv, cos, sin, *, use_rope, want_q, tm):
    b, l, d = h.shape
    n_heads = wuk.shape[1] // MLA_NOPE
    q_rank, kv_rank = wuq.shape[0], wuk.shape[0]
    kern = functools.partial(_pre1_kernel, use_rope=use_rope, want_q=want_q,
                             q_rank=q_rank, kv_rank=kv_rank, n_heads=n_heads)
    full = lambda x: pl.BlockSpec(x.shape, lambda b_, i: (0,) * x.ndim)
    qk_w = n_heads * MLA_QK_PAD
    v_w = n_heads * MLA_V
    out_specs = [pl.BlockSpec((1, tm, qk_w), lambda b_, i: (b_, i, 0)),
                 pl.BlockSpec((1, 1, v_w, tm), lambda b_, i: (b_, i, 0, 0))]
    out_shape = [jax.ShapeDtypeStruct((b, l, qk_w), BF16),
                 jax.ShapeDtypeStruct((b, l // tm, v_w, tm), BF16)]
    if want_q:
        out_specs = [pl.BlockSpec((1, tm, qk_w), lambda b_, i: (b_, i, 0))] + out_specs
        out_shape = [jax.ShapeDtypeStruct((b, l, qk_w), BF16)] + out_shape
    return pl.pallas_call(
        kern,
        grid=(b, l // tm),
        in_specs=[pl.BlockSpec((1, tm, d), lambda b_, i: (b_, i, 0)),
                  pl.BlockSpec((1, d), lambda b_, i: (0, 0)),
                  _bvec_spec(sh, d), _bvec_spec(sc, d),
                  full(w_in), full(qg), full(kvg), full(wuq), full(wuk), full(wuv),
                  pl.BlockSpec((tm, LANES), lambda b_, i: (i, 0)),
                  pl.BlockSpec((tm, LANES), lambda b_, i: (i, 0))],
        out_specs=out_specs,
        out_shape=out_shape,
        compiler_params=_params(("parallel", "parallel")),
        name="pre1_lat" if want_q else "pre1_ctx",
    )(h, g, sh, sc, w_in, qg, kvg, wuq, wuk, wuv, cos, sin)


def _post1_kernel(o_ref, wout_ref, h_ref, g1_ref, n2g_ref, sh2_ref, sc2_ref, rw_ref,
                  h1_ref, a2_ref, gates_ref):
    y = _dot(o_ref[0], wout_ref[...])
    h1 = h_ref[0] + g1_ref[0] * y
    h1_ref[0] = h1
    a2 = _norm_mod(h1, n2g_ref[...], sh2_ref[0], sc2_ref[0])
    a2_ref[0] = a2.astype(BF16)
    logits = jnp.dot(a2, rw_ref[...], preferred_element_type=F32,
                     precision=lax.Precision.HIGHEST)
    lane = lax.broadcasted_iota(jnp.int32, logits.shape, 1)
    lg = jnp.where(lane < N_EXPERTS, logits, -jnp.inf)
    m1 = jnp.max(lg, axis=1, keepdims=True)
    i1 = jnp.min(jnp.where(lg == m1, lane, LANES), axis=1, keepdims=True)
    lg2 = jnp.where(lane == i1, -jnp.inf, lg)
    m2 = jnp.max(lg2, axis=1, keepdims=True)
    i2 = jnp.min(jnp.where(lg2 == m2, lane, LANES), axis=1, keepdims=True)
    p2 = jnp.exp(m2 - m1)
    w1 = 1.0 / (1.0 + p2)
    gates_ref[0] = jnp.where(lane == i1, w1, 0.0) + jnp.where(lane == i2, p2 * w1, 0.0)


def _post1(o, w_out, h, g1, n2g, sh2, sc2, rw_pad, *, tm):
    b, l, d = h.shape
    return pl.pallas_call(
        _post1_kernel,
        grid=(b, l // tm),
        in_specs=[pl.BlockSpec((1, tm, o.shape[2]), lambda b_, i: (b_, i, 0)),
                  pl.BlockSpec(w_out.shape, lambda b_, i: (0, 0)),
                  pl.BlockSpec((1, tm, d), lambda b_, i: (b_, i, 0)),
                  _bvec_spec(g1, d),
                  pl.BlockSpec((1, d), lambda b_, i: (0, 0)),
                  _bvec_spec(sh2, d), _bvec_spec(sc2, d),
                  pl.BlockSpec(rw_pad.shape, lambda b_, i: (0, 0))],
        out_specs=[pl.BlockSpec((1, tm, d), lambda b_, i: (b_, i, 0)),
                   pl.BlockSpec((1, tm, d), lambda b_, i: (b_, i, 0)),
                   pl.BlockSpec((1, tm, LANES), lambda b_, i: (b_, i, 0))],
        out_shape=[jax.ShapeDtypeStruct((b, l, d), F32),
                   jax.ShapeDtypeStruct((b, l, d), BF16),
                   jax.ShapeDtypeStruct((b, l, LANES), F32)],
        compiler_params=_params(("parallel", "parallel")),
        name="post1_router",
    )(o, w_out, h, g1, n2g, sh2, sc2, rw_pad)


def _moe_kernel(a2_ref, gates_ref, wg_ref, wu_ref, wd_ref, h1_ref, g2_ref, fng_ref, o_ref, acc_ref):
    e = pl.program_id(2)

    @pl.when(e == 0)
    def _():
        acc_ref[...] = jnp.zeros_like(acc_ref)

    y = _swiglu_tile(a2_ref[0], wg_ref, wu_ref, wd_ref, lead=(0,))
    gates = gates_ref[0]
    lane = lax.broadcasted_iota(jnp.int32, gates.shape, 1)
    gate = jnp.sum(jnp.where(lane == e, gates, 0.0), axis=1, keepdims=True)
    acc_ref[...] += gate * y

    @pl.when(e == pl.num_programs(2) - 1)
    def _():
        out = h1_ref[0] + g2_ref[0] * acc_ref[...]
        o_ref[0] = _rms(out) * fng_ref[...]


def _moe(a2, gates, wg, wu, wd, h1, g2, fng, *, tm):
    b, l, d = h1.shape
    ne, _, f = wg.shape
    return pl.pallas_call(
        _moe_kernel,
        grid=(b, l // tm, ne),
        in_specs=[pl.BlockSpec((1, tm, d), lambda b_, i, e: (b_, i, 0)),
                  pl.BlockSpec((1, tm, LANES), lambda b_, i, e: (b_, i, 0)),
                  pl.BlockSpec((1, d, f), lambda b_, i, e: (e, 0, 0)),
                  pl.BlockSpec((1, d, f), lambda b_, i, e: (e, 0, 0)),
                  pl.BlockSpec((1, f, d), lambda b_, i, e: (e, 0, 0)),
                  pl.BlockSpec((1, tm, d), lambda b_, i, e: (b_, i, 0)),
                  pl.BlockSpec((1, 1, d), lambda b_, i, e: (b_, 0, 0)),
                  pl.BlockSpec((1, d), lambda b_, i, e: (0, 0))],
        out_specs=pl.BlockSpec((1, tm, d), lambda b_, i, e: (b_, i, 0)),
        out_shape=jax.ShapeDtypeStruct((b, l, d), F32),
        scratch_shapes=[pltpu.VMEM((tm, d), F32)],
        compiler_params=_params(("parallel", "parallel", "arbitrary")),
        name="moe_dense_gated",
    )(a2, gates, wg, wu, wd, h1, g2, fng)


def _rope_tables(seq_len, dim):
    t = jnp.arange(seq_len, dtype=jnp.int32)
    row = (t // GRID_W).astype(F32)
    col = (t % GRID_W).astype(F32)
    quarter = dim // 4
    inv = ROPE_BASE ** (-jnp.arange(quarter, dtype=F32) / quarter)
    ar = row[:, None] * inv[None, :]
    ac = col[:, None] * inv[None, :]
    cos = jnp.concatenate([jnp.cos(ar), jnp.cos(ar), jnp.cos(ac), jnp.cos(ac)], axis=-1)
    sin = jnp.concatenate([jnp.sin(ar), jnp.sin(ar), jnp.sin(ac), jnp.sin(ac)], axis=-1)
    reps = LANES // dim
    return jnp.tile(cos, (1, reps)), jnp.tile(sin, (1, reps))


def kernel(x, c, ctx, c_ctx, norm1_g, norm2_g, mod_w, mod_b, even_w_in, pool_w, pool_scale,
           lambda_q1, lambda_k1, lambda_q2, lambda_k2, even_w_out, ffn_w_gate, ffn_w_up,
           ffn_w_down, odd_w_in, q_norm_g, kv_norm_g, w_uq, w_ukv, odd_w_out, router_w,
           moe_w_gate, moe_w_up, moe_w_down, final_norm_g):
    b, l, d = x.shape
    n_ctx = ctx.shape[1]
    tm = min(ROW_TILE, l)
    tq = min(Q_TILE, l)
    assert tm == KV_TILE or l < KV_TILE
    cos, sin = _rope_tables(l, DIFF_HEAD_DIM)
    cos_c, sin_c = cos[:n_ctx], sin[:n_ctx]

    cond = jnp.zeros((8, d), F32).at[:b].set(c).at[b].set(c_ctx)
    mod = _modulation(cond, mod_w, mod_b)
    mod = mod.reshape(mod.shape[0], 8, N_MOD, d)

    def mod_vecs(layer):
        lat = [mod[layer, :b, k][:, None, :] for k in range(N_MOD)]
        cx = [mod[layer, b:b + 1, k][:, None, :] for k in range(N_MOD)]
        return lat, cx

    (sh1, sc1, g1, sh2, sc2, g2), (csh1, csc1, cg1, csh2, csc2, cg2) = mod_vecs(0)
    lam_init = 0.8 - 0.6 * math.exp(-0.3 * 0)
    w_in0 = even_w_in[0].astype(BF16)
    n1g = norm1_g[0][None, :]
    n2g = norm2_g[0][None, :]
    p_lat, q_lat, k_lat, vt_lat = _pre0(x, n1g, sh1, sc1, w_in0, cos, sin, use_rope=True, tm=tm)
    p_ctx, q_ctx, k_ctx, vt_ctx = _pre0(ctx, n1g, csh1, csc1, w_in0, cos_c, sin_c,
                                        use_rope=False, tm=n_ctx)
    lq = jnp.stack([lambda_q1[0], lambda_q2[0]])
    lk = jnp.stack([lambda_k1[0], lambda_k2[0]])
    diff_kern = functools.partial(_diff_attn_kernel, lam_init=lam_init)
    attn_lat = _attention(diff_kern, 2, q_lat, [(k_lat, vt_lat), (k_ctx, vt_ctx)], 2 * DIFF_HEAD_DIM,
                          extra=(lq, lk), tq=tq, name="diff_attn_lat")
    attn_ctx = _attention(diff_kern, 2, q_ctx, [(k_ctx, vt_ctx)], 2 * DIFF_HEAD_DIM,
                          extra=(lq, lk), tq=n_ctx, name="diff_attn_ctx")
    ng = len(POOL_WINDOWS)
    poolw_bd = jnp.zeros((POOL_WIDTH, POOL_WIDTH), F32)
    for g in range(ng):
        sl = slice(g * POOL_GROUP, (g + 1) * POOL_GROUP)
        poolw_bd = poolw_bd.at[sl, sl].set(pool_w[0, g])
    poolw_bd = poolw_bd.astype(BF16)
    pscale = pool_scale[0][None, :]
    w_out0 = even_w_out[0].astype(BF16)
    wg0, wu0, wd0 = ffn_w_gate[0].astype(BF16), ffn_w_up[0].astype(BF16), ffn_w_down[0].astype(BF16)
    h1_lat, a2_lat = _post0(attn_lat, p_lat, poolw_bd, pscale, w_out0, x, g1, n2g, sh2, sc2, tm=tm)
    h_lat = _ffn(a2_lat, h1_lat, g2, wg0, wu0, wd0, tm=tm)
    h1_ctx, a2_ctx = _post0(attn_ctx, p_ctx, poolw_bd, pscale, w_out0, ctx, cg1, n2g, csh2, csc2,
                            tm=n_ctx)
    h_ctx = _ffn(a2_ctx, h1_ctx, cg2, wg0, wu0, wd0, tm=n_ctx)

    (sh1, sc1, g1, sh2, sc2, g2), (csh1, csc1, _, _, _, _) = mod_vecs(1)
    n1g = norm1_g[1][None, :]
    n2g = norm2_g[1][None, :]
    q_rank, kv_rank = w_uq.shape[1], w_ukv.shape[1]
    n_heads = w_uq.shape[2] // (MLA_NOPE + MLA_ROPE)
    w_in1 = jnp.pad(odd_w_in[0], ((0, 0), (0, LANES - MLA_ROPE))).astype(BF16)
    wuq = w_uq[0].reshape(q_rank, n_heads, MLA_NOPE + MLA_ROPE)
    wuq = jnp.pad(wuq, ((0, 0), (0, 0), (0, MLA_QK_PAD - MLA_NOPE - MLA_ROPE)))
    wuq = wuq.reshape(q_rank, n_heads * MLA_QK_PAD).astype(BF16)
    wukv = w_ukv[0].reshape(kv_rank, n_heads, MLA_NOPE + MLA_V)
    wuk = wukv[:, :, :MLA_NOPE].reshape(kv_rank, n_heads * MLA_NOPE).astype(BF16)
    wuv = wukv[:, :, MLA_NOPE:].reshape(kv_rank, n_heads * MLA_V).astype(BF16)
    qg = q_norm_g[0][None, :]
    kvg = kv_norm_g[0][None, :]
    q1, k1, vt1 = _pre1(h_lat, n1g, sh1, sc1, w_in1, qg, kvg, wuq, wuk, wuv, cos, sin,
                        use_rope=True, want_q=True, tm=tm)
    k1c, vt1c = _pre1(h_ctx, n1g, csh1, csc1, w_in1, qg, kvg, wuq, wuk, wuv, cos_c, sin_c,
                      use_rope=False, want_q=False, tm=n_ctx)
    o1 = _attention(_mla_attn_kernel, 1, q1, [(k1, vt1), (k1c, vt1c)], MLA_QK_PAD,
                    tq=tq, name="mla_attn")
    rw_pad = jnp.pad(router_w[0], ((0, 0), (0, LANES - N_EXPERTS)))
    h1, a2, gates = _post1(o1, odd_w_out[0].astype(BF16), h_lat, g1, n2g, sh2, sc2, rw_pad, tm=tm)
    return _moe(a2, gates, moe_w_gate[0].astype(BF16), moe_w_up[0].astype(BF16),
                moe_w_down[0].astype(BF16), h1, g2, final_norm_g[None, :], tm=tm)
```

```python
import functools
import math

import jax
import jax.numpy as jnp
from jax import lax
from jax.experimental import pallas as pl
from jax.experimental.pallas import tpu as pltpu

F32 = jnp.float32
BF16 = jnp.bfloat16

EPS = 1e-6
GRID_W = 64
ROPE_BASE = 10000.0
N_MOD = 6
POOL_WINDOWS = (2, 4, 8, 16)
POOL_GROUP = 64
POOL_WIDTH = POOL_GROUP * len(POOL_WINDOWS)
POOL_HALO = max(POOL_WINDOWS) // 2
DIFF_HEAD_DIM = 64
DIFF_V_DIM = 128
MLA_NOPE = 128
MLA_ROPE = 64
MLA_V = 128
MLA_QK_PAD = 256
N_EXPERTS = 8
ROPE_QUARTER = 16
LOG2E = math.log2(math.e)

LANES = 128
ROW_TILE = 512
KV_TILE = 512
Q_TILE = 512
FFN_CHUNKS = (512, 512, 512, 512, 512, 256)
VMEM_LIMIT = 56 * 1024 * 1024

NT_DIMS = (((1,), (1,)), ((), ()))


def _params(sem):
    return pltpu.CompilerParams(dimension_semantics=sem, vmem_limit_bytes=VMEM_LIMIT)


def _dot(a, b):
    return jnp.dot(a, b, preferred_element_type=F32)


def _dot_nt(a, b):
    return lax.dot_general(a, b, NT_DIMS, preferred_element_type=F32)


def _rms(x):
    return x * lax.rsqrt(jnp.mean(x * x, axis=-1, keepdims=True) + EPS)


def _silu(x):
    return x * (1.0 / (1.0 + jnp.exp(-x)))


def _rope(x, cos, sin):
    lane = lax.broadcasted_iota(jnp.int32, x.shape, 1)
    even = (lane // ROPE_QUARTER) % 2 == 0
    rot = jnp.where(even, -pltpu.roll(x, LANES - ROPE_QUARTER, 1), pltpu.roll(x, ROPE_QUARTER, 1))
    return x * cos + rot * sin


def _mod_kernel(cond_ref, w_ref, b_ref, o_ref):
    s = _silu(cond_ref[...])
    o_ref[0] = jnp.dot(s, w_ref[0], preferred_element_type=F32,
                       precision=lax.Precision.HIGHEST) + b_ref[0]


def _modulation(cond, mod_w, mod_b):
    depth, d, n = mod_w.shape
    tn = n // 4
    return pl.pallas_call(
        _mod_kernel,
        grid=(depth, n // tn),
        in_specs=[pl.BlockSpec((8, d), lambda i, j: (0, 0)),
                  pl.BlockSpec((1, d, tn), lambda i, j: (i, 0, j)),
                  pl.BlockSpec((1, 1, tn), lambda i, j: (i, 0, j))],
        out_specs=pl.BlockSpec((1, 8, tn), lambda i, j: (i, 0, j)),
        out_shape=jax.ShapeDtypeStruct((depth, 8, n), F32),
        compiler_params=_params(("arbitrary", "arbitrary")),
        name="modulation",
    )(cond, mod_w, mod_b.reshape(depth, 1, n))


def _bvec_spec(arr, d):
    if arr.shape[0] == 1:
        return pl.BlockSpec((1, 1, d), lambda b, i: (0, 0, 0))
    return pl.BlockSpec((1, 1, d), lambda b, i: (b, 0, 0))


def _norm_mod(h, g, sh, sc):
    a = _rms(h) * g
    return a * (1.0 + sc) + sh


def _pre0_kernel(h_ref, g_ref, sh_ref, sc_ref, w_ref, cos_ref, sin_ref,
                 p_ref, q_ref, k_ref, vt_ref, *, use_rope, n_heads):
    a = _norm_mod(h_ref[0], g_ref[...], sh_ref[0], sc_ref[0])
    z = _dot(a.astype(BF16), w_ref[...])
    p_ref[0] = z[:, :POOL_WIDTH]
    qk_w = n_heads * 2 * DIFF_HEAD_DIM
    q_scale = DIFF_HEAD_DIM ** -0.5 * LOG2E
    cos = cos_ref[...]
    sin = sin_ref[...]
    for j in range(qk_w // LANES):
        xq = z[:, POOL_WIDTH + j * LANES:POOL_WIDTH + (j + 1) * LANES]
        xk = z[:, POOL_WIDTH + qk_w + j * LANES:POOL_WIDTH + qk_w + (j + 1) * LANES]
        if use_rope:
            xq = _rope(xq, cos, sin)
            xk = _rope(xk, cos, sin)
        q_ref[0, :, j * LANES:(j + 1) * LANES] = (xq * q_scale).astype(BF16)
        k_ref[0, :, j * LANES:(j + 1) * LANES] = xk.astype(BF16)
    v = z[:, POOL_WIDTH + 2 * qk_w:]
    vt_ref[0, 0] = v.T.astype(BF16)


def _pre0(h, g, sh, sc, w_in, cos, sin, *, use_rope, tm):
    b, l, d = h.shape
    n_in = w_in.shape[1]
    v_w = (n_in - POOL_WIDTH) // 3
    n_heads = v_w // DIFF_V_DIM
    kern = functools.partial(_pre0_kernel, use_rope=use_rope, n_heads=n_heads)
    return pl.pallas_call(
        kern,
        grid=(b, l // tm),
        in_specs=[pl.BlockSpec((1, tm, d), lambda b_, i: (b_, i, 0)),
                  pl.BlockSpec((1, d), lambda b_, i: (0, 0)),
                  _bvec_spec(sh, d), _bvec_spec(sc, d),
                  pl.BlockSpec((d, n_in), lambda b_, i: (0, 0)),
                  pl.BlockSpec((tm, LANES), lambda b_, i: (i, 0)),
                  pl.BlockSpec((tm, LANES), lambda b_, i: (i, 0))],
        out_specs=[pl.BlockSpec((1, tm, POOL_WIDTH), lambda b_, i: (b_, i, 0)),
                   pl.BlockSpec((1, tm, v_w), lambda b_, i: (b_, i, 0)),
                   pl.BlockSpec((1, tm, v_w), lambda b_, i: (b_, i, 0)),
                   pl.BlockSpec((1, 1, v_w, tm), lambda b_, i: (b_, i, 0, 0))],
        out_shape=[jax.ShapeDtypeStruct((b, l, POOL_WIDTH), F32),
                   jax.ShapeDtypeStruct((b, l, v_w), BF16),
                   jax.ShapeDtypeStruct((b, l, v_w), BF16),
                   jax.ShapeDtypeStruct((b, l // tm, v_w, tm), BF16)],
        compiler_params=_params(("parallel", "parallel")),
        name="pre0_rope" if use_rope else "pre0_ctx",
    )(h, g, sh, sc, w_in, cos, sin)


def _attn_core(qs, kc_ref, vc_ref, lat_refs, acc_refs, sbuf_refs):
    n = len(qs)
    kc = kc_ref[0]
    vc = vc_ref[0, 0]
    ms, ls = [], []
    for i in range(n):
        s = _dot_nt(kc, qs[i])
        m = jnp.max(s, axis=0, keepdims=True)
        p = jnp.exp2(s - m)
        ms.append(m)
        ls.append(jnp.sum(p, axis=0, keepdims=True))
        acc_refs[i][...] = _dot(vc, p.astype(BF16))
    if lat_refs is None:
        return ls
    kl_ref, vl_ref = lat_refs
    nblk, tk = vl_ref.shape[1], vl_ref.shape[3]
    assert nblk >= 2 and nblk % 2 == 0

    def qk(t, which):
        start = t * tk if isinstance(t, int) else pl.multiple_of(t * tk, tk)
        k = kl_ref[0, pl.ds(start, tk), :]
        cmax = []
        for i in range(n):
            s = _dot_nt(k, qs[i])
            sbuf_refs[2 * i + which][...] = s
            cmax.append(jnp.max(s, axis=0, keepdims=True))
        return tuple(cmax)

    def softmax_pv(t, which, cmax, ms, ls):
        vt = vl_ref[0, t]
        new_m, new_l = [], []
        for i in range(n):
            m_new = jnp.maximum(ms[i], cmax[i])
            alpha = jnp.exp2(ms[i] - m_new)
            p = jnp.exp2(sbuf_refs[2 * i + which][...] - m_new)
            new_l.append(alpha * ls[i] + jnp.sum(p, axis=0, keepdims=True))
            acc_refs[i][...] = alpha * acc_refs[i][...] + _dot(vt, p.astype(BF16))
            new_m.append(m_new)
        return tuple(new_m), tuple(new_l)

    def pair(jj, carry):
        cmax_a, ms, ls = carry
        t = 2 * jj
        cmax_b = qk(t + 1, 1)
        ms, ls = softmax_pv(t, 0, cmax_a, ms, ls)
        cmax_a = qk(t + 2, 0)
        ms, ls = softmax_pv(t + 1, 1, cmax_b, ms, ls)
        return cmax_a, ms, ls

    cmax_a = qk(0, 0)
    cmax_a, ms, ls = lax.fori_loop(0, nblk // 2 - 1, pair, (cmax_a, tuple(ms), tuple(ls)))
    cmax_b = qk(nblk - 1, 1)
    ms, ls = softmax_pv(nblk - 2, 0, cmax_a, ms, ls)
    ms, ls = softmax_pv(nblk - 1, 1, cmax_b, ms, ls)
    return ls


def _diff_attn_kernel(lq_ref, lk_ref, q_ref, kc_ref, vc_ref, *rest, has_lat, lam_init):
    lat_refs = rest[:2] if has_lat else None
    rest = rest[2:] if has_lat else rest
    o_ref, acc1_ref, acc2_ref = rest[:3]
    q = q_ref[0]
    lane = lax.broadcasted_iota(jnp.int32, q.shape, 1)
    zero = jnp.zeros_like(q)
    q1 = jnp.where(lane < DIFF_HEAD_DIM, q, zero)
    q2 = jnp.where(lane >= DIFF_HEAD_DIM, q, zero)
    l1, l2 = _attn_core([q1, q2], kc_ref, vc_ref, lat_refs, (acc1_ref, acc2_ref), rest[3:])
    e = jnp.exp(jnp.sum(lq_ref[...] * lk_ref[...], axis=-1, keepdims=True))
    lam = e[0:1] - e[1:2] + lam_init
    o = acc1_ref[...] * (1.0 / l1) - lam * (acc2_ref[...] * (1.0 / l2))
    ms = jnp.mean(o * o, axis=0, keepdims=True)
    y = o * (lax.rsqrt(ms + EPS) * (1.0 - lam_init))
    o_ref[0] = y.T.astype(BF16)


def _mla_attn_kernel(q_ref, kc_ref, vc_ref, *rest, has_lat):
    lat_refs = rest[:2] if has_lat else None
    rest = rest[2:] if has_lat else rest
    o_ref, acc_ref = rest[:2]
    (l,) = _attn_core([q_ref[0]], kc_ref, vc_ref, lat_refs, (acc_ref,), rest[2:])
    o_ref[0] = (acc_ref[...] * (1.0 / l)).T.astype(BF16)


def _attention(kern, n_maps, q, ctx_kv, lat_kv, qk_w, extra=(), *, tq, name):
    b, lq, hw = q.shape
    n_heads = hw // qk_w
    in_specs = [pl.BlockSpec(x.shape, lambda b_, h, i: (0, 0)) for x in extra]
    in_specs.append(pl.BlockSpec((1, tq, qk_w), lambda b_, h, i: (b_, i, h)))
    args = list(extra) + [q]
    scratch = [pltpu.VMEM((DIFF_V_DIM, tq), F32) for _ in range(n_maps)]
    for kv in (ctx_kv, lat_kv):
        if kv is None:
            continue
        k, vt = kv
        nblk, tk = vt.shape[1], vt.shape[3]
        in_specs.append(pl.BlockSpec((1, k.shape[1], qk_w), lambda b_, h, i: (b_, 0, h)))
        in_specs.append(pl.BlockSpec((1, nblk, DIFF_V_DIM, tk), lambda b_, h, i: (b_, 0, h, 0)))
        args += [k, vt]
    if lat_kv is not None:
        scratch += [pltpu.VMEM((lat_kv[1].shape[3], tq), F32) for _ in range(2 * n_maps)]
    return pl.pallas_call(
        functools.partial(kern, has_lat=lat_kv is not None),
        grid=(b, n_heads, lq // tq),
        in_specs=in_specs,
        out_specs=pl.BlockSpec((1, tq, DIFF_V_DIM), lambda b_, h, i: (b_, i, h)),
        out_shape=jax.ShapeDtypeStruct((b, lq, n_heads * DIFF_V_DIM), BF16),
        scratch_shapes=scratch,
        compiler_params=_params(("parallel", "parallel", "arbitrary")),
        name=name,
    )(*args)


def _post0_kernel(attn_ref, p_ref, pprev_ref, pnext_ref, poolw_ref, pscale_ref, wout_ref,
                  h_ref, g1_ref, n2g_ref, sh2_ref, sc2_ref, h1_ref, a2_ref, *, seq_len):
    i = pl.program_id(1)
    u = p_ref[0]
    tm = u.shape[0]
    ext = jnp.concatenate([pprev_ref[0], u, pnext_ref[0]], axis=0)
    row = i * tm - POOL_HALO + lax.broadcasted_iota(jnp.int32, (tm + 2 * POOL_HALO, 1), 0)
    ext = jnp.where((row >= 0) & (row < seq_len), ext, 0.0)

    def shifted(j):
        return ext[POOL_HALO + j:POOL_HALO + j + tm]

    t = i * tm + lax.broadcasted_iota(jnp.int32, (tm, 1), 0)
    lane = lax.broadcasted_iota(jnp.int32, (tm, POOL_WIDTH), 1)
    win_sum = shifted(-1) + shifted(0)
    pooled = jnp.zeros_like(u)
    for g, w in enumerate(POOL_WINDOWS):
        half = w // 2
        if g > 0:
            prev_half = POOL_WINDOWS[g - 1] // 2
            for j in range(prev_half, half):
                win_sum = win_sum + shifted(-j - 1) + shifted(j)
        cnt = (jnp.minimum(t + half, seq_len) - jnp.maximum(t - half, 0)).astype(F32)
        in_group = (lane >= g * POOL_GROUP) & (lane < (g + 1) * POOL_GROUP)
        pooled = jnp.where(in_group, win_sum / cnt, pooled)
    pooled = pooled - u
    y_pool = _dot(pooled.astype(BF16), poolw_ref[...]) * pscale_ref[...]
    y = _dot(y_pool.astype(BF16), wout_ref[:POOL_WIDTH, :]) + _dot(attn_ref[0], wout_ref[POOL_WIDTH:, :])
    h1 = h_ref[0] + g1_ref[0] * y
    h1_ref[0] = h1
    a2_ref[0] = _norm_mod(h1, n2g_ref[...], sh2_ref[0], sc2_ref[0]).astype(BF16)


def _post0(attn, p, poolw_bd, pscale, w_out, h, g1, n2g, sh2, sc2, *, tm):
    b, l, d = h.shape
    hb = tm // POOL_HALO
    nhb = l // POOL_HALO
    return pl.pallas_call(
        functools.partial(_post0_kernel, seq_len=l),
        grid=(b, l // tm),
        in_specs=[pl.BlockSpec((1, tm, attn.shape[2]), lambda b_, i: (b_, i, 0)),
                  pl.BlockSpec((1, tm, POOL_WIDTH), lambda b_, i: (b_, i, 0)),
                  pl.BlockSpec((1, POOL_HALO, POOL_WIDTH),
                               lambda b_, i: (b_, jnp.maximum(i * hb - 1, 0), 0)),
                  pl.BlockSpec((1, POOL_HALO, POOL_WIDTH),
                               lambda b_, i: (b_, jnp.minimum((i + 1) * hb, nhb - 1), 0)),
                  pl.BlockSpec(poolw_bd.shape, lambda b_, i: (0, 0)),
                  pl.BlockSpec((1, POOL_WIDTH), lambda b_, i: (0, 0)),
                  pl.BlockSpec(w_out.shape, lambda b_, i: (0, 0)),
                  pl.BlockSpec((1, tm, d), lambda b_, i: (b_, i, 0)),
                  _bvec_spec(g1, d),
                  pl.BlockSpec((1, d), lambda b_, i: (0, 0)),
                  _bvec_spec(sh2, d), _bvec_spec(sc2, d)],
        out_specs=[pl.BlockSpec((1, tm, d), lambda b_, i: (b_, i, 0)),
                   pl.BlockSpec((1, tm, d), lambda b_, i: (b_, i, 0))],
        out_shape=[jax.ShapeDtypeStruct((b, l, d), F32),
                   jax.ShapeDtypeStruct((b, l, d), BF16)],
        compiler_params=_params(("parallel", "parallel")),
        name="post0",
    )(attn, p, p, p, poolw_bd, pscale, w_out, h, g1, n2g, sh2, sc2)


def _swiglu_tile(x, wg_ref, wu_ref, wd_ref, lead=()):
    acc = None
    c0 = 0
    for cw in FFN_CHUNKS:
        hg = _dot(x, wg_ref[lead + (slice(None), slice(c0, c0 + cw))])
        hu = _dot(x, wu_ref[lead + (slice(None), slice(c0, c0 + cw))])
        act = (_silu(hg) * hu).astype(BF16)
        part = _dot(act, wd_ref[lead + (slice(c0, c0 + cw), slice(None))])
        acc = part if acc is None else acc + part
        c0 += cw
    return acc


def _ffn_kernel(a2_ref, h1_ref, g2_ref, wg_ref, wu_ref, wd_ref, o_ref):
    y = _swiglu_tile(a2_ref[0], wg_ref, wu_ref, wd_ref)
    o_ref[0] = h1_ref[0] + g2_ref[0] * y


def _ffn(a2, h1, g2, wg, wu, wd, *, tm):
    b, l, d = h1.shape
    f = wg.shape[1]
    assert sum(FFN_CHUNKS) == f
    return pl.pallas_call(
        _ffn_kernel,
        grid=(b, l // tm),
        in_specs=[pl.BlockSpec((1, tm, d), lambda b_, i: (b_, i, 0)),
                  pl.BlockSpec((1, tm, d), lambda b_, i: (b_, i, 0)),
                  _bvec_spec(g2, d),
                  pl.BlockSpec((d, f), lambda b_, i: (0, 0)),
                  pl.BlockSpec((d, f), lambda b_, i: (0, 0)),
                  pl.BlockSpec((f, d), lambda b_, i: (0, 0))],
        out_specs=pl.BlockSpec((1, tm, d), lambda b_, i: (b_, i, 0)),
        out_shape=jax.ShapeDtypeStruct((b, l, d), F32),
        compiler_params=_params(("parallel", "parallel")),
        name="ffn_dense",
    )(a2, h1, g2, wg, wu, wd)


def _pre1_kernel(h_ref, g_ref, sh_ref, sc_ref, win_ref, qg_ref, kvg_ref, wuq_ref, wuk_ref, wuv_ref,
                 cos_ref, sin_ref, *out_refs, use_rope, want_q, q_rank, kv_rank, n_heads):
    a = _norm_mod(h_ref[0], g_ref[...], sh_ref[0], sc_ref[0])
    z = _dot(a.astype(BF16), win_ref[...])
    cos = cos_ref[...]
    sin = sin_ref[...]
    if want_q:
        q_ref, k_ref, vt_ref = out_refs
        cq = _rms(z[:, :q_rank]) * qg_ref[...]
        qf = _dot(cq.astype(BF16), wuq_ref[...])
        q_scale = (MLA_NOPE + MLA_ROPE) ** -0.5 * LOG2E
        for h in range(n_heads):
            c0 = h * MLA_QK_PAD
            q_ref[0, :, c0:c0 + MLA_NOPE] = (qf[:, c0:c0 + MLA_NOPE] * q_scale).astype(BF16)
            qr = _rope(qf[:, c0 + MLA_NOPE:c0 + MLA_QK_PAD], cos, sin)
            q_ref[0, :, c0 + MLA_NOPE:c0 + MLA_QK_PAD] = (qr * q_scale).astype(BF16)
    else:
        k_ref, vt_ref = out_refs
    ckv = (_rms(z[:, q_rank:q_rank + kv_rank]) * kvg_ref[...]).astype(BF16)
    kn = _dot(ckv, wuk_ref[...])
    vv = _dot(ckv, wuv_ref[...])
    kr = z[:, q_rank + kv_rank:]
    if use_rope:
        kr = _rope(kr, cos, sin)
    kr = kr.astype(BF16)
    for h in range(n_heads):
        c0 = h * MLA_QK_PAD
        k_ref[0, :, c0:c0 + MLA_NOPE] = kn[:, h * MLA_NOPE:(h + 1) * MLA_NOPE].astype(BF16)
        k_ref[0, :, c0 + MLA_NOPE:c0 + MLA_QK_PAD] = kr
    vt_ref[0, 0] = vv.T.astype(BF16)


def _pre1(h, g, sh, sc, w_in, qg, kvg, wuq, wuk, wuv, cos, sin, *, use_rope, want_q, tm):
    b, l, d = h.shape
    n_heads = wuk.shape[1] // MLA_NOPE
    q_rank, kv_rank = wuq.shape[0], wuk.shape[0]
    kern = functools.partial(_pre1_kernel, use_rope=use_rope, want_q=want_q,
                             q_rank=q_rank, kv_rank=kv_rank, n_heads=n_heads)
    full = lambda x: pl.BlockSpec(x.shape, lambda b_, i: (0,) * x.ndim)
    qk_w = n_heads * MLA_QK_PAD
    v_w = n_heads * MLA_V
    out_specs = [pl.BlockSpec((1, tm, qk_w), lambda b_, i: (b_, i, 0)),
                 pl.BlockSpec((1, 1, v_w, tm), lambda b_, i: (b_, i, 0, 0))]
    out_shape = [jax.ShapeDtypeStruct((b, l, qk_w), BF16),
                 jax.ShapeDtypeStruct((b, l // tm, v_w, tm), BF16)]
    if want_q:
        out_specs = [pl.BlockSpec((1, tm, qk_w), lambda b_, i: (b_, i, 0))] + out_specs
        out_shape = [jax.ShapeDtypeStruct((b, l, qk_w), BF16)] + out_shape
    return pl.pallas_call(
        kern,
        grid=(b, l // tm),
        in_specs=[pl.BlockSpec((1, tm, d), lambda b_, i: (b_, i, 0)),
                  pl.BlockSpec((1, d), lambda b_, i: (0, 0)),
                  _bvec_spec(sh, d), _bvec_spec(sc, d),
                  full(w_in), full(qg), full(kvg), full(wuq), full(wuk), full(wuv),
                  pl.BlockSpec((tm, LANES), lambda b_, i: (i, 0)),
                  pl.BlockSpec((tm, LANES), lambda b_, i: (i, 0))],
        out_specs=out_specs,
        out_shape=out_shape,
        compiler_params=_params(("parallel", "parallel")),
        name="pre1_lat" if want_q else "pre1_ctx",
    )(h, g, sh, sc, w_in, qg, kvg, wuq, wuk, wuv, cos, sin)


def _post1_kernel(o_ref, wout_ref, h_ref, g1_ref, n2g_ref, sh2_ref, sc2_ref, rw_ref,
                  h1_ref, a2_ref, gates_ref):
    y = _dot(o_ref[0], wout_ref[...])
    h1 = h_ref[0] + g1_ref[0] * y
    h1_ref[0] = h1
    a2 = _norm_mod(h1, n2g_ref[...], sh2_ref[0], sc2_ref[0])
    a2_ref[0] = a2.astype(BF16)
    logits = jnp.dot(a2, rw_ref[...], preferred_element_type=F32,
                     precision=lax.Precision.HIGHEST)
    lane = lax.broadcasted_iota(jnp.int32, logits.shape, 1)
    lg = jnp.where(lane < N_EXPERTS, logits, -jnp.inf)
    m1 = jnp.max(lg, axis=1, keepdims=True)
    i1 = jnp.min(jnp.where(lg == m1, lane, LANES), axis=1, keepdims=True)
    lg2 = jnp.where(lane == i1, -jnp.inf, lg)
    m2 = jnp.max(lg2, axis=1, keepdims=True)
    i2 = jnp.min(jnp.where(lg2 == m2, lane, LANES), axis=1, keepdims=True)
    p2 = jnp.exp(m2 - m1)
    w1 = 1.0 / (1.0 + p2)
    gates_ref[0] = jnp.where(lane == i1, w1, 0.0) + jnp.where(lane == i2, p2 * w1, 0.0)


def _post1(o, w_out, h, g1, n2g, sh2, sc2, rw_pad, *, tm):
    b, l, d = h.shape
    return pl.pallas_call(
        _post1_kernel,
        grid=(b, l // tm),
        in_specs=[pl.BlockSpec((1, tm, o.shape[2]), lambda b_, i: (b_, i, 0)),
                  pl.BlockSpec(w_out.shape, lambda b_, i: (0, 0)),
                  pl.BlockSpec((1, tm, d), lambda b_, i: (b_, i, 0)),
                  _bvec_spec(g1, d),
                  pl.BlockSpec((1, d), lambda b_, i: (0, 0)),
                  _bvec_spec(sh2, d), _bvec_spec(sc2, d),
                  pl.BlockSpec(rw_pad.shape, lambda b_, i: (0, 0))],
        out_specs=[pl.BlockSpec((1, tm, d), lambda b_, i: (b_, i, 0)),
                   pl.BlockSpec((1, tm, d), lambda b_, i: (b_, i, 0)),
                   pl.BlockSpec((1, tm, LANES), lambda b_, i: (b_, i, 0))],
        out_shape=[jax.ShapeDtypeStruct((b, l, d), F32),
                   jax.ShapeDtypeStruct((b, l, d), BF16),
                   jax.ShapeDtypeStruct((b, l, LANES), F32)],
        compiler_params=_params(("parallel", "parallel")),
        name="post1_router",
    )(o, w_out, h, g1, n2g, sh2, sc2, rw_pad)


def _moe_kernel(a2_ref, gates_ref, wg_ref, wu_ref, wd_ref, h1_ref, g2_ref, fng_ref, o_ref, acc_ref):
    e = pl.program_id(2)

    @pl.when(e == 0)
    def _():
        acc_ref[...] = jnp.zeros_like(acc_ref)

    y = _swiglu_tile(a2_ref[0], wg_ref, wu_ref, wd_ref, lead=(0,))
    gates = gates_ref[0]
    lane = lax.broadcasted_iota(jnp.int32, gates.shape, 1)
    gate = jnp.sum(jnp.where(lane == e, gates, 0.0), axis=1, keepdims=True)
    acc_ref[...] += gate * y

    @pl.when(e == pl.num_programs(2) - 1)
    def _():
        out = h1_ref[0] + g2_ref[0] * acc_ref[...]
        o_ref[0] = _rms(out) * fng_ref[...]


def _moe(a2, gates, wg, wu, wd, h1, g2, fng, *, tm):
    b, l, d = h1.shape
    ne, _, f = wg.shape
    return pl.pallas_call(
        _moe_kernel,
        grid=(b, l // tm, ne),
        in_specs=[pl.BlockSpec((1, tm, d), lambda b_, i, e: (b_, i, 0)),
                  pl.BlockSpec((1, tm, LANES), lambda b_, i, e: (b_, i, 0)),
                  pl.BlockSpec((1, d, f), lambda b_, i, e: (e, 0, 0)),
                  pl.BlockSpec((1, d, f), lambda b_, i, e: (e, 0, 0)),
                  pl.BlockSpec((1, f, d), lambda b_, i, e: (e, 0, 0)),
                  pl.BlockSpec((1, tm, d), lambda b_, i, e: (b_, i, 0)),
                  pl.BlockSpec((1, 1, d), lambda b_, i, e: (b_, 0, 0)),
                  pl.BlockSpec((1, d), lambda b_, i, e: (0, 0))],
        out_specs=pl.BlockSpec((1, tm, d), lambda b_, i, e: (b_, i, 0)),
        out_shape=jax.ShapeDtypeStruct((b, l, d), F32),
        scratch_shapes=[pltpu.VMEM((tm, d), F32)],
        compiler_params=_params(("parallel", "parallel", "arbitrary")),
        name="moe_dense_gated",
    )(a2, gates, wg, wu, wd, h1, g2, fng)


def _rope_tables(seq_len, dim):
    t = jnp.arange(seq_len, dtype=jnp.int32)
    row = (t // GRID_W).astype(F32)
    col = (t % GRID_W).astype(F32)
    quarter = dim // 4
    inv = ROPE_BASE ** (-jnp.arange(quarter, dtype=F32) / quarter)
    ar = row[:, None] * inv[None, :]
    ac = col[:, None] * inv[None, :]
    cos = jnp.concatenate([jnp.cos(ar), jnp.cos(ar), jnp.cos(ac), jnp.cos(ac)], axis=-1)
    sin = jnp.concatenate([jnp.sin(ar), jnp.sin(ar), jnp.sin(ac), jnp.sin(ac)], axis=-1)
    reps = LANES // dim
    return jnp.tile(cos, (1, reps)), jnp.tile(sin, (1, reps))


def kernel(x, c, ctx, c_ctx, norm1_g, norm2_g, mod_w, mod_b, even_w_in, pool_w, pool_scale,
           lambda_q1, lambda_k1, lambda_q2, lambda_k2, even_w_out, ffn_w_gate, ffn_w_up,
           ffn_w_down, odd_w_in, q_norm_g, kv_norm_g, w_uq, w_ukv, odd_w_out, router_w,
           moe_w_gate, moe_w_up, moe_w_down, final_norm_g):
    b, l, d = x.shape
    n_ctx = ctx.shape[1]
    tm = min(ROW_TILE, l)
    tq = min(Q_TILE, l)
    assert tm == KV_TILE or l < KV_TILE
    cos, sin = _rope_tables(l, DIFF_HEAD_DIM)
    cos_c, sin_c = cos[:n_ctx], sin[:n_ctx]

    cond = jnp.zeros((8, d), F32).at[:b].set(c).at[b].set(c_ctx)
    mod = _modulation(cond, mod_w, mod_b)
    mod = mod.reshape(mod.shape[0], 8, N_MOD, d)

    def mod_vecs(layer):
        lat = [mod[layer, :b, k][:, None, :] for k in range(N_MOD)]
        cx = [mod[layer, b:b + 1, k][:, None, :] for k in range(N_MOD)]
        return lat, cx

    (sh1, sc1, g1, sh2, sc2, g2), (csh1, csc1, cg1, csh2, csc2, cg2) = mod_vecs(0)
    lam_init = 0.8 - 0.6 * math.exp(-0.3 * 0)
    w_in0 = even_w_in[0].astype(BF16)
    n1g = norm1_g[0][None, :]
    n2g = norm2_g[0][None, :]
    p_lat, q_lat, k_lat, vt_lat = _pre0(x, n1g, sh1, sc1, w_in0, cos, sin, use_rope=True, tm=tm)
    p_ctx, q_ctx, k_ctx, vt_ctx = _pre0(ctx, n1g, csh1, csc1, w_in0, cos_c, sin_c,
                                        use_rope=False, tm=n_ctx)
    lq = jnp.stack([lambda_q1[0], lambda_q2[0]])
    lk = jnp.stack([lambda_k1[0], lambda_k2[0]])
    diff_kern = functools.partial(_diff_attn_kernel, lam_init=lam_init)
    attn_lat = _attention(diff_kern, 2, q_lat, (k_ctx, vt_ctx), (k_lat, vt_lat), 2 * DIFF_HEAD_DIM,
                          extra=(lq, lk), tq=tq, name="diff_attn_lat")
    attn_ctx = _attention(diff_kern, 2, q_ctx, (k_ctx, vt_ctx), None, 2 * DIFF_HEAD_DIM,
                          extra=(lq, lk), tq=n_ctx, name="diff_attn_ctx")
    ng = len(POOL_WINDOWS)
    poolw_bd = jnp.zeros((POOL_WIDTH, POOL_WIDTH), F32)
    for g in range(ng):
        sl = slice(g * POOL_GROUP, (g + 1) * POOL_GROUP)
        poolw_bd = poolw_bd.at[sl, sl].set(pool_w[0, g])
    poolw_bd = poolw_bd.astype(BF16)
    pscale = pool_scale[0][None, :]
    w_out0 = even_w_out[0].astype(BF16)
    wg0, wu0, wd0 = ffn_w_gate[0].astype(BF16), ffn_w_up[0].astype(BF16), ffn_w_down[0].astype(BF16)
    h1_lat, a2_lat = _post0(attn_lat, p_lat, poolw_bd, pscale, w_out0, x, g1, n2g, sh2, sc2, tm=tm)
    h_lat = _ffn(a2_lat, h1_lat, g2, wg0, wu0, wd0, tm=tm)
    h1_ctx, a2_ctx = _post0(attn_ctx, p_ctx, poolw_bd, pscale, w_out0, ctx, cg1, n2g, csh2, csc2,
                            tm=n_ctx)
    h_ctx = _ffn(a2_ctx, h1_ctx, cg2, wg0, wu0, wd0, tm=n_ctx)

    (sh1, sc1, g1, sh2, sc2, g2), (csh1, csc1, _, _, _, _) = mod_vecs(1)
    n1g = norm1_g[1][None, :]
    n2g = norm2_g[1][None, :]
    q_rank, kv_rank = w_uq.shape[1], w_ukv.shape[1]
    n_heads = w_uq.shape[2] // (MLA_NOPE + MLA_ROPE)
    w_in1 = jnp.pad(odd_w_in[0], ((0, 0), (0, LANES - MLA_ROPE))).astype(BF16)
    wuq = w_uq[0].reshape(q_rank, n_heads, MLA_NOPE + MLA_ROPE)
    wuq = jnp.pad(wuq, ((0, 0), (0, 0), (0, MLA_QK_PAD - MLA_NOPE - MLA_ROPE)))
    wuq = wuq.reshape(q_rank, n_heads * MLA_QK_PAD).astype(BF16)
    wukv = w_ukv[0].reshape(kv_rank, n_heads, MLA_NOPE + MLA_V)
    wuk = wukv[:, :, :MLA_NOPE].reshape(kv_rank, n_heads * MLA_NOPE).astype(BF16)
    wuv = wukv[:, :, MLA_NOPE:].reshape(kv_rank, n_heads * MLA_V).astype(BF16)
    qg = q_norm_g[0][None, :]
    kvg = kv_norm_g[0][None, :]
    q1, k1, vt1 = _pre1(h_lat, n1g, sh1, sc1, w_in1, qg, kvg, wuq, wuk, wuv, cos, sin,
                        use_rope=True, want_q=True, tm=tm)
    k1c, vt1c = _pre1(h_ctx, n1g, csh1, csc1, w_in1, qg, kvg, wuq, wuk, wuv, cos_c, sin_c,
                      use_rope=False, want_q=False, tm=n_ctx)
    o1 = _attention(_mla_attn_kernel, 1, q1, (k1c, vt1c), (k1, vt1), MLA_QK_PAD,
                    tq=tq, name="mla_attn")
    rw_pad = jnp.pad(router_w[0], ((0, 0), (0, LANES - N_EXPERTS)))
    h1, a2, gates = _post1(o1, odd_w_out[0].astype(BF16), h_lat, g1, n2g, sh2, sc2, rw_pad, tm=tm)
    return _moe(a2, gates, moe_w_gate[0].astype(BF16), moe_w_up[0].astype(BF16),
                moe_w_down[0].astype(BF16), h1, g2, final_norm_g[None, :], tm=tm)
```

```python
import functools
import math

import jax
import jax.numpy as jnp
from jax import lax
from jax.experimental import pallas as pl
from jax.experimental.pallas import tpu as pltpu

F32 = jnp.float32
BF16 = jnp.bfloat16

EPS = 1e-6
GRID_W = 64
ROPE_BASE = 10000.0
N_MOD = 6
POOL_WINDOWS = (2, 4, 8, 16)
POOL_GROUP = 64
POOL_WIDTH = POOL_GROUP * len(POOL_WINDOWS)
POOL_HALO = max(POOL_WINDOWS) // 2
DIFF_HEAD_DIM = 64
DIFF_V_DIM = 128
MLA_NOPE = 128
MLA_ROPE = 64
MLA_V = 128
MLA_QK_PAD = 256
N_EXPERTS = 8
ROPE_QUARTER = 16
LOG2E = math.log2(math.e)

LANES = 128
ROW_TILE = 512
KV_TILE = 512
Q_TILE = 512
PAIR_UNROLL = 5
MOE_TILE = 512
FFN_CHUNKS = (512, 512, 512, 512, 512, 256)
VMEM_LIMIT = 56 * 1024 * 1024

NT_DIMS = (((1,), (1,)), ((), ()))


def _params(sem):
    return pltpu.CompilerParams(dimension_semantics=sem, vmem_limit_bytes=VMEM_LIMIT)


def _dot(a, b):
    return jnp.dot(a, b, preferred_element_type=F32)


def _dot_nt(a, b):
    return lax.dot_general(a, b, NT_DIMS, preferred_element_type=F32)


def _rms(x):
    return x * lax.rsqrt(jnp.mean(x * x, axis=-1, keepdims=True) + EPS)


def _silu(x):
    return x * (1.0 / (1.0 + jnp.exp(-x)))


def _rope(x, cos, sin):
    lane = lax.broadcasted_iota(jnp.int32, x.shape, 1)
    even = (lane // ROPE_QUARTER) % 2 == 0
    rot = jnp.where(even, -pltpu.roll(x, LANES - ROPE_QUARTER, 1), pltpu.roll(x, ROPE_QUARTER, 1))
    return x * cos + rot * sin


def _mod_kernel(cond_ref, w_ref, b_ref, o_ref):
    s = _silu(cond_ref[...])
    o_ref[0] = jnp.dot(s, w_ref[0], preferred_element_type=F32,
                       precision=lax.Precision.HIGHEST) + b_ref[0]


def _modulation(cond, mod_w, mod_b):
    depth, d, n = mod_w.shape
    tn = n // 4
    return pl.pallas_call(
        _mod_kernel,
        grid=(depth, n // tn),
        in_specs=[pl.BlockSpec((8, d), lambda i, j: (0, 0)),
                  pl.BlockSpec((1, d, tn), lambda i, j: (i, 0, j)),
                  pl.BlockSpec((1, 1, tn), lambda i, j: (i, 0, j))],
        out_specs=pl.BlockSpec((1, 8, tn), lambda i, j: (i, 0, j)),
        out_shape=jax.ShapeDtypeStruct((depth, 8, n), F32),
        compiler_params=_params(("arbitrary", "arbitrary")),
        name="modulation",
    )(cond, mod_w, mod_b.reshape(depth, 1, n))


def _bvec_spec(arr, d):
    if arr.shape[0] == 1:
        return pl.BlockSpec((1, 1, d), lambda b, i: (0, 0, 0))
    return pl.BlockSpec((1, 1, d), lambda b, i: (b, 0, 0))


def _norm_mod(h, g, sh, sc):
    a = _rms(h) * g
    return a * (1.0 + sc) + sh


def _pre0_kernel(h_ref, g_ref, sh_ref, sc_ref, w_ref, cos_ref, sin_ref,
                 p_ref, q_ref, k_ref, vt_ref, *, use_rope, n_heads):
    a = _norm_mod(h_ref[0], g_ref[...], sh_ref[0], sc_ref[0])
    z = _dot(a.astype(BF16), w_ref[...])
    p_ref[0] = z[:, :POOL_WIDTH]
    qk_w = n_heads * 2 * DIFF_HEAD_DIM
    q_scale = DIFF_HEAD_DIM ** -0.5 * LOG2E
    cos = cos_ref[...]
    sin = sin_ref[...]
    for j in range(qk_w // LANES):
        xq = z[:, POOL_WIDTH + j * LANES:POOL_WIDTH + (j + 1) * LANES]
        xk = z[:, POOL_WIDTH + qk_w + j * LANES:POOL_WIDTH + qk_w + (j + 1) * LANES]
        if use_rope:
            xq = _rope(xq, cos, sin)
            xk = _rope(xk, cos, sin)
        q_ref[0, :, j * LANES:(j + 1) * LANES] = (xq * q_scale).astype(BF16)
        k_ref[0, :, j * LANES:(j + 1) * LANES] = xk.astype(BF16)
    v = z[:, POOL_WIDTH + 2 * qk_w:]
    vt_ref[0, 0] = v.T.astype(BF16)


def _pre0(h, g, sh, sc, w_in, cos, sin, *, use_rope, tm):
    b, l, d = h.shape
    n_in = w_in.shape[1]
    v_w = (n_in - POOL_WIDTH) // 3
    n_heads = v_w // DIFF_V_DIM
    kern = functools.partial(_pre0_kernel, use_rope=use_rope, n_heads=n_heads)
    return pl.pallas_call(
        kern,
        grid=(b, l // tm),
        in_specs=[pl.BlockSpec((1, tm, d), lambda b_, i: (b_, i, 0)),
                  pl.BlockSpec((1, d), lambda b_, i: (0, 0)),
                  _bvec_spec(sh, d), _bvec_spec(sc, d),
                  pl.BlockSpec((d, n_in), lambda b_, i: (0, 0)),
                  pl.BlockSpec((tm, LANES), lambda b_, i: (i, 0)),
                  pl.BlockSpec((tm, LANES), lambda b_, i: (i, 0))],
        out_specs=[pl.BlockSpec((1, tm, POOL_WIDTH), lambda b_, i: (b_, i, 0)),
                   pl.BlockSpec((1, tm, v_w), lambda b_, i: (b_, i, 0)),
                   pl.BlockSpec((1, tm, v_w), lambda b_, i: (b_, i, 0)),
                   pl.BlockSpec((1, 1, v_w, tm), lambda b_, i: (b_, i, 0, 0))],
        out_shape=[jax.ShapeDtypeStruct((b, l, POOL_WIDTH), F32),
                   jax.ShapeDtypeStruct((b, l, v_w), BF16),
                   jax.ShapeDtypeStruct((b, l, v_w), BF16),
                   jax.ShapeDtypeStruct((b, l // tm, v_w, tm), BF16)],
        compiler_params=_params(("parallel", "parallel")),
        name="pre0_rope" if use_rope else "pre0_ctx",
    )(h, g, sh, sc, w_in, cos, sin)


def _attn_core(qs, kc_ref, vc_ref, lat_refs, acc_refs, sbuf_refs):
    n = len(qs)
    kc = kc_ref[0]
    vc = vc_ref[0, 0]
    ms, ls = [], []
    for i in range(n):
        s = _dot_nt(kc, qs[i])
        m = jnp.max(s, axis=0, keepdims=True)
        p = jnp.exp2(s - m)
        ms.append(m)
        ls.append(jnp.sum(p, axis=0, keepdims=True))
        acc_refs[i][...] = _dot(vc, p.astype(BF16))
    if lat_refs is None:
        return ls
    kl_ref, vl_ref = lat_refs
    nblk, tk = vl_ref.shape[1], vl_ref.shape[3]
    assert nblk >= 2 and nblk % 2 == 0

    def qk(t, which):
        start = t * tk if isinstance(t, int) else pl.multiple_of(t * tk, tk)
        k = kl_ref[0, pl.ds(start, tk), :]
        cmax = []
        for i in range(n):
            s = _dot_nt(k, qs[i])
            sbuf_refs[2 * i + which][...] = s
            cmax.append(jnp.max(s, axis=0, keepdims=True))
        return tuple(cmax)

    def softmax_pv(t, which, cmax, ms, ls):
        vt = vl_ref[0, t]
        new_m, new_l = [], []
        for i in range(n):
            m_new = jnp.maximum(ms[i], cmax[i])
            alpha = jnp.exp2(ms[i] - m_new)
            p = jnp.exp2(sbuf_refs[2 * i + which][...] - m_new)
            new_l.append(alpha * ls[i] + jnp.sum(p, axis=0, keepdims=True))
            acc_refs[i][...] = alpha * acc_refs[i][...] + _dot(vt, p.astype(BF16))
            new_m.append(m_new)
        return tuple(new_m), tuple(new_l)

    def pair(jj, carry):
        cmax_a, ms, ls = carry
        t = 2 * jj
        cmax_b = qk(t + 1, 1)
        ms, ls = softmax_pv(t, 0, cmax_a, ms, ls)
        cmax_a = qk(t + 2, 0)
        ms, ls = softmax_pv(t + 1, 1, cmax_b, ms, ls)
        return cmax_a, ms, ls

    cmax_a = qk(0, 0)
    cmax_a, ms, ls = lax.fori_loop(0, nblk // 2 - 1, pair, (cmax_a, tuple(ms), tuple(ls)),
                                   unroll=PAIR_UNROLL)
    cmax_b = qk(nblk - 1, 1)
    ms, ls = softmax_pv(nblk - 2, 0, cmax_a, ms, ls)
    ms, ls = softmax_pv(nblk - 1, 1, cmax_b, ms, ls)
    return ls


def _diff_attn_kernel(lq_ref, lk_ref, q_ref, kc_ref, vc_ref, *rest, has_lat, lam_init):
    lat_refs = rest[:2] if has_lat else None
    rest = rest[2:] if has_lat else rest
    o_ref, acc1_ref, acc2_ref = rest[:3]
    q = q_ref[0]
    lane = lax.broadcasted_iota(jnp.int32, q.shape, 1)
    zero = jnp.zeros_like(q)
    q1 = jnp.where(lane < DIFF_HEAD_DIM, q, zero)
    q2 = jnp.where(lane >= DIFF_HEAD_DIM, q, zero)
    l1, l2 = _attn_core([q1, q2], kc_ref, vc_ref, lat_refs, (acc1_ref, acc2_ref), rest[3:])
    e = jnp.exp(jnp.sum(lq_ref[...] * lk_ref[...], axis=-1, keepdims=True))
    lam = e[0:1] - e[1:2] + lam_init
    o = acc1_ref[...] * (1.0 / l1) - lam * (acc2_ref[...] * (1.0 / l2))
    ms = jnp.mean(o * o, axis=0, keepdims=True)
    y = o * (lax.rsqrt(ms + EPS) * (1.0 - lam_init))
    o_ref[0] = y.T.astype(BF16)


def _mla_attn_kernel(q_ref, kc_ref, vc_ref, *rest, has_lat):
    lat_refs = rest[:2] if has_lat else None
    rest = rest[2:] if has_lat else rest
    o_ref, acc_ref = rest[:2]
    (l,) = _attn_core([q_ref[0]], kc_ref, vc_ref, lat_refs, (acc_ref,), rest[2:])
    o_ref[0] = (acc_ref[...] * (1.0 / l)).T.astype(BF16)


def _attention(kern, n_maps, q, ctx_kv, lat_kv, qk_w, extra=(), *, tq, name):
    b, lq, hw = q.shape
    n_heads = hw // qk_w
    in_specs = [pl.BlockSpec(x.shape, lambda b_, h, i: (0, 0)) for x in extra]
    in_specs.append(pl.BlockSpec((1, tq, qk_w), lambda b_, h, i: (b_, i, h)))
    args = list(extra) + [q]
    scratch = [pltpu.VMEM((DIFF_V_DIM, tq), F32) for _ in range(n_maps)]
    for kv in (ctx_kv, lat_kv):
        if kv is None:
            continue
        k, vt = kv
        nblk, tk = vt.shape[1], vt.shape[3]
        in_specs.append(pl.BlockSpec((1, k.shape[1], qk_w), lambda b_, h, i: (b_, 0, h)))
        in_specs.append(pl.BlockSpec((1, nblk, DIFF_V_DIM, tk), lambda b_, h, i: (b_, 0, h, 0)))
        args += [k, vt]
    if lat_kv is not None:
        scratch += [pltpu.VMEM((lat_kv[1].shape[3], tq), F32) for _ in range(2 * n_maps)]
    return pl.pallas_call(
        functools.partial(kern, has_lat=lat_kv is not None),
        grid=(b, n_heads, lq // tq),
        in_specs=in_specs,
        out_specs=pl.BlockSpec((1, tq, DIFF_V_DIM), lambda b_, h, i: (b_, i, h)),
        out_shape=jax.ShapeDtypeStruct((b, lq, n_heads * DIFF_V_DIM), BF16),
        scratch_shapes=scratch,
        compiler_params=_params(("parallel", "parallel", "arbitrary")),
        name=name,
    )(*args)


def _post0_kernel(attn_ref, p_ref, pprev_ref, pnext_ref, poolw_ref, pscale_ref, wout_ref,
                  h_ref, g1_ref, n2g_ref, sh2_ref, sc2_ref, h1_ref, a2_ref, *, seq_len):
    i = pl.program_id(1)
    u = p_ref[0]
    tm = u.shape[0]
    ext = jnp.concatenate([pprev_ref[0], u, pnext_ref[0]], axis=0)
    row = i * tm - POOL_HALO + lax.broadcasted_iota(jnp.int32, (tm + 2 * POOL_HALO, 1), 0)
    ext = jnp.where((row >= 0) & (row < seq_len), ext, 0.0)

    def shifted(j):
        return ext[POOL_HALO + j:POOL_HALO + j + tm]

    t = i * tm + lax.broadcasted_iota(jnp.int32, (tm, 1), 0)
    lane = lax.broadcasted_iota(jnp.int32, (tm, POOL_WIDTH), 1)
    win_sum = shifted(-1) + shifted(0)
    pooled = jnp.zeros_like(u)
    for g, w in enumerate(POOL_WINDOWS):
        half = w // 2
        if g > 0:
            prev_half = POOL_WINDOWS[g - 1] // 2
            for j in range(prev_half, half):
                win_sum = win_sum + shifted(-j - 1) + shifted(j)
        cnt = (jnp.minimum(t + half, seq_len) - jnp.maximum(t - half, 0)).astype(F32)
        in_group = (lane >= g * POOL_GROUP) & (lane < (g + 1) * POOL_GROUP)
        pooled = jnp.where(in_group, win_sum / cnt, pooled)
    pooled = pooled - u
    y_pool = _dot(pooled.astype(BF16), poolw_ref[...]) * pscale_ref[...]
    y = _dot(y_pool.astype(BF16), wout_ref[:POOL_WIDTH, :]) + _dot(attn_ref[0], wout_ref[POOL_WIDTH:, :])
    h1 = h_ref[0] + g1_ref[0] * y
    h1_ref[0] = h1
    a2_ref[0] = _norm_mod(h1, n2g_ref[...], sh2_ref[0], sc2_ref[0]).astype(BF16)


def _post0(attn, p, poolw_bd, pscale, w_out, h, g1, n2g, sh2, sc2, *, tm):
    b, l, d = h.shape
    hb = tm // POOL_HALO
    nhb = l // POOL_HALO
    return pl.pallas_call(
        functools.partial(_post0_kernel, seq_len=l),
        grid=(b, l // tm),
        in_specs=[pl.BlockSpec((1, tm, attn.shape[2]), lambda b_, i: (b_, i, 0)),
                  pl.BlockSpec((1, tm, POOL_WIDTH), lambda b_, i: (b_, i, 0)),
                  pl.BlockSpec((1, POOL_HALO, POOL_WIDTH),
                               lambda b_, i: (b_, jnp.maximum(i * hb - 1, 0), 0)),
                  pl.BlockSpec((1, POOL_HALO, POOL_WIDTH),
                               lambda b_, i: (b_, jnp.minimum((i + 1) * hb, nhb - 1), 0)),
                  pl.BlockSpec(poolw_bd.shape, lambda b_, i: (0, 0)),
                  pl.BlockSpec((1, POOL_WIDTH), lambda b_, i: (0, 0)),
                  pl.BlockSpec(w_out.shape, lambda b_, i: (0, 0)),
                  pl.BlockSpec((1, tm, d), lambda b_, i: (b_, i, 0)),
                  _bvec_spec(g1, d),
                  pl.BlockSpec((1, d), lambda b_, i: (0, 0)),
                  _bvec_spec(sh2, d), _bvec_spec(sc2, d)],
        out_specs=[pl.BlockSpec((1, tm, d), lambda b_, i: (b_, i, 0)),
                   pl.BlockSpec((1, tm, d), lambda b_, i: (b_, i, 0))],
        out_shape=[jax.ShapeDtypeStruct((b, l, d), F32),
                   jax.ShapeDtypeStruct((b, l, d), BF16)],
        compiler_params=_params(("parallel", "parallel")),
        name="post0",
    )(attn, p, p, p, poolw_bd, pscale, w_out, h, g1, n2g, sh2, sc2)


def _swiglu_tile(x, wg_ref, wu_ref, wd_ref, lead=()):
    acc = None
    c0 = 0
    for cw in FFN_CHUNKS:
        hg = _dot(x, wg_ref[lead + (slice(None), slice(c0, c0 + cw))])
        hu = _dot(x, wu_ref[lead + (slice(None), slice(c0, c0 + cw))])
        act = (_silu(hg) * hu).astype(BF16)
        part = _dot(act, wd_ref[lead + (slice(c0, c0 + cw), slice(None))])
        acc = part if acc is None else acc + part
        c0 += cw
    return acc


def _ffn_kernel(a2_ref, h1_ref, g2_ref, wg_ref, wu_ref, wd_ref, o_ref):
    y = _swiglu_tile(a2_ref[0], wg_ref, wu_ref, wd_ref)
    o_ref[0] = h1_ref[0] + g2_ref[0] * y


def _ffn(a2, h1, g2, wg, wu, wd, *, tm):
    b, l, d = h1.shape
    f = wg.shape[1]
    assert sum(FFN_CHUNKS) == f
    return pl.pallas_call(
        _ffn_kernel,
        grid=(b, l // tm),
        in_specs=[pl.BlockSpec((1, tm, d), lambda b_, i: (b_, i, 0)),
                  pl.BlockSpec((1, tm, d), lambda b_, i: (b_, i, 0)),
                  _bvec_spec(g2, d),
                  pl.BlockSpec((d, f), lambda b_, i: (0, 0)),
                  pl.BlockSpec((d, f), lambda b_, i: (0, 0)),
                  pl.BlockSpec((f, d), lambda b_, i: (0, 0))],
        out_specs=pl.BlockSpec((1, tm, d), lambda b_, i: (b_, i, 0)),
        out_shape=jax.ShapeDtypeStruct((b, l, d), F32),
        compiler_params=_params(("parallel", "parallel")),
        name="ffn_dense",
    )(a2, h1, g2, wg, wu, wd)


def _pre1_kernel(h_ref, g_ref, sh_ref, sc_ref, win_ref, qg_ref, kvg_ref, wuq_ref, wuk_ref, wuv_ref,
                 cos_ref, sin_ref, *out_refs, use_rope, want_q, q_rank, kv_rank, n_heads):
    a = _norm_mod(h_ref[0], g_ref[...], sh_ref[0], sc_ref[0])
    z = _dot(a.astype(BF16), win_ref[...])
    cos = cos_ref[...]
    sin = sin_ref[...]
    if want_q:
        q_ref, k_ref, vt_ref = out_refs
        cq = _rms(z[:, :q_rank]) * qg_ref[...]
        qf = _dot(cq.astype(BF16), wuq_ref[...])
        q_scale = (MLA_NOPE + MLA_ROPE) ** -0.5 * LOG2E
        for h in range(n_heads):
            c0 = h * MLA_QK_PAD
            q_ref[0, :, c0:c0 + MLA_NOPE] = (qf[:, c0:c0 + MLA_NOPE] * q_scale).astype(BF16)
            qr = _rope(qf[:, c0 + MLA_NOPE:c0 + MLA_QK_PAD], cos, sin)
            q_ref[0, :, c0 + MLA_NOPE:c0 + MLA_QK_PAD] = (qr * q_scale).astype(BF16)
    else:
        k_ref, vt_ref = out_refs
    ckv = (_rms(z[:, q_rank:q_rank + kv_rank]) * kvg_ref[...]).astype(BF16)
    kn = _dot(ckv, wuk_ref[...])
    vv = _dot(ckv, wuv_ref[...])
    kr = z[:, q_rank + kv_rank:]
    if use_rope:
        kr = _rope(kr, cos, sin)
    kr = kr.astype(BF16)
    for h in range(n_heads):
        c0 = h * MLA_QK_PAD
        k_ref[0, :, c0:c0 + MLA_NOPE] = kn[:, h * MLA_NOPE:(h + 1) * MLA_NOPE].astype(BF16)
        k_ref[0, :, c0 + MLA_NOPE:c0 + MLA_QK_PAD] = kr
    vt_ref[0, 0] = vv.T.astype(BF16)


def _pre1(h, g, sh, sc, w_in, qg, kvg, wuq, wuk, wuv, cos, sin, *, use_rope, want_q, tm):
    b, l, d = h.shape
    n_heads = wuk.shape[1] // MLA_NOPE
    q_rank, kv_rank = wuq.shape[0], wuk.shape[0]
    kern = functools.partial(_pre1_kernel, use_rope=use_rope, want_q=want_q,
                             q_rank=q_rank, kv_rank=kv_rank, n_heads=n_heads)
    full = lambda x: pl.BlockSpec(x.shape, lambda b_, i: (0,) * x.ndim)
    qk_w = n_heads * MLA_QK_PAD
    v_w = n_heads * MLA_V
    out_specs = [pl.BlockSpec((1, tm, qk_w), lambda b_, i: (b_, i, 0)),
                 pl.BlockSpec((1, 1, v_w, tm), lambda b_, i: (b_, i, 0, 0))]
    out_shape = [jax.ShapeDtypeStruct((b, l, qk_w), BF16),
                 jax.ShapeDtypeStruct((b, l // tm, v_w, tm), BF16)]
    if want_q:
        out_specs = [pl.BlockSpec((1, tm, qk_w), lambda b_, i: (b_, i, 0))] + out_specs
        out_shape = [jax.ShapeDtypeStruct((b, l, qk_w), BF16)] + out_shape
    return pl.pallas_call(
        kern,
        grid=(b, l // tm),
        in_specs=[pl.BlockSpec((1, tm, d), lambda b_, i: (b_, i, 0)),
                  pl.BlockSpec((1, d), lambda b_, i: (0, 0)),
                  _bvec_spec(sh, d), _bvec_spec(sc, d),
                  full(w_in), full(qg), full(kvg), full(wuq), full(wuk), full(wuv),
                  pl.BlockSpec((tm, LANES), lambda b_, i: (i, 0)),
                  pl.BlockSpec((tm, LANES), lambda b_, i: (i, 0))],
        out_specs=out_specs,
        out_shape=out_shape,
        compiler_params=_params(("parallel", "parallel")),
        name="pre1_lat" if want_q else "pre1_ctx",
    )(h, g, sh, sc, w_in, qg, kvg, wuq, wuk, wuv, cos, sin)


def _post1_kernel(o_ref, wout_ref, h_ref, g1_ref, n2g_ref, sh2_ref, sc2_ref, rw_ref,
                  h1_ref, a2_ref, gates_ref, sel_ref):
    y = _dot(o_ref[0], wout_ref[...])
    h1 = h_ref[0] + g1_ref[0] * y
    h1_ref[0] = h1
    a2 = _norm_mod(h1, n2g_ref[...], sh2_ref[0], sc2_ref[0])
    a2_ref[0] = a2.astype(BF16)
    logits = jnp.dot(a2, rw_ref[...], preferred_element_type=F32,
                     precision=lax.Precision.HIGHEST)
    lane = lax.broadcasted_iota(jnp.int32, logits.shape, 1)
    lg = jnp.where(lane < N_EXPERTS, logits, -jnp.inf)
    m1 = jnp.max(lg, axis=1, keepdims=True)
    i1 = jnp.min(jnp.where(lg == m1, lane, LANES), axis=1, keepdims=True)
    lg2 = jnp.where(lane == i1, -jnp.inf, lg)
    m2 = jnp.max(lg2, axis=1, keepdims=True)
    i2 = jnp.min(jnp.where(lg2 == m2, lane, LANES), axis=1, keepdims=True)
    p2 = jnp.exp(m2 - m1)
    w1 = 1.0 / (1.0 + p2)
    gates_ref[0] = jnp.where(lane == i1, w1, 0.0) + jnp.where(lane == i2, p2 * w1, 0.0)
    sel_ref[0] = jnp.where((lane == i1) | (lane == i2), 1.0, 0.0).astype(BF16)


def _post1(o, w_out, h, g1, n2g, sh2, sc2, rw_pad, *, tm):
    b, l, d = h.shape
    return pl.pallas_call(
        _post1_kernel,
        grid=(b, l // tm),
        in_specs=[pl.BlockSpec((1, tm, o.shape[2]), lambda b_, i: (b_, i, 0)),
                  pl.BlockSpec(w_out.shape, lambda b_, i: (0, 0)),
                  pl.BlockSpec((1, tm, d), lambda b_, i: (b_, i, 0)),
                  _bvec_spec(g1, d),
                  pl.BlockSpec((1, d), lambda b_, i: (0, 0)),
                  _bvec_spec(sh2, d), _bvec_spec(sc2, d),
                  pl.BlockSpec(rw_pad.shape, lambda b_, i: (0, 0))],
        out_specs=[pl.BlockSpec((1, tm, d), lambda b_, i: (b_, i, 0)),
                   pl.BlockSpec((1, tm, d), lambda b_, i: (b_, i, 0)),
                   pl.BlockSpec((1, tm, LANES), lambda b_, i: (b_, i, 0)),
                   pl.BlockSpec((1, tm, LANES), lambda b_, i: (b_, i, 0))],
        out_shape=[jax.ShapeDtypeStruct((b, l, d), F32),
                   jax.ShapeDtypeStruct((b, l, d), BF16),
                   jax.ShapeDtypeStruct((b, l, LANES), F32),
                   jax.ShapeDtypeStruct((b, l, LANES), BF16)],
        compiler_params=_params(("parallel", "parallel")),
        name="post1_router",
    )(o, w_out, h, g1, n2g, sh2, sc2, rw_pad)


def _route_kernel(sel_ref, gates_ref, r_ref, tile_ref, cnt_ref, off_ref, *, tile_rows):
    phase = pl.program_id(0)
    i = pl.program_id(1)
    sel = sel_ref[...]
    tr = sel.shape[0]
    col_count = jnp.sum(sel.astype(F32), axis=0, keepdims=True)
    lane = lax.broadcasted_iota(jnp.int32, (1, LANES), 1)

    @pl.when((phase == 0) & (i == 0))
    def _():
        cnt_ref[...] = jnp.zeros_like(cnt_ref)

    @pl.when(phase == 0)
    def _():
        cnt_ref[...] += col_count

    @pl.when((phase == 0) & (i == pl.num_programs(1) - 1))
    def _():
        padded = jnp.ceil(cnt_ref[...] / tile_rows) * tile_rows
        incl = padded
        for sh in (1, 2, 4):
            incl = incl + jnp.where(lane >= sh, pltpu.roll(incl, sh, 1), 0.0)
        off_ref[...] = incl - padded
        tile_start = (lax.broadcasted_iota(jnp.int32, tile_ref.shape, 1) * tile_rows).astype(F32)
        tile_expert = jnp.zeros(tile_ref.shape, jnp.int32)
        for e in range(N_EXPERTS):
            end_e = jnp.sum(jnp.where(lane == e, incl, 0.0), axis=1, keepdims=True)
            tile_expert = tile_expert + (tile_start >= end_e).astype(jnp.int32)
        tile_ref[...] = tile_expert
        cnt_ref[...] = jnp.zeros_like(cnt_ref)

    @pl.when(phase == 1)
    def _():
        rr = lax.broadcasted_iota(jnp.int32, (tr, tr), 0)
        cc = lax.broadcasted_iota(jnp.int32, (tr, tr), 1)
        earlier = jnp.where(cc < rr, 1.0, 0.0).astype(BF16)
        rank = _dot(earlier, sel) + cnt_ref[...]
        cnt_ref[...] += col_count
        pos = off_ref[...] + rank
        chosen = sel > 0
        lo = jnp.min(jnp.where(chosen, pos, 3e38), axis=1, keepdims=True)
        hi = jnp.max(jnp.where(chosen, pos, -1.0), axis=1, keepdims=True)
        g = gates_ref[...]
        w_lo = jnp.sum(jnp.where(chosen & (pos == lo), g, 0.0), axis=1, keepdims=True)
        w_hi = jnp.sum(jnp.where(chosen & (pos == hi), g, 0.0), axis=1, keepdims=True)
        lane_t = lax.broadcasted_iota(jnp.int32, (tr, LANES), 1)
        r_ref[...] = jnp.where(lane_t == 0, lo, jnp.where(lane_t == 1, hi,
                               jnp.where(lane_t == 2, w_lo, jnp.where(lane_t == 3, w_hi, 0.0))))


def _route(sel, gates, *, tr, tile_rows, n_tiles):
    n = sel.shape[0]
    n_tiles_pad = -(-n_tiles // LANES) * LANES
    return pl.pallas_call(
        functools.partial(_route_kernel, tile_rows=tile_rows),
        grid=(2, n // tr),
        in_specs=[pl.BlockSpec((tr, LANES), lambda ph, i: (i, 0)),
                  pl.BlockSpec((tr, LANES), lambda ph, i: (i, 0))],
        out_specs=[pl.BlockSpec((tr, LANES), lambda ph, i: (i * ph, 0)),
                   pl.BlockSpec((1, n_tiles_pad), lambda ph, i: (0, 0))],
        out_shape=[jax.ShapeDtypeStruct((n, LANES), F32),
                   jax.ShapeDtypeStruct((1, n_tiles_pad), jnp.int32)],
        scratch_shapes=[pltpu.VMEM((1, LANES), F32), pltpu.VMEM((1, LANES), F32)],
        compiler_params=_params(("arbitrary", "arbitrary")),
        name="moe_route",
    )(sel, gates)


def _dispatch_kernel(dest_ref, x_hbm, xs_in_hbm, xs_hbm, sem, *, tm):
    del xs_in_hbm
    base = pl.program_id(0) * tm

    def row_copy(src_row, dst_row):
        return pltpu.make_async_copy(x_hbm.at[src_row], xs_hbm.at[dst_row], sem)

    def issue(r, carry):
        row_copy(base + r, dest_ref[0, 0, r]).start()
        row_copy(base + r, dest_ref[0, 0, tm + r]).start()
        return carry

    lax.fori_loop(0, tm, issue, 0)
    pltpu.make_async_copy(x_hbm.at[pl.ds(0, 2 * tm)], xs_hbm.at[pl.ds(0, 2 * tm)], sem).wait()


def _dispatch(dest, x3, n_rows, *, tm):
    n = x3.shape[0]
    xs0 = jnp.zeros((n_rows,) + x3.shape[1:], x3.dtype)
    return pl.pallas_call(
        functools.partial(_dispatch_kernel, tm=tm),
        grid=(n // tm,),
        in_specs=[pl.BlockSpec((1, 1, 2 * tm), lambda i: (i, 0, 0), memory_space=pltpu.SMEM),
                  pl.BlockSpec(memory_space=pl.ANY),
                  pl.BlockSpec(memory_space=pl.ANY)],
        out_specs=pl.BlockSpec(memory_space=pl.ANY),
        out_shape=jax.ShapeDtypeStruct(xs0.shape, xs0.dtype),
        scratch_shapes=[pltpu.SemaphoreType.DMA(())],
        input_output_aliases={2: 0},
        compiler_params=pltpu.CompilerParams(dimension_semantics=("arbitrary",),
                                             has_side_effects=True),
        name="moe_dispatch",
    )(dest, x3, xs0)


def _experts_kernel(te_ref, x_ref, wg_ref, wu_ref, wd_ref, o_ref):
    valid = te_ref[pl.program_id(0)] < N_EXPERTS

    @pl.when(valid)
    def _():
        o_ref[...] = _swiglu_tile(x_ref[...], wg_ref, wu_ref, wd_ref, lead=(0,))

    @pl.when(jnp.logical_not(valid))
    def _():
        o_ref[...] = jnp.zeros_like(o_ref)


def _experts(tile_expert, xs, wg, wu, wd, *, tile_rows):
    n_rows, d = xs.shape
    ne, _, f = wg.shape
    w_idx = lambda j, te: (jnp.minimum(te[j], ne - 1), 0, 0)
    return pl.pallas_call(
        _experts_kernel,
        grid_spec=pltpu.PrefetchScalarGridSpec(
            num_scalar_prefetch=1,
            grid=(n_rows // tile_rows,),
            in_specs=[pl.BlockSpec((tile_rows, d), lambda j, te: (j, 0)),
                      pl.BlockSpec((1, d, f), w_idx),
                      pl.BlockSpec((1, d, f), w_idx),
                      pl.BlockSpec((1, f, d), w_idx)],
            out_specs=pl.BlockSpec((tile_rows, d), lambda j, te: (j, 0))),
        out_shape=jax.ShapeDtypeStruct((n_rows, d), F32),
        compiler_params=_params(("arbitrary",)),
        name="moe_experts",
    )(tile_expert, xs, wg, wu, wd)


def _combine_kernel(dest_ref, w_ref, ys_hbm, h1_ref, g2_ref, fng_ref, o_ref, ylo_ref, yhi_ref, sem,
                    *, tm):
    def issue(r, carry):
        pltpu.make_async_copy(ys_hbm.at[dest_ref[0, 0, r]], ylo_ref.at[r], sem).start()
        pltpu.make_async_copy(ys_hbm.at[dest_ref[0, 0, tm + r]], yhi_ref.at[r], sem).start()
        return carry

    lax.fori_loop(0, tm, issue, 0)
    pltpu.make_async_copy(ys_hbm.at[pl.ds(0, tm)], ylo_ref, sem).wait()
    pltpu.make_async_copy(ys_hbm.at[pl.ds(0, tm)], yhi_ref, sem).wait()

    def weigh(r, carry):
        ylo_ref[r] = w_ref[0, 0, r] * ylo_ref[r] + w_ref[0, 0, tm + r] * yhi_ref[r]
        return carry

    lax.fori_loop(0, tm, weigh, 0)
    out = h1_ref[...] + g2_ref[...] * ylo_ref[...]
    ms = jnp.mean(jnp.mean(out * out, axis=2, keepdims=True), axis=1, keepdims=True)
    o_ref[...] = out * lax.rsqrt(ms + EPS) * fng_ref[...]


def _combine(dest, w, ys3, h1_3, g2_3, fng_3, *, tm, tiles_per_sample):
    n, sub, _ = h1_3.shape
    slab = lambda i: (i, 0, 0)
    return pl.pallas_call(
        functools.partial(_combine_kernel, tm=tm),
        grid=(n // tm,),
        in_specs=[pl.BlockSpec((1, 1, 2 * tm), slab, memory_space=pltpu.SMEM),
                  pl.BlockSpec((1, 1, 2 * tm), slab, memory_space=pltpu.SMEM),
                  pl.BlockSpec(memory_space=pl.ANY),
                  pl.BlockSpec((tm, sub, LANES), slab),
                  pl.BlockSpec((1, sub, LANES), lambda i: (i // tiles_per_sample, 0, 0)),
                  pl.BlockSpec((1, sub, LANES), lambda i: (0, 0, 0))],
        out_specs=pl.BlockSpec((tm, sub, LANES), slab),
        out_shape=jax.ShapeDtypeStruct(h1_3.shape, F32),
        scratch_shapes=[pltpu.VMEM((tm, sub, LANES), F32), pltpu.VMEM((tm, sub, LANES), F32),
                        pltpu.SemaphoreType.DMA(())],
        compiler_params=_params(("arbitrary",)),
        name="moe_combine",
    )(dest, w, ys3, h1_3, g2_3, fng_3)


def _moe(a2, gates, sel, wg, wu, wd, h1, g2, fng, *, tm):
    b, l, d = h1.shape
    n = b * l
    sub = d // LANES
    n_rows = 2 * n + N_EXPERTS * MOE_TILE
    n_tiles = n_rows // MOE_TILE
    routed, tile_expert = _route(sel.reshape(n, LANES), gates.reshape(n, LANES),
                                 tr=tm, tile_rows=MOE_TILE, n_tiles=n_tiles)

    def per_tile(cols, dtype):
        return cols.astype(dtype).reshape(n // tm, tm, 2).transpose(0, 2, 1).reshape(n // tm, 1, 2 * tm)

    dest = per_tile(routed[:, 0:2], jnp.int32)
    w = per_tile(routed[:, 2:4], F32)
    xs = _dispatch(dest, a2.reshape(n, sub, LANES), n_rows, tm=tm)
    ys = _experts(tile_expert[0, :n_tiles], xs.reshape(n_rows, d), wg, wu, wd, tile_rows=MOE_TILE)
    out = _combine(dest, w, ys.reshape(n_rows, sub, LANES), h1.reshape(n, sub, LANES),
                   g2.reshape(b, sub, LANES), fng.reshape(1, sub, LANES),
                   tm=tm, tiles_per_sample=l // tm)
    return out.reshape(b, l, d)


def _rope_tables(seq_len, dim):
    t = jnp.arange(seq_len, dtype=jnp.int32)
    row = (t // GRID_W).astype(F32)
    col = (t % GRID_W).astype(F32)
    quarter = dim // 4
    inv = ROPE_BASE ** (-jnp.arange(quarter, dtype=F32) / quarter)
    ar = row[:, None] * inv[None, :]
    ac = col[:, None] * inv[None, :]
    cos = jnp.concatenate([jnp.cos(ar), jnp.cos(ar), jnp.cos(ac), jnp.cos(ac)], axis=-1)
    sin = jnp.concatenate([jnp.sin(ar), jnp.sin(ar), jnp.sin(ac), jnp.sin(ac)], axis=-1)
    reps = LANES // dim
    return jnp.tile(cos, (1, reps)), jnp.tile(sin, (1, reps))


def kernel(x, c, ctx, c_ctx, norm1_g, norm2_g, mod_w, mod_b, even_w_in, pool_w, pool_scale,
           lambda_q1, lambda_k1, lambda_q2, lambda_k2, even_w_out, ffn_w_gate, ffn_w_up,
           ffn_w_down, odd_w_in, q_norm_g, kv_norm_g, w_uq, w_ukv, odd_w_out, router_w,
           moe_w_gate, moe_w_up, moe_w_down, final_norm_g):
    b, l, d = x.shape
    n_ctx = ctx.shape[1]
    tm = min(ROW_TILE, l)
    tq = min(Q_TILE, l)
    assert tm == KV_TILE or l < KV_TILE
    cos, sin = _rope_tables(l, DIFF_HEAD_DIM)
    cos_c, sin_c = cos[:n_ctx], sin[:n_ctx]

    cond = jnp.zeros((8, d), F32).at[:b].set(c).at[b].set(c_ctx)
    mod = _modulation(cond, mod_w, mod_b)
    mod = mod.reshape(mod.shape[0], 8, N_MOD, d)

    def mod_vecs(layer):
        lat = [mod[layer, :b, k][:, None, :] for k in range(N_MOD)]
        cx = [mod[layer, b:b + 1, k][:, None, :] for k in range(N_MOD)]
        return lat, cx

    (sh1, sc1, g1, sh2, sc2, g2), (csh1, csc1, cg1, csh2, csc2, cg2) = mod_vecs(0)
    lam_init = 0.8 - 0.6 * math.exp(-0.3 * 0)
    w_in0 = even_w_in[0].astype(BF16)
    n1g = norm1_g[0][None, :]
    n2g = norm2_g[0][None, :]
    p_lat, q_lat, k_lat, vt_lat = _pre0(x, n1g, sh1, sc1, w_in0, cos, sin, use_rope=True, tm=tm)
    p_ctx, q_ctx, k_ctx, vt_ctx = _pre0(ctx, n1g, csh1, csc1, w_in0, cos_c, sin_c,
                                        use_rope=False, tm=n_ctx)
    lq = jnp.stack([lambda_q1[0], lambda_q2[0]])
    lk = jnp.stack([lambda_k1[0], lambda_k2[0]])
    diff_kern = functools.partial(_diff_attn_kernel, lam_init=lam_init)
    attn_lat = _attention(diff_kern, 2, q_lat, (k_ctx, vt_ctx), (k_lat, vt_lat), 2 * DIFF_HEAD_DIM,
                          extra=(lq, lk), tq=tq, name="diff_attn_lat")
    attn_ctx = _attention(diff_kern, 2, q_ctx, (k_ctx, vt_ctx), None, 2 * DIFF_HEAD_DIM,
                          extra=(lq, lk), tq=n_ctx, name="diff_attn_ctx")
    ng = len(POOL_WINDOWS)
    poolw_bd = jnp.zeros((POOL_WIDTH, POOL_WIDTH), F32)
    for g in range(ng):
        sl = slice(g * POOL_GROUP, (g + 1) * POOL_GROUP)
        poolw_bd = poolw_bd.at[sl, sl].set(pool_w[0, g])
    poolw_bd = poolw_bd.astype(BF16)
    pscale = pool_scale[0][None, :]
    w_out0 = even_w_out[0].astype(BF16)
    wg0, wu0, wd0 = ffn_w_gate[0].astype(BF16), ffn_w_up[0].astype(BF16), ffn_w_down[0].astype(BF16)
    h1_lat, a2_lat = _post0(attn_lat, p_lat, poolw_bd, pscale, w_out0, x, g1, n2g, sh2, sc2, tm=tm)
    h_lat = _ffn(a2_lat, h1_lat, g2, wg0, wu0, wd0, tm=tm)
    h1_ctx, a2_ctx = _post0(attn_ctx, p_ctx, poolw_bd, pscale, w_out0, ctx, cg1, n2g, csh2, csc2,
                            tm=n_ctx)
    h_ctx = _ffn(a2_ctx, h1_ctx, cg2, wg0, wu0, wd0, tm=n_ctx)

    (sh1, sc1, g1, sh2, sc2, g2), (csh1, csc1, _, _, _, _) = mod_vecs(1)
    n1g = norm1_g[1][None, :]
    n2g = norm2_g[1][None, :]
    q_rank, kv_rank = w_uq.shape[1], w_ukv.shape[1]
    n_heads = w_uq.shape[2] // (MLA_NOPE + MLA_ROPE)
    w_in1 = jnp.pad(odd_w_in[0], ((0, 0), (0, LANES - MLA_ROPE))).astype(BF16)
    wuq = w_uq[0].reshape(q_rank, n_heads, MLA_NOPE + MLA_ROPE)
    wuq = jnp.pad(wuq, ((0, 0), (0, 0), (0, MLA_QK_PAD - MLA_NOPE - MLA_ROPE)))
    wuq = wuq.reshape(q_rank, n_heads * MLA_QK_PAD).astype(BF16)
    wukv = w_ukv[0].reshape(kv_rank, n_heads, MLA_NOPE + MLA_V)
    wuk = wukv[:, :, :MLA_NOPE].reshape(kv_rank, n_heads * MLA_NOPE).astype(BF16)
    wuv = wukv[:, :, MLA_NOPE:].reshape(kv_rank, n_heads * MLA_V).astype(BF16)
    qg = q_norm_g[0][None, :]
    kvg = kv_norm_g[0][None, :]
    q1, k1, vt1 = _pre1(h_lat, n1g, sh1, sc1, w_in1, qg, kvg, wuq, wuk, wuv, cos, sin,
                        use_rope=True, want_q=True, tm=tm)
    k1c, vt1c = _pre1(h_ctx, n1g, csh1, csc1, w_in1, qg, kvg, wuq, wuk, wuv, cos_c, sin_c,
                      use_rope=False, want_q=False, tm=n_ctx)
    o1 = _attention(_mla_attn_kernel, 1, q1, (k1c, vt1c), (k1, vt1), MLA_QK_PAD,
                    tq=tq, name="mla_attn")
    rw_pad = jnp.pad(router_w[0], ((0, 0), (0, LANES - N_EXPERTS)))
    h1, a2, gates, sel = _post1(o1, odd_w_out[0].astype(BF16), h_lat, g1, n2g, sh2, sc2, rw_pad,
                                tm=tm)
    return _moe(a2, gates, sel, moe_w_gate[0].astype(BF16), moe_w_up[0].astype(BF16),
                moe_w_down[0].astype(BF16), h1, g2, final_norm_g[None, :], tm=tm)
```

```python
import functools
import math

import jax
import jax.numpy as jnp
from jax import lax
from jax.experimental import pallas as pl
from jax.experimental.pallas import tpu as pltpu

F32 = jnp.float32
BF16 = jnp.bfloat16

EPS = 1e-6
GRID_W = 64
ROPE_BASE = 10000.0
N_MOD = 6
POOL_WINDOWS = (2, 4, 8, 16)
POOL_GROUP = 64
POOL_WIDTH = POOL_GROUP * len(POOL_WINDOWS)
POOL_HALO = max(POOL_WINDOWS) // 2
DIFF_HEAD_DIM = 64
DIFF_V_DIM = 128
MLA_NOPE = 128
MLA_ROPE = 64
MLA_V = 128
MLA_QK_PAD = 256
N_EXPERTS = 8
ROPE_QUARTER = 16
LOG2E = math.log2(math.e)

LANES = 128
ROW_TILE = 512
KV_TILE = 512
DIFF_Q_TILE = 512
MLA_Q_TILE = 1024
PAIR_UNROLL = 5
MOE_TILE = 512
FFN_CHUNKS = (512, 512, 512, 512, 512, 256)
VMEM_LIMIT = 56 * 1024 * 1024

NT_DIMS = (((1,), (1,)), ((), ()))


def _params(sem):
    return pltpu.CompilerParams(dimension_semantics=sem, vmem_limit_bytes=VMEM_LIMIT)


def _dot(a, b):
    return jnp.dot(a, b, preferred_element_type=F32)


def _dot_nt(a, b):
    return lax.dot_general(a, b, NT_DIMS, preferred_element_type=F32)


def _rms(x):
    return x * lax.rsqrt(jnp.mean(x * x, axis=-1, keepdims=True) + EPS)


def _silu(x):
    return x * (1.0 / (1.0 + jnp.exp(-x)))


def _rope(x, cos, sin):
    lane = lax.broadcasted_iota(jnp.int32, x.shape, 1)
    even = (lane // ROPE_QUARTER) % 2 == 0
    rot = jnp.where(even, -pltpu.roll(x, LANES - ROPE_QUARTER, 1), pltpu.roll(x, ROPE_QUARTER, 1))
    return x * cos + rot * sin


def _mod_kernel(cond_ref, w_ref, b_ref, o_ref):
    s = _silu(cond_ref[...])
    o_ref[0] = jnp.dot(s, w_ref[0], preferred_element_type=F32,
                       precision=lax.Precision.HIGHEST) + b_ref[0]


def _modulation(cond, mod_w, mod_b):
    depth, d, n = mod_w.shape
    tn = n // 4
    return pl.pallas_call(
        _mod_kernel,
        grid=(depth, n // tn),
        in_specs=[pl.BlockSpec((8, d), lambda i, j: (0, 0)),
                  pl.BlockSpec((1, d, tn), lambda i, j: (i, 0, j)),
                  pl.BlockSpec((1, 1, tn), lambda i, j: (i, 0, j))],
        out_specs=pl.BlockSpec((1, 8, tn), lambda i, j: (i, 0, j)),
        out_shape=jax.ShapeDtypeStruct((depth, 8, n), F32),
        compiler_params=_params(("arbitrary", "arbitrary")),
        name="modulation",
    )(cond, mod_w, mod_b.reshape(depth, 1, n))


def _bvec_spec(arr, d):
    if arr.shape[0] == 1:
        return pl.BlockSpec((1, 1, d), lambda b, i: (0, 0, 0))
    return pl.BlockSpec((1, 1, d), lambda b, i: (b, 0, 0))


def _norm_mod(h, g, sh, sc):
    a = _rms(h) * g
    return a * (1.0 + sc) + sh


def _pre0_kernel(h_ref, g_ref, sh_ref, sc_ref, w_ref, cos_ref, sin_ref,
                 p_ref, q_ref, k_ref, vt_ref, *, use_rope, n_heads):
    a = _norm_mod(h_ref[0], g_ref[...], sh_ref[0], sc_ref[0])
    z = _dot(a.astype(BF16), w_ref[...])
    p_ref[0] = z[:, :POOL_WIDTH]
    qk_w = n_heads * 2 * DIFF_HEAD_DIM
    q_scale = DIFF_HEAD_DIM ** -0.5 * LOG2E
    cos = cos_ref[...]
    sin = sin_ref[...]
    for j in range(qk_w // LANES):
        xq = z[:, POOL_WIDTH + j * LANES:POOL_WIDTH + (j + 1) * LANES]
        xk = z[:, POOL_WIDTH + qk_w + j * LANES:POOL_WIDTH + qk_w + (j + 1) * LANES]
        if use_rope:
            xq = _rope(xq, cos, sin)
            xk = _rope(xk, cos, sin)
        q_ref[0, :, j * LANES:(j + 1) * LANES] = (xq * q_scale).astype(BF16)
        k_ref[0, :, j * LANES:(j + 1) * LANES] = xk.astype(BF16)
    v = z[:, POOL_WIDTH + 2 * qk_w:]
    vt_ref[0, 0] = v.T.astype(BF16)


def _pre0(h, g, sh, sc, w_in, cos, sin, *, use_rope, tm):
    b, l, d = h.shape
    n_in = w_in.shape[1]
    v_w = (n_in - POOL_WIDTH) // 3
    n_heads = v_w // DIFF_V_DIM
    kern = functools.partial(_pre0_kernel, use_rope=use_rope, n_heads=n_heads)
    return pl.pallas_call(
        kern,
        grid=(b, l // tm),
        in_specs=[pl.BlockSpec((1, tm, d), lambda b_, i: (b_, i, 0)),
                  pl.BlockSpec((1, d), lambda b_, i: (0, 0)),
                  _bvec_spec(sh, d), _bvec_spec(sc, d),
                  pl.BlockSpec((d, n_in), lambda b_, i: (0, 0)),
                  pl.BlockSpec((tm, LANES), lambda b_, i: (i, 0)),
                  pl.BlockSpec((tm, LANES), lambda b_, i: (i, 0))],
        out_specs=[pl.BlockSpec((1, tm, POOL_WIDTH), lambda b_, i: (b_, i, 0)),
                   pl.BlockSpec((1, tm, v_w), lambda b_, i: (b_, i, 0)),
                   pl.BlockSpec((1, tm, v_w), lambda b_, i: (b_, i, 0)),
                   pl.BlockSpec((1, 1, v_w, tm), lambda b_, i: (b_, i, 0, 0))],
        out_shape=[jax.ShapeDtypeStruct((b, l, POOL_WIDTH), F32),
                   jax.ShapeDtypeStruct((b, l, v_w), BF16),
                   jax.ShapeDtypeStruct((b, l, v_w), BF16),
                   jax.ShapeDtypeStruct((b, l // tm, v_w, tm), BF16)],
        compiler_params=_params(("parallel", "parallel")),
        name="pre0_rope" if use_rope else "pre0_ctx",
    )(h, g, sh, sc, w_in, cos, sin)


def _attn_core(qs, kc_ref, vc_ref, lat_refs, acc_refs, sbuf_refs):
    n = len(qs)
    qts = [q.astype(F32).T.astype(BF16) for q in qs]
    kc = kc_ref[0]
    vc = vc_ref[0, 0]
    ms, ls = [], []
    for i in range(n):
        s = _dot(kc, qts[i])
        m = jnp.max(s, axis=0, keepdims=True)
        p = jnp.exp2(s - m)
        ms.append(m)
        ls.append(jnp.sum(p, axis=0, keepdims=True))
        acc_refs[i][...] = _dot(vc, p.astype(BF16))
    if lat_refs is None:
        return ls
    kl_ref, vl_ref = lat_refs
    nblk, tk = vl_ref.shape[1], vl_ref.shape[3]
    assert nblk >= 2 and nblk % 2 == 0

    def qk(t, which):
        start = t * tk if isinstance(t, int) else pl.multiple_of(t * tk, tk)
        k = kl_ref[0, pl.ds(start, tk), :]
        cmax = []
        for i in range(n):
            s = _dot(k, qts[i])
            sbuf_refs[2 * i + which][...] = s
            cmax.append(jnp.max(s, axis=0, keepdims=True))
        return tuple(cmax)

    def softmax_pv(t, which, cmax, ms, ls):
        vt = vl_ref[0, t]
        new_m, new_l = [], []
        for i in range(n):
            m_new = jnp.maximum(ms[i], cmax[i])
            alpha = jnp.exp2(ms[i] - m_new)
            p = jnp.exp2(sbuf_refs[2 * i + which][...] - m_new)
            new_l.append(alpha * ls[i] + jnp.sum(p, axis=0, keepdims=True))
            acc_refs[i][...] = alpha * acc_refs[i][...] + _dot(vt, p.astype(BF16))
            new_m.append(m_new)
        return tuple(new_m), tuple(new_l)

    def pair(jj, carry):
        cmax_a, ms, ls = carry
        t = 2 * jj
        cmax_b = qk(t + 1, 1)
        ms, ls = softmax_pv(t, 0, cmax_a, ms, ls)
        cmax_a = qk(t + 2, 0)
        ms, ls = softmax_pv(t + 1, 1, cmax_b, ms, ls)
        return cmax_a, ms, ls

    cmax_a = qk(0, 0)
    cmax_a, ms, ls = lax.fori_loop(0, nblk // 2 - 1, pair, (cmax_a, tuple(ms), tuple(ls)),
                                   unroll=PAIR_UNROLL)
    cmax_b = qk(nblk - 1, 1)
    ms, ls = softmax_pv(nblk - 2, 0, cmax_a, ms, ls)
    ms, ls = softmax_pv(nblk - 1, 1, cmax_b, ms, ls)
    return ls


def _diff_attn_kernel(lq_ref, lk_ref, q_ref, kc_ref, vc_ref, *rest, has_lat, lam_init):
    lat_refs = rest[:2] if has_lat else None
    rest = rest[2:] if has_lat else rest
    o_ref, acc1_ref, acc2_ref = rest[:3]
    q = q_ref[0]
    lane = lax.broadcasted_iota(jnp.int32, q.shape, 1)
    zero = jnp.zeros_like(q)
    q1 = jnp.where(lane < DIFF_HEAD_DIM, q, zero)
    q2 = jnp.where(lane >= DIFF_HEAD_DIM, q, zero)
    l1, l2 = _attn_core([q1, q2], kc_ref, vc_ref, lat_refs, (acc1_ref, acc2_ref), rest[3:])
    e = jnp.exp(jnp.sum(lq_ref[...] * lk_ref[...], axis=-1, keepdims=True))
    lam = e[0:1] - e[1:2] + lam_init
    o = acc1_ref[...] * (1.0 / l1) - lam * (acc2_ref[...] * (1.0 / l2))
    ms = jnp.mean(o * o, axis=0, keepdims=True)
    y = o * (lax.rsqrt(ms + EPS) * (1.0 - lam_init))
    o_ref[0] = y.T.astype(BF16)


def _mla_attn_kernel(q_ref, kc_ref, vc_ref, *rest, has_lat):
    lat_refs = rest[:2] if has_lat else None
    rest = rest[2:] if has_lat else rest
    o_ref, acc_ref = rest[:2]
    (l,) = _attn_core([q_ref[0]], kc_ref, vc_ref, lat_refs, (acc_ref,), rest[2:])
    o_ref[0] = (acc_ref[...] * (1.0 / l)).T.astype(BF16)


def _attention(kern, n_maps, q, ctx_kv, lat_kv, qk_w, extra=(), *, tq, name):
    b, lq, hw = q.shape
    n_heads = hw // qk_w
    in_specs = [pl.BlockSpec(x.shape, lambda b_, h, i: (0, 0)) for x in extra]
    in_specs.append(pl.BlockSpec((1, tq, qk_w), lambda b_, h, i: (b_, i, h)))
    args = list(extra) + [q]
    scratch = [pltpu.VMEM((DIFF_V_DIM, tq), F32) for _ in range(n_maps)]
    for kv in (ctx_kv, lat_kv):
        if kv is None:
            continue
        k, vt = kv
        nblk, tk = vt.shape[1], vt.shape[3]
        in_specs.append(pl.BlockSpec((1, k.shape[1], qk_w), lambda b_, h, i: (b_, 0, h)))
        in_specs.append(pl.BlockSpec((1, nblk, DIFF_V_DIM, tk), lambda b_, h, i: (b_, 0, h, 0)))
        args += [k, vt]
    if lat_kv is not None:
        scratch += [pltpu.VMEM((lat_kv[1].shape[3], tq), F32) for _ in range(2 * n_maps)]
    return pl.pallas_call(
        functools.partial(kern, has_lat=lat_kv is not None),
        grid=(b, n_heads, lq // tq),
        in_specs=in_specs,
        out_specs=pl.BlockSpec((1, tq, DIFF_V_DIM), lambda b_, h, i: (b_, i, h)),
        out_shape=jax.ShapeDtypeStruct((b, lq, n_heads * DIFF_V_DIM), BF16),
        scratch_shapes=scratch,
        compiler_params=_params(("parallel", "parallel", "arbitrary")),
        name=name,
    )(*args)


def _post0_kernel(attn_ref, p_ref, pprev_ref, pnext_ref, poolw_ref, pscale_ref, wout_ref,
                  h_ref, g1_ref, n2g_ref, sh2_ref, sc2_ref, h1_ref, a2_ref, *, seq_len):
    i = pl.program_id(1)
    u = p_ref[0]
    tm = u.shape[0]
    ext = jnp.concatenate([pprev_ref[0], u, pnext_ref[0]], axis=0)
    row = i * tm - POOL_HALO + lax.broadcasted_iota(jnp.int32, (tm + 2 * POOL_HALO, 1), 0)
    ext = jnp.where((row >= 0) & (row < seq_len), ext, 0.0)

    def shifted(j):
        return ext[POOL_HALO + j:POOL_HALO + j + tm]

    t = i * tm + lax.broadcasted_iota(jnp.int32, (tm, 1), 0)
    lane = lax.broadcasted_iota(jnp.int32, (tm, POOL_WIDTH), 1)
    win_sum = shifted(-1) + shifted(0)
    pooled = jnp.zeros_like(u)
    for g, w in enumerate(POOL_WINDOWS):
        half = w // 2
        if g > 0:
            prev_half = POOL_WINDOWS[g - 1] // 2
            for j in range(prev_half, half):
                win_sum = win_sum + shifted(-j - 1) + shifted(j)
        cnt = (jnp.minimum(t + half, seq_len) - jnp.maximum(t - half, 0)).astype(F32)
        in_group = (lane >= g * POOL_GROUP) & (lane < (g + 1) * POOL_GROUP)
        pooled = jnp.where(in_group, win_sum / cnt, pooled)
    pooled = pooled - u
    y_pool = _dot(pooled.astype(BF16), poolw_ref[...]) * pscale_ref[...]
    y = _dot(y_pool.astype(BF16), wout_ref[:POOL_WIDTH, :]) + _dot(attn_ref[0], wout_ref[POOL_WIDTH:, :])
    h1 = h_ref[0] + g1_ref[0] * y
    h1_ref[0] = h1
    a2_ref[0] = _norm_mod(h1, n2g_ref[...], sh2_ref[0], sc2_ref[0]).astype(BF16)


def _post0(attn, p, poolw_bd, pscale, w_out, h, g1, n2g, sh2, sc2, *, tm):
    b, l, d = h.shape
    hb = tm // POOL_HALO
    nhb = l // POOL_HALO
    return pl.pallas_call(
        functools.partial(_post0_kernel, seq_len=l),
        grid=(b, l // tm),
        in_specs=[pl.BlockSpec((1, tm, attn.shape[2]), lambda b_, i: (b_, i, 0)),
                  pl.BlockSpec((1, tm, POOL_WIDTH), lambda b_, i: (b_, i, 0)),
                  pl.BlockSpec((1, POOL_HALO, POOL_WIDTH),
                               lambda b_, i: (b_, jnp.maximum(i * hb - 1, 0), 0)),
                  pl.BlockSpec((1, POOL_HALO, POOL_WIDTH),
                               lambda b_, i: (b_, jnp.minimum((i + 1) * hb, nhb - 1), 0)),
                  pl.BlockSpec(poolw_bd.shape, lambda b_, i: (0, 0)),
                  pl.BlockSpec((1, POOL_WIDTH), lambda b_, i: (0, 0)),
                  pl.BlockSpec(w_out.shape, lambda b_, i: (0, 0)),
                  pl.BlockSpec((1, tm, d), lambda b_, i: (b_, i, 0)),
                  _bvec_spec(g1, d),
                  pl.BlockSpec((1, d), lambda b_, i: (0, 0)),
                  _bvec_spec(sh2, d), _bvec_spec(sc2, d)],
        out_specs=[pl.BlockSpec((1, tm, d), lambda b_, i: (b_, i, 0)),
                   pl.BlockSpec((1, tm, d), lambda b_, i: (b_, i, 0))],
        out_shape=[jax.ShapeDtypeStruct((b, l, d), F32),
                   jax.ShapeDtypeStruct((b, l, d), BF16)],
        compiler_params=_params(("parallel", "parallel")),
        name="post0",
    )(attn, p, p, p, poolw_bd, pscale, w_out, h, g1, n2g, sh2, sc2)


def _swiglu_tile(x, wg_ref, wu_ref, wd_ref, lead=()):
    acc = None
    c0 = 0
    for cw in FFN_CHUNKS:
        hg = _dot(x, wg_ref[lead + (slice(None), slice(c0, c0 + cw))])
        hu = _dot(x, wu_ref[lead + (slice(None), slice(c0, c0 + cw))])
        act = (_silu(hg) * hu).astype(BF16)
        part = _dot(act, wd_ref[lead + (slice(c0, c0 + cw), slice(None))])
        acc = part if acc is None else acc + part
        c0 += cw
    return acc


def _ffn_kernel(a2_ref, h1_ref, g2_ref, wg_ref, wu_ref, wd_ref, o_ref):
    y = _swiglu_tile(a2_ref[0], wg_ref, wu_ref, wd_ref)
    o_ref[0] = h1_ref[0] + g2_ref[0] * y


def _ffn(a2, h1, g2, wg, wu, wd, *, tm):
    b, l, d = h1.shape
    f = wg.shape[1]
    assert sum(FFN_CHUNKS) == f
    return pl.pallas_call(
        _ffn_kernel,
        grid=(b, l // tm),
        in_specs=[pl.BlockSpec((1, tm, d), lambda b_, i: (b_, i, 0)),
                  pl.BlockSpec((1, tm, d), lambda b_, i: (b_, i, 0)),
                  _bvec_spec(g2, d),
                  pl.BlockSpec((d, f), lambda b_, i: (0, 0)),
                  pl.BlockSpec((d, f), lambda b_, i: (0, 0)),
                  pl.BlockSpec((f, d), lambda b_, i: (0, 0))],
        out_specs=pl.BlockSpec((1, tm, d), lambda b_, i: (b_, i, 0)),
        out_shape=jax.ShapeDtypeStruct((b, l, d), F32),
        compiler_params=_params(("parallel", "parallel")),
        name="ffn_dense",
    )(a2, h1, g2, wg, wu, wd)


def _pre1_kernel(h_ref, g_ref, sh_ref, sc_ref, win_ref, qg_ref, kvg_ref, wuq_ref, wuk_ref, wuv_ref,
                 cos_ref, sin_ref, *out_refs, use_rope, want_q, q_rank, kv_rank, n_heads):
    a = _norm_mod(h_ref[0], g_ref[...], sh_ref[0], sc_ref[0])
    z = _dot(a.astype(BF16), win_ref[...])
    cos = cos_ref[...]
    sin = sin_ref[...]
    if want_q:
        q_ref, k_ref, vt_ref = out_refs
        cq = _rms(z[:, :q_rank]) * qg_ref[...]
        qf = _dot(cq.astype(BF16), wuq_ref[...])
        q_scale = (MLA_NOPE + MLA_ROPE) ** -0.5 * LOG2E
        for h in range(n_heads):
            c0 = h * MLA_QK_PAD
            q_ref[0, :, c0:c0 + MLA_NOPE] = (qf[:, c0:c0 + MLA_NOPE] * q_scale).astype(BF16)
            qr = _rope(qf[:, c0 + MLA_NOPE:c0 + MLA_QK_PAD], cos, sin)
            q_ref[0, :, c0 + MLA_NOPE:c0 + MLA_QK_PAD] = (qr * q_scale).astype(BF16)
    else:
        k_ref, vt_ref = out_refs
    ckv = (_rms(z[:, q_rank:q_rank + kv_rank]) * kvg_ref[...]).astype(BF16)
    kn = _dot(ckv, wuk_ref[...])
    vv = _dot(ckv, wuv_ref[...])
    kr = z[:, q_rank + kv_rank:]
    if use_rope:
        kr = _rope(kr, cos, sin)
    kr = kr.astype(BF16)
    for h in range(n_heads):
        c0 = h * MLA_QK_PAD
        k_ref[0, :, c0:c0 + MLA_NOPE] = kn[:, h * MLA_NOPE:(h + 1) * MLA_NOPE].astype(BF16)
        k_ref[0, :, c0 + MLA_NOPE:c0 + MLA_QK_PAD] = kr
    vt_ref[0, 0] = vv.T.astype(BF16)


def _pre1(h, g, sh, sc, w_in, qg, kvg, wuq, wuk, wuv, cos, sin, *, use_rope, want_q, tm):
    b, l, d = h.shape
    n_heads = wuk.shape[1] // MLA_NOPE
    q_rank, kv_rank = wuq.shape[0], wuk.shape[0]
    kern = functools.partial(_pre1_kernel, use_rope=use_rope, want_q=want_q,
                             q_rank=q_rank, kv_rank=kv_rank, n_heads=n_heads)
    full = lambda x: pl.BlockSpec(x.shape, lambda b_, i: (0,) * x.ndim)
    qk_w = n_heads * MLA_QK_PAD
    v_w = n_heads * MLA_V
    out_specs = [pl.BlockSpec((1, tm, qk_w), lambda b_, i: (b_, i, 0)),
                 pl.BlockSpec((1, 1, v_w, tm), lambda b_, i: (b_, i, 0, 0))]
    out_shape = [jax.ShapeDtypeStruct((b, l, qk_w), BF16),
                 jax.ShapeDtypeStruct((b, l // tm, v_w, tm), BF16)]
    if want_q:
        out_specs = [pl.BlockSpec((1, tm, qk_w), lambda b_, i: (b_, i, 0))] + out_specs
        out_shape = [jax.ShapeDtypeStruct((b, l, qk_w), BF16)] + out_shape
    return pl.pallas_call(
        kern,
        grid=(b, l // tm),
        in_specs=[pl.BlockSpec((1, tm, d), lambda b_, i: (b_, i, 0)),
                  pl.BlockSpec((1, d), lambda b_, i: (0, 0)),
                  _bvec_spec(sh, d), _bvec_spec(sc, d),
                  full(w_in), full(qg), full(kvg), full(wuq), full(wuk), full(wuv),
                  pl.BlockSpec((tm, LANES), lambda b_, i: (i, 0)),
                  pl.BlockSpec((tm, LANES), lambda b_, i: (i, 0))],
        out_specs=out_specs,
        out_shape=out_shape,
        compiler_params=_params(("parallel", "parallel")),
        name="pre1_lat" if want_q else "pre1_ctx",
    )(h, g, sh, sc, w_in, qg, kvg, wuq, wuk, wuv, cos, sin)


def _post1_kernel(o_ref, wout_ref, h_ref, g1_ref, n2g_ref, sh2_ref, sc2_ref, rw_ref,
                  h1_ref, a2_ref, gates_ref, sel_ref):
    y = _dot(o_ref[0], wout_ref[...])
    h1 = h_ref[0] + g1_ref[0] * y
    h1_ref[0] = h1
    a2 = _norm_mod(h1, n2g_ref[...], sh2_ref[0], sc2_ref[0])
    a2_ref[0] = a2
    logits = jnp.dot(a2, rw_ref[...], preferred_element_type=F32,
                     precision=lax.Precision.HIGHEST)
    lane = lax.broadcasted_iota(jnp.int32, logits.shape, 1)
    lg = jnp.where(lane < N_EXPERTS, logits, -jnp.inf)
    m1 = jnp.max(lg, axis=1, keepdims=True)
    i1 = jnp.min(jnp.where(lg == m1, lane, LANES), axis=1, keepdims=True)
    lg2 = jnp.where(lane == i1, -jnp.inf, lg)
    m2 = jnp.max(lg2, axis=1, keepdims=True)
    i2 = jnp.min(jnp.where(lg2 == m2, lane, LANES), axis=1, keepdims=True)
    p2 = jnp.exp(m2 - m1)
    w1 = 1.0 / (1.0 + p2)
    gates_ref[0] = jnp.where(lane == i1, w1, 0.0) + jnp.where(lane == i2, p2 * w1, 0.0)
    sel_ref[0] = jnp.where((lane == i1) | (lane == i2), 1.0, 0.0).astype(BF16)


def _post1(o, w_out, h, g1, n2g, sh2, sc2, rw_pad, *, tm):
    b, l, d = h.shape
    return pl.pallas_call(
        _post1_kernel,
        grid=(b, l // tm),
        in_specs=[pl.BlockSpec((1, tm, o.shape[2]), lambda b_, i: (b_, i, 0)),
                  pl.BlockSpec(w_out.shape, lambda b_, i: (0, 0)),
                  pl.BlockSpec((1, tm, d), lambda b_, i: (b_, i, 0)),
                  _bvec_spec(g1, d),
                  pl.BlockSpec((1, d), lambda b_, i: (0, 0)),
                  _bvec_spec(sh2, d), _bvec_spec(sc2, d),
                  pl.BlockSpec(rw_pad.shape, lambda b_, i: (0, 0))],
        out_specs=[pl.BlockSpec((1, tm, d), lambda b_, i: (b_, i, 0)),
                   pl.BlockSpec((1, tm, d), lambda b_, i: (b_, i, 0)),
                   pl.BlockSpec((1, tm, LANES), lambda b_, i: (b_, i, 0)),
                   pl.BlockSpec((1, tm, LANES), lambda b_, i: (b_, i, 0))],
        out_shape=[jax.ShapeDtypeStruct((b, l, d), F32),
                   jax.ShapeDtypeStruct((b, l, d), F32),
                   jax.ShapeDtypeStruct((b, l, LANES), F32),
                   jax.ShapeDtypeStruct((b, l, LANES), BF16)],
        compiler_params=_params(("parallel", "parallel")),
        name="post1_router",
    )(o, w_out, h, g1, n2g, sh2, sc2, rw_pad)


def _route_kernel(sel_ref, gates_ref, r_ref, tile_ref, cnt_ref, off_ref, *, tile_rows):
    phase = pl.program_id(0)
    i = pl.program_id(1)
    sel = sel_ref[...]
    tr = sel.shape[0]
    col_count = jnp.sum(sel.astype(F32), axis=0, keepdims=True)
    lane = lax.broadcasted_iota(jnp.int32, (1, LANES), 1)

    @pl.when((phase == 0) & (i == 0))
    def _():
        cnt_ref[...] = jnp.zeros_like(cnt_ref)

    @pl.when(phase == 0)
    def _():
        cnt_ref[...] += col_count

    @pl.when((phase == 0) & (i == pl.num_programs(1) - 1))
    def _():
        padded = jnp.ceil(cnt_ref[...] / tile_rows) * tile_rows
        incl = padded
        for sh in (1, 2, 4):
            incl = incl + jnp.where(lane >= sh, pltpu.roll(incl, sh, 1), 0.0)
        off_ref[...] = incl - padded
        tile_start = (lax.broadcasted_iota(jnp.int32, tile_ref.shape, 1) * tile_rows).astype(F32)
        tile_expert = jnp.zeros(tile_ref.shape, jnp.int32)
        for e in range(N_EXPERTS):
            end_e = jnp.sum(jnp.where(lane == e, incl, 0.0), axis=1, keepdims=True)
            tile_expert = tile_expert + (tile_start >= end_e).astype(jnp.int32)
        tile_ref[...] = tile_expert
        cnt_ref[...] = jnp.zeros_like(cnt_ref)

    @pl.when(phase == 1)
    def _():
        rr = lax.broadcasted_iota(jnp.int32, (tr, tr), 0)
        cc = lax.broadcasted_iota(jnp.int32, (tr, tr), 1)
        earlier = jnp.where(cc < rr, 1.0, 0.0).astype(BF16)
        rank = _dot(earlier, sel) + cnt_ref[...]
        cnt_ref[...] += col_count
        pos = off_ref[...] + rank
        chosen = sel > 0
        lo = jnp.min(jnp.where(chosen, pos, 3e38), axis=1, keepdims=True)
        hi = jnp.max(jnp.where(chosen, pos, -1.0), axis=1, keepdims=True)
        g = gates_ref[...]
        w_lo = jnp.sum(jnp.where(chosen & (pos == lo), g, 0.0), axis=1, keepdims=True)
        w_hi = jnp.sum(jnp.where(chosen & (pos == hi), g, 0.0), axis=1, keepdims=True)
        lane_t = lax.broadcasted_iota(jnp.int32, (tr, LANES), 1)
        r_ref[...] = jnp.where(lane_t == 0, lo, jnp.where(lane_t == 1, hi,
                               jnp.where(lane_t == 2, w_lo, jnp.where(lane_t == 3, w_hi, 0.0))))


def _route(sel, gates, *, tr, tile_rows, n_tiles):
    n = sel.shape[0]
    n_tiles_pad = -(-n_tiles // LANES) * LANES
    return pl.pallas_call(
        functools.partial(_route_kernel, tile_rows=tile_rows),
        grid=(2, n // tr),
        in_specs=[pl.BlockSpec((tr, LANES), lambda ph, i: (i, 0)),
                  pl.BlockSpec((tr, LANES), lambda ph, i: (i, 0))],
        out_specs=[pl.BlockSpec((tr, LANES), lambda ph, i: (i * ph, 0)),
                   pl.BlockSpec((1, n_tiles_pad), lambda ph, i: (0, 0))],
        out_shape=[jax.ShapeDtypeStruct((n, LANES), F32),
                   jax.ShapeDtypeStruct((1, n_tiles_pad), jnp.int32)],
        scratch_shapes=[pltpu.VMEM((1, LANES), F32), pltpu.VMEM((1, LANES), F32)],
        compiler_params=_params(("arbitrary", "arbitrary")),
        name="moe_route",
    )(sel, gates)


def _dispatch_kernel(dest_ref, x_hbm, xs_in_hbm, xs_hbm, sem, *, tm):
    del xs_in_hbm
    base = pl.program_id(0) * tm

    def row_copy(src_row, dst_row):
        return pltpu.make_async_copy(x_hbm.at[pl.ds(src_row, 1)], xs_hbm.at[pl.ds(dst_row, 1)], sem)

    def issue(r, carry):
        row_copy(base + r, dest_ref[0, 0, r]).start()
        row_copy(base + r, dest_ref[0, 0, tm + r]).start()
        return carry

    lax.fori_loop(0, tm, issue, 0)
    pltpu.make_async_copy(x_hbm.at[pl.ds(0, 2 * tm)], xs_hbm.at[pl.ds(0, 2 * tm)], sem).wait()


def _dispatch(dest, x, n_rows, *, tm):
    n = x.shape[0]
    xs0 = jnp.zeros((n_rows,) + x.shape[1:], x.dtype)
    return pl.pallas_call(
        functools.partial(_dispatch_kernel, tm=tm),
        grid=(n // tm,),
        in_specs=[pl.BlockSpec((1, 1, 2 * tm), lambda i: (i, 0, 0), memory_space=pltpu.SMEM),
                  pl.BlockSpec(memory_space=pl.ANY),
                  pl.BlockSpec(memory_space=pl.ANY)],
        out_specs=pl.BlockSpec(memory_space=pl.ANY),
        out_shape=jax.ShapeDtypeStruct(xs0.shape, xs0.dtype),
        scratch_shapes=[pltpu.SemaphoreType.DMA(())],
        input_output_aliases={2: 0},
        compiler_params=pltpu.CompilerParams(dimension_semantics=("arbitrary",),
                                             has_side_effects=True),
        name="moe_dispatch",
    )(dest, x, xs0)


def _experts_kernel(te_ref, x_ref, wg_ref, wu_ref, wd_ref, o_ref):
    valid = te_ref[pl.program_id(0)] < N_EXPERTS

    @pl.when(valid)
    def _():
        o_ref[...] = _swiglu_tile(x_ref[...].astype(BF16), wg_ref, wu_ref, wd_ref, lead=(0,))

    @pl.when(jnp.logical_not(valid))
    def _():
        o_ref[...] = jnp.zeros_like(o_ref)


def _experts(tile_expert, xs, wg, wu, wd, *, tile_rows):
    n_rows, d = xs.shape
    ne, _, f = wg.shape
    w_idx = lambda j, te: (jnp.minimum(te[j], ne - 1), 0, 0)
    return pl.pallas_call(
        _experts_kernel,
        grid_spec=pltpu.PrefetchScalarGridSpec(
            num_scalar_prefetch=1,
            grid=(n_rows // tile_rows,),
            in_specs=[pl.BlockSpec((tile_rows, d), lambda j, te: (j, 0)),
                      pl.BlockSpec((1, d, f), w_idx),
                      pl.BlockSpec((1, d, f), w_idx),
                      pl.BlockSpec((1, f, d), w_idx)],
            out_specs=pl.BlockSpec((tile_rows, d), lambda j, te: (j, 0))),
        out_shape=jax.ShapeDtypeStruct((n_rows, d), F32),
        compiler_params=_params(("arbitrary",)),
        name="moe_experts",
    )(tile_expert, xs, wg, wu, wd)


def _combine_kernel(dest_ref, routed_ref, ys_hbm, h1_ref, g2_ref, fng_ref, o_ref, ylo_ref, yhi_ref,
                    sem, *, tm):
    def issue(r, carry):
        pltpu.make_async_copy(ys_hbm.at[pl.ds(dest_ref[0, 0, r], 1)],
                              ylo_ref.at[pl.ds(r, 1)], sem).start()
        pltpu.make_async_copy(ys_hbm.at[pl.ds(dest_ref[0, 0, tm + r], 1)],
                              yhi_ref.at[pl.ds(r, 1)], sem).start()
        return carry

    lax.fori_loop(0, tm, issue, 0)
    pltpu.make_async_copy(ys_hbm.at[pl.ds(0, tm)], ylo_ref, sem).wait()
    pltpu.make_async_copy(ys_hbm.at[pl.ds(0, tm)], yhi_ref, sem).wait()
    routed = routed_ref[...]
    y = routed[:, 2:3] * ylo_ref[...] + routed[:, 3:4] * yhi_ref[...]
    out = h1_ref[...] + g2_ref[0] * y
    o_ref[...] = _rms(out) * fng_ref[...]


def _combine(dest, routed, ys, h1, g2, fng, *, tm, tiles_per_sample):
    n, d = h1.shape
    return pl.pallas_call(
        functools.partial(_combine_kernel, tm=tm),
        grid=(n // tm,),
        in_specs=[pl.BlockSpec((1, 1, 2 * tm), lambda i: (i, 0, 0), memory_space=pltpu.SMEM),
                  pl.BlockSpec((tm, LANES), lambda i: (i, 0)),
                  pl.BlockSpec(memory_space=pl.ANY),
                  pl.BlockSpec((tm, d), lambda i: (i, 0)),
                  pl.BlockSpec((1, 1, d), lambda i: (i // tiles_per_sample, 0, 0)),
                  pl.BlockSpec((1, d), lambda i: (0, 0))],
        out_specs=pl.BlockSpec((tm, d), lambda i: (i, 0)),
        out_shape=jax.ShapeDtypeStruct((n, d), F32),
        scratch_shapes=[pltpu.VMEM((tm, d), F32), pltpu.VMEM((tm, d), F32),
                        pltpu.SemaphoreType.DMA(())],
        compiler_params=_params(("arbitrary",)),
        name="moe_combine",
    )(dest, routed, ys, h1, g2, fng)


def _moe(a2, gates, sel, wg, wu, wd, h1, g2, fng, *, tm):
    b, l, d = h1.shape
    n = b * l
    n_rows = 2 * n + N_EXPERTS * MOE_TILE
    n_tiles = n_rows // MOE_TILE
    routed, tile_expert = _route(sel.reshape(n, LANES), gates.reshape(n, LANES),
                                 tr=tm, tile_rows=MOE_TILE, n_tiles=n_tiles)
    dest = routed[:, 0:2].astype(jnp.int32).reshape(n // tm, tm, 2)
    dest = dest.transpose(0, 2, 1).reshape(n // tm, 1, 2 * tm)
    xs = _dispatch(dest, a2.reshape(n, d), n_rows, tm=tm)
    ys = _experts(tile_expert[0, :n_tiles], xs, wg, wu, wd, tile_rows=MOE_TILE)
    out = _combine(dest, routed, ys, h1.reshape(n, d), g2, fng, tm=tm, tiles_per_sample=l // tm)
    return out.reshape(b, l, d)


def _rope_tables(seq_len, dim):
    t = jnp.arange(seq_len, dtype=jnp.int32)
    row = (t // GRID_W).astype(F32)
    col = (t % GRID_W).astype(F32)
    quarter = dim // 4
    inv = ROPE_BASE ** (-jnp.arange(quarter, dtype=F32) / quarter)
    ar = row[:, None] * inv[None, :]
    ac = col[:, None] * inv[None, :]
    cos = jnp.concatenate([jnp.cos(ar), jnp.cos(ar), jnp.cos(ac), jnp.cos(ac)], axis=-1)
    sin = jnp.concatenate([jnp.sin(ar), jnp.sin(ar), jnp.sin(ac), jnp.sin(ac)], axis=-1)
    reps = LANES // dim
    return jnp.tile(cos, (1, reps)), jnp.tile(sin, (1, reps))


def kernel(x, c, ctx, c_ctx, norm1_g, norm2_g, mod_w, mod_b, even_w_in, pool_w, pool_scale,
           lambda_q1, lambda_k1, lambda_q2, lambda_k2, even_w_out, ffn_w_gate, ffn_w_up,
           ffn_w_down, odd_w_in, q_norm_g, kv_norm_g, w_uq, w_ukv, odd_w_out, router_w,
           moe_w_gate, moe_w_up, moe_w_down, final_norm_g):
    b, l, d = x.shape
    n_ctx = ctx.shape[1]
    tm = min(ROW_TILE, l)
    tq = min(DIFF_Q_TILE, l)
    assert tm == KV_TILE or l < KV_TILE
    cos, sin = _rope_tables(l, DIFF_HEAD_DIM)
    cos_c, sin_c = cos[:n_ctx], sin[:n_ctx]

    cond = jnp.zeros((8, d), F32).at[:b].set(c).at[b].set(c_ctx)
    mod = _modulation(cond, mod_w, mod_b)
    mod = mod.reshape(mod.shape[0], 8, N_MOD, d)

    def mod_vecs(layer):
        lat = [mod[layer, :b, k][:, None, :] for k in range(N_MOD)]
        cx = [mod[layer, b:b + 1, k][:, None, :] for k in range(N_MOD)]
        return lat, cx

    (sh1, sc1, g1, sh2, sc2, g2), (csh1, csc1, cg1, csh2, csc2, cg2) = mod_vecs(0)
    lam_init = 0.8 - 0.6 * math.exp(-0.3 * 0)
    w_in0 = even_w_in[0].astype(BF16)
    n1g = norm1_g[0][None, :]
    n2g = norm2_g[0][None, :]
    p_lat, q_lat, k_lat, vt_lat = _pre0(x, n1g, sh1, sc1, w_in0, cos, sin, use_rope=True, tm=tm)
    p_ctx, q_ctx, k_ctx, vt_ctx = _pre0(ctx, n1g, csh1, csc1, w_in0, cos_c, sin_c,
                                        use_rope=False, tm=n_ctx)
    lq = jnp.stack([lambda_q1[0], lambda_q2[0]])
    lk = jnp.stack([lambda_k1[0], lambda_k2[0]])
    diff_kern = functools.partial(_diff_attn_kernel, lam_init=lam_init)
    attn_lat = _attention(diff_kern, 2, q_lat, (k_ctx, vt_ctx), (k_lat, vt_lat), 2 * DIFF_HEAD_DIM,
                          extra=(lq, lk), tq=tq, name="diff_attn_lat")
    attn_ctx = _attention(diff_kern, 2, q_ctx, (k_ctx, vt_ctx), None, 2 * DIFF_HEAD_DIM,
                          extra=(lq, lk), tq=n_ctx, name="diff_attn_ctx")
    ng = len(POOL_WINDOWS)
    poolw_bd = jnp.zeros((POOL_WIDTH, POOL_WIDTH), F32)
    for g in range(ng):
        sl = slice(g * POOL_GROUP, (g + 1) * POOL_GROUP)
        poolw_bd = poolw_bd.at[sl, sl].set(pool_w[0, g])
    poolw_bd = poolw_bd.astype(BF16)
    pscale = pool_scale[0][None, :]
    w_out0 = even_w_out[0].astype(BF16)
    wg0, wu0, wd0 = ffn_w_gate[0].astype(BF16), ffn_w_up[0].astype(BF16), ffn_w_down[0].astype(BF16)
    h1_lat, a2_lat = _post0(attn_lat, p_lat, poolw_bd, pscale, w_out0, x, g1, n2g, sh2, sc2, tm=tm)
    h_lat = _ffn(a2_lat, h1_lat, g2, wg0, wu0, wd0, tm=tm)
    h1_ctx, a2_ctx = _post0(attn_ctx, p_ctx, poolw_bd, pscale, w_out0, ctx, cg1, n2g, csh2, csc2,
                            tm=n_ctx)
    h_ctx = _ffn(a2_ctx, h1_ctx, cg2, wg0, wu0, wd0, tm=n_ctx)

    (sh1, sc1, g1, sh2, sc2, g2), (csh1, csc1, _, _, _, _) = mod_vecs(1)
    n1g = norm1_g[1][None, :]
    n2g = norm2_g[1][None, :]
    q_rank, kv_rank = w_uq.shape[1], w_ukv.shape[1]
    n_heads = w_uq.shape[2] // (MLA_NOPE + MLA_ROPE)
    w_in1 = jnp.pad(odd_w_in[0], ((0, 0), (0, LANES - MLA_ROPE))).astype(BF16)
    wuq = w_uq[0].reshape(q_rank, n_heads, MLA_NOPE + MLA_ROPE)
    wuq = jnp.pad(wuq, ((0, 0), (0, 0), (0, MLA_QK_PAD - MLA_NOPE - MLA_ROPE)))
    wuq = wuq.reshape(q_rank, n_heads * MLA_QK_PAD).astype(BF16)
    wukv = w_ukv[0].reshape(kv_rank, n_heads, MLA_NOPE + MLA_V)
    wuk = wukv[:, :, :MLA_NOPE].reshape(kv_rank, n_heads * MLA_NOPE).astype(BF16)
    wuv = wukv[:, :, MLA_NOPE:].reshape(kv_rank, n_heads * MLA_V).astype(BF16)
    qg = q_norm_g[0][None, :]
    kvg = kv_norm_g[0][None, :]
    q1, k1, vt1 = _pre1(h_lat, n1g, sh1, sc1, w_in1, qg, kvg, wuq, wuk, wuv, cos, sin,
                        use_rope=True, want_q=True, tm=tm)
    k1c, vt1c = _pre1(h_ctx, n1g, csh1, csc1, w_in1, qg, kvg, wuq, wuk, wuv, cos_c, sin_c,
                      use_rope=False, want_q=False, tm=n_ctx)
    o1 = _attention(_mla_attn_kernel, 1, q1, (k1c, vt1c), (k1, vt1), MLA_QK_PAD,
                    tq=min(MLA_Q_TILE, l), name="mla_attn")
    rw_pad = jnp.pad(router_w[0], ((0, 0), (0, LANES - N_EXPERTS)))
    h1, a2, gates, sel = _post1(o1, odd_w_out[0].astype(BF16), h_lat, g1, n2g, sh2, sc2, rw_pad,
                                tm=tm)
    return _moe(a2, gates, sel, moe_w_gate[0].astype(BF16), moe_w_up[0].astype(BF16),
                moe_w_down[0].astype(BF16), h1, g2, final_norm_g[None, :], tm=tm)
```

```python
import functools
import math

import jax
import jax.numpy as jnp
from jax import lax
from jax.experimental import pallas as pl
from jax.experimental.pallas import tpu as pltpu

F32 = jnp.float32
BF16 = jnp.bfloat16

EPS = 1e-6
GRID_W = 64
ROPE_BASE = 10000.0
N_MOD = 6
POOL_WINDOWS = (2, 4, 8, 16)
POOL_GROUP = 64
POOL_WIDTH = POOL_GROUP * len(POOL_WINDOWS)
POOL_HALO = max(POOL_WINDOWS) // 2
DIFF_HEAD_DIM = 64
DIFF_V_DIM = 128
MLA_NOPE = 128
MLA_ROPE = 64
MLA_V = 128
MLA_QK_PAD = 256
N_EXPERTS = 8
ROPE_QUARTER = 16
LOG2E = math.log2(math.e)

LANES = 128
ROW_TILE = 512
KV_TILE = 512
DIFF_Q_TILE = 512
MLA_Q_TILE = 1024
PAIR_UNROLL = 5
MOE_TILE = 512
FFN_CHUNKS = (512, 512, 512, 512, 512, 256)
VMEM_LIMIT = 56 * 1024 * 1024

NT_DIMS = (((1,), (1,)), ((), ()))


def _params(sem):
    return pltpu.CompilerParams(dimension_semantics=sem, vmem_limit_bytes=VMEM_LIMIT)


def _dot(a, b):
    return jnp.dot(a, b, preferred_element_type=F32)


def _dot_nt(a, b):
    return lax.dot_general(a, b, NT_DIMS, preferred_element_type=F32)


def _rms(x):
    return x * lax.rsqrt(jnp.mean(x * x, axis=-1, keepdims=True) + EPS)


def _silu(x):
    return x * (1.0 / (1.0 + jnp.exp(-x)))


def _rope(x, cos, sin):
    lane = lax.broadcasted_iota(jnp.int32, x.shape, 1)
    even = (lane // ROPE_QUARTER) % 2 == 0
    rot = jnp.where(even, -pltpu.roll(x, LANES - ROPE_QUARTER, 1), pltpu.roll(x, ROPE_QUARTER, 1))
    return x * cos + rot * sin


def _mod_kernel(cond_ref, w_ref, b_ref, o_ref):
    s = _silu(cond_ref[...])
    o_ref[0] = jnp.dot(s, w_ref[0], preferred_element_type=F32,
                       precision=lax.Precision.HIGHEST) + b_ref[0]


def _modulation(cond, mod_w, mod_b):
    depth, d, n = mod_w.shape
    tn = n // 4
    return pl.pallas_call(
        _mod_kernel,
        grid=(depth, n // tn),
        in_specs=[pl.BlockSpec((8, d), lambda i, j: (0, 0)),
                  pl.BlockSpec((1, d, tn), lambda i, j: (i, 0, j)),
                  pl.BlockSpec((1, 1, tn), lambda i, j: (i, 0, j))],
        out_specs=pl.BlockSpec((1, 8, tn), lambda i, j: (i, 0, j)),
        out_shape=jax.ShapeDtypeStruct((depth, 8, n), F32),
        compiler_params=_params(("arbitrary", "arbitrary")),
        name="modulation",
    )(cond, mod_w, mod_b.reshape(depth, 1, n))


def _bvec_spec(arr, d):
    if arr.shape[0] == 1:
        return pl.BlockSpec((1, 1, d), lambda b, i: (0, 0, 0))
    return pl.BlockSpec((1, 1, d), lambda b, i: (b, 0, 0))


def _norm_mod(h, g, sh, sc):
    a = _rms(h) * g
    return a * (1.0 + sc) + sh


def _pre0_kernel(h_ref, g_ref, sh_ref, sc_ref, w_ref, cos_ref, sin_ref,
                 p_ref, q_ref, k_ref, vt_ref, *, use_rope, n_heads):
    a = _norm_mod(h_ref[0], g_ref[...], sh_ref[0], sc_ref[0])
    z = _dot(a.astype(BF16), w_ref[...])
    p_ref[0] = z[:, :POOL_WIDTH]
    qk_w = n_heads * 2 * DIFF_HEAD_DIM
    q_scale = DIFF_HEAD_DIM ** -0.5 * LOG2E
    cos = cos_ref[...]
    sin = sin_ref[...]
    for j in range(qk_w // LANES):
        xq = z[:, POOL_WIDTH + j * LANES:POOL_WIDTH + (j + 1) * LANES]
        xk = z[:, POOL_WIDTH + qk_w + j * LANES:POOL_WIDTH + qk_w + (j + 1) * LANES]
        if use_rope:
            xq = _rope(xq, cos, sin)
            xk = _rope(xk, cos, sin)
        q_ref[0, :, j * LANES:(j + 1) * LANES] = (xq * q_scale).astype(BF16)
        k_ref[0, :, j * LANES:(j + 1) * LANES] = xk.astype(BF16)
    v = z[:, POOL_WIDTH + 2 * qk_w:]
    vt_ref[0, 0] = v.T.astype(BF16)


def _pre0(h, g, sh, sc, w_in, cos, sin, *, use_rope, tm):
    b, l, d = h.shape
    n_in = w_in.shape[1]
    v_w = (n_in - POOL_WIDTH) // 3
    n_heads = v_w // DIFF_V_DIM
    kern = functools.partial(_pre0_kernel, use_rope=use_rope, n_heads=n_heads)
    return pl.pallas_call(
        kern,
        grid=(b, l // tm),
        in_specs=[pl.BlockSpec((1, tm, d), lambda b_, i: (b_, i, 0)),
                  pl.BlockSpec((1, d), lambda b_, i: (0, 0)),
                  _bvec_spec(sh, d), _bvec_spec(sc, d),
                  pl.BlockSpec((d, n_in), lambda b_, i: (0, 0)),
                  pl.BlockSpec((tm, LANES), lambda b_, i: (i, 0)),
                  pl.BlockSpec((tm, LANES), lambda b_, i: (i, 0))],
        out_specs=[pl.BlockSpec((1, tm, POOL_WIDTH), lambda b_, i: (b_, i, 0)),
                   pl.BlockSpec((1, tm, v_w), lambda b_, i: (b_, i, 0)),
                   pl.BlockSpec((1, tm, v_w), lambda b_, i: (b_, i, 0)),
                   pl.BlockSpec((1, 1, v_w, tm), lambda b_, i: (b_, i, 0, 0))],
        out_shape=[jax.ShapeDtypeStruct((b, l, POOL_WIDTH), F32),
                   jax.ShapeDtypeStruct((b, l, v_w), BF16),
                   jax.ShapeDtypeStruct((b, l, v_w), BF16),
                   jax.ShapeDtypeStruct((b, l // tm, v_w, tm), BF16)],
        compiler_params=_params(("parallel", "parallel")),
        name="pre0_rope" if use_rope else "pre0_ctx",
    )(h, g, sh, sc, w_in, cos, sin)


def _attn_core(qs, kc_ref, vc_ref, lat_refs, acc_refs, sbuf_refs):
    n = len(qs)
    qts = [q.astype(F32).T.astype(BF16) for q in qs]
    kc = kc_ref[0]
    vc = vc_ref[0, 0]
    ms, ls = [], []
    for i in range(n):
        s = _dot(kc, qts[i])
        m = jnp.max(s, axis=0, keepdims=True)
        p = jnp.exp2(s - m)
        ms.append(m)
        ls.append(jnp.sum(p, axis=0, keepdims=True))
        acc_refs[i][...] = _dot(vc, p.astype(BF16))
    if lat_refs is None:
        return ls
    kl_ref, vl_ref = lat_refs
    nblk, tk = vl_ref.shape[1], vl_ref.shape[3]
    assert nblk >= 2 and nblk % 2 == 0

    def qk(t, which):
        start = t * tk if isinstance(t, int) else pl.multiple_of(t * tk, tk)
        k = kl_ref[0, pl.ds(start, tk), :]
        cmax = []
        for i in range(n):
            s = _dot(k, qts[i])
            sbuf_refs[2 * i + which][...] = s
            cmax.append(jnp.max(s, axis=0, keepdims=True))
        return tuple(cmax)

    def softmax_pv(t, which, cmax, ms, ls):
        vt = vl_ref[0, t]
        new_m, new_l = [], []
        for i in range(n):
            m_new = jnp.maximum(ms[i], cmax[i])
            alpha = jnp.exp2(ms[i] - m_new)
            p = jnp.exp2(sbuf_refs[2 * i + which][...] - m_new)
            new_l.append(alpha * ls[i] + jnp.sum(p, axis=0, keepdims=True))
            acc_refs[i][...] = alpha * acc_refs[i][...] + _dot(vt, p.astype(BF16))
            new_m.append(m_new)
        return tuple(new_m), tuple(new_l)

    def pair(jj, carry):
        cmax_a, ms, ls = carry
        t = 2 * jj
        cmax_b = qk(t + 1, 1)
        ms, ls = softmax_pv(t, 0, cmax_a, ms, ls)
        cmax_a = qk(t + 2, 0)
        ms, ls = softmax_pv(t + 1, 1, cmax_b, ms, ls)
        return cmax_a, ms, ls

    cmax_a = qk(0, 0)
    cmax_a, ms, ls = lax.fori_loop(0, nblk // 2 - 1, pair, (cmax_a, tuple(ms), tuple(ls)),
                                   unroll=PAIR_UNROLL)
    cmax_b = qk(nblk - 1, 1)
    ms, ls = softmax_pv(nblk - 2, 0, cmax_a, ms, ls)
    ms, ls = softmax_pv(nblk - 1, 1, cmax_b, ms, ls)
    return ls


def _diff_attn_kernel(lq_ref, lk_ref, q_ref, kc_ref, vc_ref, *rest, has_lat, lam_init):
    lat_refs = rest[:2] if has_lat else None
    rest = rest[2:] if has_lat else rest
    o_ref, acc1_ref, acc2_ref = rest[:3]
    q = q_ref[0]
    lane = lax.broadcasted_iota(jnp.int32, q.shape, 1)
    zero = jnp.zeros_like(q)
    q1 = jnp.where(lane < DIFF_HEAD_DIM, q, zero)
    q2 = jnp.where(lane >= DIFF_HEAD_DIM, q, zero)
    l1, l2 = _attn_core([q1, q2], kc_ref, vc_ref, lat_refs, (acc1_ref, acc2_ref), rest[3:])
    e = jnp.exp(jnp.sum(lq_ref[...] * lk_ref[...], axis=-1, keepdims=True))
    lam = e[0:1] - e[1:2] + lam_init
    o = acc1_ref[...] * (1.0 / l1) - lam * (acc2_ref[...] * (1.0 / l2))
    ms = jnp.mean(o * o, axis=0, keepdims=True)
    y = o * (lax.rsqrt(ms + EPS) * (1.0 - lam_init))
    o_ref[0] = y.T.astype(BF16)


def _mla_attn_kernel(q_ref, kc_ref, vc_ref, *rest, has_lat):
    lat_refs = rest[:2] if has_lat else None
    rest = rest[2:] if has_lat else rest
    o_ref, acc_ref = rest[:2]
    (l,) = _attn_core([q_ref[0]], kc_ref, vc_ref, lat_refs, (acc_ref,), rest[2:])
    o_ref[0] = (acc_ref[...] * (1.0 / l)).T.astype(BF16)


def _attention(kern, n_maps, q, ctx_kv, lat_kv, qk_w, extra=(), *, tq, name):
    b, lq, hw = q.shape
    n_heads = hw // qk_w
    in_specs = [pl.BlockSpec(x.shape, lambda b_, h, i: (0, 0)) for x in extra]
    in_specs.append(pl.BlockSpec((1, tq, qk_w), lambda b_, h, i: (b_, i, h)))
    args = list(extra) + [q]
    scratch = [pltpu.VMEM((DIFF_V_DIM, tq), F32) for _ in range(n_maps)]
    for kv in (ctx_kv, lat_kv):
        if kv is None:
            continue
        k, vt = kv
        nblk, tk = vt.shape[1], vt.shape[3]
        in_specs.append(pl.BlockSpec((1, k.shape[1], qk_w), lambda b_, h, i: (b_, 0, h)))
        in_specs.append(pl.BlockSpec((1, nblk, DIFF_V_DIM, tk), lambda b_, h, i: (b_, 0, h, 0)))
        args += [k, vt]
    if lat_kv is not None:
        scratch += [pltpu.VMEM((lat_kv[1].shape[3], tq), F32) for _ in range(2 * n_maps)]
    return pl.pallas_call(
        functools.partial(kern, has_lat=lat_kv is not None),
        grid=(b, n_heads, lq // tq),
        in_specs=in_specs,
        out_specs=pl.BlockSpec((1, tq, DIFF_V_DIM), lambda b_, h, i: (b_, i, h)),
        out_shape=jax.ShapeDtypeStruct((b, lq, n_heads * DIFF_V_DIM), BF16),
        scratch_shapes=scratch,
        compiler_params=_params(("parallel", "parallel", "arbitrary")),
        name=name,
    )(*args)


def _post0_kernel(attn_ref, p_ref, pprev_ref, pnext_ref, poolw_ref, pscale_ref, wout_ref,
                  h_ref, g1_ref, n2g_ref, sh2_ref, sc2_ref, h1_ref, a2_ref, *, seq_len):
    i = pl.program_id(1)
    u = p_ref[0]
    tm = u.shape[0]
    ext = jnp.concatenate([pprev_ref[0], u, pnext_ref[0]], axis=0)
    row = i * tm - POOL_HALO + lax.broadcasted_iota(jnp.int32, (tm + 2 * POOL_HALO, 1), 0)
    ext = jnp.where((row >= 0) & (row < seq_len), ext, 0.0)

    def shifted(j):
        return ext[POOL_HALO + j:POOL_HALO + j + tm]

    t = i * tm + lax.broadcasted_iota(jnp.int32, (tm, 1), 0)
    lane = lax.broadcasted_iota(jnp.int32, (tm, POOL_WIDTH), 1)
    win_sum = shifted(-1) + shifted(0)
    pooled = jnp.zeros_like(u)
    for g, w in enumerate(POOL_WINDOWS):
        half = w // 2
        if g > 0:
            prev_half = POOL_WINDOWS[g - 1] // 2
            for j in range(prev_half, half):
                win_sum = win_sum + shifted(-j - 1) + shifted(j)
        cnt = (jnp.minimum(t + half, seq_len) - jnp.maximum(t - half, 0)).astype(F32)
        in_group = (lane >= g * POOL_GROUP) & (lane < (g + 1) * POOL_GROUP)
        pooled = jnp.where(in_group, win_sum / cnt, pooled)
    pooled = pooled - u
    y_pool = _dot(pooled.astype(BF16), poolw_ref[...]) * pscale_ref[...]
    y = _dot(y_pool.astype(BF16), wout_ref[:POOL_WIDTH, :]) + _dot(attn_ref[0], wout_ref[POOL_WIDTH:, :])
    h1 = h_ref[0] + g1_ref[0] * y
    h1_ref[0] = h1
    a2_ref[0] = _norm_mod(h1, n2g_ref[...], sh2_ref[0], sc2_ref[0]).astype(BF16)


def _post0(attn, p, poolw_bd, pscale, w_out, h, g1, n2g, sh2, sc2, *, tm):
    b, l, d = h.shape
    hb = tm // POOL_HALO
    nhb = l // POOL_HALO
    return pl.pallas_call(
        functools.partial(_post0_kernel, seq_len=l),
        grid=(b, l // tm),
        in_specs=[pl.BlockSpec((1, tm, attn.shape[2]), lambda b_, i: (b_, i, 0)),
                  pl.BlockSpec((1, tm, POOL_WIDTH), lambda b_, i: (b_, i, 0)),
                  pl.BlockSpec((1, POOL_HALO, POOL_WIDTH),
                               lambda b_, i: (b_, jnp.maximum(i * hb - 1, 0), 0)),
                  pl.BlockSpec((1, POOL_HALO, POOL_WIDTH),
                               lambda b_, i: (b_, jnp.minimum((i + 1) * hb, nhb - 1), 0)),
                  pl.BlockSpec(poolw_bd.shape, lambda b_, i: (0, 0)),
                  pl.BlockSpec((1, POOL_WIDTH), lambda b_, i: (0, 0)),
                  pl.BlockSpec(w_out.shape, lambda b_, i: (0, 0)),
                  pl.BlockSpec((1, tm, d), lambda b_, i: (b_, i, 0)),
                  _bvec_spec(g1, d),
                  pl.BlockSpec((1, d), lambda b_, i: (0, 0)),
                  _bvec_spec(sh2, d), _bvec_spec(sc2, d)],
        out_specs=[pl.BlockSpec((1, tm, d), lambda b_, i: (b_, i, 0)),
                   pl.BlockSpec((1, tm, d), lambda b_, i: (b_, i, 0))],
        out_shape=[jax.ShapeDtypeStruct((b, l, d), F32),
                   jax.ShapeDtypeStruct((b, l, d), BF16)],
        compiler_params=_params(("parallel", "parallel")),
        name="post0",
    )(attn, p, p, p, poolw_bd, pscale, w_out, h, g1, n2g, sh2, sc2)


def _swiglu_tile(x, wg_ref, wu_ref, wd_ref, lead=()):
    acc = None
    c0 = 0
    for cw in FFN_CHUNKS:
        hg = _dot(x, wg_ref[lead + (slice(None), slice(c0, c0 + cw))])
        hu = _dot(x, wu_ref[lead + (slice(None), slice(c0, c0 + cw))])
        act = (_silu(hg) * hu).astype(BF16)
        part = _dot(act, wd_ref[lead + (slice(c0, c0 + cw), slice(None))])
        acc = part if acc is None else acc + part
        c0 += cw
    return acc


def _ffn_kernel(a2_ref, h1_ref, g2_ref, wg_ref, wu_ref, wd_ref, o_ref):
    y = _swiglu_tile(a2_ref[0], wg_ref, wu_ref, wd_ref)
    o_ref[0] = h1_ref[0] + g2_ref[0] * y


def _ffn(a2, h1, g2, wg, wu, wd, *, tm):
    b, l, d = h1.shape
    f = wg.shape[1]
    assert sum(FFN_CHUNKS) == f
    return pl.pallas_call(
        _ffn_kernel,
        grid=(b, l // tm),
        in_specs=[pl.BlockSpec((1, tm, d), lambda b_, i: (b_, i, 0)),
                  pl.BlockSpec((1, tm, d), lambda b_, i: (b_, i, 0)),
                  _bvec_spec(g2, d),
                  pl.BlockSpec((d, f), lambda b_, i: (0, 0)),
                  pl.BlockSpec((d, f), lambda b_, i: (0, 0)),
                  pl.BlockSpec((f, d), lambda b_, i: (0, 0))],
        out_specs=pl.BlockSpec((1, tm, d), lambda b_, i: (b_, i, 0)),
        out_shape=jax.ShapeDtypeStruct((b, l, d), F32),
        compiler_params=_params(("parallel", "parallel")),
        name="ffn_dense",
    )(a2, h1, g2, wg, wu, wd)


def _pre1_kernel(h_ref, g_ref, sh_ref, sc_ref, win_ref, qg_ref, kvg_ref, wuq_ref, wuk_ref, wuv_ref,
                 cos_ref, sin_ref, *out_refs, use_rope, want_q, q_rank, kv_rank, n_heads):
    a = _norm_mod(h_ref[0], g_ref[...], sh_ref[0], sc_ref[0])
    z = _dot(a.astype(BF16), win_ref[...])
    cos = cos_ref[...]
    sin = sin_ref[...]
    if want_q:
        q_ref, k_ref, vt_ref = out_refs
        cq = _rms(z[:, :q_rank]) * qg_ref[...]
        qf = _dot(cq.astype(BF16), wuq_ref[...])
        q_scale = (MLA_NOPE + MLA_ROPE) ** -0.5 * LOG2E
        for h in range(n_heads):
            c0 = h * MLA_QK_PAD
            q_ref[0, :, c0:c0 + MLA_NOPE] = (qf[:, c0:c0 + MLA_NOPE] * q_scale).astype(BF16)
            qr = _rope(qf[:, c0 + MLA_NOPE:c0 + MLA_QK_PAD], cos, sin)
            q_ref[0, :, c0 + MLA_NOPE:c0 + MLA_QK_PAD] = (qr * q_scale).astype(BF16)
    else:
        k_ref, vt_ref = out_refs
    ckv = (_rms(z[:, q_rank:q_rank + kv_rank]) * kvg_ref[...]).astype(BF16)
    kn = _dot(ckv, wuk_ref[...])
    vv = _dot(ckv, wuv_ref[...])
    kr = z[:, q_rank + kv_rank:]
    if use_rope:
        kr = _rope(kr, cos, sin)
    kr = kr.astype(BF16)
    for h in range(n_heads):
        c0 = h * MLA_QK_PAD
        k_ref[0, :, c0:c0 + MLA_NOPE] = kn[:, h * MLA_NOPE:(h + 1) * MLA_NOPE].astype(BF16)
        k_ref[0, :, c0 + MLA_NOPE:c0 + MLA_QK_PAD] = kr
    vt_ref[0, 0] = vv.T.astype(BF16)


def _pre1(h, g, sh, sc, w_in, qg, kvg, wuq, wuk, wuv, cos, sin, *, use_rope, want_q, tm):
    b, l, d = h.shape
    n_heads = wuk.shape[1] // MLA_NOPE
    q_rank, kv_rank = wuq.shape[0], wuk.shape[0]
    kern = functools.partial(_pre1_kernel, use_rope=use_rope, want_q=want_q,
                             q_rank=q_rank, kv_rank=kv_rank, n_heads=n_heads)
    full = lambda x: pl.BlockSpec(x.shape, lambda b_, i: (0,) * x.ndim)
    qk_w = n_heads * MLA_QK_PAD
    v_w = n_heads * MLA_V
    out_specs = [pl.BlockSpec((1, tm, qk_w), lambda b_, i: (b_, i, 0)),
                 pl.BlockSpec((1, 1, v_w, tm), lambda b_, i: (b_, i, 0, 0))]
    out_shape = [jax.ShapeDtypeStruct((b, l, qk_w), BF16),
                 jax.ShapeDtypeStruct((b, l // tm, v_w, tm), BF16)]
    if want_q:
        out_specs = [pl.BlockSpec((1, tm, qk_w), lambda b_, i: (b_, i, 0))] + out_specs
        out_shape = [jax.ShapeDtypeStruct((b, l, qk_w), BF16)] + out_shape
    return pl.pallas_call(
        kern,
        grid=(b, l // tm),
        in_specs=[pl.BlockSpec((1, tm, d), lambda b_, i: (b_, i, 0)),
                  pl.BlockSpec((1, d), lambda b_, i: (0, 0)),
                  _bvec_spec(sh, d), _bvec_spec(sc, d),
                  full(w_in), full(qg), full(kvg), full(wuq), full(wuk), full(wuv),
                  pl.BlockSpec((tm, LANES), lambda b_, i: (i, 0)),
                  pl.BlockSpec((tm, LANES), lambda b_, i: (i, 0))],
        out_specs=out_specs,
        out_shape=out_shape,
        compiler_params=_params(("parallel", "parallel")),
        name="pre1_lat" if want_q else "pre1_ctx",
    )(h, g, sh, sc, w_in, qg, kvg, wuq, wuk, wuv, cos, sin)


def _post1_kernel(o_ref, wout_ref, h_ref, g1_ref, n2g_ref, sh2_ref, sc2_ref, rw_ref,
                  h1_ref, a2_ref, gates_ref, sel_ref):
    y = _dot(o_ref[0], wout_ref[...])
    h1 = h_ref[0] + g1_ref[0] * y
    h1_ref[0] = h1
    a2 = _norm_mod(h1, n2g_ref[...], sh2_ref[0], sc2_ref[0])
    a2_ref[0] = a2
    logits = jnp.dot(a2, rw_ref[...], preferred_element_type=F32,
                     precision=lax.Precision.HIGHEST)
    lane = lax.broadcasted_iota(jnp.int32, logits.shape, 1)
    lg = jnp.where(lane < N_EXPERTS, logits, -jnp.inf)
    m1 = jnp.max(lg, axis=1, keepdims=True)
    i1 = jnp.min(jnp.where(lg == m1, lane, LANES), axis=1, keepdims=True)
    lg2 = jnp.where(lane == i1, -jnp.inf, lg)
    m2 = jnp.max(lg2, axis=1, keepdims=True)
    i2 = jnp.min(jnp.where(lg2 == m2, lane, LANES), axis=1, keepdims=True)
    p2 = jnp.exp(m2 - m1)
    w1 = 1.0 / (1.0 + p2)
    gates_ref[0] = jnp.where(lane == i1, w1, 0.0) + jnp.where(lane == i2, p2 * w1, 0.0)
    sel_ref[0] = jnp.where((lane == i1) | (lane == i2), 1.0, 0.0).astype(BF16)


def _post1(o, w_out, h, g1, n2g, sh2, sc2, rw_pad, *, tm):
    b, l, d = h.shape
    return pl.pallas_call(
        _post1_kernel,
        grid=(b, l // tm),
        in_specs=[pl.BlockSpec((1, tm, o.shape[2]), lambda b_, i: (b_, i, 0)),
                  pl.BlockSpec(w_out.shape, lambda b_, i: (0, 0)),
                  pl.BlockSpec((1, tm, d), lambda b_, i: (b_, i, 0)),
                  _bvec_spec(g1, d),
                  pl.BlockSpec((1, d), lambda b_, i: (0, 0)),
                  _bvec_spec(sh2, d), _bvec_spec(sc2, d),
                  pl.BlockSpec(rw_pad.shape, lambda b_, i: (0, 0))],
        out_specs=[pl.BlockSpec((1, tm, d), lambda b_, i: (b_, i, 0)),
                   pl.BlockSpec((1, tm, d), lambda b_, i: (b_, i, 0)),
                   pl.BlockSpec((1, tm, LANES), lambda b_, i: (b_, i, 0)),
                   pl.BlockSpec((1, tm, LANES), lambda b_, i: (b_, i, 0))],
        out_shape=[jax.ShapeDtypeStruct((b, l, d), F32),
                   jax.ShapeDtypeStruct((b, l, d), F32),
                   jax.ShapeDtypeStruct((b, l, LANES), F32),
                   jax.ShapeDtypeStruct((b, l, LANES), BF16)],
        compiler_params=_params(("parallel", "parallel")),
        name="post1_router",
    )(o, w_out, h, g1, n2g, sh2, sc2, rw_pad)


def _route_kernel(sel_ref, gates_ref, r_ref, tile_ref, cnt_ref, off_ref, *, tile_rows):
    phase = pl.program_id(0)
    i = pl.program_id(1)
    sel = sel_ref[...]
    tr = sel.shape[0]
    col_count = jnp.sum(sel.astype(F32), axis=0, keepdims=True)
    lane = lax.broadcasted_iota(jnp.int32, (1, LANES), 1)

    @pl.when((phase == 0) & (i == 0))
    def _():
        cnt_ref[...] = jnp.zeros_like(cnt_ref)

    @pl.when(phase == 0)
    def _():
        cnt_ref[...] += col_count

    @pl.when((phase == 0) & (i == pl.num_programs(1) - 1))
    def _():
        padded = jnp.ceil(cnt_ref[...] / tile_rows) * tile_rows
        incl = padded
        for sh in (1, 2, 4):
            incl = incl + jnp.where(lane >= sh, pltpu.roll(incl, sh, 1), 0.0)
        off_ref[...] = incl - padded
        tile_start = (lax.broadcasted_iota(jnp.int32, tile_ref.shape, 1) * tile_rows).astype(F32)
        tile_expert = jnp.zeros(tile_ref.shape, jnp.int32)
        for e in range(N_EXPERTS):
            end_e = jnp.sum(jnp.where(lane == e, incl, 0.0), axis=1, keepdims=True)
            tile_expert = tile_expert + (tile_start >= end_e).astype(jnp.int32)
        tile_ref[...] = tile_expert
        cnt_ref[...] = jnp.zeros_like(cnt_ref)

    @pl.when(phase == 1)
    def _():
        rr = lax.broadcasted_iota(jnp.int32, (tr, tr), 0)
        cc = lax.broadcasted_iota(jnp.int32, (tr, tr), 1)
        earlier = jnp.where(cc < rr, 1.0, 0.0).astype(BF16)
        rank = _dot(earlier, sel) + cnt_ref[...]
        cnt_ref[...] += col_count
        pos = off_ref[...] + rank
        chosen = sel > 0
        lo = jnp.min(jnp.where(chosen, pos, 3e38), axis=1, keepdims=True)
        hi = jnp.max(jnp.where(chosen, pos, -1.0), axis=1, keepdims=True)
        g = gates_ref[...]
        w_lo = jnp.sum(jnp.where(chosen & (pos == lo), g, 0.0), axis=1, keepdims=True)
        w_hi = jnp.sum(jnp.where(chosen & (pos == hi), g, 0.0), axis=1, keepdims=True)
        lane_t = lax.broadcasted_iota(jnp.int32, (tr, LANES), 1)
        r_ref[...] = jnp.where(lane_t == 0, lo, jnp.where(lane_t == 1, hi,
                               jnp.where(lane_t == 2, w_lo, jnp.where(lane_t == 3, w_hi, 0.0))))


def _route(sel, gates, *, tr, tile_rows, n_tiles):
    n = sel.shape[0]
    n_tiles_pad = -(-n_tiles // LANES) * LANES
    return pl.pallas_call(
        functools.partial(_route_kernel, tile_rows=tile_rows),
        grid=(2, n // tr),
        in_specs=[pl.BlockSpec((tr, LANES), lambda ph, i: (i, 0)),
                  pl.BlockSpec((tr, LANES), lambda ph, i: (i, 0))],
        out_specs=[pl.BlockSpec((tr, LANES), lambda ph, i: (i * ph, 0)),
                   pl.BlockSpec((1, n_tiles_pad), lambda ph, i: (0, 0))],
        out_shape=[jax.ShapeDtypeStruct((n, LANES), F32),
                   jax.ShapeDtypeStruct((1, n_tiles_pad), jnp.int32)],
        scratch_shapes=[pltpu.VMEM((1, LANES), F32), pltpu.VMEM((1, LANES), F32)],
        compiler_params=_params(("arbitrary", "arbitrary")),
        name="moe_route",
    )(sel, gates)


def _dispatch_kernel(dest_ref, x_ref, xs_in_hbm, xs_hbm, sem, *, tm):
    del xs_in_hbm

    def row_copy(src_row, dst_row):
        return pltpu.make_async_copy(x_ref.at[pl.ds(src_row, 1)], xs_hbm.at[pl.ds(dst_row, 1)], sem)

    def issue(r, carry):
        row_copy(r, dest_ref[0, 0, r]).start()
        row_copy(r, dest_ref[0, 0, tm + r]).start()
        return carry

    lax.fori_loop(0, tm, issue, 0)
    pltpu.make_async_copy(x_ref, xs_hbm.at[pl.ds(0, tm)], sem).wait()
    pltpu.make_async_copy(x_ref, xs_hbm.at[pl.ds(0, tm)], sem).wait()


def _dispatch(dest, x, n_rows, *, tm):
    n = x.shape[0]
    xs0 = jnp.zeros((n_rows,) + x.shape[1:], x.dtype)
    return pl.pallas_call(
        functools.partial(_dispatch_kernel, tm=tm),
        grid=(n // tm,),
        in_specs=[pl.BlockSpec((1, 1, 2 * tm), lambda i: (i, 0, 0), memory_space=pltpu.SMEM),
                  pl.BlockSpec((tm, x.shape[1]), lambda i: (i, 0)),
                  pl.BlockSpec(memory_space=pl.ANY)],
        out_specs=pl.BlockSpec(memory_space=pl.ANY),
        out_shape=jax.ShapeDtypeStruct(xs0.shape, xs0.dtype),
        scratch_shapes=[pltpu.SemaphoreType.DMA(())],
        input_output_aliases={2: 0},
        compiler_params=pltpu.CompilerParams(dimension_semantics=("arbitrary",),
                                             has_side_effects=True),
        name="moe_dispatch",
    )(dest, x, xs0)


def _experts_kernel(te_ref, x_ref, wg_ref, wu_ref, wd_ref, o_ref):
    valid = te_ref[pl.program_id(0)] < N_EXPERTS

    @pl.when(valid)
    def _():
        o_ref[...] = _swiglu_tile(x_ref[...].astype(BF16), wg_ref, wu_ref, wd_ref, lead=(0,))

    @pl.when(jnp.logical_not(valid))
    def _():
        o_ref[...] = jnp.zeros_like(o_ref)


def _experts(tile_expert, xs, wg, wu, wd, *, tile_rows):
    n_rows, d = xs.shape
    ne, _, f = wg.shape
    w_idx = lambda j, te: (jnp.minimum(te[j], ne - 1), 0, 0)
    return pl.pallas_call(
        _experts_kernel,
        grid_spec=pltpu.PrefetchScalarGridSpec(
            num_scalar_prefetch=1,
            grid=(n_rows // tile_rows,),
            in_specs=[pl.BlockSpec((tile_rows, d), lambda j, te: (j, 0)),
                      pl.BlockSpec((1, d, f), w_idx),
                      pl.BlockSpec((1, d, f), w_idx),
                      pl.BlockSpec((1, f, d), w_idx)],
            out_specs=pl.BlockSpec((tile_rows, d), lambda j, te: (j, 0))),
        out_shape=jax.ShapeDtypeStruct((n_rows, d), F32),
        compiler_params=_params(("arbitrary",)),
        name="moe_experts",
    )(tile_expert, xs, wg, wu, wd)


def _combine_kernel(dest_ref, routed_ref, ys_hbm, h1_ref, g2_ref, fng_ref, o_ref, ylo_ref, yhi_ref,
                    sem, *, tm):
    def issue(r, carry):
        pltpu.make_async_copy(ys_hbm.at[pl.ds(dest_ref[0, 0, r], 1)],
                              ylo_ref.at[pl.ds(r, 1)], sem).start()
        pltpu.make_async_copy(ys_hbm.at[pl.ds(dest_ref[0, 0, tm + r], 1)],
                              yhi_ref.at[pl.ds(r, 1)], sem).start()
        return carry

    lax.fori_loop(0, tm, issue, 0)
    pltpu.make_async_copy(ys_hbm.at[pl.ds(0, tm)], ylo_ref, sem).wait()
    pltpu.make_async_copy(ys_hbm.at[pl.ds(0, tm)], yhi_ref, sem).wait()
    routed = routed_ref[...]
    y = routed[:, 2:3] * ylo_ref[...] + routed[:, 3:4] * yhi_ref[...]
    out = h1_ref[...] + g2_ref[0] * y
    o_ref[...] = _rms(out) * fng_ref[...]


def _combine(dest, routed, ys, h1, g2, fng, *, tm, tiles_per_sample):
    n, d = h1.shape
    return pl.pallas_call(
        functools.partial(_combine_kernel, tm=tm),
        grid=(n // tm,),
        in_specs=[pl.BlockSpec((1, 1, 2 * tm), lambda i: (i, 0, 0), memory_space=pltpu.SMEM),
                  pl.BlockSpec((tm, LANES), lambda i: (i, 0)),
                  pl.BlockSpec(memory_space=pl.ANY),
                  pl.BlockSpec((tm, d), lambda i: (i, 0)),
                  pl.BlockSpec((1, 1, d), lambda i: (i // tiles_per_sample, 0, 0)),
                  pl.BlockSpec((1, d), lambda i: (0, 0))],
        out_specs=pl.BlockSpec((tm, d), lambda i: (i, 0)),
        out_shape=jax.ShapeDtypeStruct((n, d), F32),
        scratch_shapes=[pltpu.VMEM((tm, d), F32), pltpu.VMEM((tm, d), F32),
                        pltpu.SemaphoreType.DMA(())],
        compiler_params=_params(("arbitrary",)),
        name="moe_combine",
    )(dest, routed, ys, h1, g2, fng)


def _moe(a2, gates, sel, wg, wu, wd, h1, g2, fng, *, tm):
    b, l, d = h1.shape
    n = b * l
    n_rows = 2 * n + N_EXPERTS * MOE_TILE
    n_tiles = n_rows // MOE_TILE
    routed, tile_expert = _route(sel.reshape(n, LANES), gates.reshape(n, LANES),
                                 tr=tm, tile_rows=MOE_TILE, n_tiles=n_tiles)
    dest = routed[:, 0:2].astype(jnp.int32).reshape(n // tm, tm, 2)
    dest = dest.transpose(0, 2, 1).reshape(n // tm, 1, 2 * tm)
    xs = _dispatch(dest, a2.reshape(n, d), n_rows, tm=tm)
    ys = _experts(tile_expert[0, :n_tiles], xs, wg, wu, wd, tile_rows=MOE_TILE)
    out = _combine(dest, routed, ys, h1.reshape(n, d), g2, fng, tm=tm, tiles_per_sample=l // tm)
    return out.reshape(b, l, d)


def _rope_tables(seq_len, dim):
    t = jnp.arange(seq_len, dtype=jnp.int32)
    row = (t // GRID_W).astype(F32)
    col = (t % GRID_W).astype(F32)
    quarter = dim // 4
    inv = ROPE_BASE ** (-jnp.arange(quarter, dtype=F32) / quarter)
    ar = row[:, None] * inv[None, :]
    ac = col[:, None] * inv[None, :]
    cos = jnp.concatenate([jnp.cos(ar), jnp.cos(ar), jnp.cos(ac), jnp.cos(ac)], axis=-1)
    sin = jnp.concatenate([jnp.sin(ar), jnp.sin(ar), jnp.sin(ac), jnp.sin(ac)], axis=-1)
    reps = LANES // dim
    return jnp.tile(cos, (1, reps)), jnp.tile(sin, (1, reps))


def kernel(x, c, ctx, c_ctx, norm1_g, norm2_g, mod_w, mod_b, even_w_in, pool_w, pool_scale,
           lambda_q1, lambda_k1, lambda_q2, lambda_k2, even_w_out, ffn_w_gate, ffn_w_up,
           ffn_w_down, odd_w_in, q_norm_g, kv_norm_g, w_uq, w_ukv, odd_w_out, router_w,
           moe_w_gate, moe_w_up, moe_w_down, final_norm_g):
    b, l, d = x.shape
    n_ctx = ctx.shape[1]
    tm = min(ROW_TILE, l)
    tq = min(DIFF_Q_TILE, l)
    assert tm == KV_TILE or l < KV_TILE
    cos, sin = _rope_tables(l, DIFF_HEAD_DIM)
    cos_c, sin_c = cos[:n_ctx], sin[:n_ctx]

    cond = jnp.zeros((8, d), F32).at[:b].set(c).at[b].set(c_ctx)
    mod = _modulation(cond, mod_w, mod_b)
    mod = mod.reshape(mod.shape[0], 8, N_MOD, d)

    def mod_vecs(layer):
        lat = [mod[layer, :b, k][:, None, :] for k in range(N_MOD)]
        cx = [mod[layer, b:b + 1, k][:, None, :] for k in range(N_MOD)]
        return lat, cx

    (sh1, sc1, g1, sh2, sc2, g2), (csh1, csc1, cg1, csh2, csc2, cg2) = mod_vecs(0)
    lam_init = 0.8 - 0.6 * math.exp(-0.3 * 0)
    w_in0 = even_w_in[0].astype(BF16)
    n1g = norm1_g[0][None, :]
    n2g = norm2_g[0][None, :]
    p_lat, q_lat, k_lat, vt_lat = _pre0(x, n1g, sh1, sc1, w_in0, cos, sin, use_rope=True, tm=tm)
    p_ctx, q_ctx, k_ctx, vt_ctx = _pre0(ctx, n1g, csh1, csc1, w_in0, cos_c, sin_c,
                                        use_rope=False, tm=n_ctx)
    lq = jnp.stack([lambda_q1[0], lambda_q2[0]])
    lk = jnp.stack([lambda_k1[0], lambda_k2[0]])
    diff_kern = functools.partial(_diff_attn_kernel, lam_init=lam_init)
    attn_lat = _attention(diff_kern, 2, q_lat, (k_ctx, vt_ctx), (k_lat, vt_lat), 2 * DIFF_HEAD_DIM,
                          extra=(lq, lk), tq=tq, name="diff_attn_lat")
    attn_ctx = _attention(diff_kern, 2, q_ctx, (k_ctx, vt_ctx), None, 2 * DIFF_HEAD_DIM,
                          extra=(lq, lk), tq=n_ctx, name="diff_attn_ctx")
    ng = len(POOL_WINDOWS)
    poolw_bd = jnp.zeros((POOL_WIDTH, POOL_WIDTH), F32)
    for g in range(ng):
        sl = slice(g * POOL_GROUP, (g + 1) * POOL_GROUP)
        poolw_bd = poolw_bd.at[sl, sl].set(pool_w[0, g])
    poolw_bd = poolw_bd.astype(BF16)
    pscale = pool_scale[0][None, :]
    w_out0 = even_w_out[0].astype(BF16)
    wg0, wu0, wd0 = ffn_w_gate[0].astype(BF16), ffn_w_up[0].astype(BF16), ffn_w_down[0].astype(BF16)
    h1_lat, a2_lat = _post0(attn_lat, p_lat, poolw_bd, pscale, w_out0, x, g1, n2g, sh2, sc2, tm=tm)
    h_lat = _ffn(a2_lat, h1_lat, g2, wg0, wu0, wd0, tm=tm)
    h1_ctx, a2_ctx = _post0(attn_ctx, p_ctx, poolw_bd, pscale, w_out0, ctx, cg1, n2g, csh2, csc2,
                            tm=n_ctx)
    h_ctx = _ffn(a2_ctx, h1_ctx, cg2, wg0, wu0, wd0, tm=n_ctx)

    (sh1, sc1, g1, sh2, sc2, g2), (csh1, csc1, _, _, _, _) = mod_vecs(1)
    n1g = norm1_g[1][None, :]
    n2g = norm2_g[1][None, :]
    q_rank, kv_rank = w_uq.shape[1], w_ukv.shape[1]
    n_heads = w_uq.shape[2] // (MLA_NOPE + MLA_ROPE)
    w_in1 = jnp.pad(odd_w_in[0], ((0, 0), (0, LANES - MLA_ROPE))).astype(BF16)
    wuq = w_uq[0].reshape(q_rank, n_heads, MLA_NOPE + MLA_ROPE)
    wuq = jnp.pad(wuq, ((0, 0), (0, 0), (0, MLA_QK_PAD - MLA_NOPE - MLA_ROPE)))
    wuq = wuq.reshape(q_rank, n_heads * MLA_QK_PAD).astype(BF16)
    wukv = w_ukv[0].reshape(kv_rank, n_heads, MLA_NOPE + MLA_V)
    wuk = wukv[:, :, :MLA_NOPE].reshape(kv_rank, n_heads * MLA_NOPE).astype(BF16)
    wuv = wukv[:, :, MLA_NOPE:].reshape(kv_rank, n_heads * MLA_V).astype(BF16)
    qg = q_norm_g[0][None, :]
    kvg = kv_norm_g[0][None, :]
    q1, k1, vt1 = _pre1(h_lat, n1g, sh1, sc1, w_in1, qg, kvg, wuq, wuk, wuv, cos, sin,
                        use_rope=True, want_q=True, tm=tm)
    k1c, vt1c = _pre1(h_ctx, n1g, csh1, csc1, w_in1, qg, kvg, wuq, wuk, wuv, cos_c, sin_c,
                      use_rope=False, want_q=False, tm=n_ctx)
    o1 = _attention(_mla_attn_kernel, 1, q1, (k1c, vt1c), (k1, vt1), MLA_QK_PAD,
                    tq=min(MLA_Q_TILE, l), name="mla_attn")
    rw_pad = jnp.pad(router_w[0], ((0, 0), (0, LANES - N_EXPERTS)))
    h1, a2, gates, sel = _post1(o1, odd_w_out[0].astype(BF16), h_lat, g1, n2g, sh2, sc2, rw_pad,
                                tm=tm)
    return _moe(a2, gates, sel, moe_w_gate[0].astype(BF16), moe_w_up[0].astype(BF16),
                moe_w_down[0].astype(BF16), h1, g2, final_norm_g[None, :], tm=tm)
```

```python
import functools
import math

import jax
import jax.numpy as jnp
from jax import lax
from jax.experimental import pallas as pl
from jax.experimental.pallas import tpu as pltpu

F32 = jnp.float32
BF16 = jnp.bfloat16

EPS = 1e-6
GRID_W = 64
ROPE_BASE = 10000.0
N_MOD = 6
POOL_WINDOWS = (2, 4, 8, 16)
POOL_GROUP = 64
POOL_WIDTH = POOL_GROUP * len(POOL_WINDOWS)
POOL_HALO = max(POOL_WINDOWS) // 2
DIFF_HEAD_DIM = 64
DIFF_V_DIM = 128
V_ONES_ROWS = 16
V_AUG = DIFF_V_DIM + V_ONES_ROWS
MLA_NOPE = 128
MLA_ROPE = 64
MLA_V = 128
MLA_QK_PAD = 256
N_EXPERTS = 8
ROPE_QUARTER = 16
LOG2E = math.log2(math.e)

LANES = 128
ROW_TILE = 512
KV_TILE = 512
DIFF_Q_TILE = 512
MLA_Q_TILE = 1024
PAIR_UNROLL = 5
MOE_TILE = 512
DMA_ISSUE_UNROLL = 8
FFN_CHUNKS = (512, 512, 512, 512, 512, 256)
VMEM_LIMIT = 56 * 1024 * 1024


def _params(sem):
    return pltpu.CompilerParams(dimension_semantics=sem, vmem_limit_bytes=VMEM_LIMIT)


def _dot(a, b):
    return jnp.dot(a, b, preferred_element_type=F32)


def _rms(x):
    return x * lax.rsqrt(jnp.mean(x * x, axis=-1, keepdims=True) + EPS)


def _silu(x):
    return x * (1.0 / (1.0 + jnp.exp(-x)))


def _rope(x, cos, sin):
    lane = lax.broadcasted_iota(jnp.int32, x.shape, 1)
    even = (lane // ROPE_QUARTER) % 2 == 0
    rot = jnp.where(even, -pltpu.roll(x, LANES - ROPE_QUARTER, 1), pltpu.roll(x, ROPE_QUARTER, 1))
    return x * cos + rot * sin


def _store_vt(vt_ref, v, n_heads):
    tm = v.shape[0]
    ones = jnp.ones((V_ONES_ROWS, tm), BF16)
    for h in range(n_heads):
        r0 = h * V_AUG
        vt_ref[0, 0, r0:r0 + DIFF_V_DIM, :] = v[:, h * DIFF_V_DIM:(h + 1) * DIFF_V_DIM].T.astype(BF16)
        vt_ref[0, 0, r0 + DIFF_V_DIM:r0 + V_AUG, :] = ones


def _mod_kernel(cond_ref, w_ref, b_ref, o_ref):
    s = _silu(cond_ref[...])
    o_ref[0] = jnp.dot(s, w_ref[0], preferred_element_type=F32,
                       precision=lax.Precision.HIGHEST) + b_ref[0]


def _modulation(cond, mod_w, mod_b):
    depth, d, n = mod_w.shape
    tn = n // 4
    return pl.pallas_call(
        _mod_kernel,
        grid=(depth, n // tn),
        in_specs=[pl.BlockSpec((8, d), lambda i, j: (0, 0)),
                  pl.BlockSpec((1, d, tn), lambda i, j: (i, 0, j)),
                  pl.BlockSpec((1, 1, tn), lambda i, j: (i, 0, j))],
        out_specs=pl.BlockSpec((1, 8, tn), lambda i, j: (i, 0, j)),
        out_shape=jax.ShapeDtypeStruct((depth, 8, n), F32),
        compiler_params=_params(("arbitrary", "arbitrary")),
        name="modulation",
    )(cond, mod_w, mod_b.reshape(depth, 1, n))


def _bvec_spec(arr, d):
    if arr.shape[0] == 1:
        return pl.BlockSpec((1, 1, d), lambda b, i: (0, 0, 0))
    return pl.BlockSpec((1, 1, d), lambda b, i: (b, 0, 0))


def _norm_mod(h, g, sh, sc):
    a = _rms(h) * g
    return a * (1.0 + sc) + sh


def _pre0_kernel(h_ref, g_ref, sh_ref, sc_ref, w_ref, cos_ref, sin_ref,
                 p_ref, q_ref, k_ref, vt_ref, *, use_rope, n_heads):
    a = _norm_mod(h_ref[0], g_ref[...], sh_ref[0], sc_ref[0])
    z = _dot(a.astype(BF16), w_ref[...])
    p_ref[0] = z[:, :POOL_WIDTH]
    qk_w = n_heads * 2 * DIFF_HEAD_DIM
    q_scale = DIFF_HEAD_DIM ** -0.5 * LOG2E
    cos = cos_ref[...]
    sin = sin_ref[...]
    for j in range(qk_w // LANES):
        xq = z[:, POOL_WIDTH + j * LANES:POOL_WIDTH + (j + 1) * LANES]
        xk = z[:, POOL_WIDTH + qk_w + j * LANES:POOL_WIDTH + qk_w + (j + 1) * LANES]
        if use_rope:
            xq = _rope(xq, cos, sin)
            xk = _rope(xk, cos, sin)
        q_ref[0, :, j * LANES:(j + 1) * LANES] = (xq * q_scale).astype(BF16)
        k_ref[0, :, j * LANES:(j + 1) * LANES] = xk.astype(BF16)
    _store_vt(vt_ref, z[:, POOL_WIDTH + 2 * qk_w:], n_heads)


def _pre0(h, g, sh, sc, w_in, cos, sin, *, use_rope, tm):
    b, l, d = h.shape
    n_in = w_in.shape[1]
    v_w = (n_in - POOL_WIDTH) // 3
    n_heads = v_w // DIFF_V_DIM
    kern = functools.partial(_pre0_kernel, use_rope=use_rope, n_heads=n_heads)
    return pl.pallas_call(
        kern,
        grid=(b, l // tm),
        in_specs=[pl.BlockSpec((1, tm, d), lambda b_, i: (b_, i, 0)),
                  pl.BlockSpec((1, d), lambda b_, i: (0, 0)),
                  _bvec_spec(sh, d), _bvec_spec(sc, d),
                  pl.BlockSpec((d, n_in), lambda b_, i: (0, 0)),
                  pl.BlockSpec((tm, LANES), lambda b_, i: (i, 0)),
                  pl.BlockSpec((tm, LANES), lambda b_, i: (i, 0))],
        out_specs=[pl.BlockSpec((1, tm, POOL_WIDTH), lambda b_, i: (b_, i, 0)),
                   pl.BlockSpec((1, tm, v_w), lambda b_, i: (b_, i, 0)),
                   pl.BlockSpec((1, tm, v_w), lambda b_, i: (b_, i, 0)),
                   pl.BlockSpec((1, 1, n_heads * V_AUG, tm), lambda b_, i: (b_, i, 0, 0))],
        out_shape=[jax.ShapeDtypeStruct((b, l, POOL_WIDTH), F32),
                   jax.ShapeDtypeStruct((b, l, v_w), BF16),
                   jax.ShapeDtypeStruct((b, l, v_w), BF16),
                   jax.ShapeDtypeStruct((b, l // tm, n_heads * V_AUG, tm), BF16)],
        compiler_params=_params(("parallel", "parallel")),
        name="pre0_rope" if use_rope else "pre0_ctx",
    )(h, g, sh, sc, w_in, cos, sin)


def _attn_core(qs, kc_ref, vc_ref, lat_refs, acc_refs, sbuf_refs):
    n = len(qs)
    qts = [q.astype(F32).T.astype(BF16) for q in qs]
    kc = kc_ref[0]
    vc = vc_ref[0, 0]
    ms = []
    for i in range(n):
        s = _dot(kc, qts[i])
        m = jnp.max(s, axis=0, keepdims=True)
        ms.append(m)
        acc_refs[i][...] = _dot(vc, jnp.exp2(s - m).astype(BF16))
    if lat_refs is None:
        return
    kl_ref, vl_ref = lat_refs
    nblk, tk = vl_ref.shape[1], vl_ref.shape[3]
    assert nblk >= 2 and nblk % 2 == 0

    def qk(t, which):
        start = t * tk if isinstance(t, int) else pl.multiple_of(t * tk, tk)
        k = kl_ref[0, pl.ds(start, tk), :]
        cmax = []
        for i in range(n):
            s = _dot(k, qts[i])
            sbuf_refs[2 * i + which][...] = s
            cmax.append(jnp.max(s, axis=0, keepdims=True))
        return tuple(cmax)

    def softmax_pv(t, which, cmax, ms):
        vt = vl_ref[0, t]
        new_m = []
        for i in range(n):
            m_new = jnp.maximum(ms[i], cmax[i])
            alpha = jnp.exp2(ms[i] - m_new)
            p = jnp.exp2(sbuf_refs[2 * i + which][...] - m_new)
            acc_refs[i][...] = alpha * acc_refs[i][...] + _dot(vt, p.astype(BF16))
            new_m.append(m_new)
        return tuple(new_m)

    def pair(jj, carry):
        cmax_a, ms = carry
        t = 2 * jj
        cmax_b = qk(t + 1, 1)
        ms = softmax_pv(t, 0, cmax_a, ms)
        cmax_a = qk(t + 2, 0)
        ms = softmax_pv(t + 1, 1, cmax_b, ms)
        return cmax_a, ms

    cmax_a = qk(0, 0)
    n_pairs = nblk // 2 - 1
    cmax_a, ms = lax.fori_loop(0, n_pairs, pair, (cmax_a, tuple(ms)),
                               unroll=max(1, min(PAIR_UNROLL, n_pairs)))
    cmax_b = qk(nblk - 1, 1)
    ms = softmax_pv(nblk - 2, 0, cmax_a, ms)
    softmax_pv(nblk - 1, 1, cmax_b, ms)


def _normalised(acc_ref):
    acc = acc_ref[...]
    return acc[:DIFF_V_DIM] * (1.0 / acc[DIFF_V_DIM:DIFF_V_DIM + 1])


def _diff_attn_kernel(lq_ref, lk_ref, q_ref, kc_ref, vc_ref, *rest, has_lat, lam_init):
    lat_refs = rest[:2] if has_lat else None
    rest = rest[2:] if has_lat else rest
    o_ref, acc1_ref, acc2_ref = rest[:3]
    q = q_ref[0]
    lane = lax.broadcasted_iota(jnp.int32, q.shape, 1)
    zero = jnp.zeros_like(q)
    q1 = jnp.where(lane < DIFF_HEAD_DIM, q, zero)
    q2 = jnp.where(lane >= DIFF_HEAD_DIM, q, zero)
    _attn_core([q1, q2], kc_ref, vc_ref, lat_refs, (acc1_ref, acc2_ref), rest[3:])
    e = jnp.exp(jnp.sum(lq_ref[...] * lk_ref[...], axis=-1, keepdims=True))
    lam = e[0:1] - e[1:2] + lam_init
    o = _normalised(acc1_ref) - lam * _normalised(acc2_ref)
    ms = jnp.mean(o * o, axis=0, keepdims=True)
    y = o * (lax.rsqrt(ms + EPS) * (1.0 - lam_init))
    o_ref[0] = y.T.astype(BF16)


def _mla_attn_kernel(q_ref, kc_ref, vc_ref, *rest, has_lat):
    lat_refs = rest[:2] if has_lat else None
    rest = rest[2:] if has_lat else rest
    o_ref, acc_ref = rest[:2]
    _attn_core([q_ref[0]], kc_ref, vc_ref, lat_refs, (acc_ref,), rest[2:])
    o_ref[0] = _normalised(acc_ref).T.astype(BF16)


def _attention(kern, n_maps, q, ctx_kv, lat_kv, qk_w, extra=(), *, tq, name):
    b, lq, hw = q.shape
    n_heads = hw // qk_w
    in_specs = [pl.BlockSpec(x.shape, lambda b_, h, i: (0, 0)) for x in extra]
    in_specs.append(pl.BlockSpec((1, tq, qk_w), lambda b_, h, i: (b_, i, h)))
    args = list(extra) + [q]
    scratch = [pltpu.VMEM((V_AUG, tq), F32) for _ in range(n_maps)]
    for kv in (ctx_kv, lat_kv):
        if kv is None:
            continue
        k, vt = kv
        nblk, tk = vt.shape[1], vt.shape[3]
        in_specs.append(pl.BlockSpec((1, k.shape[1], qk_w), lambda b_, h, i: (b_, 0, h)))
        in_specs.append(pl.BlockSpec((1, nblk, V_AUG, tk), lambda b_, h, i: (b_, 0, h, 0)))
        args += [k, vt]
    if lat_kv is not None:
        scratch += [pltpu.VMEM((lat_kv[1].shape[3], tq), F32) for _ in range(2 * n_maps)]
    return pl.pallas_call(
        functools.partial(kern, has_lat=lat_kv is not None),
        grid=(b, n_heads, lq // tq),
        in_specs=in_specs,
        out_specs=pl.BlockSpec((1, tq, DIFF_V_DIM), lambda b_, h, i: (b_, i, h)),
        out_shape=jax.ShapeDtypeStruct((b, lq, n_heads * DIFF_V_DIM), BF16),
        scratch_shapes=scratch,
        compiler_params=_params(("parallel", "parallel", "arbitrary")),
        name=name,
    )(*args)


def _post0_kernel(attn_ref, p_ref, pprev_ref, pnext_ref, poolw_ref, pscale_ref, wout_ref,
                  h_ref, g1_ref, n2g_ref, sh2_ref, sc2_ref, h1_ref, a2_ref, *, seq_len):
    i = pl.program_id(1)
    u = p_ref[0]
    tm = u.shape[0]
    ext = jnp.concatenate([pprev_ref[0], u, pnext_ref[0]], axis=0)
    row = i * tm - POOL_HALO + lax.broadcasted_iota(jnp.int32, (tm + 2 * POOL_HALO, 1), 0)
    ext = jnp.where((row >= 0) & (row < seq_len), ext, 0.0)

    def shifted(j):
        return ext[POOL_HALO + j:POOL_HALO + j + tm]

    t = i * tm + lax.broadcasted_iota(jnp.int32, (tm, 1), 0)
    lane = lax.broadcasted_iota(jnp.int32, (tm, POOL_WIDTH), 1)
    win_sum = shifted(-1) + shifted(0)
    pooled = jnp.zeros_like(u)
    for g, w in enumerate(POOL_WINDOWS):
        half = w // 2
        if g > 0:
            prev_half = POOL_WINDOWS[g - 1] // 2
            for j in range(prev_half, half):
                win_sum = win_sum + shifted(-j - 1) + shifted(j)
        cnt = (jnp.minimum(t + half, seq_len) - jnp.maximum(t - half, 0)).astype(F32)
        in_group = (lane >= g * POOL_GROUP) & (lane < (g + 1) * POOL_GROUP)
        pooled = jnp.where(in_group, win_sum / cnt, pooled)
    pooled = pooled - u
    y_pool = _dot(pooled.astype(BF16), poolw_ref[...]) * pscale_ref[...]
    y = _dot(y_pool.astype(BF16), wout_ref[:POOL_WIDTH, :]) + _dot(attn_ref[0], wout_ref[POOL_WIDTH:, :])
    h1 = h_ref[0] + g1_ref[0] * y
    h1_ref[0] = h1
    a2_ref[0] = _norm_mod(h1, n2g_ref[...], sh2_ref[0], sc2_ref[0]).astype(BF16)


def _post0(attn, p, poolw_bd, pscale, w_out, h, g1, n2g, sh2, sc2, *, tm):
    b, l, d = h.shape
    hb = tm // POOL_HALO
    nhb = l // POOL_HALO
    return pl.pallas_call(
        functools.partial(_post0_kernel, seq_len=l),
        grid=(b, l // tm),
        in_specs=[pl.BlockSpec((1, tm, attn.shape[2]), lambda b_, i: (b_, i, 0)),
                  pl.BlockSpec((1, tm, POOL_WIDTH), lambda b_, i: (b_, i, 0)),
                  pl.BlockSpec((1, POOL_HALO, POOL_WIDTH),
                               lambda b_, i: (b_, jnp.maximum(i * hb - 1, 0), 0)),
                  pl.BlockSpec((1, POOL_HALO, POOL_WIDTH),
                               lambda b_, i: (b_, jnp.minimum((i + 1) * hb, nhb - 1), 0)),
                  pl.BlockSpec(poolw_bd.shape, lambda b_, i: (0, 0)),
                  pl.BlockSpec((1, POOL_WIDTH), lambda b_, i: (0, 0)),
                  pl.BlockSpec(w_out.shape, lambda b_, i: (0, 0)),
                  pl.BlockSpec((1, tm, d), lambda b_, i: (b_, i, 0)),
                  _bvec_spec(g1, d),
                  pl.BlockSpec((1, d), lambda b_, i: (0, 0)),
                  _bvec_spec(sh2, d), _bvec_spec(sc2, d)],
        out_specs=[pl.BlockSpec((1, tm, d), lambda b_, i: (b_, i, 0)),
                   pl.BlockSpec((1, tm, d), lambda b_, i: (b_, i, 0))],
        out_shape=[jax.ShapeDtypeStruct((b, l, d), F32),
                   jax.ShapeDtypeStruct((b, l, d), BF16)],
        compiler_params=_params(("parallel", "parallel")),
        name="post0",
    )(attn, p, p, p, poolw_bd, pscale, w_out, h, g1, n2g, sh2, sc2)


def _swiglu_tile(x, wg_ref, wu_ref, wd_ref, lead=()):
    acc = None
    c0 = 0
    for cw in FFN_CHUNKS:
        hg = _dot(x, wg_ref[lead + (slice(None), slice(c0, c0 + cw))])
        hu = _dot(x, wu_ref[lead + (slice(None), slice(c0, c0 + cw))])
        act = (_silu(hg) * hu).astype(BF16)
        part = _dot(act, wd_ref[lead + (slice(c0, c0 + cw), slice(None))])
        acc = part if acc is None else acc + part
        c0 += cw
    return acc


def _ffn_kernel(a2_ref, h1_ref, g2_ref, wg_ref, wu_ref, wd_ref, o_ref):
    y = _swiglu_tile(a2_ref[0], wg_ref, wu_ref, wd_ref)
    o_ref[0] = h1_ref[0] + g2_ref[0] * y


def _ffn(a2, h1, g2, wg, wu, wd, *, tm):
    b, l, d = h1.shape
    f = wg.shape[1]
    assert sum(FFN_CHUNKS) == f
    return pl.pallas_call(
        _ffn_kernel,
        grid=(b, l // tm),
        in_specs=[pl.BlockSpec((1, tm, d), lambda b_, i: (b_, i, 0)),
                  pl.BlockSpec((1, tm, d), lambda b_, i: (b_, i, 0)),
                  _bvec_spec(g2, d),
                  pl.BlockSpec((d, f), lambda b_, i: (0, 0)),
                  pl.BlockSpec((d, f), lambda b_, i: (0, 0)),
                  pl.BlockSpec((f, d), lambda b_, i: (0, 0))],
        out_specs=pl.BlockSpec((1, tm, d), lambda b_, i: (b_, i, 0)),
        out_shape=jax.ShapeDtypeStruct((b, l, d), F32),
        compiler_params=_params(("parallel", "parallel")),
        name="ffn_dense",
    )(a2, h1, g2, wg, wu, wd)


def _pre1_kernel(h_ref, g_ref, sh_ref, sc_ref, win_ref, qg_ref, kvg_ref, wuq_ref, wuk_ref, wuv_ref,
                 cos_ref, sin_ref, *out_refs, use_rope, want_q, q_rank, kv_rank, n_heads):
    a = _norm_mod(h_ref[0], g_ref[...], sh_ref[0], sc_ref[0])
    z = _dot(a.astype(BF16), win_ref[...])
    cos = cos_ref[...]
    sin = sin_ref[...]
    if want_q:
        q_ref, k_ref, vt_ref = out_refs
        cq = _rms(z[:, :q_rank]) * qg_ref[...]
        qf = _dot(cq.astype(BF16), wuq_ref[...])
        q_scale = (MLA_NOPE + MLA_ROPE) ** -0.5 * LOG2E
        for h in range(n_heads):
            c0 = h * MLA_QK_PAD
            q_ref[0, :, c0:c0 + MLA_NOPE] = (qf[:, c0:c0 + MLA_NOPE] * q_scale).astype(BF16)
            qr = _rope(qf[:, c0 + MLA_NOPE:c0 + MLA_QK_PAD], cos, sin)
            q_ref[0, :, c0 + MLA_NOPE:c0 + MLA_QK_PAD] = (qr * q_scale).astype(BF16)
    else:
        k_ref, vt_ref = out_refs
    ckv = (_rms(z[:, q_rank:q_rank + kv_rank]) * kvg_ref[...]).astype(BF16)
    kn = _dot(ckv, wuk_ref[...])
    vv = _dot(ckv, wuv_ref[...])
    kr = z[:, q_rank + kv_rank:]
    if use_rope:
        kr = _rope(kr, cos, sin)
    kr = kr.astype(BF16)
    for h in range(n_heads):
        c0 = h * MLA_QK_PAD
        k_ref[0, :, c0:c0 + MLA_NOPE] = kn[:, h * MLA_NOPE:(h + 1) * MLA_NOPE].astype(BF16)
        k_ref[0, :, c0 + MLA_NOPE:c0 + MLA_QK_PAD] = kr
    _store_vt(vt_ref, vv, n_heads)


def _pre1(h, g, sh, sc, w_in, qg, kvg, wuq, wuk, wuv, cos, sin, *, use_rope, want_q, tm):
    b, l, d = h.shape
    n_heads = wuk.shape[1] // MLA_NOPE
    q_rank, kv_rank = wuq.shape[0], wuk.shape[0]
    kern = functools.partial(_pre1_kernel, use_rope=use_rope, want_q=want_q,
                             q_rank=q_rank, kv_rank=kv_rank, n_heads=n_heads)
    full = lambda x: pl.BlockSpec(x.shape, lambda b_, i: (0,) * x.ndim)
    qk_w = n_heads * MLA_QK_PAD
    v_rows = n_heads * V_AUG
    out_specs = [pl.BlockSpec((1, tm, qk_w), lambda b_, i: (b_, i, 0)),
                 pl.BlockSpec((1, 1, v_rows, tm), lambda b_, i: (b_, i, 0, 0))]
    out_shape = [jax.ShapeDtypeStruct((b, l, qk_w), BF16),
                 jax.ShapeDtypeStruct((b, l // tm, v_rows, tm), BF16)]
    if want_q:
        out_specs = [pl.BlockSpec((1, tm, qk_w), lambda b_, i: (b_, i, 0))] + out_specs
        out_shape = [jax.ShapeDtypeStruct((b, l, qk_w), BF16)] + out_shape
    return pl.pallas_call(
        kern,
        grid=(b, l // tm),
        in_specs=[pl.BlockSpec((1, tm, d), lambda b_, i: (b_, i, 0)),
                  pl.BlockSpec((1, d), lambda b_, i: (0, 0)),
                  _bvec_spec(sh, d), _bvec_spec(sc, d),
                  full(w_in), full(qg), full(kvg), full(wuq), full(wuk), full(wuv),
                  pl.BlockSpec((tm, LANES), lambda b_, i: (i, 0)),
                  pl.BlockSpec((tm, LANES), lambda b_, i: (i, 0))],
        out_specs=out_specs,
        out_shape=out_shape,
        compiler_params=_params(("parallel", "parallel")),
        name="pre1_lat" if want_q else "pre1_ctx",
    )(h, g, sh, sc, w_in, qg, kvg, wuq, wuk, wuv, cos, sin)


def _post1_kernel(o_ref, wout_ref, h_ref, g1_ref, n2g_ref, sh2_ref, sc2_ref, rwh_ref, rwl_ref,
                  h1_ref, a2_ref, gates_ref, sel_ref):
    y = _dot(o_ref[0], wout_ref[...])
    h1 = h_ref[0] + g1_ref[0] * y
    h1_ref[0] = h1
    a2 = _norm_mod(h1, n2g_ref[...], sh2_ref[0], sc2_ref[0])
    a2_ref[0] = a2
    a_hi = a2.astype(BF16)
    a_lo = (a2 - a_hi.astype(F32)).astype(BF16)
    logits = _dot(a_hi, rwh_ref[...]) + (_dot(a_lo, rwh_ref[...]) + _dot(a_hi, rwl_ref[...]))
    lane = lax.broadcasted_iota(jnp.int32, logits.shape, 1)
    lg = jnp.where(lane < N_EXPERTS, logits, -jnp.inf)
    m1 = jnp.max(lg, axis=1, keepdims=True)
    i1 = jnp.min(jnp.where(lg == m1, lane, LANES), axis=1, keepdims=True)
    lg2 = jnp.where(lane == i1, -jnp.inf, lg)
    m2 = jnp.max(lg2, axis=1, keepdims=True)
    i2 = jnp.min(jnp.where(lg2 == m2, lane, LANES), axis=1, keepdims=True)
    p2 = jnp.exp(m2 - m1)
    w1 = 1.0 / (1.0 + p2)
    gates_ref[0] = jnp.where(lane == i1, w1, 0.0) + jnp.where(lane == i2, p2 * w1, 0.0)
    sel_ref[0] = jnp.where((lane == i1) | (lane == i2), 1.0, 0.0).astype(BF16)


def _post1(o, w_out, h, g1, n2g, sh2, sc2, rw_hi, rw_lo, *, tm):
    b, l, d = h.shape
    return pl.pallas_call(
        _post1_kernel,
        grid=(b, l // tm),
        in_specs=[pl.BlockSpec((1, tm, o.shape[2]), lambda b_, i: (b_, i, 0)),
                  pl.BlockSpec(w_out.shape, lambda b_, i: (0, 0)),
                  pl.BlockSpec((1, tm, d), lambda b_, i: (b_, i, 0)),
                  _bvec_spec(g1, d),
                  pl.BlockSpec((1, d), lambda b_, i: (0, 0)),
                  _bvec_spec(sh2, d), _bvec_spec(sc2, d),
                  pl.BlockSpec(rw_hi.shape, lambda b_, i: (0, 0)),
                  pl.BlockSpec(rw_lo.shape, lambda b_, i: (0, 0))],
        out_specs=[pl.BlockSpec((1, tm, d), lambda b_, i: (b_, i, 0)),
                   pl.BlockSpec((1, tm, d), lambda b_, i: (b_, i, 0)),
                   pl.BlockSpec((1, tm, LANES), lambda b_, i: (b_, i, 0)),
                   pl.BlockSpec((1, tm, LANES), lambda b_, i: (b_, i, 0))],
        out_shape=[jax.ShapeDtypeStruct((b, l, d), F32),
                   jax.ShapeDtypeStruct((b, l, d), F32),
                   jax.ShapeDtypeStruct((b, l, LANES), F32),
                   jax.ShapeDtypeStruct((b, l, LANES), BF16)],
        compiler_params=_params(("parallel", "parallel")),
        name="post1_router",
    )(o, w_out, h, g1, n2g, sh2, sc2, rw_hi, rw_lo)


def _route_kernel(sel_ref, gates_ref, r_ref, tile_ref, cnt_ref, off_ref, *, tile_rows):
    phase = pl.program_id(0)
    i = pl.program_id(1)
    sel = sel_ref[...]
    tr = sel.shape[0]
    col_count = jnp.sum(sel.astype(F32), axis=0, keepdims=True)
    lane = lax.broadcasted_iota(jnp.int32, (1, LANES), 1)

    @pl.when((phase == 0) & (i == 0))
    def _():
        cnt_ref[...] = jnp.zeros_like(cnt_ref)

    @pl.when(phase == 0)
    def _():
        cnt_ref[...] += col_count

    @pl.when((phase == 0) & (i == pl.num_programs(1) - 1))
    def _():
        padded = jnp.ceil(cnt_ref[...] / tile_rows) * tile_rows
        incl = padded
        for sh in (1, 2, 4):
            incl = incl + jnp.where(lane >= sh, pltpu.roll(incl, sh, 1), 0.0)
        off_ref[...] = incl - padded
        tile_start = (lax.broadcasted_iota(jnp.int32, tile_ref.shape, 1) * tile_rows).astype(F32)
        tile_expert = jnp.zeros(tile_ref.shape, jnp.int32)
        for e in range(N_EXPERTS):
            end_e = jnp.sum(jnp.where(lane == e, incl, 0.0), axis=1, keepdims=True)
            tile_expert = tile_expert + (tile_start >= end_e).astype(jnp.int32)
        tile_ref[...] = tile_expert
        cnt_ref[...] = jnp.zeros_like(cnt_ref)

    @pl.when(phase == 1)
    def _():
        rr = lax.broadcasted_iota(jnp.int32, (tr, tr), 0)
        cc = lax.broadcasted_iota(jnp.int32, (tr, tr), 1)
        earlier = jnp.where(cc < rr, 1.0, 0.0).astype(BF16)
        rank = _dot(earlier, sel) + cnt_ref[...]
        cnt_ref[...] += col_count
        pos = off_ref[...] + rank
        chosen = sel > 0
        lo = jnp.min(jnp.where(chosen, pos, 3e38), axis=1, keepdims=True)
        hi = jnp.max(jnp.where(chosen, pos, -1.0), axis=1, keepdims=True)
        g = gates_ref[...]
        w_lo = jnp.sum(jnp.where(chosen & (pos == lo), g, 0.0), axis=1, keepdims=True)
        w_hi = jnp.sum(jnp.where(chosen & (pos == hi), g, 0.0), axis=1, keepdims=True)
        lane_t = lax.broadcasted_iota(jnp.int32, (tr, LANES), 1)
        r_ref[...] = jnp.where(lane_t == 0, lo, jnp.where(lane_t == 1, hi,
                               jnp.where(lane_t == 2, w_lo, jnp.where(lane_t == 3, w_hi, 0.0))))


def _route(sel, gates, *, tr, tile_rows, n_tiles):
    n = sel.shape[0]
    n_tiles_pad = -(-n_tiles // LANES) * LANES
    return pl.pallas_call(
        functools.partial(_route_kernel, tile_rows=tile_rows),
        grid=(2, n // tr),
        in_specs=[pl.BlockSpec((tr, LANES), lambda ph, i: (i, 0)),
                  pl.BlockSpec((tr, LANES), lambda ph, i: (i, 0))],
        out_specs=[pl.BlockSpec((tr, LANES), lambda ph, i: (i * ph, 0)),
                   pl.BlockSpec((1, n_tiles_pad), lambda ph, i: (0, 0))],
        out_shape=[jax.ShapeDtypeStruct((n, LANES), F32),
                   jax.ShapeDtypeStruct((1, n_tiles_pad), jnp.int32)],
        scratch_shapes=[pltpu.VMEM((1, LANES), F32), pltpu.VMEM((1, LANES), F32)],
        compiler_params=_params(("arbitrary", "arbitrary")),
        name="moe_route",
    )(sel, gates)


def _dispatch_kernel(dest_ref, x_ref, xs_in_hbm, xs_hbm, sem, *, tm):
    del xs_in_hbm

    def row_copy(src_row, dst_row):
        return pltpu.make_async_copy(x_ref.at[pl.ds(src_row, 1)], xs_hbm.at[pl.ds(dst_row, 1)], sem)

    def issue(r, carry):
        row_copy(r, dest_ref[0, 0, r]).start()
        row_copy(r, dest_ref[0, 0, tm + r]).start()
        return carry

    lax.fori_loop(0, tm, issue, 0, unroll=DMA_ISSUE_UNROLL)
    pltpu.make_async_copy(x_ref, xs_hbm.at[pl.ds(0, tm)], sem).wait()
    pltpu.make_async_copy(x_ref, xs_hbm.at[pl.ds(0, tm)], sem).wait()


def _dispatch(dest, x, n_rows, *, tm):
    n = x.shape[0]
    xs0 = jnp.zeros((n_rows,) + x.shape[1:], x.dtype)
    return pl.pallas_call(
        functools.partial(_dispatch_kernel, tm=tm),
        grid=(n // tm,),
        in_specs=[pl.BlockSpec((1, 1, 2 * tm), lambda i: (i, 0, 0), memory_space=pltpu.SMEM),
                  pl.BlockSpec((tm, x.shape[1]), lambda i: (i, 0)),
                  pl.BlockSpec(memory_space=pl.ANY)],
        out_specs=pl.BlockSpec(memory_space=pl.ANY),
        out_shape=jax.ShapeDtypeStruct(xs0.shape, xs0.dtype),
        scratch_shapes=[pltpu.SemaphoreType.DMA(())],
        input_output_aliases={2: 0},
        compiler_params=pltpu.CompilerParams(dimension_semantics=("arbitrary",),
                                             has_side_effects=True),
        name="moe_dispatch",
    )(dest, x, xs0)


def _experts_kernel(te_ref, x_ref, wg_ref, wu_ref, wd_ref, o_ref):
    valid = te_ref[pl.program_id(0)] < N_EXPERTS

    @pl.when(valid)
    def _():
        o_ref[...] = _swiglu_tile(x_ref[...].astype(BF16), wg_ref, wu_ref, wd_ref, lead=(0,))

    @pl.when(jnp.logical_not(valid))
    def _():
        o_ref[...] = jnp.zeros_like(o_ref)


def _experts(tile_expert, xs, wg, wu, wd, *, tile_rows):
    n_rows, d = xs.shape
    ne, _, f = wg.shape
    w_idx = lambda j, te: (jnp.minimum(te[j], ne - 1), 0, 0)
    return pl.pallas_call(
        _experts_kernel,
        grid_spec=pltpu.PrefetchScalarGridSpec(
            num_scalar_prefetch=1,
            grid=(n_rows // tile_rows,),
            in_specs=[pl.BlockSpec((tile_rows, d), lambda j, te: (j, 0)),
                      pl.BlockSpec((1, d, f), w_idx),
                      pl.BlockSpec((1, d, f), w_idx),
                      pl.BlockSpec((1, f, d), w_idx)],
            out_specs=pl.BlockSpec((tile_rows, d), lambda j, te: (j, 0))),
        out_shape=jax.ShapeDtypeStruct((n_rows, d), F32),
        compiler_params=_params(("arbitrary",)),
        name="moe_experts",
    )(tile_expert, xs, wg, wu, wd)


def _combine_kernel(dest_ref, routed_ref, ys_hbm, h1_ref, g2_ref, fng_ref, o_ref, ylo_ref, yhi_ref,
                    sem, *, tm):
    def issue(r, carry):
        pltpu.make_async_copy(ys_hbm.at[pl.ds(dest_ref[0, 0, r], 1)],
                              ylo_ref.at[pl.ds(r, 1)], sem).start()
        pltpu.make_async_copy(ys_hbm.at[pl.ds(dest_ref[0, 0, tm + r], 1)],
                              yhi_ref.at[pl.ds(r, 1)], sem).start()
        return carry

    lax.fori_loop(0, tm, issue, 0, unroll=DMA_ISSUE_UNROLL)
    pltpu.make_async_copy(ys_hbm.at[pl.ds(0, tm)], ylo_ref, sem).wait()
    pltpu.make_async_copy(ys_hbm.at[pl.ds(0, tm)], yhi_ref, sem).wait()
    routed = routed_ref[...]
    y = routed[:, 2:3] * ylo_ref[...] + routed[:, 3:4] * yhi_ref[...]
    out = h1_ref[...] + g2_ref[0] * y
    o_ref[...] = _rms(out) * fng_ref[...]


def _combine(dest, routed, ys, h1, g2, fng, *, tm, tiles_per_sample):
    n, d = h1.shape
    return pl.pallas_call(
        functools.partial(_combine_kernel, tm=tm),
        grid=(n // tm,),
        in_specs=[pl.BlockSpec((1, 1, 2 * tm), lambda i: (i, 0, 0), memory_space=pltpu.SMEM),
                  pl.BlockSpec((tm, LANES), lambda i: (i, 0)),
                  pl.BlockSpec(memory_space=pl.ANY),
                  pl.BlockSpec((tm, d), lambda i: (i, 0)),
                  pl.BlockSpec((1, 1, d), lambda i: (i // tiles_per_sample, 0, 0)),
                  pl.BlockSpec((1, d), lambda i: (0, 0))],
        out_specs=pl.BlockSpec((tm, d), lambda i: (i, 0)),
        out_shape=jax.ShapeDtypeStruct((n, d), F32),
        scratch_shapes=[pltpu.VMEM((tm, d), F32), pltpu.VMEM((tm, d), F32),
                        pltpu.SemaphoreType.DMA(())],
        compiler_params=_params(("arbitrary",)),
        name="moe_combine",
    )(dest, routed, ys, h1, g2, fng)


def _moe(a2, gates, sel, wg, wu, wd, h1, g2, fng, *, tm):
    b, l, d = h1.shape
    n = b * l
    n_rows = 2 * n + N_EXPERTS * MOE_TILE
    n_tiles = n_rows // MOE_TILE
    routed, tile_expert = _route(sel.reshape(n, LANES), gates.reshape(n, LANES),
                                 tr=tm, tile_rows=MOE_TILE, n_tiles=n_tiles)
    dest = routed[:, 0:2].astype(jnp.int32).reshape(n // tm, tm, 2)
    dest = dest.transpose(0, 2, 1).reshape(n // tm, 1, 2 * tm)
    xs = _dispatch(dest, a2.reshape(n, d), n_rows, tm=tm)
    ys = _experts(tile_expert[0, :n_tiles], xs, wg, wu, wd, tile_rows=MOE_TILE)
    out = _combine(dest, routed, ys, h1.reshape(n, d), g2, fng, tm=tm, tiles_per_sample=l // tm)
    return out.reshape(b, l, d)


def _rope_tables(seq_len, dim):
    t = jnp.arange(seq_len, dtype=jnp.int32)
    row = (t // GRID_W).astype(F32)
    col = (t % GRID_W).astype(F32)
    quarter = dim // 4
    inv = ROPE_BASE ** (-jnp.arange(quarter, dtype=F32) / quarter)
    ar = row[:, None] * inv[None, :]
    ac = col[:, None] * inv[None, :]
    cos = jnp.concatenate([jnp.cos(ar), jnp.cos(ar), jnp.cos(ac), jnp.cos(ac)], axis=-1)
    sin = jnp.concatenate([jnp.sin(ar), jnp.sin(ar), jnp.sin(ac), jnp.sin(ac)], axis=-1)
    reps = LANES // dim
    return jnp.tile(cos, (1, reps)), jnp.tile(sin, (1, reps))


def kernel(x, c, ctx, c_ctx, norm1_g, norm2_g, mod_w, mod_b, even_w_in, pool_w, pool_scale,
           lambda_q1, lambda_k1, lambda_q2, lambda_k2, even_w_out, ffn_w_gate, ffn_w_up,
           ffn_w_down, odd_w_in, q_norm_g, kv_norm_g, w_uq, w_ukv, odd_w_out, router_w,
           moe_w_gate, moe_w_up, moe_w_down, final_norm_g):
    b, l, d = x.shape
    n_ctx = ctx.shape[1]
    tm = min(ROW_TILE, l)
    tkv = min(KV_TILE, l)
    tq = min(DIFF_Q_TILE, l)
    cos, sin = _rope_tables(l, DIFF_HEAD_DIM)
    cos_c, sin_c = cos[:n_ctx], sin[:n_ctx]

    cond = jnp.zeros((8, d), F32).at[:b].set(c).at[b].set(c_ctx)
    mod = _modulation(cond, mod_w, mod_b)
    mod = mod.reshape(mod.shape[0], 8, N_MOD, d)

    def mod_vecs(layer):
        lat = [mod[layer, :b, k][:, None, :] for k in range(N_MOD)]
        cx = [mod[layer, b:b + 1, k][:, None, :] for k in range(N_MOD)]
        return lat, cx

    (sh1, sc1, g1, sh2, sc2, g2), (csh1, csc1, cg1, csh2, csc2, cg2) = mod_vecs(0)
    lam_init = 0.8 - 0.6 * math.exp(-0.3 * 0)
    w_in0 = even_w_in[0].astype(BF16)
    n1g = norm1_g[0][None, :]
    n2g = norm2_g[0][None, :]
    p_lat, q_lat, k_lat, vt_lat = _pre0(x, n1g, sh1, sc1, w_in0, cos, sin, use_rope=True, tm=tkv)
    p_ctx, q_ctx, k_ctx, vt_ctx = _pre0(ctx, n1g, csh1, csc1, w_in0, cos_c, sin_c,
                                        use_rope=False, tm=n_ctx)
    lq = jnp.stack([lambda_q1[0], lambda_q2[0]])
    lk = jnp.stack([lambda_k1[0], lambda_k2[0]])
    diff_kern = functools.partial(_diff_attn_kernel, lam_init=lam_init)
    attn_lat = _attention(diff_kern, 2, q_lat, (k_ctx, vt_ctx), (k_lat, vt_lat), 2 * DIFF_HEAD_DIM,
                          extra=(lq, lk), tq=tq, name="diff_attn_lat")
    attn_ctx = _attention(diff_kern, 2, q_ctx, (k_ctx, vt_ctx), None, 2 * DIFF_HEAD_DIM,
                          extra=(lq, lk), tq=n_ctx, name="diff_attn_ctx")
    ng = len(POOL_WINDOWS)
    poolw_bd = jnp.zeros((POOL_WIDTH, POOL_WIDTH), F32)
    for g in range(ng):
        sl = slice(g * POOL_GROUP, (g + 1) * POOL_GROUP)
        poolw_bd = poolw_bd.at[sl, sl].set(pool_w[0, g])
    poolw_bd = poolw_bd.astype(BF16)
    pscale = pool_scale[0][None, :]
    w_out0 = even_w_out[0].astype(BF16)
    wg0, wu0, wd0 = ffn_w_gate[0].astype(BF16), ffn_w_up[0].astype(BF16), ffn_w_down[0].astype(BF16)
    h1_lat, a2_lat = _post0(attn_lat, p_lat, poolw_bd, pscale, w_out0, x, g1, n2g, sh2, sc2, tm=tm)
    h_lat = _ffn(a2_lat, h1_lat, g2, wg0, wu0, wd0, tm=tm)
    h1_ctx, a2_ctx = _post0(attn_ctx, p_ctx, poolw_bd, pscale, w_out0, ctx, cg1, n2g, csh2, csc2,
                            tm=n_ctx)
    h_ctx = _ffn(a2_ctx, h1_ctx, cg2, wg0, wu0, wd0, tm=n_ctx)

    (sh1, sc1, g1, sh2, sc2, g2), (csh1, csc1, _, _, _, _) = mod_vecs(1)
    n1g = norm1_g[1][None, :]
    n2g = norm2_g[1][None, :]
    q_rank, kv_rank = w_uq.shape[1], w_ukv.shape[1]
    n_heads = w_uq.shape[2] // (MLA_NOPE + MLA_ROPE)
    w_in1 = jnp.pad(odd_w_in[0], ((0, 0), (0, LANES - MLA_ROPE))).astype(BF16)
    wuq = w_uq[0].reshape(q_rank, n_heads, MLA_NOPE + MLA_ROPE)
    wuq = jnp.pad(wuq, ((0, 0), (0, 0), (0, MLA_QK_PAD - MLA_NOPE - MLA_ROPE)))
    wuq = wuq.reshape(q_rank, n_heads * MLA_QK_PAD).astype(BF16)
    wukv = w_ukv[0].reshape(kv_rank, n_heads, MLA_NOPE + MLA_V)
    wuk = wukv[:, :, :MLA_NOPE].reshape(kv_rank, n_heads * MLA_NOPE).astype(BF16)
    wuv = wukv[:, :, MLA_NOPE:].reshape(kv_rank, n_heads * MLA_V).astype(BF16)
    qg = q_norm_g[0][None, :]
    kvg = kv_norm_g[0][None, :]
    q1, k1, vt1 = _pre1(h_lat, n1g, sh1, sc1, w_in1, qg, kvg, wuq, wuk, wuv, cos, sin,
                        use_rope=True, want_q=True, tm=tkv)
    k1c, vt1c = _pre1(h_ctx, n1g, csh1, csc1, w_in1, qg, kvg, wuq, wuk, wuv, cos_c, sin_c,
                      use_rope=False, want_q=False, tm=n_ctx)
    o1 = _attention(_mla_attn_kernel, 1, q1, (k1c, vt1c), (k1, vt1), MLA_QK_PAD,
                    tq=min(MLA_Q_TILE, l), name="mla_attn")
    rw_pad = jnp.pad(router_w[0], ((0, 0), (0, LANES - N_EXPERTS)))
    rw_hi = rw_pad.astype(BF16)
    rw_lo = (rw_pad - rw_hi.astype(F32)).astype(BF16)
    h1, a2, gates, sel = _post1(o1, odd_w_out[0].astype(BF16), h_lat, g1, n2g, sh2, sc2,
                                rw_hi, rw_lo, tm=tm)
    return _moe(a2, gates, sel, moe_w_gate[0].astype(BF16), moe_w_up[0].astype(BF16),
                moe_w_down[0].astype(BF16), h1, g2, final_norm_g[None, :], tm=tm)
```

```python
import functools
import math

import jax
import jax.numpy as jnp
from jax import lax
from jax.experimental import pallas as pl
from jax.experimental.pallas import tpu as pltpu

F32 = jnp.float32
BF16 = jnp.bfloat16

EPS = 1e-6
GRID_W = 64
ROPE_BASE = 10000.0
N_MOD = 6
POOL_WINDOWS = (2, 4, 8, 16)
POOL_GROUP = 64
POOL_WIDTH = POOL_GROUP * len(POOL_WINDOWS)
POOL_HALO = max(POOL_WINDOWS) // 2
DIFF_HEAD_DIM = 64
DIFF_V_DIM = 128
V_ONES_ROWS = 16
V_AUG = DIFF_V_DIM + V_ONES_ROWS
MLA_NOPE = 128
MLA_ROPE = 64
MLA_V = 128
MLA_QK_PAD = 256
N_EXPERTS = 8
ROPE_QUARTER = 16
LOG2E = math.log2(math.e)

LANES = 128
ROW_TILE = 512
KV_TILE = 512
DIFF_Q_TILE = 512
MLA_Q_TILE = 1024
PAIR_UNROLL = 5
MOE_TILE = 512
DMA_ISSUE_UNROLL = 8
FFN_CHUNKS = (512, 512, 512, 512, 512, 256)
VMEM_LIMIT = 56 * 1024 * 1024


def _params(sem):
    return pltpu.CompilerParams(dimension_semantics=sem, vmem_limit_bytes=VMEM_LIMIT)


def _dot(a, b):
    return jnp.dot(a, b, preferred_element_type=F32)


def _rms(x):
    return x * lax.rsqrt(jnp.mean(x * x, axis=-1, keepdims=True) + EPS)


def _silu(x):
    return x * (1.0 / (1.0 + jnp.exp(-x)))


def _rope(x, cos, sin):
    lane = lax.broadcasted_iota(jnp.int32, x.shape, 1)
    even = (lane // ROPE_QUARTER) % 2 == 0
    rot = jnp.where(even, -pltpu.roll(x, LANES - ROPE_QUARTER, 1), pltpu.roll(x, ROPE_QUARTER, 1))
    return x * cos + rot * sin


def _store_vt(vt_ref, v, n_heads):
    tm = v.shape[0]
    ones = jnp.ones((V_ONES_ROWS, tm), BF16)
    for h in range(n_heads):
        r0 = h * V_AUG
        vt_ref[0, 0, r0:r0 + DIFF_V_DIM, :] = v[:, h * DIFF_V_DIM:(h + 1) * DIFF_V_DIM].T.astype(BF16)
        vt_ref[0, 0, r0 + DIFF_V_DIM:r0 + V_AUG, :] = ones


def _mod_kernel(cond_ref, w_ref, b_ref, o_ref):
    s = _silu(cond_ref[...])
    o_ref[0] = jnp.dot(s, w_ref[0], preferred_element_type=F32,
                       precision=lax.Precision.HIGHEST) + b_ref[0]


def _modulation(cond, mod_w, mod_b):
    depth, d, n = mod_w.shape
    tn = n // 4
    return pl.pallas_call(
        _mod_kernel,
        grid=(depth, n // tn),
        in_specs=[pl.BlockSpec((8, d), lambda i, j: (0, 0)),
                  pl.BlockSpec((1, d, tn), lambda i, j: (i, 0, j)),
                  pl.BlockSpec((1, 1, tn), lambda i, j: (i, 0, j))],
        out_specs=pl.BlockSpec((1, 8, tn), lambda i, j: (i, 0, j)),
        out_shape=jax.ShapeDtypeStruct((depth, 8, n), F32),
        compiler_params=_params(("arbitrary", "arbitrary")),
        name="modulation",
    )(cond, mod_w, mod_b.reshape(depth, 1, n))


def _bvec_spec(arr, d):
    if arr.shape[0] == 1:
        return pl.BlockSpec((1, 1, d), lambda b, i: (0, 0, 0))
    return pl.BlockSpec((1, 1, d), lambda b, i: (b, 0, 0))


def _norm_mod(h, g, sh, sc):
    a = _rms(h) * g
    return a * (1.0 + sc) + sh


def _pre0_kernel(h_ref, g_ref, sh_ref, sc_ref, w_ref, cos_ref, sin_ref,
                 p_ref, q1t_ref, q2t_ref, k_ref, vt_ref, *, use_rope, n_heads):
    a = _norm_mod(h_ref[0], g_ref[...], sh_ref[0], sc_ref[0])
    z = _dot(a.astype(BF16), w_ref[...])
    p_ref[0] = z[:, :POOL_WIDTH]
    head_w = 2 * DIFF_HEAD_DIM
    qk_w = n_heads * head_w
    q_scale = DIFF_HEAD_DIM ** -0.5 * LOG2E
    cos = cos_ref[...]
    sin = sin_ref[...]
    for h in range(n_heads):
        xq = z[:, POOL_WIDTH + h * head_w:POOL_WIDTH + (h + 1) * head_w]
        xk = z[:, POOL_WIDTH + qk_w + h * head_w:POOL_WIDTH + qk_w + (h + 1) * head_w]
        if use_rope:
            xq = _rope(xq, cos, sin)
            xk = _rope(xk, cos, sin)
        xqt = (xq * q_scale).T
        row = lax.broadcasted_iota(jnp.int32, xqt.shape, 0)
        q1t_ref[0, 0, h * head_w:(h + 1) * head_w, :] = jnp.where(row < DIFF_HEAD_DIM, xqt, 0.0).astype(BF16)
        q2t_ref[0, 0, h * head_w:(h + 1) * head_w, :] = jnp.where(row >= DIFF_HEAD_DIM, xqt, 0.0).astype(BF16)
        k_ref[0, :, h * head_w:(h + 1) * head_w] = xk.astype(BF16)
    _store_vt(vt_ref, z[:, POOL_WIDTH + 2 * qk_w:], n_heads)


def _pre0(h, g, sh, sc, w_in, cos, sin, *, use_rope, tm):
    b, l, d = h.shape
    n_in = w_in.shape[1]
    v_w = (n_in - POOL_WIDTH) // 3
    n_heads = v_w // DIFF_V_DIM
    kern = functools.partial(_pre0_kernel, use_rope=use_rope, n_heads=n_heads)
    return pl.pallas_call(
        kern,
        grid=(b, l // tm),
        in_specs=[pl.BlockSpec((1, tm, d), lambda b_, i: (b_, i, 0)),
                  pl.BlockSpec((1, d), lambda b_, i: (0, 0)),
                  _bvec_spec(sh, d), _bvec_spec(sc, d),
                  pl.BlockSpec((d, n_in), lambda b_, i: (0, 0)),
                  pl.BlockSpec((tm, LANES), lambda b_, i: (i, 0)),
                  pl.BlockSpec((tm, LANES), lambda b_, i: (i, 0))],
        out_specs=[pl.BlockSpec((1, tm, POOL_WIDTH), lambda b_, i: (b_, i, 0)),
                   pl.BlockSpec((1, 1, v_w, tm), lambda b_, i: (b_, i, 0, 0)),
                   pl.BlockSpec((1, 1, v_w, tm), lambda b_, i: (b_, i, 0, 0)),
                   pl.BlockSpec((1, tm, v_w), lambda b_, i: (b_, i, 0)),
                   pl.BlockSpec((1, 1, n_heads * V_AUG, tm), lambda b_, i: (b_, i, 0, 0))],
        out_shape=[jax.ShapeDtypeStruct((b, l, POOL_WIDTH), F32),
                   jax.ShapeDtypeStruct((b, l // tm, v_w, tm), BF16),
                   jax.ShapeDtypeStruct((b, l // tm, v_w, tm), BF16),
                   jax.ShapeDtypeStruct((b, l, v_w), BF16),
                   jax.ShapeDtypeStruct((b, l // tm, n_heads * V_AUG, tm), BF16)],
        compiler_params=_params(("parallel", "parallel")),
        name="pre0_rope" if use_rope else "pre0_ctx",
    )(h, g, sh, sc, w_in, cos, sin)


def _attn_core(qt_refs, kc_ref, vc_ref, lat_refs, acc_refs, sbuf_refs):
    n = len(qt_refs)
    qts = [jnp.concatenate([r[0, j] for j in range(r.shape[1])], axis=1) for r in qt_refs]
    tq = qts[0].shape[1]
    ctx_bufs = sbuf_refs[:n]
    lat_bufs = sbuf_refs[n:]
    ms = tuple(jnp.full((1, tq), -jnp.inf, F32) for _ in range(n))
    for i in range(n):
        acc_refs[i][...] = jnp.zeros_like(acc_refs[i])

    def scores(k, bufs):
        cmax = []
        for i in range(n):
            s = _dot(k, qts[i])
            bufs[i][...] = s
            cmax.append(jnp.max(s, axis=0, keepdims=True))
        return tuple(cmax)

    def softmax_pv(vt, bufs, cmax, ms):
        new_m = []
        for i in range(n):
            m_new = jnp.maximum(ms[i], cmax[i])
            alpha = jnp.exp2(ms[i] - m_new)
            p = jnp.exp2(bufs[i][...] - m_new)
            acc_refs[i][...] = alpha * acc_refs[i][...] + _dot(vt, p.astype(BF16))
            new_m.append(m_new)
        return tuple(new_m)

    if lat_refs is None:
        softmax_pv(vc_ref[0, 0], ctx_bufs, scores(kc_ref[0], ctx_bufs), ms)
        return
    kl_ref, vl_ref = lat_refs
    nblk, tk = vl_ref.shape[1], vl_ref.shape[3]
    assert nblk >= 2 and nblk % 2 == 0

    def qk(t, which):
        start = t * tk if isinstance(t, int) else pl.multiple_of(t * tk, tk)
        return scores(kl_ref[0, pl.ds(start, tk), :], lat_bufs[which::2])

    def pair(jj, carry):
        cmax_a, ms = carry
        t = 2 * jj
        cmax_b = qk(t + 1, 1)
        ms = softmax_pv(vl_ref[0, t], lat_bufs[0::2], cmax_a, ms)
        cmax_a = qk(t + 2, 0)
        ms = softmax_pv(vl_ref[0, t + 1], lat_bufs[1::2], cmax_b, ms)
        return cmax_a, ms

    cmax_a = qk(0, 0)
    n_pairs = nblk // 2 - 1
    cmax_a, ms = lax.fori_loop(0, n_pairs, pair, (cmax_a, ms),
                               unroll=max(1, min(PAIR_UNROLL, n_pairs)))
    cmax_b = qk(nblk - 1, 1)
    ms = softmax_pv(vl_ref[0, nblk - 2], lat_bufs[0::2], cmax_a, ms)
    cmax_c = scores(kc_ref[0], ctx_bufs)
    ms = softmax_pv(vl_ref[0, nblk - 1], lat_bufs[1::2], cmax_b, ms)
    softmax_pv(vc_ref[0, 0], ctx_bufs, cmax_c, ms)


def _normalised(acc_ref):
    acc = acc_ref[...]
    return acc[:DIFF_V_DIM] * (1.0 / acc[DIFF_V_DIM:DIFF_V_DIM + 1])


def _diff_attn_kernel(lq_ref, lk_ref, q1t_ref, q2t_ref, kc_ref, vc_ref, *rest, has_lat, lam_init):
    lat_refs = rest[:2] if has_lat else None
    rest = rest[2:] if has_lat else rest
    o_ref, acc1_ref, acc2_ref = rest[:3]
    _attn_core((q1t_ref, q2t_ref), kc_ref, vc_ref, lat_refs, (acc1_ref, acc2_ref), rest[3:])
    e = jnp.exp(jnp.sum(lq_ref[...] * lk_ref[...], axis=-1, keepdims=True))
    lam = e[0:1] - e[1:2] + lam_init
    o = _normalised(acc1_ref) - lam * _normalised(acc2_ref)
    ms = jnp.mean(o * o, axis=0, keepdims=True)
    y = o * (lax.rsqrt(ms + EPS) * (1.0 - lam_init))
    o_ref[0] = y.T.astype(BF16)


def _mla_attn_kernel(qt_ref, kc_ref, vc_ref, *rest, has_lat):
    lat_refs = rest[:2] if has_lat else None
    rest = rest[2:] if has_lat else rest
    o_ref, acc_ref = rest[:2]
    _attn_core((qt_ref,), kc_ref, vc_ref, lat_refs, (acc_ref,), rest[2:])
    o_ref[0] = _normalised(acc_ref).T.astype(BF16)


def _attention(kern, qts, ctx_kv, lat_kv, qk_w, extra=(), *, tq, name):
    n_maps = len(qts)
    b, nqb, hw, tmq = qts[0].shape
    n_heads = hw // qk_w
    lq = nqb * tmq
    in_specs = [pl.BlockSpec(x.shape, lambda b_, h, i: (0, 0)) for x in extra]
    in_specs += [pl.BlockSpec((1, tq // tmq, qk_w, tmq), lambda b_, h, i: (b_, i, h, 0)) for _ in qts]
    args = list(extra) + list(qts)
    scratch = [pltpu.VMEM((V_AUG, tq), F32) for _ in range(n_maps)]
    scratch += [pltpu.VMEM((ctx_kv[0].shape[1], tq), F32) for _ in range(n_maps)]
    for kv in (ctx_kv, lat_kv):
        if kv is None:
            continue
        k, vt = kv
        nblk, tk = vt.shape[1], vt.shape[3]
        in_specs.append(pl.BlockSpec((1, k.shape[1], qk_w), lambda b_, h, i: (b_, 0, h)))
        in_specs.append(pl.BlockSpec((1, nblk, V_AUG, tk), lambda b_, h, i: (b_, 0, h, 0)))
        args += [k, vt]
    if lat_kv is not None:
        scratch += [pltpu.VMEM((lat_kv[1].shape[3], tq), F32) for _ in range(2 * n_maps)]
    return pl.pallas_call(
        functools.partial(kern, has_lat=lat_kv is not None),
        grid=(b, n_heads, lq // tq),
        in_specs=in_specs,
        out_specs=pl.BlockSpec((1, tq, DIFF_V_DIM), lambda b_, h, i: (b_, i, h)),
        out_shape=jax.ShapeDtypeStruct((b, lq, n_heads * DIFF_V_DIM), BF16),
        scratch_shapes=scratch,
        compiler_params=_params(("parallel", "parallel", "arbitrary")),
        name=name,
    )(*args)


def _post0_kernel(attn_ref, p_ref, pprev_ref, pnext_ref, poolw_ref, pscale_ref, wout_ref,
                  h_ref, g1_ref, n2g_ref, sh2_ref, sc2_ref, h1_ref, a2_ref, *, seq_len):
    i = pl.program_id(1)
    u = p_ref[0]
    tm = u.shape[0]
    ext = jnp.concatenate([pprev_ref[0], u, pnext_ref[0]], axis=0)
    row = i * tm - POOL_HALO + lax.broadcasted_iota(jnp.int32, (tm + 2 * POOL_HALO, 1), 0)
    ext = jnp.where((row >= 0) & (row < seq_len), ext, 0.0)

    def shifted(j):
        return ext[POOL_HALO + j:POOL_HALO + j + tm]

    t = i * tm + lax.broadcasted_iota(jnp.int32, (tm, 1), 0)
    lane = lax.broadcasted_iota(jnp.int32, (tm, POOL_WIDTH), 1)
    win_sum = shifted(-1) + shifted(0)
    pooled = jnp.zeros_like(u)
    for g, w in enumerate(POOL_WINDOWS):
        half = w // 2
        if g > 0:
            prev_half = POOL_WINDOWS[g - 1] // 2
            for j in range(prev_half, half):
                win_sum = win_sum + shifted(-j - 1) + shifted(j)
        cnt = (jnp.minimum(t + half, seq_len) - jnp.maximum(t - half, 0)).astype(F32)
        in_group = (lane >= g * POOL_GROUP) & (lane < (g + 1) * POOL_GROUP)
        pooled = jnp.where(in_group, win_sum / cnt, pooled)
    pooled = pooled - u
    y_pool = _dot(pooled.astype(BF16), poolw_ref[...]) * pscale_ref[...]
    y = _dot(y_pool.astype(BF16), wout_ref[:POOL_WIDTH, :]) + _dot(attn_ref[0], wout_ref[POOL_WIDTH:, :])
    h1 = h_ref[0] + g1_ref[0] * y
    h1_ref[0] = h1
    a2_ref[0] = _norm_mod(h1, n2g_ref[...], sh2_ref[0], sc2_ref[0]).astype(BF16)


def _post0(attn, p, poolw_bd, pscale, w_out, h, g1, n2g, sh2, sc2, *, tm):
    b, l, d = h.shape
    hb = tm // POOL_HALO
    nhb = l // POOL_HALO
    return pl.pallas_call(
        functools.partial(_post0_kernel, seq_len=l),
        grid=(b, l // tm),
        in_specs=[pl.BlockSpec((1, tm, attn.shape[2]), lambda b_, i: (b_, i, 0)),
                  pl.BlockSpec((1, tm, POOL_WIDTH), lambda b_, i: (b_, i, 0)),
                  pl.BlockSpec((1, POOL_HALO, POOL_WIDTH),
                               lambda b_, i: (b_, jnp.maximum(i * hb - 1, 0), 0)),
                  pl.BlockSpec((1, POOL_HALO, POOL_WIDTH),
                               lambda b_, i: (b_, jnp.minimum((i + 1) * hb, nhb - 1), 0)),
                  pl.BlockSpec(poolw_bd.shape, lambda b_, i: (0, 0)),
                  pl.BlockSpec((1, POOL_WIDTH), lambda b_, i: (0, 0)),
                  pl.BlockSpec(w_out.shape, lambda b_, i: (0, 0)),
                  pl.BlockSpec((1, tm, d), lambda b_, i: (b_, i, 0)),
                  _bvec_spec(g1, d),
                  pl.BlockSpec((1, d), lambda b_, i: (0, 0)),
                  _bvec_spec(sh2, d), _bvec_spec(sc2, d)],
        out_specs=[pl.BlockSpec((1, tm, d), lambda b_, i: (b_, i, 0)),
                   pl.BlockSpec((1, tm, d), lambda b_, i: (b_, i, 0))],
        out_shape=[jax.ShapeDtypeStruct((b, l, d), F32),
                   jax.ShapeDtypeStruct((b, l, d), BF16)],
        compiler_params=_params(("parallel", "parallel")),
        name="post0",
    )(attn, p, p, p, poolw_bd, pscale, w_out, h, g1, n2g, sh2, sc2)


def _swiglu_tile(x, wg_ref, wu_ref, wd_ref, lead=()):
    acc = None
    c0 = 0
    for cw in FFN_CHUNKS:
        hg = _dot(x, wg_ref[lead + (slice(None), slice(c0, c0 + cw))])
        hu = _dot(x, wu_ref[lead + (slice(None), slice(c0, c0 + cw))])
        act = (_silu(hg) * hu).astype(BF16)
        part = _dot(act, wd_ref[lead + (slice(c0, c0 + cw), slice(None))])
        acc = part if acc is None else acc + part
        c0 += cw
    return acc


def _ffn_kernel(a2_ref, h1_ref, g2_ref, wg_ref, wu_ref, wd_ref, o_ref):
    y = _swiglu_tile(a2_ref[0], wg_ref, wu_ref, wd_ref)
    o_ref[0] = h1_ref[0] + g2_ref[0] * y


def _ffn(a2, h1, g2, wg, wu, wd, *, tm):
    b, l, d = h1.shape
    f = wg.shape[1]
    assert sum(FFN_CHUNKS) == f
    return pl.pallas_call(
        _ffn_kernel,
        grid=(b, l // tm),
        in_specs=[pl.BlockSpec((1, tm, d), lambda b_, i: (b_, i, 0)),
                  pl.BlockSpec((1, tm, d), lambda b_, i: (b_, i, 0)),
                  _bvec_spec(g2, d),
                  pl.BlockSpec((d, f), lambda b_, i: (0, 0)),
                  pl.BlockSpec((d, f), lambda b_, i: (0, 0)),
                  pl.BlockSpec((f, d), lambda b_, i: (0, 0))],
        out_specs=pl.BlockSpec((1, tm, d), lambda b_, i: (b_, i, 0)),
        out_shape=jax.ShapeDtypeStruct((b, l, d), F32),
        compiler_params=_params(("parallel", "parallel")),
        name="ffn_dense",
    )(a2, h1, g2, wg, wu, wd)


def _pre1_kernel(h_ref, g_ref, sh_ref, sc_ref, win_ref, qg_ref, kvg_ref, wuq_ref, wuk_ref, wuv_ref,
                 cos_ref, sin_ref, *out_refs, use_rope, want_q, q_rank, kv_rank, n_heads):
    a = _norm_mod(h_ref[0], g_ref[...], sh_ref[0], sc_ref[0])
    z = _dot(a.astype(BF16), win_ref[...])
    cos = cos_ref[...]
    sin = sin_ref[...]
    if want_q:
        qt_ref, k_ref, vt_ref = out_refs
        cq = _rms(z[:, :q_rank]) * qg_ref[...]
        qf = _dot(cq.astype(BF16), wuq_ref[...])
        q_scale = (MLA_NOPE + MLA_ROPE) ** -0.5 * LOG2E
        for h in range(n_heads):
            c0 = h * MLA_QK_PAD
            qn = qf[:, c0:c0 + MLA_NOPE] * q_scale
            qr = _rope(qf[:, c0 + MLA_NOPE:c0 + MLA_QK_PAD], cos, sin) * q_scale
            qt_ref[0, 0, c0:c0 + MLA_NOPE, :] = qn.T.astype(BF16)
            qt_ref[0, 0, c0 + MLA_NOPE:c0 + MLA_QK_PAD, :] = qr.T.astype(BF16)
    else:
        k_ref, vt_ref = out_refs
    ckv = (_rms(z[:, q_rank:q_rank + kv_rank]) * kvg_ref[...]).astype(BF16)
    kn = _dot(ckv, wuk_ref[...])
    vv = _dot(ckv, wuv_ref[...])
    kr = z[:, q_rank + kv_rank:]
    if use_rope:
        kr = _rope(kr, cos, sin)
    kr = kr.astype(BF16)
    for h in range(n_heads):
        c0 = h * MLA_QK_PAD
        k_ref[0, :, c0:c0 + MLA_NOPE] = kn[:, h * MLA_NOPE:(h + 1) * MLA_NOPE].astype(BF16)
        k_ref[0, :, c0 + MLA_NOPE:c0 + MLA_QK_PAD] = kr
    _store_vt(vt_ref, vv, n_heads)


def _pre1(h, g, sh, sc, w_in, qg, kvg, wuq, wuk, wuv, cos, sin, *, use_rope, want_q, tm):
    b, l, d = h.shape
    n_heads = wuk.shape[1] // MLA_NOPE
    q_rank, kv_rank = wuq.shape[0], wuk.shape[0]
    kern = functools.partial(_pre1_kernel, use_rope=use_rope, want_q=want_q,
                             q_rank=q_rank, kv_rank=kv_rank, n_heads=n_heads)
    full = lambda x: pl.BlockSpec(x.shape, lambda b_, i: (0,) * x.ndim)
    qk_w = n_heads * MLA_QK_PAD
    v_rows = n_heads * V_AUG
    out_specs = [pl.BlockSpec((1, tm, qk_w), lambda b_, i: (b_, i, 0)),
                 pl.BlockSpec((1, 1, v_rows, tm), lambda b_, i: (b_, i, 0, 0))]
    out_shape = [jax.ShapeDtypeStruct((b, l, qk_w), BF16),
                 jax.ShapeDtypeStruct((b, l // tm, v_rows, tm), BF16)]
    if want_q:
        out_specs = [pl.BlockSpec((1, 1, qk_w, tm), lambda b_, i: (b_, i, 0, 0))] + out_specs
        out_shape = [jax.ShapeDtypeStruct((b, l // tm, qk_w, tm), BF16)] + out_shape
    return pl.pallas_call(
        kern,
        grid=(b, l // tm),
        in_specs=[pl.BlockSpec((1, tm, d), lambda b_, i: (b_, i, 0)),
                  pl.BlockSpec((1, d), lambda b_, i: (0, 0)),
                  _bvec_spec(sh, d), _bvec_spec(sc, d),
                  full(w_in), full(qg), full(kvg), full(wuq), full(wuk), full(wuv),
                  pl.BlockSpec((tm, LANES), lambda b_, i: (i, 0)),
                  pl.BlockSpec((tm, LANES), lambda b_, i: (i, 0))],
        out_specs=out_specs,
        out_shape=out_shape,
        compiler_params=_params(("parallel", "parallel")),
        name="pre1_lat" if want_q else "pre1_ctx",
    )(h, g, sh, sc, w_in, qg, kvg, wuq, wuk, wuv, cos, sin)


def _post1_kernel(o_ref, wout_ref, h_ref, g1_ref, n2g_ref, sh2_ref, sc2_ref, rwh_ref, rwl_ref,
                  h1_ref, a2_ref, gates_ref, sel_ref):
    y = _dot(o_ref[0], wout_ref[...])
    h1 = h_ref[0] + g1_ref[0] * y
    h1_ref[0] = h1
    a2 = _norm_mod(h1, n2g_ref[...], sh2_ref[0], sc2_ref[0])
    a2_ref[0] = a2
    a_hi = a2.astype(BF16)
    a_lo = (a2 - a_hi.astype(F32)).astype(BF16)
    logits = _dot(a_hi, rwh_ref[...]) + (_dot(a_lo, rwh_ref[...]) + _dot(a_hi, rwl_ref[...]))
    lane = lax.broadcasted_iota(jnp.int32, logits.shape, 1)
    lg = jnp.where(lane < N_EXPERTS, logits, -jnp.inf)
    m1 = jnp.max(lg, axis=1, keepdims=True)
    i1 = jnp.min(jnp.where(lg == m1, lane, LANES), axis=1, keepdims=True)
    lg2 = jnp.where(lane == i1, -jnp.inf, lg)
    m2 = jnp.max(lg2, axis=1, keepdims=True)
    i2 = jnp.min(jnp.where(lg2 == m2, lane, LANES), axis=1, keepdims=True)
    p2 = jnp.exp(m2 - m1)
    w1 = 1.0 / (1.0 + p2)
    gates_ref[0] = jnp.where(lane == i1, w1, 0.0) + jnp.where(lane == i2, p2 * w1, 0.0)
    sel_ref[0] = jnp.where((lane == i1) | (lane == i2), 1.0, 0.0).astype(BF16)


def _post1(o, w_out, h, g1, n2g, sh2, sc2, rw_hi, rw_lo, *, tm):
    b, l, d = h.shape
    return pl.pallas_call(
        _post1_kernel,
        grid=(b, l // tm),
        in_specs=[pl.BlockSpec((1, tm, o.shape[2]), lambda b_, i: (b_, i, 0)),
                  pl.BlockSpec(w_out.shape, lambda b_, i: (0, 0)),
                  pl.BlockSpec((1, tm, d), lambda b_, i: (b_, i, 0)),
                  _bvec_spec(g1, d),
                  pl.BlockSpec((1, d), lambda b_, i: (0, 0)),
                  _bvec_spec(sh2, d), _bvec_spec(sc2, d),
                  pl.BlockSpec(rw_hi.shape, lambda b_, i: (0, 0)),
                  pl.BlockSpec(rw_lo.shape, lambda b_, i: (0, 0))],
        out_specs=[pl.BlockSpec((1, tm, d), lambda b_, i: (b_, i, 0)),
                   pl.BlockSpec((1, tm, d), lambda b_, i: (b_, i, 0)),
                   pl.BlockSpec((1, tm, LANES), lambda b_, i: (b_, i, 0)),
                   pl.BlockSpec((1, tm, LANES), lambda b_, i: (b_, i, 0))],
        out_shape=[jax.ShapeDtypeStruct((b, l, d), F32),
                   jax.ShapeDtypeStruct((b, l, d), F32),
                   jax.ShapeDtypeStruct((b, l, LANES), F32),
                   jax.ShapeDtypeStruct((b, l, LANES), BF16)],
        compiler_params=_params(("parallel", "parallel")),
        name="post1_router",
    )(o, w_out, h, g1, n2g, sh2, sc2, rw_hi, rw_lo)


def _route_kernel(sel_ref, gates_ref, r_ref, tile_ref, cnt_ref, off_ref, *, tile_rows):
    phase = pl.program_id(0)
    i = pl.program_id(1)
    sel = sel_ref[...]
    tr = sel.shape[0]
    col_count = jnp.sum(sel.astype(F32), axis=0, keepdims=True)
    lane = lax.broadcasted_iota(jnp.int32, (1, LANES), 1)

    @pl.when((phase == 0) & (i == 0))
    def _():
        cnt_ref[...] = jnp.zeros_like(cnt_ref)

    @pl.when(phase == 0)
    def _():
        cnt_ref[...] += col_count

    @pl.when((phase == 0) & (i == pl.num_programs(1) - 1))
    def _():
        padded = jnp.ceil(cnt_ref[...] / tile_rows) * tile_rows
        incl = padded
        for sh in (1, 2, 4):
            incl = incl + jnp.where(lane >= sh, pltpu.roll(incl, sh, 1), 0.0)
        off_ref[...] = incl - padded
        tile_start = (lax.broadcasted_iota(jnp.int32, tile_ref.shape, 1) * tile_rows).astype(F32)
        tile_expert = jnp.zeros(tile_ref.shape, jnp.int32)
        for e in range(N_EXPERTS):
            end_e = jnp.sum(jnp.where(lane == e, incl, 0.0), axis=1, keepdims=True)
            tile_expert = tile_expert + (tile_start >= end_e).astype(jnp.int32)
        tile_ref[...] = tile_expert
        cnt_ref[...] = jnp.zeros_like(cnt_ref)

    @pl.when(phase == 1)
    def _():
        rr = lax.broadcasted_iota(jnp.int32, (tr, tr), 0)
        cc = lax.broadcasted_iota(jnp.int32, (tr, tr), 1)
        earlier = jnp.where(cc < rr, 1.0, 0.0).astype(BF16)
        rank = _dot(earlier, sel) + cnt_ref[...]
        cnt_ref[...] += col_count
        pos = off_ref[...] + rank
        chosen = sel > 0
        lo = jnp.min(jnp.where(chosen, pos, 3e38), axis=1, keepdims=True)
        hi = jnp.max(jnp.where(chosen, pos, -1.0), axis=1, keepdims=True)
        g = gates_ref[...]
        w_lo = jnp.sum(jnp.where(chosen & (pos == lo), g, 0.0), axis=1, keepdims=True)
        w_hi = jnp.sum(jnp.where(chosen & (pos == hi), g, 0.0), axis=1, keepdims=True)
        lane_t = lax.broadcasted_iota(jnp.int32, (tr, LANES), 1)
        r_ref[...] = jnp.where(lane_t == 0, lo, jnp.where(lane_t == 1, hi,
                               jnp.where(lane_t == 2, w_lo, jnp.where(lane_t == 3, w_hi, 0.0))))


def _route(sel, gates, *, tr, tile_rows, n_tiles):
    n = sel.shape[0]
    n_tiles_pad = -(-n_tiles // LANES) * LANES
    return pl.pallas_call(
        functools.partial(_route_kernel, tile_rows=tile_rows),
        grid=(2, n // tr),
        in_specs=[pl.BlockSpec((tr, LANES), lambda ph, i: (i, 0)),
                  pl.BlockSpec((tr, LANES), lambda ph, i: (i, 0))],
        out_specs=[pl.BlockSpec((tr, LANES), lambda ph, i: (i * ph, 0)),
                   pl.BlockSpec((1, n_tiles_pad), lambda ph, i: (0, 0))],
        out_shape=[jax.ShapeDtypeStruct((n, LANES), F32),
                   jax.ShapeDtypeStruct((1, n_tiles_pad), jnp.int32)],
        scratch_shapes=[pltpu.VMEM((1, LANES), F32), pltpu.VMEM((1, LANES), F32)],
        compiler_params=_params(("arbitrary", "arbitrary")),
        name="moe_route",
    )(sel, gates)


def _dispatch_kernel(dest_ref, x_ref, xs_in_hbm, xs_hbm, sem, *, tm):
    del xs_in_hbm

    def row_copy(src_row, dst_row):
        return pltpu.make_async_copy(x_ref.at[pl.ds(src_row, 1)], xs_hbm.at[pl.ds(dst_row, 1)], sem)

    def issue(r, carry):
        row_copy(r, dest_ref[0, 0, r]).start()
        row_copy(r, dest_ref[0, 0, tm + r]).start()
        return carry

    lax.fori_loop(0, tm, issue, 0, unroll=DMA_ISSUE_UNROLL)
    pltpu.make_async_copy(x_ref, xs_hbm.at[pl.ds(0, tm)], sem).wait()
    pltpu.make_async_copy(x_ref, xs_hbm.at[pl.ds(0, tm)], sem).wait()


def _dispatch(dest, x, n_rows, *, tm):
    n = x.shape[0]
    xs0 = jnp.zeros((n_rows,) + x.shape[1:], x.dtype)
    return pl.pallas_call(
        functools.partial(_dispatch_kernel, tm=tm),
        grid=(n // tm,),
        in_specs=[pl.BlockSpec((1, 1, 2 * tm), lambda i: (i, 0, 0), memory_space=pltpu.SMEM),
                  pl.BlockSpec((tm, x.shape[1]), lambda i: (i, 0)),
                  pl.BlockSpec(memory_space=pl.ANY)],
        out_specs=pl.BlockSpec(memory_space=pl.ANY),
        out_shape=jax.ShapeDtypeStruct(xs0.shape, xs0.dtype),
        scratch_shapes=[pltpu.SemaphoreType.DMA(())],
        input_output_aliases={2: 0},
        compiler_params=pltpu.CompilerParams(dimension_semantics=("arbitrary",),
                                             has_side_effects=True),
        name="moe_dispatch",
    )(dest, x, xs0)


def _experts_kernel(te_ref, x_ref, wg_ref, wu_ref, wd_ref, o_ref):
    valid = te_ref[pl.program_id(0)] < N_EXPERTS

    @pl.when(valid)
    def _():
        o_ref[...] = _swiglu_tile(x_ref[...].astype(BF16), wg_ref, wu_ref, wd_ref, lead=(0,))

    @pl.when(jnp.logical_not(valid))
    def _():
        o_ref[...] = jnp.zeros_like(o_ref)


def _experts(tile_expert, xs, wg, wu, wd, *, tile_rows):
    n_rows, d = xs.shape
    ne, _, f = wg.shape
    w_idx = lambda j, te: (jnp.minimum(te[j], ne - 1), 0, 0)
    return pl.pallas_call(
        _experts_kernel,
        grid_spec=pltpu.PrefetchScalarGridSpec(
            num_scalar_prefetch=1,
            grid=(n_rows // tile_rows,),
            in_specs=[pl.BlockSpec((tile_rows, d), lambda j, te: (j, 0)),
                      pl.BlockSpec((1, d, f), w_idx),
                      pl.BlockSpec((1, d, f), w_idx),
                      pl.BlockSpec((1, f, d), w_idx)],
            out_specs=pl.BlockSpec((tile_rows, d), lambda j, te: (j, 0))),
        out_shape=jax.ShapeDtypeStruct((n_rows, d), F32),
        compiler_params=_params(("arbitrary",)),
        name="moe_experts",
    )(tile_expert, xs, wg, wu, wd)


def _combine_kernel(dest_ref, routed_ref, ys_hbm, h1_ref, g2_ref, fng_ref, o_ref, ylo_ref, yhi_ref,
                    sem, *, tm):
    def issue(r, carry):
        pltpu.make_async_copy(ys_hbm.at[pl.ds(dest_ref[0, 0, r], 1)],
                              ylo_ref.at[pl.ds(r, 1)], sem).start()
        pltpu.make_async_copy(ys_hbm.at[pl.ds(dest_ref[0, 0, tm + r], 1)],
                              yhi_ref.at[pl.ds(r, 1)], sem).start()
        return carry

    lax.fori_loop(0, tm, issue, 0, unroll=DMA_ISSUE_UNROLL)
    pltpu.make_async_copy(ys_hbm.at[pl.ds(0, tm)], ylo_ref, sem).wait()
    pltpu.make_async_copy(ys_hbm.at[pl.ds(0, tm)], yhi_ref, sem).wait()
    routed = routed_ref[...]
    y = routed[:, 2:3] * ylo_ref[...] + routed[:, 3:4] * yhi_ref[...]
    out = h1_ref[...] + g2_ref[0] * y
    o_ref[...] = _rms(out) * fng_ref[...]


def _combine(dest, routed, ys, h1, g2, fng, *, tm, tiles_per_sample):
    n, d = h1.shape
    return pl.pallas_call(
        functools.partial(_combine_kernel, tm=tm),
        grid=(n // tm,),
        in_specs=[pl.BlockSpec((1, 1, 2 * tm), lambda i: (i, 0, 0), memory_space=pltpu.SMEM),
                  pl.BlockSpec((tm, LANES), lambda i: (i, 0)),
                  pl.BlockSpec(memory_space=pl.ANY),
                  pl.BlockSpec((tm, d), lambda i: (i, 0)),
                  pl.BlockSpec((1, 1, d), lambda i: (i // tiles_per_sample, 0, 0)),
                  pl.BlockSpec((1, d), lambda i: (0, 0))],
        out_specs=pl.BlockSpec((tm, d), lambda i: (i, 0)),
        out_shape=jax.ShapeDtypeStruct((n, d), F32),
        scratch_shapes=[pltpu.VMEM((tm, d), F32), pltpu.VMEM((tm, d), F32),
                        pltpu.SemaphoreType.DMA(())],
        compiler_params=_params(("arbitrary",)),
        name="moe_combine",
    )(dest, routed, ys, h1, g2, fng)


def _moe(a2, gates, sel, wg, wu, wd, h1, g2, fng, *, tm):
    b, l, d = h1.shape
    n = b * l
    n_rows = 2 * n + N_EXPERTS * MOE_TILE
    n_tiles = n_rows // MOE_TILE
    routed, tile_expert = _route(sel.reshape(n, LANES), gates.reshape(n, LANES),
                                 tr=tm, tile_rows=MOE_TILE, n_tiles=n_tiles)
    dest = routed[:, 0:2].astype(jnp.int32).reshape(n // tm, tm, 2)
    dest = dest.transpose(0, 2, 1).reshape(n // tm, 1, 2 * tm)
    xs = _dispatch(dest, a2.reshape(n, d), n_rows, tm=tm)
    ys = _experts(tile_expert[0, :n_tiles], xs, wg, wu, wd, tile_rows=MOE_TILE)
    out = _combine(dest, routed, ys, h1.reshape(n, d), g2, fng, tm=tm, tiles_per_sample=l // tm)
    return out.reshape(b, l, d)


def _rope_tables(seq_len, dim):
    t = jnp.arange(seq_len, dtype=jnp.int32)
    row = (t // GRID_W).astype(F32)
    col = (t % GRID_W).astype(F32)
    quarter = dim // 4
    inv = ROPE_BASE ** (-jnp.arange(quarter, dtype=F32) / quarter)
    ar = row[:, None] * inv[None, :]
    ac = col[:, None] * inv[None, :]
    cos = jnp.concatenate([jnp.cos(ar), jnp.cos(ar), jnp.cos(ac), jnp.cos(ac)], axis=-1)
    sin = jnp.concatenate([jnp.sin(ar), jnp.sin(ar), jnp.sin(ac), jnp.sin(ac)], axis=-1)
    reps = LANES // dim
    return jnp.tile(cos, (1, reps)), jnp.tile(sin, (1, reps))


def kernel(x, c, ctx, c_ctx, norm1_g, norm2_g, mod_w, mod_b, even_w_in, pool_w, pool_scale,
           lambda_q1, lambda_k1, lambda_q2, lambda_k2, even_w_out, ffn_w_gate, ffn_w_up,
           ffn_w_down, odd_w_in, q_norm_g, kv_norm_g, w_uq, w_ukv, odd_w_out, router_w,
           moe_w_gate, moe_w_up, moe_w_down, final_norm_g):
    b, l, d = x.shape
    n_ctx = ctx.shape[1]
    tm = min(ROW_TILE, l)
    tkv = min(KV_TILE, l)
    tq = min(DIFF_Q_TILE, l)
    cos, sin = _rope_tables(l, DIFF_HEAD_DIM)
    cos_c, sin_c = cos[:n_ctx], sin[:n_ctx]

    cond = jnp.zeros((8, d), F32).at[:b].set(c).at[b].set(c_ctx)
    mod = _modulation(cond, mod_w, mod_b)
    mod = mod.reshape(mod.shape[0], 8, N_MOD, d)

    def mod_vecs(layer):
        lat = [mod[layer, :b, k][:, None, :] for k in range(N_MOD)]
        cx = [mod[layer, b:b + 1, k][:, None, :] for k in range(N_MOD)]
        return lat, cx

    (sh1, sc1, g1, sh2, sc2, g2), (csh1, csc1, cg1, csh2, csc2, cg2) = mod_vecs(0)
    lam_init = 0.8 - 0.6 * math.exp(-0.3 * 0)
    w_in0 = even_w_in[0].astype(BF16)
    n1g = norm1_g[0][None, :]
    n2g = norm2_g[0][None, :]
    p_lat, q1t_lat, q2t_lat, k_lat, vt_lat = _pre0(x, n1g, sh1, sc1, w_in0, cos, sin,
                                                   use_rope=True, tm=tkv)
    p_ctx, q1t_ctx, q2t_ctx, k_ctx, vt_ctx = _pre0(ctx, n1g, csh1, csc1, w_in0, cos_c, sin_c,
                                                   use_rope=False, tm=n_ctx)
    lq = jnp.stack([lambda_q1[0], lambda_q2[0]])
    lk = jnp.stack([lambda_k1[0], lambda_k2[0]])
    diff_kern = functools.partial(_diff_attn_kernel, lam_init=lam_init)
    attn_lat = _attention(diff_kern, (q1t_lat, q2t_lat), (k_ctx, vt_ctx), (k_lat, vt_lat),
                          2 * DIFF_HEAD_DIM, extra=(lq, lk), tq=tq, name="diff_attn_lat")
    attn_ctx = _attention(diff_kern, (q1t_ctx, q2t_ctx), (k_ctx, vt_ctx), None,
                          2 * DIFF_HEAD_DIM, extra=(lq, lk), tq=n_ctx, name="diff_attn_ctx")
    ng = len(POOL_WINDOWS)
    poolw_bd = jnp.zeros((POOL_WIDTH, POOL_WIDTH), F32)
    for g in range(ng):
        sl = slice(g * POOL_GROUP, (g + 1) * POOL_GROUP)
        poolw_bd = poolw_bd.at[sl, sl].set(pool_w[0, g])
    poolw_bd = poolw_bd.astype(BF16)
    pscale = pool_scale[0][None, :]
    w_out0 = even_w_out[0].astype(BF16)
    wg0, wu0, wd0 = ffn_w_gate[0].astype(BF16), ffn_w_up[0].astype(BF16), ffn_w_down[0].astype(BF16)
    h1_lat, a2_lat = _post0(attn_lat, p_lat, poolw_bd, pscale, w_out0, x, g1, n2g, sh2, sc2, tm=tm)
    h_lat = _ffn(a2_lat, h1_lat, g2, wg0, wu0, wd0, tm=tm)
    h1_ctx, a2_ctx = _post0(attn_ctx, p_ctx, poolw_bd, pscale, w_out0, ctx, cg1, n2g, csh2, csc2,
                            tm=n_ctx)
    h_ctx = _ffn(a2_ctx, h1_ctx, cg2, wg0, wu0, wd0, tm=n_ctx)

    (sh1, sc1, g1, sh2, sc2, g2), (csh1, csc1, _, _, _, _) = mod_vecs(1)
    n1g = norm1_g[1][None, :]
    n2g = norm2_g[1][None, :]
    q_rank, kv_rank = w_uq.shape[1], w_ukv.shape[1]
    n_heads = w_uq.shape[2] // (MLA_NOPE + MLA_ROPE)
    w_in1 = jnp.pad(odd_w_in[0], ((0, 0), (0, LANES - MLA_ROPE))).astype(BF16)
    wuq = w_uq[0].reshape(q_rank, n_heads, MLA_NOPE + MLA_ROPE)
    wuq = jnp.pad(wuq, ((0, 0), (0, 0), (0, MLA_QK_PAD - MLA_NOPE - MLA_ROPE)))
    wuq = wuq.reshape(q_rank, n_heads * MLA_QK_PAD).astype(BF16)
    wukv = w_ukv[0].reshape(kv_rank, n_heads, MLA_NOPE + MLA_V)
    wuk = wukv[:, :, :MLA_NOPE].reshape(kv_rank, n_heads * MLA_NOPE).astype(BF16)
    wuv = wukv[:, :, MLA_NOPE:].reshape(kv_rank, n_heads * MLA_V).astype(BF16)
    qg = q_norm_g[0][None, :]
    kvg = kv_norm_g[0][None, :]
    q1t, k1, vt1 = _pre1(h_lat, n1g, sh1, sc1, w_in1, qg, kvg, wuq, wuk, wuv, cos, sin,
                         use_rope=True, want_q=True, tm=tkv)
    k1c, vt1c = _pre1(h_ctx, n1g, csh1, csc1, w_in1, qg, kvg, wuq, wuk, wuv, cos_c, sin_c,
                      use_rope=False, want_q=False, tm=n_ctx)
    o1 = _attention(_mla_attn_kernel, (q1t,), (k1c, vt1c), (k1, vt1), MLA_QK_PAD,
                    tq=min(MLA_Q_TILE, l), name="mla_attn")
    rw_pad = jnp.pad(router_w[0], ((0, 0), (0, LANES - N_EXPERTS)))
    rw_hi = rw_pad.astype(BF16)
    rw_lo = (rw_pad - rw_hi.astype(F32)).astype(BF16)
    h1, a2, gates, sel = _post1(o1, odd_w_out[0].astype(BF16), h_lat, g1, n2g, sh2, sc2,
                                rw_hi, rw_lo, tm=tm)
    return _moe(a2, gates, sel, moe_w_gate[0].astype(BF16), moe_w_up[0].astype(BF16),
                moe_w_down[0].astype(BF16), h1, g2, final_norm_g[None, :], tm=tm)
```

```python
import functools
import math

import jax
import jax.numpy as jnp
from jax import lax
from jax.experimental import pallas as pl
from jax.experimental.pallas import tpu as pltpu

F32 = jnp.float32
BF16 = jnp.bfloat16

EPS = 1e-6
GRID_W = 64
ROPE_BASE = 10000.0
N_MOD = 6
POOL_WINDOWS = (2, 4, 8, 16)
POOL_GROUP = 64
POOL_WIDTH = POOL_GROUP * len(POOL_WINDOWS)
POOL_HALO = max(POOL_WINDOWS) // 2
DIFF_HEAD_DIM = 64
DIFF_V_DIM = 128
V_ONES_ROWS = 16
V_AUG = DIFF_V_DIM + V_ONES_ROWS
MLA_NOPE = 128
MLA_ROPE = 64
MLA_V = 128
MLA_QK_PAD = 256
N_EXPERTS = 8
ROPE_QUARTER = 16
LOG2E = math.log2(math.e)

LANES = 128
ROW_TILE = 512
KV_TILE = 512
DIFF_Q_TILE = 512
MLA_Q_TILE = 1024
PAIR_UNROLL = 5
MOE_TILE = 512
DMA_ISSUE_UNROLL = 8
FFN_CHUNKS = (512, 512, 512, 512, 512, 256)
VMEM_LIMIT = 56 * 1024 * 1024


def _params(sem):
    return pltpu.CompilerParams(dimension_semantics=sem, vmem_limit_bytes=VMEM_LIMIT)


def _dot(a, b):
    return jnp.dot(a, b, preferred_element_type=F32)


def _rms(x):
    return x * lax.rsqrt(jnp.mean(x * x, axis=-1, keepdims=True) + EPS)


def _silu(x):
    return x * (1.0 / (1.0 + jnp.exp(-x)))


def _rope(x, cos, sin):
    lane = lax.broadcasted_iota(jnp.int32, x.shape, 1)
    even = (lane // ROPE_QUARTER) % 2 == 0
    rot = jnp.where(even, -pltpu.roll(x, LANES - ROPE_QUARTER, 1), pltpu.roll(x, ROPE_QUARTER, 1))
    return x * cos + rot * sin


def _store_vt(vt_ref, v, n_heads):
    tm = v.shape[0]
    ones = jnp.ones((V_ONES_ROWS, tm), BF16)
    for h in range(n_heads):
        r0 = h * V_AUG
        vt_ref[0, 0, r0:r0 + DIFF_V_DIM, :] = v[:, h * DIFF_V_DIM:(h + 1) * DIFF_V_DIM].T.astype(BF16)
        vt_ref[0, 0, r0 + DIFF_V_DIM:r0 + V_AUG, :] = ones


def _mod_kernel(cond_ref, w_ref, b_ref, o_ref):
    s = _silu(cond_ref[...])
    o_ref[0] = jnp.dot(s, w_ref[0], preferred_element_type=F32,
                       precision=lax.Precision.HIGHEST) + b_ref[0]


def _modulation(cond, mod_w, mod_b):
    depth, d, n = mod_w.shape
    tn = n // 4
    return pl.pallas_call(
        _mod_kernel,
        grid=(depth, n // tn),
        in_specs=[pl.BlockSpec((8, d), lambda i, j: (0, 0)),
                  pl.BlockSpec((1, d, tn), lambda i, j: (i, 0, j)),
                  pl.BlockSpec((1, 1, tn), lambda i, j: (i, 0, j))],
        out_specs=pl.BlockSpec((1, 8, tn), lambda i, j: (i, 0, j)),
        out_shape=jax.ShapeDtypeStruct((depth, 8, n), F32),
        compiler_params=_params(("arbitrary", "arbitrary")),
        name="modulation",
    )(cond, mod_w, mod_b.reshape(depth, 1, n))


def _bvec_spec(arr, d):
    if arr.shape[0] == 1:
        return pl.BlockSpec((1, 1, d), lambda b, i: (0, 0, 0))
    return pl.BlockSpec((1, 1, d), lambda b, i: (b, 0, 0))


def _norm_mod(h, g, sh, sc):
    a = _rms(h) * g
    return a * (1.0 + sc) + sh


def _pre0_kernel(h_ref, g_ref, sh_ref, sc_ref, w_ref, cos_ref, sin_ref,
                 p_ref, q1t_ref, q2t_ref, k_ref, vt_ref, *, use_rope, n_heads):
    a = _norm_mod(h_ref[0], g_ref[...], sh_ref[0], sc_ref[0])
    z = _dot(a.astype(BF16), w_ref[...])
    p_ref[0] = z[:, :POOL_WIDTH]
    head_w = 2 * DIFF_HEAD_DIM
    qk_w = n_heads * head_w
    q_scale = DIFF_HEAD_DIM ** -0.5 * LOG2E
    cos = cos_ref[...]
    sin = sin_ref[...]
    for h in range(n_heads):
        xq = z[:, POOL_WIDTH + h * head_w:POOL_WIDTH + (h + 1) * head_w]
        xk = z[:, POOL_WIDTH + qk_w + h * head_w:POOL_WIDTH + qk_w + (h + 1) * head_w]
        if use_rope:
            xq = _rope(xq, cos, sin)
            xk = _rope(xk, cos, sin)
        xqt = (xq * q_scale).T
        row = lax.broadcasted_iota(jnp.int32, xqt.shape, 0)
        q1t_ref[0, 0, h * head_w:(h + 1) * head_w, :] = jnp.where(row < DIFF_HEAD_DIM, xqt, 0.0).astype(BF16)
        q2t_ref[0, 0, h * head_w:(h + 1) * head_w, :] = jnp.where(row >= DIFF_HEAD_DIM, xqt, 0.0).astype(BF16)
        k_ref[0, :, h * head_w:(h + 1) * head_w] = xk.astype(BF16)
    _store_vt(vt_ref, z[:, POOL_WIDTH + 2 * qk_w:], n_heads)


def _pre0(h, g, sh, sc, w_in, cos, sin, *, use_rope, tm):
    b, l, d = h.shape
    n_in = w_in.shape[1]
    v_w = (n_in - POOL_WIDTH) // 3
    n_heads = v_w // DIFF_V_DIM
    kern = functools.partial(_pre0_kernel, use_rope=use_rope, n_heads=n_heads)
    return pl.pallas_call(
        kern,
        grid=(b, l // tm),
        in_specs=[pl.BlockSpec((1, tm, d), lambda b_, i: (b_, i, 0)),
                  pl.BlockSpec((1, d), lambda b_, i: (0, 0)),
                  _bvec_spec(sh, d), _bvec_spec(sc, d),
                  pl.BlockSpec((d, n_in), lambda b_, i: (0, 0)),
                  pl.BlockSpec((tm, LANES), lambda b_, i: (i, 0)),
                  pl.BlockSpec((tm, LANES), lambda b_, i: (i, 0))],
        out_specs=[pl.BlockSpec((1, tm, POOL_WIDTH), lambda b_, i: (b_, i, 0)),
                   pl.BlockSpec((1, 1, v_w, tm), lambda b_, i: (b_, i, 0, 0)),
                   pl.BlockSpec((1, 1, v_w, tm), lambda b_, i: (b_, i, 0, 0)),
                   pl.BlockSpec((1, tm, v_w), lambda b_, i: (b_, i, 0)),
                   pl.BlockSpec((1, 1, n_heads * V_AUG, tm), lambda b_, i: (b_, i, 0, 0))],
        out_shape=[jax.ShapeDtypeStruct((b, l, POOL_WIDTH), F32),
                   jax.ShapeDtypeStruct((b, l // tm, v_w, tm), BF16),
                   jax.ShapeDtypeStruct((b, l // tm, v_w, tm), BF16),
                   jax.ShapeDtypeStruct((b, l, v_w), BF16),
                   jax.ShapeDtypeStruct((b, l // tm, n_heads * V_AUG, tm), BF16)],
        compiler_params=_params(("parallel", "parallel")),
        name="pre0_rope" if use_rope else "pre0_ctx",
    )(h, g, sh, sc, w_in, cos, sin)


def _attn_core(qt_refs, kc_ref, vc_ref, lat_refs, acc_refs, sbuf_refs):
    n = len(qt_refs)
    qts = [jnp.concatenate([r[0, j] for j in range(r.shape[1])], axis=1) for r in qt_refs]
    tq = qts[0].shape[1]
    ctx_bufs = sbuf_refs[:n]
    lat_bufs = sbuf_refs[n:]
    ms = tuple(jnp.full((1, tq), -jnp.inf, F32) for _ in range(n))
    for i in range(n):
        acc_refs[i][...] = jnp.zeros_like(acc_refs[i])

    def scores(k, bufs):
        cmax = []
        for i in range(n):
            s = _dot(k, qts[i])
            bufs[i][...] = s
            cmax.append(jnp.max(s, axis=0, keepdims=True))
        return tuple(cmax)

    def softmax_pv(vt, bufs, cmax, ms):
        new_m = []
        for i in range(n):
            m_new = jnp.maximum(ms[i], cmax[i])
            alpha = jnp.exp2(ms[i] - m_new)
            p = jnp.exp2(bufs[i][...] - m_new)
            acc_refs[i][...] = alpha * acc_refs[i][...] + _dot(vt, p.astype(BF16))
            new_m.append(m_new)
        return tuple(new_m)

    if lat_refs is None:
        softmax_pv(vc_ref[0, 0], ctx_bufs, scores(kc_ref[0], ctx_bufs), ms)
        return
    kl_ref, vl_ref = lat_refs
    nblk, tk = vl_ref.shape[1], vl_ref.shape[3]
    assert nblk >= 2 and nblk % 2 == 0

    def qk(t, which):
        start = t * tk if isinstance(t, int) else pl.multiple_of(t * tk, tk)
        return scores(kl_ref[0, pl.ds(start, tk), :], lat_bufs[which::2])

    def pair(jj, carry):
        cmax_a, ms = carry
        t = 2 * jj
        cmax_b = qk(t + 1, 1)
        ms = softmax_pv(vl_ref[0, t], lat_bufs[0::2], cmax_a, ms)
        cmax_a = qk(t + 2, 0)
        ms = softmax_pv(vl_ref[0, t + 1], lat_bufs[1::2], cmax_b, ms)
        return cmax_a, ms

    cmax_a = qk(0, 0)
    n_pairs = nblk // 2 - 1
    cmax_a, ms = lax.fori_loop(0, n_pairs, pair, (cmax_a, ms),
                               unroll=max(1, min(PAIR_UNROLL, n_pairs)))
    cmax_b = qk(nblk - 1, 1)
    ms = softmax_pv(vl_ref[0, nblk - 2], lat_bufs[0::2], cmax_a, ms)
    cmax_c = scores(kc_ref[0], ctx_bufs)
    ms = softmax_pv(vl_ref[0, nblk - 1], lat_bufs[1::2], cmax_b, ms)
    softmax_pv(vc_ref[0, 0], ctx_bufs, cmax_c, ms)


def _normalised(acc_ref):
    acc = acc_ref[...]
    return acc[:DIFF_V_DIM] * (1.0 / acc[DIFF_V_DIM:DIFF_V_DIM + 1])


def _diff_attn_kernel(lq_ref, lk_ref, q1t_ref, q2t_ref, kc_ref, vc_ref, *rest, has_lat, lam_init):
    lat_refs = rest[:2] if has_lat else None
    rest = rest[2:] if has_lat else rest
    o_ref, acc1_ref, acc2_ref = rest[:3]
    _attn_core((q1t_ref, q2t_ref), kc_ref, vc_ref, lat_refs, (acc1_ref, acc2_ref), rest[3:])
    e = jnp.exp(jnp.sum(lq_ref[...] * lk_ref[...], axis=-1, keepdims=True))
    lam = e[0:1] - e[1:2] + lam_init
    o = _normalised(acc1_ref) - lam * _normalised(acc2_ref)
    ms = jnp.mean(o * o, axis=0, keepdims=True)
    y = o * (lax.rsqrt(ms + EPS) * (1.0 - lam_init))
    o_ref[0] = y.T.astype(BF16)


def _mla_attn_kernel(qt_ref, kc_ref, vc_ref, *rest, has_lat):
    lat_refs = rest[:2] if has_lat else None
    rest = rest[2:] if has_lat else rest
    o_ref, acc_ref = rest[:2]
    _attn_core((qt_ref,), kc_ref, vc_ref, lat_refs, (acc_ref,), rest[2:])
    o_ref[0] = _normalised(acc_ref).T.astype(BF16)


def _attention(kern, qts, ctx_kv, lat_kv, qk_w, extra=(), *, tq, name):
    n_maps = len(qts)
    b, nqb, hw, tmq = qts[0].shape
    n_heads = hw // qk_w
    lq = nqb * tmq
    in_specs = [pl.BlockSpec(x.shape, lambda b_, h, i: (0, 0)) for x in extra]
    in_specs += [pl.BlockSpec((1, tq // tmq, qk_w, tmq), lambda b_, h, i: (b_, i, h, 0)) for _ in qts]
    args = list(extra) + list(qts)
    scratch = [pltpu.VMEM((V_AUG, tq), F32) for _ in range(n_maps)]
    scratch += [pltpu.VMEM((ctx_kv[0].shape[1], tq), F32) for _ in range(n_maps)]
    for kv in (ctx_kv, lat_kv):
        if kv is None:
            continue
        k, vt = kv
        nblk, tk = vt.shape[1], vt.shape[3]
        in_specs.append(pl.BlockSpec((1, k.shape[1], qk_w), lambda b_, h, i: (b_, 0, h)))
        in_specs.append(pl.BlockSpec((1, nblk, V_AUG, tk), lambda b_, h, i: (b_, 0, h, 0)))
        args += [k, vt]
    if lat_kv is not None:
        scratch += [pltpu.VMEM((lat_kv[1].shape[3], tq), F32) for _ in range(2 * n_maps)]
    return pl.pallas_call(
        functools.partial(kern, has_lat=lat_kv is not None),
        grid=(b, n_heads, lq // tq),
        in_specs=in_specs,
        out_specs=pl.BlockSpec((1, tq, DIFF_V_DIM), lambda b_, h, i: (b_, i, h)),
        out_shape=jax.ShapeDtypeStruct((b, lq, n_heads * DIFF_V_DIM), BF16),
        scratch_shapes=scratch,
        compiler_params=_params(("parallel", "parallel", "arbitrary")),
        name=name,
    )(*args)


def _post0_kernel(attn_ref, p_ref, pprev_ref, pnext_ref, poolw_ref, pscale_ref, wout_ref,
                  h_ref, g1_ref, n2g_ref, sh2_ref, sc2_ref, h1_ref, a2_ref, *, seq_len):
    i = pl.program_id(1)
    u = p_ref[0]
    tm = u.shape[0]
    ext = jnp.concatenate([pprev_ref[0], u, pnext_ref[0]], axis=0)
    row = i * tm - POOL_HALO + lax.broadcasted_iota(jnp.int32, (tm + 2 * POOL_HALO, 1), 0)
    ext = jnp.where((row >= 0) & (row < seq_len), ext, 0.0)

    def shifted(j):
        return ext[POOL_HALO + j:POOL_HALO + j + tm]

    t = i * tm + lax.broadcasted_iota(jnp.int32, (tm, 1), 0)
    lane = lax.broadcasted_iota(jnp.int32, (tm, POOL_WIDTH), 1)
    win_sum = shifted(-1) + shifted(0)
    pooled = jnp.zeros_like(u)
    for g, w in enumerate(POOL_WINDOWS):
        half = w // 2
        if g > 0:
            prev_half = POOL_WINDOWS[g - 1] // 2
            for j in range(prev_half, half):
                win_sum = win_sum + shifted(-j - 1) + shifted(j)
        cnt = (jnp.minimum(t + half, seq_len) - jnp.maximum(t - half, 0)).astype(F32)
        in_group = (lane >= g * POOL_GROUP) & (lane < (g + 1) * POOL_GROUP)
        pooled = jnp.where(in_group, win_sum / cnt, pooled)
    pooled = pooled - u
    y_pool = _dot(pooled.astype(BF16), poolw_ref[...]) * pscale_ref[...]
    y = _dot(y_pool.astype(BF16), wout_ref[:POOL_WIDTH, :]) + _dot(attn_ref[0], wout_ref[POOL_WIDTH:, :])
    h1 = h_ref[0] + g1_ref[0] * y
    h1_ref[0] = h1
    a2_ref[0] = _norm_mod(h1, n2g_ref[...], sh2_ref[0], sc2_ref[0]).astype(BF16)


def _post0(attn, p, poolw_bd, pscale, w_out, h, g1, n2g, sh2, sc2, *, tm):
    b, l, d = h.shape
    hb = tm // POOL_HALO
    nhb = l // POOL_HALO
    return pl.pallas_call(
        functools.partial(_post0_kernel, seq_len=l),
        grid=(b, l // tm),
        in_specs=[pl.BlockSpec((1, tm, attn.shape[2]), lambda b_, i: (b_, i, 0)),
                  pl.BlockSpec((1, tm, POOL_WIDTH), lambda b_, i: (b_, i, 0)),
                  pl.BlockSpec((1, POOL_HALO, POOL_WIDTH),
                               lambda b_, i: (b_, jnp.maximum(i * hb - 1, 0), 0)),
                  pl.BlockSpec((1, POOL_HALO, POOL_WIDTH),
                               lambda b_, i: (b_, jnp.minimum((i + 1) * hb, nhb - 1), 0)),
                  pl.BlockSpec(poolw_bd.shape, lambda b_, i: (0, 0)),
                  pl.BlockSpec((1, POOL_WIDTH), lambda b_, i: (0, 0)),
                  pl.BlockSpec(w_out.shape, lambda b_, i: (0, 0)),
                  pl.BlockSpec((1, tm, d), lambda b_, i: (b_, i, 0)),
                  _bvec_spec(g1, d),
                  pl.BlockSpec((1, d), lambda b_, i: (0, 0)),
                  _bvec_spec(sh2, d), _bvec_spec(sc2, d)],
        out_specs=[pl.BlockSpec((1, tm, d), lambda b_, i: (b_, i, 0)),
                   pl.BlockSpec((1, tm, d), lambda b_, i: (b_, i, 0))],
        out_shape=[jax.ShapeDtypeStruct((b, l, d), F32),
                   jax.ShapeDtypeStruct((b, l, d), BF16)],
        compiler_params=_params(("parallel", "parallel")),
        name="post0",
    )(attn, p, p, p, poolw_bd, pscale, w_out, h, g1, n2g, sh2, sc2)


def _swiglu_tile(x, wg_ref, wu_ref, wd_ref, lead=()):
    acc = None
    c0 = 0
    for cw in FFN_CHUNKS:
        hg = _dot(x, wg_ref[lead + (slice(None), slice(c0, c0 + cw))])
        hu = _dot(x, wu_ref[lead + (slice(None), slice(c0, c0 + cw))])
        act = (_silu(hg) * hu).astype(BF16)
        part = _dot(act, wd_ref[lead + (slice(c0, c0 + cw), slice(None))])
        acc = part if acc is None else acc + part
        c0 += cw
    return acc


def _ffn_kernel(a2_ref, h1_ref, g2_ref, wg_ref, wu_ref, wd_ref, o_ref):
    y = _swiglu_tile(a2_ref[0], wg_ref, wu_ref, wd_ref)
    o_ref[0] = h1_ref[0] + g2_ref[0] * y


def _ffn(a2, h1, g2, wg, wu, wd, *, tm):
    b, l, d = h1.shape
    f = wg.shape[1]
    assert sum(FFN_CHUNKS) == f
    return pl.pallas_call(
        _ffn_kernel,
        grid=(b, l // tm),
        in_specs=[pl.BlockSpec((1, tm, d), lambda b_, i: (b_, i, 0)),
                  pl.BlockSpec((1, tm, d), lambda b_, i: (b_, i, 0)),
                  _bvec_spec(g2, d),
                  pl.BlockSpec((d, f), lambda b_, i: (0, 0)),
                  pl.BlockSpec((d, f), lambda b_, i: (0, 0)),
                  pl.BlockSpec((f, d), lambda b_, i: (0, 0))],
        out_specs=pl.BlockSpec((1, tm, d), lambda b_, i: (b_, i, 0)),
        out_shape=jax.ShapeDtypeStruct((b, l, d), F32),
        compiler_params=_params(("parallel", "parallel")),
        name="ffn_dense",
    )(a2, h1, g2, wg, wu, wd)


def _pre1_kernel(h_ref, g_ref, sh_ref, sc_ref, win_ref, qg_ref, kvg_ref, wuq_ref, wuk_ref, wuv_ref,
                 cos_ref, sin_ref, *out_refs, use_rope, want_q, q_rank, kv_rank, n_heads):
    a = _norm_mod(h_ref[0], g_ref[...], sh_ref[0], sc_ref[0])
    z = _dot(a.astype(BF16), win_ref[...])
    cos = cos_ref[...]
    sin = sin_ref[...]
    if want_q:
        qt_ref, k_ref, vt_ref = out_refs
        cq = _rms(z[:, :q_rank]) * qg_ref[...]
        qf = _dot(cq.astype(BF16), wuq_ref[...])
        q_scale = (MLA_NOPE + MLA_ROPE) ** -0.5 * LOG2E
        for h in range(n_heads):
            c0 = h * MLA_QK_PAD
            qn = qf[:, c0:c0 + MLA_NOPE] * q_scale
            qr = _rope(qf[:, c0 + MLA_NOPE:c0 + MLA_QK_PAD], cos, sin) * q_scale
            qt_ref[0, 0, c0:c0 + MLA_NOPE, :] = qn.T.astype(BF16)
            qt_ref[0, 0, c0 + MLA_NOPE:c0 + MLA_QK_PAD, :] = qr.T.astype(BF16)
    else:
        k_ref, vt_ref = out_refs
    ckv = (_rms(z[:, q_rank:q_rank + kv_rank]) * kvg_ref[...]).astype(BF16)
    kn = _dot(ckv, wuk_ref[...])
    vv = _dot(ckv, wuv_ref[...])
    kr = z[:, q_rank + kv_rank:]
    if use_rope:
        kr = _rope(kr, cos, sin)
    kr = kr.astype(BF16)
    for h in range(n_heads):
        c0 = h * MLA_QK_PAD
        k_ref[0, :, c0:c0 + MLA_NOPE] = kn[:, h * MLA_NOPE:(h + 1) * MLA_NOPE].astype(BF16)
        k_ref[0, :, c0 + MLA_NOPE:c0 + MLA_QK_PAD] = kr
    _store_vt(vt_ref, vv, n_heads)


def _pre1(h, g, sh, sc, w_in, qg, kvg, wuq, wuk, wuv, cos, sin, *, use_rope, want_q, tm):
    b, l, d = h.shape
    n_heads = wuk.shape[1] // MLA_NOPE
    q_rank, kv_rank = wuq.shape[0], wuk.shape[0]
    kern = functools.partial(_pre1_kernel, use_rope=use_rope, want_q=want_q,
                             q_rank=q_rank, kv_rank=kv_rank, n_heads=n_heads)
    full = lambda x: pl.BlockSpec(x.shape, lambda b_, i: (0,) * x.ndim)
    qk_w = n_heads * MLA_QK_PAD
    v_rows = n_heads * V_AUG
    out_specs = [pl.BlockSpec((1, tm, qk_w), lambda b_, i: (b_, i, 0)),
                 pl.BlockSpec((1, 1, v_rows, tm), lambda b_, i: (b_, i, 0, 0))]
    out_shape = [jax.ShapeDtypeStruct((b, l, qk_w), BF16),
                 jax.ShapeDtypeStruct((b, l // tm, v_rows, tm), BF16)]
    if want_q:
        out_specs = [pl.BlockSpec((1, 1, qk_w, tm), lambda b_, i: (b_, i, 0, 0))] + out_specs
        out_shape = [jax.ShapeDtypeStruct((b, l // tm, qk_w, tm), BF16)] + out_shape
    return pl.pallas_call(
        kern,
        grid=(b, l // tm),
        in_specs=[pl.BlockSpec((1, tm, d), lambda b_, i: (b_, i, 0)),
                  pl.BlockSpec((1, d), lambda b_, i: (0, 0)),
                  _bvec_spec(sh, d), _bvec_spec(sc, d),
                  full(w_in), full(qg), full(kvg), full(wuq), full(wuk), full(wuv),
                  pl.BlockSpec((tm, LANES), lambda b_, i: (i, 0)),
                  pl.BlockSpec((tm, LANES), lambda b_, i: (i, 0))],
        out_specs=out_specs,
        out_shape=out_shape,
        compiler_params=_params(("parallel", "parallel")),
        name="pre1_lat" if want_q else "pre1_ctx",
    )(h, g, sh, sc, w_in, qg, kvg, wuq, wuk, wuv, cos, sin)


def _post1_kernel(o_ref, wout_ref, h_ref, g1_ref, n2g_ref, sh2_ref, sc2_ref, rwh_ref, rwl_ref,
                  h1_ref, a2_ref, gates_ref, sel_ref):
    y = _dot(o_ref[0], wout_ref[...])
    h1 = h_ref[0] + g1_ref[0] * y
    h1_ref[0] = h1
    a2 = _norm_mod(h1, n2g_ref[...], sh2_ref[0], sc2_ref[0])
    a2_ref[0] = a2
    a_hi = a2.astype(BF16)
    a_lo = (a2 - a_hi.astype(F32)).astype(BF16)
    logits = _dot(a_hi, rwh_ref[...]) + (_dot(a_lo, rwh_ref[...]) + _dot(a_hi, rwl_ref[...]))
    lane = lax.broadcasted_iota(jnp.int32, logits.shape, 1)
    lg = jnp.where(lane < N_EXPERTS, logits, -jnp.inf)
    m1 = jnp.max(lg, axis=1, keepdims=True)
    i1 = jnp.min(jnp.where(lg == m1, lane, LANES), axis=1, keepdims=True)
    lg2 = jnp.where(lane == i1, -jnp.inf, lg)
    m2 = jnp.max(lg2, axis=1, keepdims=True)
    i2 = jnp.min(jnp.where(lg2 == m2, lane, LANES), axis=1, keepdims=True)
    p2 = jnp.exp(m2 - m1)
    w1 = 1.0 / (1.0 + p2)
    gates_ref[0] = jnp.where(lane == i1, w1, 0.0) + jnp.where(lane == i2, p2 * w1, 0.0)
    sel_ref[0] = jnp.where((lane == i1) | (lane == i2), 1.0, 0.0).astype(BF16)


def _post1(o, w_out, h, g1, n2g, sh2, sc2, rw_hi, rw_lo, *, tm):
    b, l, d = h.shape
    return pl.pallas_call(
        _post1_kernel,
        grid=(b, l // tm),
        in_specs=[pl.BlockSpec((1, tm, o.shape[2]), lambda b_, i: (b_, i, 0)),
                  pl.BlockSpec(w_out.shape, lambda b_, i: (0, 0)),
                  pl.BlockSpec((1, tm, d), lambda b_, i: (b_, i, 0)),
                  _bvec_spec(g1, d),
                  pl.BlockSpec((1, d), lambda b_, i: (0, 0)),
                  _bvec_spec(sh2, d), _bvec_spec(sc2, d),
                  pl.BlockSpec(rw_hi.shape, lambda b_, i: (0, 0)),
                  pl.BlockSpec(rw_lo.shape, lambda b_, i: (0, 0))],
        out_specs=[pl.BlockSpec((1, tm, d), lambda b_, i: (b_, i, 0)),
                   pl.BlockSpec((1, tm, d), lambda b_, i: (b_, i, 0)),
                   pl.BlockSpec((1, tm, LANES), lambda b_, i: (b_, i, 0)),
                   pl.BlockSpec((1, tm, LANES), lambda b_, i: (b_, i, 0))],
        out_shape=[jax.ShapeDtypeStruct((b, l, d), F32),
                   jax.ShapeDtypeStruct((b, l, d), F32),
                   jax.ShapeDtypeStruct((b, l, LANES), F32),
                   jax.ShapeDtypeStruct((b, l, LANES), BF16)],
        compiler_params=_params(("parallel", "parallel")),
        name="post1_router",
    )(o, w_out, h, g1, n2g, sh2, sc2, rw_hi, rw_lo)


def _route_kernel(sel_ref, gates_ref, r_ref, tile_ref, cnt_ref, off_ref, *, tile_rows):
    phase = pl.program_id(0)
    i = pl.program_id(1)
    sel = sel_ref[...]
    tr = sel.shape[0]
    col_count = jnp.sum(sel.astype(F32), axis=0, keepdims=True)
    lane = lax.broadcasted_iota(jnp.int32, (1, LANES), 1)

    @pl.when((phase == 0) & (i == 0))
    def _():
        cnt_ref[...] = jnp.zeros_like(cnt_ref)

    @pl.when(phase == 0)
    def _():
        cnt_ref[...] += col_count

    @pl.when((phase == 0) & (i == pl.num_programs(1) - 1))
    def _():
        padded = jnp.ceil(cnt_ref[...] / tile_rows) * tile_rows
        incl = padded
        for sh in (1, 2, 4):
            incl = incl + jnp.where(lane >= sh, pltpu.roll(incl, sh, 1), 0.0)
        off_ref[...] = incl - padded
        tile_start = (lax.broadcasted_iota(jnp.int32, tile_ref.shape, 1) * tile_rows).astype(F32)
        tile_expert = jnp.zeros(tile_ref.shape, jnp.int32)
        for e in range(N_EXPERTS):
            end_e = jnp.sum(jnp.where(lane == e, incl, 0.0), axis=1, keepdims=True)
            tile_expert = tile_expert + (tile_start >= end_e).astype(jnp.int32)
        tile_ref[...] = tile_expert
        cnt_ref[...] = jnp.zeros_like(cnt_ref)

    @pl.when(phase == 1)
    def _():
        rr = lax.broadcasted_iota(jnp.int32, (tr, tr), 0)
        cc = lax.broadcasted_iota(jnp.int32, (tr, tr), 1)
        earlier = jnp.where(cc < rr, 1.0, 0.0).astype(BF16)
        rank = _dot(earlier, sel) + cnt_ref[...]
        cnt_ref[...] += col_count
        pos = off_ref[...] + rank
        chosen = sel > 0
        lo = jnp.min(jnp.where(chosen, pos, 3e38), axis=1, keepdims=True)
        hi = jnp.max(jnp.where(chosen, pos, -1.0), axis=1, keepdims=True)
        g = gates_ref[...]
        w_lo = jnp.sum(jnp.where(chosen & (pos == lo), g, 0.0), axis=1, keepdims=True)
        w_hi = jnp.sum(jnp.where(chosen & (pos == hi), g, 0.0), axis=1, keepdims=True)
        lane_t = lax.broadcasted_iota(jnp.int32, (tr, LANES), 1)
        r_ref[...] = jnp.where(lane_t == 0, lo, jnp.where(lane_t == 1, hi,
                               jnp.where(lane_t == 2, w_lo, jnp.where(lane_t == 3, w_hi, 0.0))))


def _route(sel, gates, *, tr, tile_rows, n_tiles):
    n = sel.shape[0]
    n_tiles_pad = -(-n_tiles // LANES) * LANES
    return pl.pallas_call(
        functools.partial(_route_kernel, tile_rows=tile_rows),
        grid=(2, n // tr),
        in_specs=[pl.BlockSpec((tr, LANES), lambda ph, i: (i, 0)),
                  pl.BlockSpec((tr, LANES), lambda ph, i: (i, 0))],
        out_specs=[pl.BlockSpec((tr, LANES), lambda ph, i: (i * ph, 0)),
                   pl.BlockSpec((1, n_tiles_pad), lambda ph, i: (0, 0))],
        out_shape=[jax.ShapeDtypeStruct((n, LANES), F32),
                   jax.ShapeDtypeStruct((1, n_tiles_pad), jnp.int32)],
        scratch_shapes=[pltpu.VMEM((1, LANES), F32), pltpu.VMEM((1, LANES), F32)],
        compiler_params=_params(("arbitrary", "arbitrary")),
        name="moe_route",
    )(sel, gates)


def _dispatch_kernel(dest_ref, x_ref, xs_in_hbm, xs_hbm, sem, *, tm):
    del xs_in_hbm

    def row_copy(src_row, dst_row):
        return pltpu.make_async_copy(x_ref.at[pl.ds(src_row, 1)], xs_hbm.at[pl.ds(dst_row, 1)], sem)

    def issue(r, carry):
        row_copy(r, dest_ref[0, 0, r]).start(priority=0)
        row_copy(r, dest_ref[0, 0, tm + r]).start(priority=1)
        return carry

    lax.fori_loop(0, tm, issue, 0, unroll=DMA_ISSUE_UNROLL)
    pltpu.make_async_copy(x_ref, xs_hbm.at[pl.ds(0, tm)], sem).wait()
    pltpu.make_async_copy(x_ref, xs_hbm.at[pl.ds(0, tm)], sem).wait()


def _dispatch(dest, x, n_rows, *, tm):
    n = x.shape[0]
    xs0 = jnp.zeros((n_rows,) + x.shape[1:], x.dtype)
    return pl.pallas_call(
        functools.partial(_dispatch_kernel, tm=tm),
        grid=(n // tm,),
        in_specs=[pl.BlockSpec((1, 1, 2 * tm), lambda i: (i, 0, 0), memory_space=pltpu.SMEM),
                  pl.BlockSpec((tm, x.shape[1]), lambda i: (i, 0)),
                  pl.BlockSpec(memory_space=pl.ANY)],
        out_specs=pl.BlockSpec(memory_space=pl.ANY),
        out_shape=jax.ShapeDtypeStruct(xs0.shape, xs0.dtype),
        scratch_shapes=[pltpu.SemaphoreType.DMA(())],
        input_output_aliases={2: 0},
        compiler_params=pltpu.CompilerParams(dimension_semantics=("arbitrary",),
                                             has_side_effects=True),
        name="moe_dispatch",
    )(dest, x, xs0)


def _experts_kernel(te_ref, x_ref, wg_ref, wu_ref, wd_ref, o_ref):
    valid = te_ref[pl.program_id(0)] < N_EXPERTS

    @pl.when(valid)
    def _():
        o_ref[...] = _swiglu_tile(x_ref[...].astype(BF16), wg_ref, wu_ref, wd_ref, lead=(0,))

    @pl.when(jnp.logical_not(valid))
    def _():
        o_ref[...] = jnp.zeros_like(o_ref)


def _experts(tile_expert, xs, wg, wu, wd, *, tile_rows):
    n_rows, d = xs.shape
    ne, _, f = wg.shape
    w_idx = lambda j, te: (jnp.minimum(te[j], ne - 1), 0, 0)
    return pl.pallas_call(
        _experts_kernel,
        grid_spec=pltpu.PrefetchScalarGridSpec(
            num_scalar_prefetch=1,
            grid=(n_rows // tile_rows,),
            in_specs=[pl.BlockSpec((tile_rows, d), lambda j, te: (j, 0)),
                      pl.BlockSpec((1, d, f), w_idx),
                      pl.BlockSpec((1, d, f), w_idx),
                      pl.BlockSpec((1, f, d), w_idx)],
            out_specs=pl.BlockSpec((tile_rows, d), lambda j, te: (j, 0))),
        out_shape=jax.ShapeDtypeStruct((n_rows, d), F32),
        compiler_params=_params(("arbitrary",)),
        name="moe_experts",
    )(tile_expert, xs, wg, wu, wd)


def _combine_kernel(dest_ref, routed_ref, ys_hbm, h1_ref, g2_ref, fng_ref, o_ref, ylo_ref, yhi_ref,
                    sem, *, tm):
    def issue(r, carry):
        pltpu.make_async_copy(ys_hbm.at[pl.ds(dest_ref[0, 0, r], 1)],
                              ylo_ref.at[pl.ds(r, 1)], sem).start(priority=0)
        pltpu.make_async_copy(ys_hbm.at[pl.ds(dest_ref[0, 0, tm + r], 1)],
                              yhi_ref.at[pl.ds(r, 1)], sem).start(priority=1)
        return carry

    lax.fori_loop(0, tm, issue, 0, unroll=DMA_ISSUE_UNROLL)
    pltpu.make_async_copy(ys_hbm.at[pl.ds(0, tm)], ylo_ref, sem).wait()
    pltpu.make_async_copy(ys_hbm.at[pl.ds(0, tm)], yhi_ref, sem).wait()
    routed = routed_ref[...]
    y = routed[:, 2:3] * ylo_ref[...] + routed[:, 3:4] * yhi_ref[...]
    out = h1_ref[...] + g2_ref[0] * y
    o_ref[...] = _rms(out) * fng_ref[...]


def _combine(dest, routed, ys, h1, g2, fng, *, tm, tiles_per_sample):
    n, d = h1.shape
    return pl.pallas_call(
        functools.partial(_combine_kernel, tm=tm),
        grid=(n // tm,),
        in_specs=[pl.BlockSpec((1, 1, 2 * tm), lambda i: (i, 0, 0), memory_space=pltpu.SMEM),
                  pl.BlockSpec((tm, LANES), lambda i: (i, 0)),
                  pl.BlockSpec(memory_space=pl.ANY),
                  pl.BlockSpec((tm, d), lambda i: (i, 0)),
                  pl.BlockSpec((1, 1, d), lambda i: (i // tiles_per_sample, 0, 0)),
                  pl.BlockSpec((1, d), lambda i: (0, 0))],
        out_specs=pl.BlockSpec((tm, d), lambda i: (i, 0)),
        out_shape=jax.ShapeDtypeStruct((n, d), F32),
        scratch_shapes=[pltpu.VMEM((tm, d), F32), pltpu.VMEM((tm, d), F32),
                        pltpu.SemaphoreType.DMA(())],
        compiler_params=_params(("arbitrary",)),
        name="moe_combine",
    )(dest, routed, ys, h1, g2, fng)


def _moe(a2, gates, sel, wg, wu, wd, h1, g2, fng, *, tm):
    b, l, d = h1.shape
    n = b * l
    n_rows = 2 * n + N_EXPERTS * MOE_TILE
    n_tiles = n_rows // MOE_TILE
    routed, tile_expert = _route(sel.reshape(n, LANES), gates.reshape(n, LANES),
                                 tr=tm, tile_rows=MOE_TILE, n_tiles=n_tiles)
    dest = routed[:, 0:2].astype(jnp.int32).reshape(n // tm, tm, 2)
    dest = dest.transpose(0, 2, 1).reshape(n // tm, 1, 2 * tm)
    xs = _dispatch(dest, a2.reshape(n, d), n_rows, tm=tm)
    ys = _experts(tile_expert[0, :n_tiles], xs, wg, wu, wd, tile_rows=MOE_TILE)
    out = _combine(dest, routed, ys, h1.reshape(n, d), g2, fng, tm=tm, tiles_per_sample=l // tm)
    return out.reshape(b, l, d)


def _rope_tables(seq_len, dim):
    t = jnp.arange(seq_len, dtype=jnp.int32)
    row = (t // GRID_W).astype(F32)
    col = (t % GRID_W).astype(F32)
    quarter = dim // 4
    inv = ROPE_BASE ** (-jnp.arange(quarter, dtype=F32) / quarter)
    ar = row[:, None] * inv[None, :]
    ac = col[:, None] * inv[None, :]
    cos = jnp.concatenate([jnp.cos(ar), jnp.cos(ar), jnp.cos(ac), jnp.cos(ac)], axis=-1)
    sin = jnp.concatenate([jnp.sin(ar), jnp.sin(ar), jnp.sin(ac), jnp.sin(ac)], axis=-1)
    reps = LANES // dim
    return jnp.tile(cos, (1, reps)), jnp.tile(sin, (1, reps))


def kernel(x, c, ctx, c_ctx, norm1_g, norm2_g, mod_w, mod_b, even_w_in, pool_w, pool_scale,
           lambda_q1, lambda_k1, lambda_q2, lambda_k2, even_w_out, ffn_w_gate, ffn_w_up,
           ffn_w_down, odd_w_in, q_norm_g, kv_norm_g, w_uq, w_ukv, odd_w_out, router_w,
           moe_w_gate, moe_w_up, moe_w_down, final_norm_g):
    b, l, d = x.shape
    n_ctx = ctx.shape[1]
    tm = min(ROW_TILE, l)
    tkv = min(KV_TILE, l)
    tq = min(DIFF_Q_TILE, l)
    cos, sin = _rope_tables(l, DIFF_HEAD_DIM)
    cos_c, sin_c = cos[:n_ctx], sin[:n_ctx]

    cond = jnp.zeros((8, d), F32).at[:b].set(c).at[b].set(c_ctx)
    mod = _modulation(cond, mod_w, mod_b)
    mod = mod.reshape(mod.shape[0], 8, N_MOD, d)

    def mod_vecs(layer):
        lat = [mod[layer, :b, k][:, None, :] for k in range(N_MOD)]
        cx = [mod[layer, b:b + 1, k][:, None, :] for k in range(N_MOD)]
        return lat, cx

    (sh1, sc1, g1, sh2, sc2, g2), (csh1, csc1, cg1, csh2, csc2, cg2) = mod_vecs(0)
    lam_init = 0.8 - 0.6 * math.exp(-0.3 * 0)
    w_in0 = even_w_in[0].astype(BF16)
    n1g = norm1_g[0][None, :]
    n2g = norm2_g[0][None, :]
    p_lat, q1t_lat, q2t_lat, k_lat, vt_lat = _pre0(x, n1g, sh1, sc1, w_in0, cos, sin,
                                                   use_rope=True, tm=tkv)
    p_ctx, q1t_ctx, q2t_ctx, k_ctx, vt_ctx = _pre0(ctx, n1g, csh1, csc1, w_in0, cos_c, sin_c,
                                                   use_rope=False, tm=n_ctx)
    lq = jnp.stack([lambda_q1[0], lambda_q2[0]])
    lk = jnp.stack([lambda_k1[0], lambda_k2[0]])
    diff_kern = functools.partial(_diff_attn_kernel, lam_init=lam_init)
    attn_lat = _attention(diff_kern, (q1t_lat, q2t_lat), (k_ctx, vt_ctx), (k_lat, vt_lat),
                          2 * DIFF_HEAD_DIM, extra=(lq, lk), tq=tq, name="diff_attn_lat")
    attn_ctx = _attention(diff_kern, (q1t_ctx, q2t_ctx), (k_ctx, vt_ctx), None,
                          2 * DIFF_HEAD_DIM, extra=(lq, lk), tq=n_ctx, name="diff_attn_ctx")
    ng = len(POOL_WINDOWS)
    poolw_bd = jnp.zeros((POOL_WIDTH, POOL_WIDTH), F32)
    for g in range(ng):
        sl = slice(g * POOL_GROUP, (g + 1) * POOL_GROUP)
        poolw_bd = poolw_bd.at[sl, sl].set(pool_w[0, g])
    poolw_bd = poolw_bd.astype(BF16)
    pscale = pool_scale[0][None, :]
    w_out0 = even_w_out[0].astype(BF16)
    wg0, wu0, wd0 = ffn_w_gate[0].astype(BF16), ffn_w_up[0].astype(BF16), ffn_w_down[0].astype(BF16)
    h1_lat, a2_lat = _post0(attn_lat, p_lat, poolw_bd, pscale, w_out0, x, g1, n2g, sh2, sc2, tm=tm)
    h_lat = _ffn(a2_lat, h1_lat, g2, wg0, wu0, wd0, tm=tm)
    h1_ctx, a2_ctx = _post0(attn_ctx, p_ctx, poolw_bd, pscale, w_out0, ctx, cg1, n2g, csh2, csc2,
                            tm=n_ctx)
    h_ctx = _ffn(a2_ctx, h1_ctx, cg2, wg0, wu0, wd0, tm=n_ctx)

    (sh1, sc1, g1, sh2, sc2, g2), (csh1, csc1, _, _, _, _) = mod_vecs(1)
    n1g = norm1_g[1][None, :]
    n2g = norm2_g[1][None, :]
    q_rank, kv_rank = w_uq.shape[1], w_ukv.shape[1]
    n_heads = w_uq.shape[2] // (MLA_NOPE + MLA_ROPE)
    w_in1 = jnp.pad(odd_w_in[0], ((0, 0), (0, LANES - MLA_ROPE))).astype(BF16)
    wuq = w_uq[0].reshape(q_rank, n_heads, MLA_NOPE + MLA_ROPE)
    wuq = jnp.pad(wuq, ((0, 0), (0, 0), (0, MLA_QK_PAD - MLA_NOPE - MLA_ROPE)))
    wuq = wuq.reshape(q_rank, n_heads * MLA_QK_PAD).astype(BF16)
    wukv = w_ukv[0].reshape(kv_rank, n_heads, MLA_NOPE + MLA_V)
    wuk = wukv[:, :, :MLA_NOPE].reshape(kv_rank, n_heads * MLA_NOPE).astype(BF16)
    wuv = wukv[:, :, MLA_NOPE:].reshape(kv_rank, n_heads * MLA_V).astype(BF16)
    qg = q_norm_g[0][None, :]
    kvg = kv_norm_g[0][None, :]
    q1t, k1, vt1 = _pre1(h_lat, n1g, sh1, sc1, w_in1, qg, kvg, wuq, wuk, wuv, cos, sin,
                         use_rope=True, want_q=True, tm=tkv)
    k1c, vt1c = _pre1(h_ctx, n1g, csh1, csc1, w_in1, qg, kvg, wuq, wuk, wuv, cos_c, sin_c,
                      use_rope=False, want_q=False, tm=n_ctx)
    o1 = _attention(_mla_attn_kernel, (q1t,), (k1c, vt1c), (k1, vt1), MLA_QK_PAD,
                    tq=min(MLA_Q_TILE, l), name="mla_attn")
    rw_pad = jnp.pad(router_w[0], ((0, 0), (0, LANES - N_EXPERTS)))
    rw_hi = rw_pad.astype(BF16)
    rw_lo = (rw_pad - rw_hi.astype(F32)).astype(BF16)
    h1, a2, gates, sel = _post1(o1, odd_w_out[0].astype(BF16), h_lat, g1, n2g, sh2, sc2,
                                rw_hi, rw_lo, tm=tm)
    return _moe(a2, gates, sel, moe_w_gate[0].astype(BF16), moe_w_up[0].astype(BF16),
                moe_w_down[0].astype(BF16), h1, g2, final_norm_g[None, :], tm=tm)
```

```python
import functools
import math

import jax
import jax.numpy as jnp
from jax import lax
from jax.experimental import pallas as pl
from jax.experimental.pallas import tpu as pltpu

F32 = jnp.float32
BF16 = jnp.bfloat16

EPS = 1e-6
GRID_W = 64
ROPE_BASE = 10000.0
N_MOD = 6
POOL_WINDOWS = (2, 4, 8, 16)
POOL_GROUP = 64
POOL_WIDTH = POOL_GROUP * len(POOL_WINDOWS)
POOL_HALO = max(POOL_WINDOWS) // 2
DIFF_HEAD_DIM = 64
DIFF_V_DIM = 128
V_ONES_ROWS = 16
V_AUG = DIFF_V_DIM + V_ONES_ROWS
MLA_NOPE = 128
MLA_ROPE = 64
MLA_V = 128
MLA_QK_PAD = 256
N_EXPERTS = 8
ROPE_QUARTER = 16
LOG2E = math.log2(math.e)

LANES = 128
ROW_TILE = 512
KV_TILE = 512
DIFF_Q_TILE = 512
MLA_Q_TILE = 1024
DIFF_Q_SWEEPS = 8
MLA_Q_SWEEPS = 4
PAIR_UNROLL = 5
MOE_TILE = 512
DMA_ISSUE_UNROLL = 8
FFN_CHUNKS = (512, 512, 512, 512, 512, 256)
VMEM_LIMIT = 56 * 1024 * 1024


def _params(sem):
    return pltpu.CompilerParams(dimension_semantics=sem, vmem_limit_bytes=VMEM_LIMIT)


def _dot(a, b):
    return jnp.dot(a, b, preferred_element_type=F32)


def _rms(x):
    return x * lax.rsqrt(jnp.mean(x * x, axis=-1, keepdims=True) + EPS)


def _silu(x):
    return x * (1.0 / (1.0 + jnp.exp(-x)))


def _rope(x, cos, sin):
    lane = lax.broadcasted_iota(jnp.int32, x.shape, 1)
    even = (lane // ROPE_QUARTER) % 2 == 0
    rot = jnp.where(even, -pltpu.roll(x, LANES - ROPE_QUARTER, 1), pltpu.roll(x, ROPE_QUARTER, 1))
    return x * cos + rot * sin


def _store_vt(vt_ref, v, n_heads):
    tm = v.shape[0]
    ones = jnp.ones((V_ONES_ROWS, tm), BF16)
    for h in range(n_heads):
        r0 = h * V_AUG
        vt_ref[0, 0, r0:r0 + DIFF_V_DIM, :] = v[:, h * DIFF_V_DIM:(h + 1) * DIFF_V_DIM].T.astype(BF16)
        vt_ref[0, 0, r0 + DIFF_V_DIM:r0 + V_AUG, :] = ones


def _mod_kernel(cond_ref, w_ref, b_ref, o_ref):
    s = _silu(cond_ref[...])
    o_ref[0] = jnp.dot(s, w_ref[0], preferred_element_type=F32,
                       precision=lax.Precision.HIGHEST) + b_ref[0]


def _modulation(cond, mod_w, mod_b):
    depth, d, n = mod_w.shape
    tn = n // 4
    return pl.pallas_call(
        _mod_kernel,
        grid=(depth, n // tn),
        in_specs=[pl.BlockSpec((8, d), lambda i, j: (0, 0)),
                  pl.BlockSpec((1, d, tn), lambda i, j: (i, 0, j)),
                  pl.BlockSpec((1, 1, tn), lambda i, j: (i, 0, j))],
        out_specs=pl.BlockSpec((1, 8, tn), lambda i, j: (i, 0, j)),
        out_shape=jax.ShapeDtypeStruct((depth, 8, n), F32),
        compiler_params=_params(("arbitrary", "arbitrary")),
        name="modulation",
    )(cond, mod_w, mod_b.reshape(depth, 1, n))


def _bvec_spec(arr, d):
    if arr.shape[0] == 1:
        return pl.BlockSpec((1, 1, d), lambda b, i: (0, 0, 0))
    return pl.BlockSpec((1, 1, d), lambda b, i: (b, 0, 0))


def _norm_mod(h, g, sh, sc):
    a = _rms(h) * g
    return a * (1.0 + sc) + sh


def _pre0_kernel(h_ref, g_ref, sh_ref, sc_ref, w_ref, cos_ref, sin_ref,
                 p_ref, q1t_ref, q2t_ref, k_ref, vt_ref, *, use_rope, n_heads):
    a = _norm_mod(h_ref[0], g_ref[...], sh_ref[0], sc_ref[0])
    z = _dot(a.astype(BF16), w_ref[...])
    p_ref[0] = z[:, :POOL_WIDTH]
    head_w = 2 * DIFF_HEAD_DIM
    qk_w = n_heads * head_w
    q_scale = DIFF_HEAD_DIM ** -0.5 * LOG2E
    cos = cos_ref[...]
    sin = sin_ref[...]
    for h in range(n_heads):
        xq = z[:, POOL_WIDTH + h * head_w:POOL_WIDTH + (h + 1) * head_w]
        xk = z[:, POOL_WIDTH + qk_w + h * head_w:POOL_WIDTH + qk_w + (h + 1) * head_w]
        if use_rope:
            xq = _rope(xq, cos, sin)
            xk = _rope(xk, cos, sin)
        xqt = (xq * q_scale).T
        row = lax.broadcasted_iota(jnp.int32, xqt.shape, 0)
        q1t_ref[0, 0, h * head_w:(h + 1) * head_w, :] = jnp.where(row < DIFF_HEAD_DIM, xqt, 0.0).astype(BF16)
        q2t_ref[0, 0, h * head_w:(h + 1) * head_w, :] = jnp.where(row >= DIFF_HEAD_DIM, xqt, 0.0).astype(BF16)
        k_ref[0, :, h * head_w:(h + 1) * head_w] = xk.astype(BF16)
    _store_vt(vt_ref, z[:, POOL_WIDTH + 2 * qk_w:], n_heads)


def _pre0(h, g, sh, sc, w_in, cos, sin, *, use_rope, tm):
    b, l, d = h.shape
    n_in = w_in.shape[1]
    v_w = (n_in - POOL_WIDTH) // 3
    n_heads = v_w // DIFF_V_DIM
    kern = functools.partial(_pre0_kernel, use_rope=use_rope, n_heads=n_heads)
    return pl.pallas_call(
        kern,
        grid=(b, l // tm),
        in_specs=[pl.BlockSpec((1, tm, d), lambda b_, i: (b_, i, 0)),
                  pl.BlockSpec((1, d), lambda b_, i: (0, 0)),
                  _bvec_spec(sh, d), _bvec_spec(sc, d),
                  pl.BlockSpec((d, n_in), lambda b_, i: (0, 0)),
                  pl.BlockSpec((tm, LANES), lambda b_, i: (i, 0)),
                  pl.BlockSpec((tm, LANES), lambda b_, i: (i, 0))],
        out_specs=[pl.BlockSpec((1, tm, POOL_WIDTH), lambda b_, i: (b_, i, 0)),
                   pl.BlockSpec((1, 1, v_w, tm), lambda b_, i: (b_, i, 0, 0)),
                   pl.BlockSpec((1, 1, v_w, tm), lambda b_, i: (b_, i, 0, 0)),
                   pl.BlockSpec((1, tm, v_w), lambda b_, i: (b_, i, 0)),
                   pl.BlockSpec((1, 1, n_heads * V_AUG, tm), lambda b_, i: (b_, i, 0, 0))],
        out_shape=[jax.ShapeDtypeStruct((b, l, POOL_WIDTH), F32),
                   jax.ShapeDtypeStruct((b, l // tm, v_w, tm), BF16),
                   jax.ShapeDtypeStruct((b, l // tm, v_w, tm), BF16),
                   jax.ShapeDtypeStruct((b, l, v_w), BF16),
                   jax.ShapeDtypeStruct((b, l // tm, n_heads * V_AUG, tm), BF16)],
        compiler_params=_params(("parallel", "parallel")),
        name="pre0_rope" if use_rope else "pre0_ctx",
    )(h, g, sh, sc, w_in, cos, sin)


def _attn_core(qt_refs, j, tq, kc_ref, vc_ref, lat_refs, acc_refs, sbuf_refs):
    n = len(qt_refs)
    sub = tq // qt_refs[0].shape[3]
    qts = [jnp.concatenate([r[0, j * sub + u] for u in range(sub)], axis=1) for r in qt_refs]
    ctx_bufs = sbuf_refs[:n]
    lat_bufs = sbuf_refs[n:]
    ms = tuple(jnp.full((1, tq), -jnp.inf, F32) for _ in range(n))
    for i in range(n):
        acc_refs[i][...] = jnp.zeros_like(acc_refs[i])

    def scores(k, bufs):
        cmax = []
        for i in range(n):
            s = _dot(k, qts[i])
            bufs[i][...] = s
            cmax.append(jnp.max(s, axis=0, keepdims=True))
        return tuple(cmax)

    def softmax_pv(vt, bufs, cmax, ms):
        new_m = []
        for i in range(n):
            m_new = jnp.maximum(ms[i], cmax[i])
            alpha = jnp.exp2(ms[i] - m_new)
            p = jnp.exp2(bufs[i][...] - m_new)
            acc_refs[i][...] = alpha * acc_refs[i][...] + _dot(vt, p.astype(BF16))
            new_m.append(m_new)
        return tuple(new_m)

    if lat_refs is None:
        softmax_pv(vc_ref[0, 0], ctx_bufs, scores(kc_ref[0], ctx_bufs), ms)
        return
    kl_ref, vl_ref = lat_refs
    nblk, tk = vl_ref.shape[1], vl_ref.shape[3]
    assert nblk >= 2 and nblk % 2 == 0

    def qk(t, which):
        start = t * tk if isinstance(t, int) else pl.multiple_of(t * tk, tk)
        return scores(kl_ref[0, pl.ds(start, tk), :], lat_bufs[which::2])

    def pair(jj, carry):
        cmax_a, ms = carry
        t = 2 * jj
        cmax_b = qk(t + 1, 1)
        ms = softmax_pv(vl_ref[0, t], lat_bufs[0::2], cmax_a, ms)
        cmax_a = qk(t + 2, 0)
        ms = softmax_pv(vl_ref[0, t + 1], lat_bufs[1::2], cmax_b, ms)
        return cmax_a, ms

    cmax_a = qk(0, 0)
    n_pairs = nblk // 2 - 1
    cmax_a, ms = lax.fori_loop(0, n_pairs, pair, (cmax_a, ms),
                               unroll=max(1, min(PAIR_UNROLL, n_pairs)))
    cmax_b = qk(nblk - 1, 1)
    ms = softmax_pv(vl_ref[0, nblk - 2], lat_bufs[0::2], cmax_a, ms)
    cmax_c = scores(kc_ref[0], ctx_bufs)
    ms = softmax_pv(vl_ref[0, nblk - 1], lat_bufs[1::2], cmax_b, ms)
    softmax_pv(vc_ref[0, 0], ctx_bufs, cmax_c, ms)


def _normalised(acc_ref):
    acc = acc_ref[...]
    return acc[:DIFF_V_DIM] * (1.0 / acc[DIFF_V_DIM:DIFF_V_DIM + 1])


def _for_each_query_tile(o_ref, tq, tile_fn):
    n_tiles = o_ref.shape[1] // tq

    def body(j, carry):
        rows = pl.ds(pl.multiple_of(j * tq, tq), tq)
        o_ref[0, rows, :] = tile_fn(j)
        return carry

    if n_tiles == 1:
        o_ref[0] = tile_fn(0)
    else:
        lax.fori_loop(0, n_tiles, body, 0)


def _diff_attn_kernel(lq_ref, lk_ref, q1t_ref, q2t_ref, kc_ref, vc_ref, *rest, has_lat, lam_init, tq):
    lat_refs = rest[:2] if has_lat else None
    rest = rest[2:] if has_lat else rest
    o_ref, acc1_ref, acc2_ref = rest[:3]
    e = jnp.exp(jnp.sum(lq_ref[...] * lk_ref[...], axis=-1, keepdims=True))
    lam = e[0:1] - e[1:2] + lam_init

    def tile(j):
        _attn_core((q1t_ref, q2t_ref), j, tq, kc_ref, vc_ref, lat_refs, (acc1_ref, acc2_ref), rest[3:])
        o = _normalised(acc1_ref) - lam * _normalised(acc2_ref)
        ms = jnp.mean(o * o, axis=0, keepdims=True)
        y = o * (lax.rsqrt(ms + EPS) * (1.0 - lam_init))
        return y.T.astype(BF16)

    _for_each_query_tile(o_ref, tq, tile)


def _mla_attn_kernel(qt_ref, kc_ref, vc_ref, *rest, has_lat, tq):
    lat_refs = rest[:2] if has_lat else None
    rest = rest[2:] if has_lat else rest
    o_ref, acc_ref = rest[:2]

    def tile(j):
        _attn_core((qt_ref,), j, tq, kc_ref, vc_ref, lat_refs, (acc_ref,), rest[2:])
        return _normalised(acc_ref).T.astype(BF16)

    _for_each_query_tile(o_ref, tq, tile)


def _attention(kern, qts, ctx_kv, lat_kv, qk_w, extra=(), *, tq, n_sub, name):
    n_maps = len(qts)
    b, nqb, hw, tmq = qts[0].shape
    n_heads = hw // qk_w
    lq = nqb * tmq
    rows = n_sub * tq
    in_specs = [pl.BlockSpec(x.shape, lambda b_, h, i: (0, 0)) for x in extra]
    in_specs += [pl.BlockSpec((1, rows // tmq, qk_w, tmq), lambda b_, h, i: (b_, i, h, 0)) for _ in qts]
    args = list(extra) + list(qts)
    scratch = [pltpu.VMEM((V_AUG, tq), F32) for _ in range(n_maps)]
    scratch += [pltpu.VMEM((ctx_kv[0].shape[1], tq), F32) for _ in range(n_maps)]
    for kv in (ctx_kv, lat_kv):
        if kv is None:
            continue
        k, vt = kv
        nblk, tk = vt.shape[1], vt.shape[3]
        in_specs.append(pl.BlockSpec((1, k.shape[1], qk_w), lambda b_, h, i: (b_, 0, h)))
        in_specs.append(pl.BlockSpec((1, nblk, V_AUG, tk), lambda b_, h, i: (b_, 0, h, 0)))
        args += [k, vt]
    if lat_kv is not None:
        scratch += [pltpu.VMEM((lat_kv[1].shape[3], tq), F32) for _ in range(2 * n_maps)]
    return pl.pallas_call(
        functools.partial(kern, has_lat=lat_kv is not None, tq=tq),
        grid=(b, n_heads, lq // rows),
        in_specs=in_specs,
        out_specs=pl.BlockSpec((1, rows, DIFF_V_DIM), lambda b_, h, i: (b_, i, h)),
        out_shape=jax.ShapeDtypeStruct((b, lq, n_heads * DIFF_V_DIM), BF16),
        scratch_shapes=scratch,
        compiler_params=_params(("parallel", "parallel", "arbitrary")),
        name=name,
    )(*args)


def _post0_kernel(attn_ref, p_ref, pprev_ref, pnext_ref, poolw_ref, pscale_ref, wout_ref,
                  h_ref, g1_ref, n2g_ref, sh2_ref, sc2_ref, h1_ref, a2_ref, *, seq_len):
    i = pl.program_id(1)
    u = p_ref[0]
    tm = u.shape[0]
    ext = jnp.concatenate([pprev_ref[0], u, pnext_ref[0]], axis=0)
    row = i * tm - POOL_HALO + lax.broadcasted_iota(jnp.int32, (tm + 2 * POOL_HALO, 1), 0)
    ext = jnp.where((row >= 0) & (row < seq_len), ext, 0.0)

    def shifted(j):
        return ext[POOL_HALO + j:POOL_HALO + j + tm]

    t = i * tm + lax.broadcasted_iota(jnp.int32, (tm, 1), 0)
    lane = lax.broadcasted_iota(jnp.int32, (tm, POOL_WIDTH), 1)
    win_sum = shifted(-1) + shifted(0)
    pooled = jnp.zeros_like(u)
    for g, w in enumerate(POOL_WINDOWS):
        half = w // 2
        if g > 0:
            prev_half = POOL_WINDOWS[g - 1] // 2
            for j in range(prev_half, half):
                win_sum = win_sum + shifted(-j - 1) + shifted(j)
        cnt = (jnp.minimum(t + half, seq_len) - jnp.maximum(t - half, 0)).astype(F32)
        in_group = (lane >= g * POOL_GROUP) & (lane < (g + 1) * POOL_GROUP)
        pooled = jnp.where(in_group, win_sum / cnt, pooled)
    pooled = pooled - u
    y_pool = _dot(pooled.astype(BF16), poolw_ref[...]) * pscale_ref[...]
    y = _dot(y_pool.astype(BF16), wout_ref[:POOL_WIDTH, :]) + _dot(attn_ref[0], wout_ref[POOL_WIDTH:, :])
    h1 = h_ref[0] + g1_ref[0] * y
    h1_ref[0] = h1
    a2_ref[0] = _norm_mod(h1, n2g_ref[...], sh2_ref[0], sc2_ref[0]).astype(BF16)


def _post0(attn, p, poolw_bd, pscale, w_out, h, g1, n2g, sh2, sc2, *, tm):
    b, l, d = h.shape
    hb = tm // POOL_HALO
    nhb = l // POOL_HALO
    return pl.pallas_call(
        functools.partial(_post0_kernel, seq_len=l),
        grid=(b, l // tm),
        in_specs=[pl.BlockSpec((1, tm, attn.shape[2]), lambda b_, i: (b_, i, 0)),
                  pl.BlockSpec((1, tm, POOL_WIDTH), lambda b_, i: (b_, i, 0)),
                  pl.BlockSpec((1, POOL_HALO, POOL_WIDTH),
                               lambda b_, i: (b_, jnp.maximum(i * hb - 1, 0), 0)),
                  pl.BlockSpec((1, POOL_HALO, POOL_WIDTH),
                               lambda b_, i: (b_, jnp.minimum((i + 1) * hb, nhb - 1), 0)),
                  pl.BlockSpec(poolw_bd.shape, lambda b_, i: (0, 0)),
                  pl.BlockSpec((1, POOL_WIDTH), lambda b_, i: (0, 0)),
                  pl.BlockSpec(w_out.shape, lambda b_, i: (0, 0)),
                  pl.BlockSpec((1, tm, d), lambda b_, i: (b_, i, 0)),
                  _bvec_spec(g1, d),
                  pl.BlockSpec((1, d), lambda b_, i: (0, 0)),
                  _bvec_spec(sh2, d), _bvec_spec(sc2, d)],
        out_specs=[pl.BlockSpec((1, tm, d), lambda b_, i: (b_, i, 0)),
                   pl.BlockSpec((1, tm, d), lambda b_, i: (b_, i, 0))],
        out_shape=[jax.ShapeDtypeStruct((b, l, d), F32),
                   jax.ShapeDtypeStruct((b, l, d), BF16)],
        compiler_params=_params(("parallel", "parallel")),
        name="post0",
    )(attn, p, p, p, poolw_bd, pscale, w_out, h, g1, n2g, sh2, sc2)


def _swiglu_tile(x, wg_ref, wu_ref, wd_ref, lead=()):
    acc = None
    c0 = 0
    for cw in FFN_CHUNKS:
        hg = _dot(x, wg_ref[lead + (slice(None), slice(c0, c0 + cw))])
        hu = _dot(x, wu_ref[lead + (slice(None), slice(c0, c0 + cw))])
        act = (_silu(hg) * hu).astype(BF16)
        part = _dot(act, wd_ref[lead + (slice(c0, c0 + cw), slice(None))])
        acc = part if acc is None else acc + part
        c0 += cw
    return acc


def _ffn_kernel(a2_ref, h1_ref, g2_ref, wg_ref, wu_ref, wd_ref, o_ref):
    y = _swiglu_tile(a2_ref[0], wg_ref, wu_ref, wd_ref)
    o_ref[0] = h1_ref[0] + g2_ref[0] * y


def _ffn(a2, h1, g2, wg, wu, wd, *, tm):
    b, l, d = h1.shape
    f = wg.shape[1]
    assert sum(FFN_CHUNKS) == f
    return pl.pallas_call(
        _ffn_kernel,
        grid=(b, l // tm),
        in_specs=[pl.BlockSpec((1, tm, d), lambda b_, i: (b_, i, 0)),
                  pl.BlockSpec((1, tm, d), lambda b_, i: (b_, i, 0)),
                  _bvec_spec(g2, d),
                  pl.BlockSpec((d, f), lambda b_, i: (0, 0)),
                  pl.BlockSpec((d, f), lambda b_, i: (0, 0)),
                  pl.BlockSpec((f, d), lambda b_, i: (0, 0))],
        out_specs=pl.BlockSpec((1, tm, d), lambda b_, i: (b_, i, 0)),
        out_shape=jax.ShapeDtypeStruct((b, l, d), F32),
        compiler_params=_params(("parallel", "parallel")),
        name="ffn_dense",
    )(a2, h1, g2, wg, wu, wd)


def _pre1_kernel(h_ref, g_ref, sh_ref, sc_ref, win_ref, qg_ref, kvg_ref, wuq_ref, wuk_ref, wuv_ref,
                 cos_ref, sin_ref, *out_refs, use_rope, want_q, q_rank, kv_rank, n_heads):
    a = _norm_mod(h_ref[0], g_ref[...], sh_ref[0], sc_ref[0])
    z = _dot(a.astype(BF16), win_ref[...])
    cos = cos_ref[...]
    sin = sin_ref[...]
    if want_q:
        qt_ref, k_ref, vt_ref = out_refs
        cq = _rms(z[:, :q_rank]) * qg_ref[...]
        qf = _dot(cq.astype(BF16), wuq_ref[...])
        q_scale = (MLA_NOPE + MLA_ROPE) ** -0.5 * LOG2E
        for h in range(n_heads):
            c0 = h * MLA_QK_PAD
            qn = qf[:, c0:c0 + MLA_NOPE] * q_scale
            qr = _rope(qf[:, c0 + MLA_NOPE:c0 + MLA_QK_PAD], cos, sin) * q_scale
            qt_ref[0, 0, c0:c0 + MLA_NOPE, :] = qn.T.astype(BF16)
            qt_ref[0, 0, c0 + MLA_NOPE:c0 + MLA_QK_PAD, :] = qr.T.astype(BF16)
    else:
        k_ref, vt_ref = out_refs
    ckv = (_rms(z[:, q_rank:q_rank + kv_rank]) * kvg_ref[...]).astype(BF16)
    kn = _dot(ckv, wuk_ref[...])
    vv = _dot(ckv, wuv_ref[...])
    kr = z[:, q_rank + kv_rank:]
    if use_rope:
        kr = _rope(kr, cos, sin)
    kr = kr.astype(BF16)
    for h in range(n_heads):
        c0 = h * MLA_QK_PAD
        k_ref[0, :, c0:c0 + MLA_NOPE] = kn[:, h * MLA_NOPE:(h + 1) * MLA_NOPE].astype(BF16)
        k_ref[0, :, c0 + MLA_NOPE:c0 + MLA_QK_PAD] = kr
    _store_vt(vt_ref, vv, n_heads)


def _pre1(h, g, sh, sc, w_in, qg, kvg, wuq, wuk, wuv, cos, sin, *, use_rope, want_q, tm):
    b, l, d = h.shape
    n_heads = wuk.shape[1] // MLA_NOPE
    q_rank, kv_rank = wuq.shape[0], wuk.shape[0]
    kern = functools.partial(_pre1_kernel, use_rope=use_rope, want_q=want_q,
                             q_rank=q_rank, kv_rank=kv_rank, n_heads=n_heads)
    full = lambda x: pl.BlockSpec(x.shape, lambda b_, i: (0,) * x.ndim)
    qk_w = n_heads * MLA_QK_PAD
    v_rows = n_heads * V_AUG
    out_specs = [pl.BlockSpec((1, tm, qk_w), lambda b_, i: (b_, i, 0)),
                 pl.BlockSpec((1, 1, v_rows, tm), lambda b_, i: (b_, i, 0, 0))]
    out_shape = [jax.ShapeDtypeStruct((b, l, qk_w), BF16),
                 jax.ShapeDtypeStruct((b, l // tm, v_rows, tm), BF16)]
    if want_q:
        out_specs = [pl.BlockSpec((1, 1, qk_w, tm), lambda b_, i: (b_, i, 0, 0))] + out_specs
        out_shape = [jax.ShapeDtypeStruct((b, l // tm, qk_w, tm), BF16)] + out_shape
    return pl.pallas_call(
        kern,
        grid=(b, l // tm),
        in_specs=[pl.BlockSpec((1, tm, d), lambda b_, i: (b_, i, 0)),
                  pl.BlockSpec((1, d), lambda b_, i: (0, 0)),
                  _bvec_spec(sh, d), _bvec_spec(sc, d),
                  full(w_in), full(qg), full(kvg), full(wuq), full(wuk), full(wuv),
                  pl.BlockSpec((tm, LANES), lambda b_, i: (i, 0)),
                  pl.BlockSpec((tm, LANES), lambda b_, i: (i, 0))],
        out_specs=out_specs,
        out_shape=out_shape,
        compiler_params=_params(("parallel", "parallel")),
        name="pre1_lat" if want_q else "pre1_ctx",
    )(h, g, sh, sc, w_in, qg, kvg, wuq, wuk, wuv, cos, sin)


def _post1_kernel(o_ref, wout_ref, h_ref, g1_ref, n2g_ref, sh2_ref, sc2_ref, rwh_ref, rwl_ref,
                  h1_ref, a2_ref, gates_ref, sel_ref):
    y = _dot(o_ref[0], wout_ref[...])
    h1 = h_ref[0] + g1_ref[0] * y
    h1_ref[0] = h1
    a2 = _norm_mod(h1, n2g_ref[...], sh2_ref[0], sc2_ref[0])
    a2_ref[0] = a2
    a_hi = a2.astype(BF16)
    a_lo = (a2 - a_hi.astype(F32)).astype(BF16)
    logits = _dot(a_hi, rwh_ref[...]) + (_dot(a_lo, rwh_ref[...]) + _dot(a_hi, rwl_ref[...]))
    lane = lax.broadcasted_iota(jnp.int32, logits.shape, 1)
    lg = jnp.where(lane < N_EXPERTS, logits, -jnp.inf)
    m1 = jnp.max(lg, axis=1, keepdims=True)
    i1 = jnp.min(jnp.where(lg == m1, lane, LANES), axis=1, keepdims=True)
    lg2 = jnp.where(lane == i1, -jnp.inf, lg)
    m2 = jnp.max(lg2, axis=1, keepdims=True)
    i2 = jnp.min(jnp.where(lg2 == m2, lane, LANES), axis=1, keepdims=True)
    p2 = jnp.exp(m2 - m1)
    w1 = 1.0 / (1.0 + p2)
    gates_ref[0] = jnp.where(lane == i1, w1, 0.0) + jnp.where(lane == i2, p2 * w1, 0.0)
    sel_ref[0] = jnp.where((lane == i1) | (lane == i2), 1.0, 0.0).astype(BF16)


def _post1(o, w_out, h, g1, n2g, sh2, sc2, rw_hi, rw_lo, *, tm):
    b, l, d = h.shape
    return pl.pallas_call(
        _post1_kernel,
        grid=(b, l // tm),
        in_specs=[pl.BlockSpec((1, tm, o.shape[2]), lambda b_, i: (b_, i, 0)),
                  pl.BlockSpec(w_out.shape, lambda b_, i: (0, 0)),
                  pl.BlockSpec((1, tm, d), lambda b_, i: (b_, i, 0)),
                  _bvec_spec(g1, d),
                  pl.BlockSpec((1, d), lambda b_, i: (0, 0)),
                  _bvec_spec(sh2, d), _bvec_spec(sc2, d),
                  pl.BlockSpec(rw_hi.shape, lambda b_, i: (0, 0)),
                  pl.BlockSpec(rw_lo.shape, lambda b_, i: (0, 0))],
        out_specs=[pl.BlockSpec((1, tm, d), lambda b_, i: (b_, i, 0)),
                   pl.BlockSpec((1, tm, d), lambda b_, i: (b_, i, 0)),
                   pl.BlockSpec((1, tm, LANES), lambda b_, i: (b_, i, 0)),
                   pl.BlockSpec((1, tm, LANES), lambda b_, i: (b_, i, 0))],
        out_shape=[jax.ShapeDtypeStruct((b, l, d), F32),
                   jax.ShapeDtypeStruct((b, l, d), F32),
                   jax.ShapeDtypeStruct((b, l, LANES), F32),
                   jax.ShapeDtypeStruct((b, l, LANES), BF16)],
        compiler_params=_params(("parallel", "parallel")),
        name="post1_router",
    )(o, w_out, h, g1, n2g, sh2, sc2, rw_hi, rw_lo)


def _route_kernel(sel_ref, gates_ref, r_ref, tile_ref, cnt_ref, off_ref, *, tile_rows):
    phase = pl.program_id(0)
    i = pl.program_id(1)
    sel = sel_ref[...]
    tr = sel.shape[0]
    col_count = jnp.sum(sel.astype(F32), axis=0, keepdims=True)
    lane = lax.broadcasted_iota(jnp.int32, (1, LANES), 1)

    @pl.when((phase == 0) & (i == 0))
    def _():
        cnt_ref[...] = jnp.zeros_like(cnt_ref)

    @pl.when(phase == 0)
    def _():
        cnt_ref[...] += col_count

    @pl.when((phase == 0) & (i == pl.num_programs(1) - 1))
    def _():
        padded = jnp.ceil(cnt_ref[...] / tile_rows) * tile_rows
        incl = padded
        for sh in (1, 2, 4):
            incl = incl + jnp.where(lane >= sh, pltpu.roll(incl, sh, 1), 0.0)
        off_ref[...] = incl - padded
        tile_start = (lax.broadcasted_iota(jnp.int32, tile_ref.shape, 1) * tile_rows).astype(F32)
        tile_expert = jnp.zeros(tile_ref.shape, jnp.int32)
        for e in range(N_EXPERTS):
            end_e = jnp.sum(jnp.where(lane == e, incl, 0.0), axis=1, keepdims=True)
            tile_expert = tile_expert + (tile_start >= end_e).astype(jnp.int32)
        tile_ref[...] = tile_expert
        cnt_ref[...] = jnp.zeros_like(cnt_ref)

    @pl.when(phase == 1)
    def _():
        rr = lax.broadcasted_iota(jnp.int32, (tr, tr), 0)
        cc = lax.broadcasted_iota(jnp.int32, (tr, tr), 1)
        earlier = jnp.where(cc < rr, 1.0, 0.0).astype(BF16)
        rank = _dot(earlier, sel) + cnt_ref[...]
        cnt_ref[...] += col_count
        pos = off_ref[...] + rank
        chosen = sel > 0
        lo = jnp.min(jnp.where(chosen, pos, 3e38), axis=1, keepdims=True)
        hi = jnp.max(jnp.where(chosen, pos, -1.0), axis=1, keepdims=True)
        g = gates_ref[...]
        w_lo = jnp.sum(jnp.where(chosen & (pos == lo), g, 0.0), axis=1, keepdims=True)
        w_hi = jnp.sum(jnp.where(chosen & (pos == hi), g, 0.0), axis=1, keepdims=True)
        lane_t = lax.broadcasted_iota(jnp.int32, (tr, LANES), 1)
        r_ref[...] = jnp.where(lane_t == 0, lo, jnp.where(lane_t == 1, hi,
                               jnp.where(lane_t == 2, w_lo, jnp.where(lane_t == 3, w_hi, 0.0))))


def _route(sel, gates, *, tr, tile_rows, n_tiles):
    n = sel.shape[0]
    n_tiles_pad = -(-n_tiles // LANES) * LANES
    return pl.pallas_call(
        functools.partial(_route_kernel, tile_rows=tile_rows),
        grid=(2, n // tr),
        in_specs=[pl.BlockSpec((tr, LANES), lambda ph, i: (i, 0)),
                  pl.BlockSpec((tr, LANES), lambda ph, i: (i, 0))],
        out_specs=[pl.BlockSpec((tr, LANES), lambda ph, i: (i * ph, 0)),
                   pl.BlockSpec((1, n_tiles_pad), lambda ph, i: (0, 0))],
        out_shape=[jax.ShapeDtypeStruct((n, LANES), F32),
                   jax.ShapeDtypeStruct((1, n_tiles_pad), jnp.int32)],
        scratch_shapes=[pltpu.VMEM((1, LANES), F32), pltpu.VMEM((1, LANES), F32)],
        compiler_params=_params(("arbitrary", "arbitrary")),
        name="moe_route",
    )(sel, gates)


def _dispatch_kernel(dest_ref, x_ref, xs_in_hbm, xs_hbm, sem, *, tm):
    del xs_in_hbm

    def row_copy(src_row, dst_row):
        return pltpu.make_async_copy(x_ref.at[pl.ds(src_row, 1)], xs_hbm.at[pl.ds(dst_row, 1)], sem)

    def issue(r, carry):
        row_copy(r, dest_ref[0, 0, r]).start()
        row_copy(r, dest_ref[0, 0, tm + r]).start()
        return carry

    lax.fori_loop(0, tm, issue, 0, unroll=DMA_ISSUE_UNROLL)
    pltpu.make_async_copy(x_ref, xs_hbm.at[pl.ds(0, tm)], sem).wait()
    pltpu.make_async_copy(x_ref, xs_hbm.at[pl.ds(0, tm)], sem).wait()


def _dispatch(dest, x, n_rows, *, tm):
    n = x.shape[0]
    xs0 = jnp.zeros((n_rows,) + x.shape[1:], x.dtype)
    return pl.pallas_call(
        functools.partial(_dispatch_kernel, tm=tm),
        grid=(n // tm,),
        in_specs=[pl.BlockSpec((1, 1, 2 * tm), lambda i: (i, 0, 0), memory_space=pltpu.SMEM),
                  pl.BlockSpec((tm, x.shape[1]), lambda i: (i, 0)),
                  pl.BlockSpec(memory_space=pl.ANY)],
        out_specs=pl.BlockSpec(memory_space=pl.ANY),
        out_shape=jax.ShapeDtypeStruct(xs0.shape, xs0.dtype),
        scratch_shapes=[pltpu.SemaphoreType.DMA(())],
        input_output_aliases={2: 0},
        compiler_params=pltpu.CompilerParams(dimension_semantics=("arbitrary",),
                                             has_side_effects=True),
        name="moe_dispatch",
    )(dest, x, xs0)


def _experts_kernel(te_ref, x_ref, wg_ref, wu_ref, wd_ref, o_ref):
    valid = te_ref[pl.program_id(0)] < N_EXPERTS

    @pl.when(valid)
    def _():
        o_ref[...] = _swiglu_tile(x_ref[...].astype(BF16), wg_ref, wu_ref, wd_ref, lead=(0,))

    @pl.when(jnp.logical_not(valid))
    def _():
        o_ref[...] = jnp.zeros_like(o_ref)


def _experts(tile_expert, xs, wg, wu, wd, *, tile_rows):
    n_rows, d = xs.shape
    ne, _, f = wg.shape
    w_idx = lambda j, te: (jnp.minimum(te[j], ne - 1), 0, 0)
    return pl.pallas_call(
        _experts_kernel,
        grid_spec=pltpu.PrefetchScalarGridSpec(
            num_scalar_prefetch=1,
            grid=(n_rows // tile_rows,),
            in_specs=[pl.BlockSpec((tile_rows, d), lambda j, te: (j, 0)),
                      pl.BlockSpec((1, d, f), w_idx),
                      pl.BlockSpec((1, d, f), w_idx),
                      pl.BlockSpec((1, f, d), w_idx)],
            out_specs=pl.BlockSpec((tile_rows, d), lambda j, te: (j, 0))),
        out_shape=jax.ShapeDtypeStruct((n_rows, d), F32),
        compiler_params=_params(("arbitrary",)),
        name="moe_experts",
    )(tile_expert, xs, wg, wu, wd)


def _combine_kernel(dest_ref, routed_ref, ys_hbm, h1_ref, g2_ref, fng_ref, o_ref, ylo_ref, yhi_ref,
                    sem, *, tm):
    def issue(r, carry):
        pltpu.make_async_copy(ys_hbm.at[pl.ds(dest_ref[0, 0, r], 1)],
                              ylo_ref.at[pl.ds(r, 1)], sem).start()
        pltpu.make_async_copy(ys_hbm.at[pl.ds(dest_ref[0, 0, tm + r], 1)],
                              yhi_ref.at[pl.ds(r, 1)], sem).start()
        return carry

    lax.fori_loop(0, tm, issue, 0, unroll=DMA_ISSUE_UNROLL)
    pltpu.make_async_copy(ys_hbm.at[pl.ds(0, tm)], ylo_ref, sem).wait()
    pltpu.make_async_copy(ys_hbm.at[pl.ds(0, tm)], yhi_ref, sem).wait()
    routed = routed_ref[...]
    y = routed[:, 2:3] * ylo_ref[...] + routed[:, 3:4] * yhi_ref[...]
    out = h1_ref[...] + g2_ref[0] * y
    o_ref[...] = _rms(out) * fng_ref[...]


def _combine(dest, routed, ys, h1, g2, fng, *, tm, tiles_per_sample):
    n, d = h1.shape
    return pl.pallas_call(
        functools.partial(_combine_kernel, tm=tm),
        grid=(n // tm,),
        in_specs=[pl.BlockSpec((1, 1, 2 * tm), lambda i: (i, 0, 0), memory_space=pltpu.SMEM),
                  pl.BlockSpec((tm, LANES), lambda i: (i, 0)),
                  pl.BlockSpec(memory_space=pl.ANY),
                  pl.BlockSpec((tm, d), lambda i: (i, 0)),
                  pl.BlockSpec((1, 1, d), lambda i: (i // tiles_per_sample, 0, 0)),
                  pl.BlockSpec((1, d), lambda i: (0, 0))],
        out_specs=pl.BlockSpec((tm, d), lambda i: (i, 0)),
        out_shape=jax.ShapeDtypeStruct((n, d), F32),
        scratch_shapes=[pltpu.VMEM((tm, d), F32), pltpu.VMEM((tm, d), F32),
                        pltpu.SemaphoreType.DMA(())],
        compiler_params=_params(("arbitrary",)),
        name="moe_combine",
    )(dest, routed, ys, h1, g2, fng)


def _moe(a2, gates, sel, wg, wu, wd, h1, g2, fng, *, tm):
    b, l, d = h1.shape
    n = b * l
    n_rows = 2 * n + N_EXPERTS * MOE_TILE
    n_tiles = n_rows // MOE_TILE
    routed, tile_expert = _route(sel.reshape(n, LANES), gates.reshape(n, LANES),
                                 tr=tm, tile_rows=MOE_TILE, n_tiles=n_tiles)
    dest = routed[:, 0:2].astype(jnp.int32).reshape(n // tm, tm, 2)
    dest = dest.transpose(0, 2, 1).reshape(n // tm, 1, 2 * tm)
    xs = _dispatch(dest, a2.reshape(n, d), n_rows, tm=tm)
    ys = _experts(tile_expert[0, :n_tiles], xs, wg, wu, wd, tile_rows=MOE_TILE)
    out = _combine(dest, routed, ys, h1.reshape(n, d), g2, fng, tm=tm, tiles_per_sample=l // tm)
    return out.reshape(b, l, d)


def _rope_tables(seq_len, dim):
    t = jnp.arange(seq_len, dtype=jnp.int32)
    row = (t // GRID_W).astype(F32)
    col = (t % GRID_W).astype(F32)
    quarter = dim // 4
    inv = ROPE_BASE ** (-jnp.arange(quarter, dtype=F32) / quarter)
    ar = row[:, None] * inv[None, :]
    ac = col[:, None] * inv[None, :]
    cos = jnp.concatenate([jnp.cos(ar), jnp.cos(ar), jnp.cos(ac), jnp.cos(ac)], axis=-1)
    sin = jnp.concatenate([jnp.sin(ar), jnp.sin(ar), jnp.sin(ac), jnp.sin(ac)], axis=-1)
    reps = LANES // dim
    return jnp.tile(cos, (1, reps)), jnp.tile(sin, (1, reps))


def kernel(x, c, ctx, c_ctx, norm1_g, norm2_g, mod_w, mod_b, even_w_in, pool_w, pool_scale,
           lambda_q1, lambda_k1, lambda_q2, lambda_k2, even_w_out, ffn_w_gate, ffn_w_up,
           ffn_w_down, odd_w_in, q_norm_g, kv_norm_g, w_uq, w_ukv, odd_w_out, router_w,
           moe_w_gate, moe_w_up, moe_w_down, final_norm_g):
    b, l, d = x.shape
    n_ctx = ctx.shape[1]
    tm = min(ROW_TILE, l)
    tkv = min(KV_TILE, l)
    tq = min(DIFF_Q_TILE, l)
    cos, sin = _rope_tables(l, DIFF_HEAD_DIM)
    cos_c, sin_c = cos[:n_ctx], sin[:n_ctx]

    cond = jnp.zeros((8, d), F32).at[:b].set(c).at[b].set(c_ctx)
    mod = _modulation(cond, mod_w, mod_b)
    mod = mod.reshape(mod.shape[0], 8, N_MOD, d)

    def mod_vecs(layer):
        lat = [mod[layer, :b, k][:, None, :] for k in range(N_MOD)]
        cx = [mod[layer, b:b + 1, k][:, None, :] for k in range(N_MOD)]
        return lat, cx

    (sh1, sc1, g1, sh2, sc2, g2), (csh1, csc1, cg1, csh2, csc2, cg2) = mod_vecs(0)
    lam_init = 0.8 - 0.6 * math.exp(-0.3 * 0)
    w_in0 = even_w_in[0].astype(BF16)
    n1g = norm1_g[0][None, :]
    n2g = norm2_g[0][None, :]
    p_lat, q1t_lat, q2t_lat, k_lat, vt_lat = _pre0(x, n1g, sh1, sc1, w_in0, cos, sin,
                                                   use_rope=True, tm=tkv)
    p_ctx, q1t_ctx, q2t_ctx, k_ctx, vt_ctx = _pre0(ctx, n1g, csh1, csc1, w_in0, cos_c, sin_c,
                                                   use_rope=False, tm=n_ctx)
    lq = jnp.stack([lambda_q1[0], lambda_q2[0]])
    lk = jnp.stack([lambda_k1[0], lambda_k2[0]])
    diff_kern = functools.partial(_diff_attn_kernel, lam_init=lam_init)
    attn_lat = _attention(diff_kern, (q1t_lat, q2t_lat), (k_ctx, vt_ctx), (k_lat, vt_lat),
                          2 * DIFF_HEAD_DIM, extra=(lq, lk), tq=tq,
                          n_sub=min(DIFF_Q_SWEEPS, l // tq), name="diff_attn_lat")
    attn_ctx = _attention(diff_kern, (q1t_ctx, q2t_ctx), (k_ctx, vt_ctx), None,
                          2 * DIFF_HEAD_DIM, extra=(lq, lk), tq=n_ctx, n_sub=1, name="diff_attn_ctx")
    ng = len(POOL_WINDOWS)
    poolw_bd = jnp.zeros((POOL_WIDTH, POOL_WIDTH), F32)
    for g in range(ng):
        sl = slice(g * POOL_GROUP, (g + 1) * POOL_GROUP)
        poolw_bd = poolw_bd.at[sl, sl].set(pool_w[0, g])
    poolw_bd = poolw_bd.astype(BF16)
    pscale = pool_scale[0][None, :]
    w_out0 = even_w_out[0].astype(BF16)
    wg0, wu0, wd0 = ffn_w_gate[0].astype(BF16), ffn_w_up[0].astype(BF16), ffn_w_down[0].astype(BF16)
    h1_lat, a2_lat = _post0(attn_lat, p_lat, poolw_bd, pscale, w_out0, x, g1, n2g, sh2, sc2, tm=tm)
    h_lat = _ffn(a2_lat, h1_lat, g2, wg0, wu0, wd0, tm=tm)
    h1_ctx, a2_ctx = _post0(attn_ctx, p_ctx, poolw_bd, pscale, w_out0, ctx, cg1, n2g, csh2, csc2,
                            tm=n_ctx)
    h_ctx = _ffn(a2_ctx, h1_ctx, cg2, wg0, wu0, wd0, tm=n_ctx)

    (sh1, sc1, g1, sh2, sc2, g2), (csh1, csc1, _, _, _, _) = mod_vecs(1)
    n1g = norm1_g[1][None, :]
    n2g = norm2_g[1][None, :]
    q_rank, kv_rank = w_uq.shape[1], w_ukv.shape[1]
    n_heads = w_uq.shape[2] // (MLA_NOPE + MLA_ROPE)
    w_in1 = jnp.pad(odd_w_in[0], ((0, 0), (0, LANES - MLA_ROPE))).astype(BF16)
    wuq = w_uq[0].reshape(q_rank, n_heads, MLA_NOPE + MLA_ROPE)
    wuq = jnp.pad(wuq, ((0, 0), (0, 0), (0, MLA_QK_PAD - MLA_NOPE - MLA_ROPE)))
    wuq = wuq.reshape(q_rank, n_heads * MLA_QK_PAD).astype(BF16)
    wukv = w_ukv[0].reshape(kv_rank, n_heads, MLA_NOPE + MLA_V)
    wuk = wukv[:, :, :MLA_NOPE].reshape(kv_rank, n_heads * MLA_NOPE).astype(BF16)
    wuv = wukv[:, :, MLA_NOPE:].reshape(kv_rank, n_heads * MLA_V).astype(BF16)
    qg = q_norm_g[0][None, :]
    kvg = kv_norm_g[0][None, :]
    q1t, k1, vt1 = _pre1(h_lat, n1g, sh1, sc1, w_in1, qg, kvg, wuq, wuk, wuv, cos, sin,
                         use_rope=True, want_q=True, tm=tkv)
    k1c, vt1c = _pre1(h_ctx, n1g, csh1, csc1, w_in1, qg, kvg, wuq, wuk, wuv, cos_c, sin_c,
                      use_rope=False, want_q=False, tm=n_ctx)
    tq1 = min(MLA_Q_TILE, l)
    o1 = _attention(_mla_attn_kernel, (q1t,), (k1c, vt1c), (k1, vt1), MLA_QK_PAD,
                    tq=tq1, n_sub=min(MLA_Q_SWEEPS, l // tq1), name="mla_attn")
    rw_pad = jnp.pad(router_w[0], ((0, 0), (0, LANES - N_EXPERTS)))
    rw_hi = rw_pad.astype(BF16)
    rw_lo = (rw_pad - rw_hi.astype(F32)).astype(BF16)
    h1, a2, gates, sel = _post1(o1, odd_w_out[0].astype(BF16), h_lat, g1, n2g, sh2, sc2,
                                rw_hi, rw_lo, tm=tm)
    return _moe(a2, gates, sel, moe_w_gate[0].astype(BF16), moe_w_up[0].astype(BF16),
                moe_w_down[0].astype(BF16), h1, g2, final_norm_g[None, :], tm=tm)
```

```python
import functools
import math

import jax
import jax.numpy as jnp
from jax import lax
from jax.experimental import pallas as pl
from jax.experimental.pallas import tpu as pltpu

F32 = jnp.float32
BF16 = jnp.bfloat16

EPS = 1e-6
GRID_W = 64
ROPE_BASE = 10000.0
N_MOD = 6
POOL_WINDOWS = (2, 4, 8, 16)
POOL_GROUP = 64
POOL_WIDTH = POOL_GROUP * len(POOL_WINDOWS)
POOL_HALO = max(POOL_WINDOWS) // 2
DIFF_HEAD_DIM = 64
DIFF_V_DIM = 128
V_ONES_ROWS = 16
V_AUG = DIFF_V_DIM + V_ONES_ROWS
MLA_NOPE = 128
MLA_ROPE = 64
MLA_V = 128
MLA_QK_PAD = 256
N_EXPERTS = 8
ROPE_QUARTER = 16
LOG2E = math.log2(math.e)

LANES = 128
ROW_TILE = 512
KV_TILE = 512
DIFF_Q_TILE = 512
MLA_Q_TILE = 1024
DIFF_Q_SWEEPS = 8
MLA_Q_SWEEPS = 4
PAIR_UNROLL = 5
MOE_TILE = 512
DMA_ISSUE_UNROLL = 8
FFN_CHUNKS = (512, 512, 512, 512, 512, 256)
VMEM_LIMIT = 56 * 1024 * 1024


def _params(sem):
    return pltpu.CompilerParams(dimension_semantics=sem, vmem_limit_bytes=VMEM_LIMIT)


def _dot(a, b):
    return jnp.dot(a, b, preferred_element_type=F32)


def _rms(x):
    return x * lax.rsqrt(jnp.mean(x * x, axis=-1, keepdims=True) + EPS)


def _silu(x):
    return x * (1.0 / (1.0 + jnp.exp(-x)))


def _rope(x, cos, sin):
    lane = lax.broadcasted_iota(jnp.int32, x.shape, 1)
    even = (lane // ROPE_QUARTER) % 2 == 0
    rot = jnp.where(even, -pltpu.roll(x, LANES - ROPE_QUARTER, 1), pltpu.roll(x, ROPE_QUARTER, 1))
    return x * cos + rot * sin


def _store_vt(vt_ref, v, n_heads):
    tm = v.shape[0]
    ones = jnp.ones((V_ONES_ROWS, tm), BF16)
    for h in range(n_heads):
        r0 = h * V_AUG
        vt_ref[0, 0, r0:r0 + DIFF_V_DIM, :] = v[:, h * DIFF_V_DIM:(h + 1) * DIFF_V_DIM].T.astype(BF16)
        vt_ref[0, 0, r0 + DIFF_V_DIM:r0 + V_AUG, :] = ones


def _mod_kernel(cond_ref, w_ref, b_ref, o_ref):
    s = _silu(cond_ref[...])
    o_ref[0] = jnp.dot(s, w_ref[0], preferred_element_type=F32,
                       precision=lax.Precision.HIGHEST) + b_ref[0]


def _modulation(cond, mod_w, mod_b):
    depth, d, n = mod_w.shape
    tn = n // 4
    return pl.pallas_call(
        _mod_kernel,
        grid=(depth, n // tn),
        in_specs=[pl.BlockSpec((8, d), lambda i, j: (0, 0)),
                  pl.BlockSpec((1, d, tn), lambda i, j: (i, 0, j)),
                  pl.BlockSpec((1, 1, tn), lambda i, j: (i, 0, j))],
        out_specs=pl.BlockSpec((1, 8, tn), lambda i, j: (i, 0, j)),
        out_shape=jax.ShapeDtypeStruct((depth, 8, n), F32),
        compiler_params=_params(("arbitrary", "arbitrary")),
        name="modulation",
    )(cond, mod_w, mod_b.reshape(depth, 1, n))


def _bvec_spec(arr, d):
    if arr.shape[0] == 1:
        return pl.BlockSpec((1, 1, d), lambda b, i: (0, 0, 0))
    return pl.BlockSpec((1, 1, d), lambda b, i: (b, 0, 0))


def _norm_mod(h, g, sh, sc):
    a = _rms(h) * g
    return a * (1.0 + sc) + sh


def _pre0_kernel(h_ref, g_ref, sh_ref, sc_ref, w_ref, cos_ref, sin_ref,
                 p_ref, q1t_ref, q2t_ref, k_ref, vt_ref, *, use_rope, n_heads):
    a = _norm_mod(h_ref[0], g_ref[...], sh_ref[0], sc_ref[0])
    z = _dot(a.astype(BF16), w_ref[...])
    p_ref[0] = z[:, :POOL_WIDTH]
    head_w = 2 * DIFF_HEAD_DIM
    qk_w = n_heads * head_w
    q_scale = DIFF_HEAD_DIM ** -0.5 * LOG2E
    cos = cos_ref[...]
    sin = sin_ref[...]
    for h in range(n_heads):
        xq = z[:, POOL_WIDTH + h * head_w:POOL_WIDTH + (h + 1) * head_w]
        xk = z[:, POOL_WIDTH + qk_w + h * head_w:POOL_WIDTH + qk_w + (h + 1) * head_w]
        if use_rope:
            xq = _rope(xq, cos, sin)
            xk = _rope(xk, cos, sin)
        xqt = (xq * q_scale).T
        row = lax.broadcasted_iota(jnp.int32, xqt.shape, 0)
        q1t_ref[0, 0, h * head_w:(h + 1) * head_w, :] = jnp.where(row < DIFF_HEAD_DIM, xqt, 0.0).astype(BF16)
        q2t_ref[0, 0, h * head_w:(h + 1) * head_w, :] = jnp.where(row >= DIFF_HEAD_DIM, xqt, 0.0).astype(BF16)
        k_ref[0, :, h * head_w:(h + 1) * head_w] = xk.astype(BF16)
    _store_vt(vt_ref, z[:, POOL_WIDTH + 2 * qk_w:], n_heads)


def _pre0(h, g, sh, sc, w_in, cos, sin, *, use_rope, tm):
    b, l, d = h.shape
    n_in = w_in.shape[1]
    v_w = (n_in - POOL_WIDTH) // 3
    n_heads = v_w // DIFF_V_DIM
    kern = functools.partial(_pre0_kernel, use_rope=use_rope, n_heads=n_heads)
    return pl.pallas_call(
        kern,
        grid=(b, l // tm),
        in_specs=[pl.BlockSpec((1, tm, d), lambda b_, i: (b_, i, 0)),
                  pl.BlockSpec((1, d), lambda b_, i: (0, 0)),
                  _bvec_spec(sh, d), _bvec_spec(sc, d),
                  pl.BlockSpec((d, n_in), lambda b_, i: (0, 0)),
                  pl.BlockSpec((tm, LANES), lambda b_, i: (i, 0)),
                  pl.BlockSpec((tm, LANES), lambda b_, i: (i, 0))],
        out_specs=[pl.BlockSpec((1, tm, POOL_WIDTH), lambda b_, i: (b_, i, 0)),
                   pl.BlockSpec((1, 1, v_w, tm), lambda b_, i: (b_, i, 0, 0)),
                   pl.BlockSpec((1, 1, v_w, tm), lambda b_, i: (b_, i, 0, 0)),
                   pl.BlockSpec((1, tm, v_w), lambda b_, i: (b_, i, 0)),
                   pl.BlockSpec((1, 1, n_heads * V_AUG, tm), lambda b_, i: (b_, i, 0, 0))],
        out_shape=[jax.ShapeDtypeStruct((b, l, POOL_WIDTH), F32),
                   jax.ShapeDtypeStruct((b, l // tm, v_w, tm), BF16),
                   jax.ShapeDtypeStruct((b, l // tm, v_w, tm), BF16),
                   jax.ShapeDtypeStruct((b, l, v_w), BF16),
                   jax.ShapeDtypeStruct((b, l // tm, n_heads * V_AUG, tm), BF16)],
        compiler_params=_params(("parallel", "parallel")),
        name="pre0_rope" if use_rope else "pre0_ctx",
    )(h, g, sh, sc, w_in, cos, sin)


def _attn_core(qt_refs, j, tq, kc_ref, vc_ref, lat_refs, acc_refs, sbuf_refs):
    n = len(qt_refs)
    sub = tq // qt_refs[0].shape[3]
    qts = [jnp.concatenate([r[0, j * sub + u] for u in range(sub)], axis=1) for r in qt_refs]
    ctx_bufs = sbuf_refs[:n]
    lat_bufs = sbuf_refs[n:]
    ms = tuple(jnp.full((1, tq), -jnp.inf, F32) for _ in range(n))
    for i in range(n):
        acc_refs[i][...] = jnp.zeros_like(acc_refs[i])

    def scores(k, bufs):
        cmax = []
        for i in range(n):
            s = _dot(k, qts[i])
            bufs[i][...] = s
            cmax.append(jnp.max(s, axis=0, keepdims=True))
        return tuple(cmax)

    def softmax_pv(vt, bufs, cmax, ms):
        new_m = []
        for i in range(n):
            m_new = jnp.maximum(ms[i], cmax[i])
            alpha = jnp.exp2(ms[i] - m_new)
            p = jnp.exp2(bufs[i][...] - m_new)
            acc_refs[i][...] = alpha * acc_refs[i][...] + _dot(vt, p.astype(BF16))
            new_m.append(m_new)
        return tuple(new_m)

    if lat_refs is None:
        softmax_pv(vc_ref[0, 0], ctx_bufs, scores(kc_ref[0], ctx_bufs), ms)
        return
    kl_ref, vl_ref = lat_refs
    nblk, tk = vl_ref.shape[1], vl_ref.shape[3]
    assert nblk >= 2 and nblk % 2 == 0

    def qk(t, which):
        start = t * tk if isinstance(t, int) else pl.multiple_of(t * tk, tk)
        return scores(kl_ref[0, pl.ds(start, tk), :], lat_bufs[which::2])

    def pair(jj, carry):
        cmax_a, ms = carry
        t = 2 * jj
        cmax_b = qk(t + 1, 1)
        ms = softmax_pv(vl_ref[0, t], lat_bufs[0::2], cmax_a, ms)
        cmax_a = qk(t + 2, 0)
        ms = softmax_pv(vl_ref[0, t + 1], lat_bufs[1::2], cmax_b, ms)
        return cmax_a, ms

    cmax_a = qk(0, 0)
    n_pairs = nblk // 2 - 1
    cmax_a, ms = lax.fori_loop(0, n_pairs, pair, (cmax_a, ms),
                               unroll=max(1, min(PAIR_UNROLL, n_pairs)))
    cmax_b = qk(nblk - 1, 1)
    ms = softmax_pv(vl_ref[0, nblk - 2], lat_bufs[0::2], cmax_a, ms)
    cmax_c = scores(kc_ref[0], ctx_bufs)
    ms = softmax_pv(vl_ref[0, nblk - 1], lat_bufs[1::2], cmax_b, ms)
    softmax_pv(vc_ref[0, 0], ctx_bufs, cmax_c, ms)


def _normalised(acc_ref):
    acc = acc_ref[...]
    return acc[:DIFF_V_DIM] * (1.0 / acc[DIFF_V_DIM:DIFF_V_DIM + 1])


def _for_each_query_tile(o_ref, tq, tile_fn):
    n_tiles = o_ref.shape[1] // tq

    def body(j, carry):
        rows = pl.ds(pl.multiple_of(j * tq, tq), tq)
        o_ref[0, rows, :] = tile_fn(j)
        return carry

    if n_tiles == 1:
        o_ref[0] = tile_fn(0)
    else:
        lax.fori_loop(0, n_tiles, body, 0)


def _diff_attn_kernel(lq_ref, lk_ref, q1t_ref, q2t_ref, kc_ref, vc_ref, *rest, has_lat, lam_init, tq):
    lat_refs = rest[:2] if has_lat else None
    rest = rest[2:] if has_lat else rest
    o_ref, acc1_ref, acc2_ref = rest[:3]
    e = jnp.exp(jnp.sum(lq_ref[...] * lk_ref[...], axis=-1, keepdims=True))
    lam = e[0:1] - e[1:2] + lam_init

    def tile(j):
        _attn_core((q1t_ref, q2t_ref), j, tq, kc_ref, vc_ref, lat_refs, (acc1_ref, acc2_ref), rest[3:])
        o = _normalised(acc1_ref) - lam * _normalised(acc2_ref)
        ms = jnp.mean(o * o, axis=0, keepdims=True)
        y = o * (lax.rsqrt(ms + EPS) * (1.0 - lam_init))
        return y.T.astype(BF16)

    _for_each_query_tile(o_ref, tq, tile)


def _mla_attn_kernel(qt_ref, kc_ref, vc_ref, *rest, has_lat, tq):
    lat_refs = rest[:2] if has_lat else None
    rest = rest[2:] if has_lat else rest
    o_ref, acc_ref = rest[:2]

    def tile(j):
        _attn_core((qt_ref,), j, tq, kc_ref, vc_ref, lat_refs, (acc_ref,), rest[2:])
        return _normalised(acc_ref).T.astype(BF16)

    _for_each_query_tile(o_ref, tq, tile)


def _attention(kern, qts, ctx_kv, lat_kv, qk_w, extra=(), *, tq, n_sub, name):
    n_maps = len(qts)
    b, nqb, hw, tmq = qts[0].shape
    n_heads = hw // qk_w
    lq = nqb * tmq
    rows = n_sub * tq
    in_specs = [pl.BlockSpec(x.shape, lambda b_, h, i: (0, 0)) for x in extra]
    in_specs += [pl.BlockSpec((1, rows // tmq, qk_w, tmq), lambda b_, h, i: (b_, i, h, 0)) for _ in qts]
    args = list(extra) + list(qts)
    scratch = [pltpu.VMEM((V_AUG, tq), F32) for _ in range(n_maps)]
    scratch += [pltpu.VMEM((ctx_kv[0].shape[1], tq), F32) for _ in range(n_maps)]
    for kv in (ctx_kv, lat_kv):
        if kv is None:
            continue
        k, vt = kv
        nblk, tk = vt.shape[1], vt.shape[3]
        in_specs.append(pl.BlockSpec((1, k.shape[1], qk_w), lambda b_, h, i: (b_, 0, h)))
        in_specs.append(pl.BlockSpec((1, nblk, V_AUG, tk), lambda b_, h, i: (b_, 0, h, 0)))
        args += [k, vt]
    if lat_kv is not None:
        scratch += [pltpu.VMEM((lat_kv[1].shape[3], tq), F32) for _ in range(2 * n_maps)]
    return pl.pallas_call(
        functools.partial(kern, has_lat=lat_kv is not None, tq=tq),
        grid=(b, n_heads, lq // rows),
        in_specs=in_specs,
        out_specs=pl.BlockSpec((1, rows, DIFF_V_DIM), lambda b_, h, i: (b_, i, h)),
        out_shape=jax.ShapeDtypeStruct((b, lq, n_heads * DIFF_V_DIM), BF16),
        scratch_shapes=scratch,
        compiler_params=_params(("parallel", "parallel", "arbitrary")),
        name=name,
    )(*args)


def _post0_kernel(attn_ref, p_ref, pprev_ref, pnext_ref, poolw_ref, pscale_ref, wout_ref,
                  h_ref, g1_ref, n2g_ref, sh2_ref, sc2_ref, g2_ref, wg_ref, wu_ref, wd_ref,
                  o_ref, *, seq_len):
    i = pl.program_id(1)
    u = p_ref[0]
    tm = u.shape[0]
    ext = jnp.concatenate([pprev_ref[0], u, pnext_ref[0]], axis=0)
    row = i * tm - POOL_HALO + lax.broadcasted_iota(jnp.int32, (tm + 2 * POOL_HALO, 1), 0)
    ext = jnp.where((row >= 0) & (row < seq_len), ext, 0.0)

    def shifted(j):
        return ext[POOL_HALO + j:POOL_HALO + j + tm]

    t = i * tm + lax.broadcasted_iota(jnp.int32, (tm, 1), 0)
    lane = lax.broadcasted_iota(jnp.int32, (tm, POOL_WIDTH), 1)
    win_sum = shifted(-1) + shifted(0)
    pooled = jnp.zeros_like(u)
    for g, w in enumerate(POOL_WINDOWS):
        half = w // 2
        if g > 0:
            prev_half = POOL_WINDOWS[g - 1] // 2
            for j in range(prev_half, half):
                win_sum = win_sum + shifted(-j - 1) + shifted(j)
        cnt = (jnp.minimum(t + half, seq_len) - jnp.maximum(t - half, 0)).astype(F32)
        in_group = (lane >= g * POOL_GROUP) & (lane < (g + 1) * POOL_GROUP)
        pooled = jnp.where(in_group, win_sum / cnt, pooled)
    pooled = pooled - u
    y_pool = _dot(pooled.astype(BF16), poolw_ref[...]) * pscale_ref[...]
    y = _dot(y_pool.astype(BF16), wout_ref[:POOL_WIDTH, :]) + _dot(attn_ref[0], wout_ref[POOL_WIDTH:, :])
    h1 = h_ref[0] + g1_ref[0] * y
    a2 = _norm_mod(h1, n2g_ref[...], sh2_ref[0], sc2_ref[0]).astype(BF16)
    o_ref[0] = h1 + g2_ref[0] * _swiglu_tile(a2, wg_ref, wu_ref, wd_ref)


def _post0(attn, p, poolw_bd, pscale, w_out, h, g1, n2g, sh2, sc2, g2, wg, wu, wd, *, tm):
    b, l, d = h.shape
    hb = tm // POOL_HALO
    nhb = l // POOL_HALO
    assert sum(FFN_CHUNKS) == wg.shape[1]
    return pl.pallas_call(
        functools.partial(_post0_kernel, seq_len=l),
        grid=(b, l // tm),
        in_specs=[pl.BlockSpec((1, tm, attn.shape[2]), lambda b_, i: (b_, i, 0)),
                  pl.BlockSpec((1, tm, POOL_WIDTH), lambda b_, i: (b_, i, 0)),
                  pl.BlockSpec((1, POOL_HALO, POOL_WIDTH),
                               lambda b_, i: (b_, jnp.maximum(i * hb - 1, 0), 0)),
                  pl.BlockSpec((1, POOL_HALO, POOL_WIDTH),
                               lambda b_, i: (b_, jnp.minimum((i + 1) * hb, nhb - 1), 0)),
                  pl.BlockSpec(poolw_bd.shape, lambda b_, i: (0, 0)),
                  pl.BlockSpec((1, POOL_WIDTH), lambda b_, i: (0, 0)),
                  pl.BlockSpec(w_out.shape, lambda b_, i: (0, 0)),
                  pl.BlockSpec((1, tm, d), lambda b_, i: (b_, i, 0)),
                  _bvec_spec(g1, d),
                  pl.BlockSpec((1, d), lambda b_, i: (0, 0)),
                  _bvec_spec(sh2, d), _bvec_spec(sc2, d), _bvec_spec(g2, d),
                  pl.BlockSpec(wg.shape, lambda b_, i: (0, 0)),
                  pl.BlockSpec(wu.shape, lambda b_, i: (0, 0)),
                  pl.BlockSpec(wd.shape, lambda b_, i: (0, 0))],
        out_specs=pl.BlockSpec((1, tm, d), lambda b_, i: (b_, i, 0)),
        out_shape=jax.ShapeDtypeStruct((b, l, d), F32),
        compiler_params=_params(("parallel", "parallel")),
        name="post0_ffn",
    )(attn, p, p, p, poolw_bd, pscale, w_out, h, g1, n2g, sh2, sc2, g2, wg, wu, wd)


def _swiglu_tile(x, wg_ref, wu_ref, wd_ref, lead=()):
    acc = None
    c0 = 0
    for cw in FFN_CHUNKS:
        hg = _dot(x, wg_ref[lead + (slice(None), slice(c0, c0 + cw))])
        hu = _dot(x, wu_ref[lead + (slice(None), slice(c0, c0 + cw))])
        act = (_silu(hg) * hu).astype(BF16)
        part = _dot(act, wd_ref[lead + (slice(c0, c0 + cw), slice(None))])
        acc = part if acc is None else acc + part
        c0 += cw
    return acc


def _pre1_kernel(h_ref, g_ref, sh_ref, sc_ref, win_ref, qg_ref, kvg_ref, wuq_ref, wuk_ref, wuv_ref,
                 cos_ref, sin_ref, *out_refs, use_rope, want_q, q_rank, kv_rank, n_heads):
    a = _norm_mod(h_ref[0], g_ref[...], sh_ref[0], sc_ref[0])
    z = _dot(a.astype(BF16), win_ref[...])
    cos = cos_ref[...]
    sin = sin_ref[...]
    if want_q:
        qt_ref, k_ref, vt_ref = out_refs
        cq = _rms(z[:, :q_rank]) * qg_ref[...]
        qf = _dot(cq.astype(BF16), wuq_ref[...])
        q_scale = (MLA_NOPE + MLA_ROPE) ** -0.5 * LOG2E
        for h in range(n_heads):
            c0 = h * MLA_QK_PAD
            qn = qf[:, c0:c0 + MLA_NOPE] * q_scale
            qr = _rope(qf[:, c0 + MLA_NOPE:c0 + MLA_QK_PAD], cos, sin) * q_scale
            qt_ref[0, 0, c0:c0 + MLA_NOPE, :] = qn.T.astype(BF16)
            qt_ref[0, 0, c0 + MLA_NOPE:c0 + MLA_QK_PAD, :] = qr.T.astype(BF16)
    else:
        k_ref, vt_ref = out_refs
    ckv = (_rms(z[:, q_rank:q_rank + kv_rank]) * kvg_ref[...]).astype(BF16)
    kn = _dot(ckv, wuk_ref[...])
    vv = _dot(ckv, wuv_ref[...])
    kr = z[:, q_rank + kv_rank:]
    if use_rope:
        kr = _rope(kr, cos, sin)
    kr = kr.astype(BF16)
    for h in range(n_heads):
        c0 = h * MLA_QK_PAD
        k_ref[0, :, c0:c0 + MLA_NOPE] = kn[:, h * MLA_NOPE:(h + 1) * MLA_NOPE].astype(BF16)
        k_ref[0, :, c0 + MLA_NOPE:c0 + MLA_QK_PAD] = kr
    _store_vt(vt_ref, vv, n_heads)


def _pre1(h, g, sh, sc, w_in, qg, kvg, wuq, wuk, wuv, cos, sin, *, use_rope, want_q, tm):
    b, l, d = h.shape
    n_heads = wuk.shape[1] // MLA_NOPE
    q_rank, kv_rank = wuq.shape[0], wuk.shape[0]
    kern = functools.partial(_pre1_kernel, use_rope=use_rope, want_q=want_q,
                             q_rank=q_rank, kv_rank=kv_rank, n_heads=n_heads)
    full = lambda x: pl.BlockSpec(x.shape, lambda b_, i: (0,) * x.ndim)
    qk_w = n_heads * MLA_QK_PAD
    v_rows = n_heads * V_AUG
    out_specs = [pl.BlockSpec((1, tm, qk_w), lambda b_, i: (b_, i, 0)),
                 pl.BlockSpec((1, 1, v_rows, tm), lambda b_, i: (b_, i, 0, 0))]
    out_shape = [jax.ShapeDtypeStruct((b, l, qk_w), BF16),
                 jax.ShapeDtypeStruct((b, l // tm, v_rows, tm), BF16)]
    if want_q:
        out_specs = [pl.BlockSpec((1, 1, qk_w, tm), lambda b_, i: (b_, i, 0, 0))] + out_specs
        out_shape = [jax.ShapeDtypeStruct((b, l // tm, qk_w, tm), BF16)] + out_shape
    return pl.pallas_call(
        kern,
        grid=(b, l // tm),
        in_specs=[pl.BlockSpec((1, tm, d), lambda b_, i: (b_, i, 0)),
                  pl.BlockSpec((1, d), lambda b_, i: (0, 0)),
                  _bvec_spec(sh, d), _bvec_spec(sc, d),
                  full(w_in), full(qg), full(kvg), full(wuq), full(wuk), full(wuv),
                  pl.BlockSpec((tm, LANES), lambda b_, i: (i, 0)),
                  pl.BlockSpec((tm, LANES), lambda b_, i: (i, 0))],
        out_specs=out_specs,
        out_shape=out_shape,
        compiler_params=_params(("parallel", "parallel")),
        name="pre1_lat" if want_q else "pre1_ctx",
    )(h, g, sh, sc, w_in, qg, kvg, wuq, wuk, wuv, cos, sin)


def _post1_kernel(o_ref, wout_ref, h_ref, g1_ref, n2g_ref, sh2_ref, sc2_ref, rwh_ref, rwl_ref,
                  h1_ref, a2_ref, gates_ref, sel_ref):
    y = _dot(o_ref[0], wout_ref[...])
    h1 = h_ref[0] + g1_ref[0] * y
    h1_ref[0] = h1
    a2 = _norm_mod(h1, n2g_ref[...], sh2_ref[0], sc2_ref[0])
    a2_ref[0] = a2
    a_hi = a2.astype(BF16)
    a_lo = (a2 - a_hi.astype(F32)).astype(BF16)
    logits = _dot(a_hi, rwh_ref[...]) + (_dot(a_lo, rwh_ref[...]) + _dot(a_hi, rwl_ref[...]))
    lane = lax.broadcasted_iota(jnp.int32, logits.shape, 1)
    lg = jnp.where(lane < N_EXPERTS, logits, -jnp.inf)
    m1 = jnp.max(lg, axis=1, keepdims=True)
    i1 = jnp.min(jnp.where(lg == m1, lane, LANES), axis=1, keepdims=True)
    lg2 = jnp.where(lane == i1, -jnp.inf, lg)
    m2 = jnp.max(lg2, axis=1, keepdims=True)
    i2 = jnp.min(jnp.where(lg2 == m2, lane, LANES), axis=1, keepdims=True)
    p2 = jnp.exp(m2 - m1)
    w1 = 1.0 / (1.0 + p2)
    gates_ref[0] = jnp.where(lane == i1, w1, 0.0) + jnp.where(lane == i2, p2 * w1, 0.0)
    sel_ref[0] = jnp.where((lane == i1) | (lane == i2), 1.0, 0.0).astype(BF16)


def _post1(o, w_out, h, g1, n2g, sh2, sc2, rw_hi, rw_lo, *, tm):
    b, l, d = h.shape
    return pl.pallas_call(
        _post1_kernel,
        grid=(b, l // tm),
        in_specs=[pl.BlockSpec((1, tm, o.shape[2]), lambda b_, i: (b_, i, 0)),
                  pl.BlockSpec(w_out.shape, lambda b_, i: (0, 0)),
                  pl.BlockSpec((1, tm, d), lambda b_, i: (b_, i, 0)),
                  _bvec_spec(g1, d),
                  pl.BlockSpec((1, d), lambda b_, i: (0, 0)),
                  _bvec_spec(sh2, d), _bvec_spec(sc2, d),
                  pl.BlockSpec(rw_hi.shape, lambda b_, i: (0, 0)),
                  pl.BlockSpec(rw_lo.shape, lambda b_, i: (0, 0))],
        out_specs=[pl.BlockSpec((1, tm, d), lambda b_, i: (b_, i, 0)),
                   pl.BlockSpec((1, tm, d), lambda b_, i: (b_, i, 0)),
                   pl.BlockSpec((1, tm, LANES), lambda b_, i: (b_, i, 0)),
                   pl.BlockSpec((1, tm, LANES), lambda b_, i: (b_, i, 0))],
        out_shape=[jax.ShapeDtypeStruct((b, l, d), F32),
                   jax.ShapeDtypeStruct((b, l, d), F32),
                   jax.ShapeDtypeStruct((b, l, LANES), F32),
                   jax.ShapeDtypeStruct((b, l, LANES), BF16)],
        compiler_params=_params(("parallel", "parallel")),
        name="post1_router",
    )(o, w_out, h, g1, n2g, sh2, sc2, rw_hi, rw_lo)


def _route_kernel(sel_ref, gates_ref, r_ref, tile_ref, ends_ref, cnt_ref, off_ref, *, tile_rows):
    phase = pl.program_id(0)
    i = pl.program_id(1)
    sel = sel_ref[...]
    tr = sel.shape[0]
    col_count = jnp.sum(sel.astype(F32), axis=0, keepdims=True)
    lane = lax.broadcasted_iota(jnp.int32, (1, LANES), 1)

    @pl.when((phase == 0) & (i == 0))
    def _():
        cnt_ref[...] = jnp.zeros_like(cnt_ref)

    @pl.when(phase == 0)
    def _():
        cnt_ref[...] += col_count

    @pl.when((phase == 0) & (i == pl.num_programs(1) - 1))
    def _():
        padded = jnp.ceil(cnt_ref[...] / tile_rows) * tile_rows
        incl = padded
        for sh in (1, 2, 4):
            incl = incl + jnp.where(lane >= sh, pltpu.roll(incl, sh, 1), 0.0)
        off_ref[...] = incl - padded
        ends_ref[...] = incl.astype(jnp.int32)
        tile_start = (lax.broadcasted_iota(jnp.int32, tile_ref.shape, 1) * tile_rows).astype(F32)
        tile_expert = jnp.zeros(tile_ref.shape, jnp.int32)
        for e in range(N_EXPERTS):
            end_e = jnp.sum(jnp.where(lane == e, incl, 0.0), axis=1, keepdims=True)
            tile_expert = tile_expert + (tile_start >= end_e).astype(jnp.int32)
        tile_ref[...] = tile_expert
        cnt_ref[...] = jnp.zeros_like(cnt_ref)

    @pl.when(phase == 1)
    def _():
        rr = lax.broadcasted_iota(jnp.int32, (tr, tr), 0)
        cc = lax.broadcasted_iota(jnp.int32, (tr, tr), 1)
        earlier = jnp.where(cc < rr, 1.0, 0.0).astype(BF16)
        rank = _dot(earlier, sel) + cnt_ref[...]
        cnt_ref[...] += col_count
        pos = off_ref[...] + rank
        chosen = sel > 0
        lo = jnp.min(jnp.where(chosen, pos, 3e38), axis=1, keepdims=True)
        hi = jnp.max(jnp.where(chosen, pos, -1.0), axis=1, keepdims=True)
        g = gates_ref[...]
        w_lo = jnp.sum(jnp.where(chosen & (pos == lo), g, 0.0), axis=1, keepdims=True)
        w_hi = jnp.sum(jnp.where(chosen & (pos == hi), g, 0.0), axis=1, keepdims=True)
        lane_t = lax.broadcasted_iota(jnp.int32, (tr, LANES), 1)
        r_ref[...] = jnp.where(lane_t == 0, lo, jnp.where(lane_t == 1, hi,
                               jnp.where(lane_t == 2, w_lo, jnp.where(lane_t == 3, w_hi, 0.0))))


def _route(sel, gates, *, tr, tile_rows, n_tiles):
    n = sel.shape[0]
    n_tiles_pad = -(-n_tiles // LANES) * LANES
    return pl.pallas_call(
        functools.partial(_route_kernel, tile_rows=tile_rows),
        grid=(2, n // tr),
        in_specs=[pl.BlockSpec((tr, LANES), lambda ph, i: (i, 0)),
                  pl.BlockSpec((tr, LANES), lambda ph, i: (i, 0))],
        out_specs=[pl.BlockSpec((tr, LANES), lambda ph, i: (i * ph, 0)),
                   pl.BlockSpec((1, n_tiles_pad), lambda ph, i: (0, 0)),
                   pl.BlockSpec((1, LANES), lambda ph, i: (0, 0))],
        out_shape=[jax.ShapeDtypeStruct((n, LANES), F32),
                   jax.ShapeDtypeStruct((1, n_tiles_pad), jnp.int32),
                   jax.ShapeDtypeStruct((1, LANES), jnp.int32)],
        scratch_shapes=[pltpu.VMEM((1, LANES), F32), pltpu.VMEM((1, LANES), F32)],
        compiler_params=_params(("arbitrary", "arbitrary")),
        name="moe_route",
    )(sel, gates)


def _dispatch_kernel(ends_ref, dest_ref, x_ref, xs_hbm, zero_ref, sem, *, tm, tile_rows):
    n_tiles = xs_hbm.shape[0] // tile_rows

    def zero_tile(start):
        return pltpu.make_async_copy(zero_ref, xs_hbm.at[pl.ds(start, tile_rows)], sem)

    def group_tail(e):
        return zero_tile(pl.multiple_of(ends_ref[e] - tile_rows, tile_rows))

    @pl.when(pl.program_id(0) == 0)
    def _():
        zero_ref[...] = jnp.zeros_like(zero_ref)
        last_end = ends_ref[N_EXPERTS - 1]
        for e in range(N_EXPERTS):
            pl.when(ends_ref[e] >= tile_rows)(lambda e=e: group_tail(e).start())
        for j in range(n_tiles - N_EXPERTS, n_tiles):
            pl.when(j * tile_rows >= last_end)(lambda j=j: zero_tile(j * tile_rows).start())
        for e in range(N_EXPERTS):
            pl.when(ends_ref[e] >= tile_rows)(lambda e=e: group_tail(e).wait())
        for j in range(n_tiles - N_EXPERTS, n_tiles):
            pl.when(j * tile_rows >= last_end)(lambda j=j: zero_tile(j * tile_rows).wait())

    def row_copy(src_row, dst_row):
        return pltpu.make_async_copy(x_ref.at[pl.ds(src_row, 1)], xs_hbm.at[pl.ds(dst_row, 1)], sem)

    def issue(r, carry):
        row_copy(r, dest_ref[0, 0, r]).start()
        row_copy(r, dest_ref[0, 0, tm + r]).start()
        return carry

    lax.fori_loop(0, tm, issue, 0, unroll=DMA_ISSUE_UNROLL)
    pltpu.make_async_copy(x_ref, xs_hbm.at[pl.ds(0, tm)], sem).wait()
    pltpu.make_async_copy(x_ref, xs_hbm.at[pl.ds(0, tm)], sem).wait()


def _dispatch(group_ends, dest, x, n_rows, *, tm, tile_rows):
    n, d = x.shape
    return pl.pallas_call(
        functools.partial(_dispatch_kernel, tm=tm, tile_rows=tile_rows),
        grid_spec=pltpu.PrefetchScalarGridSpec(
            num_scalar_prefetch=1,
            grid=(n // tm,),
            in_specs=[pl.BlockSpec((1, 1, 2 * tm), lambda i, ends: (i, 0, 0), memory_space=pltpu.SMEM),
                      pl.BlockSpec((tm, d), lambda i, ends: (i, 0))],
            out_specs=pl.BlockSpec(memory_space=pl.ANY),
            scratch_shapes=[pltpu.VMEM((tile_rows, d), x.dtype), pltpu.SemaphoreType.DMA(())]),
        out_shape=jax.ShapeDtypeStruct((n_rows, d), x.dtype),
        compiler_params=pltpu.CompilerParams(dimension_semantics=("arbitrary",),
                                             has_side_effects=True),
        name="moe_dispatch",
    )(group_ends, dest, x)


def _experts_kernel(te_ref, x_ref, wg_ref, wu_ref, wd_ref, o_ref):
    valid = te_ref[pl.program_id(0)] < N_EXPERTS

    @pl.when(valid)
    def _():
        o_ref[...] = _swiglu_tile(x_ref[...].astype(BF16), wg_ref, wu_ref, wd_ref, lead=(0,))

    @pl.when(jnp.logical_not(valid))
    def _():
        o_ref[...] = jnp.zeros_like(o_ref)


def _experts(tile_expert, xs, wg, wu, wd, *, tile_rows):
    n_rows, d = xs.shape
    ne, _, f = wg.shape
    w_idx = lambda j, te: (jnp.minimum(te[j], ne - 1), 0, 0)
    return pl.pallas_call(
        _experts_kernel,
        grid_spec=pltpu.PrefetchScalarGridSpec(
            num_scalar_prefetch=1,
            grid=(n_rows // tile_rows,),
            in_specs=[pl.BlockSpec((tile_rows, d), lambda j, te: (j, 0)),
                      pl.BlockSpec((1, d, f), w_idx),
                      pl.BlockSpec((1, d, f), w_idx),
                      pl.BlockSpec((1, f, d), w_idx)],
            out_specs=pl.BlockSpec((tile_rows, d), lambda j, te: (j, 0))),
        out_shape=jax.ShapeDtypeStruct((n_rows, d), F32),
        compiler_params=_params(("arbitrary",)),
        name="moe_experts",
    )(tile_expert, xs, wg, wu, wd)


def _combine_kernel(dest_ref, routed_ref, ys_hbm, h1_ref, g2_ref, fng_ref, o_ref, ylo_ref, yhi_ref,
                    sem, *, tm):
    def issue(r, carry):
        pltpu.make_async_copy(ys_hbm.at[pl.ds(dest_ref[0, 0, r], 1)],
                              ylo_ref.at[pl.ds(r, 1)], sem).start()
        pltpu.make_async_copy(ys_hbm.at[pl.ds(dest_ref[0, 0, tm + r], 1)],
                              yhi_ref.at[pl.ds(r, 1)], sem).start()
        return carry

    lax.fori_loop(0, tm, issue, 0, unroll=DMA_ISSUE_UNROLL)
    pltpu.make_async_copy(ys_hbm.at[pl.ds(0, tm)], ylo_ref, sem).wait()
    pltpu.make_async_copy(ys_hbm.at[pl.ds(0, tm)], yhi_ref, sem).wait()
    routed = routed_ref[...]
    y = routed[:, 2:3] * ylo_ref[...] + routed[:, 3:4] * yhi_ref[...]
    out = h1_ref[...] + g2_ref[0] * y
    o_ref[...] = _rms(out) * fng_ref[...]


def _combine(dest, routed, ys, h1, g2, fng, *, tm, tiles_per_sample):
    n, d = h1.shape
    return pl.pallas_call(
        functools.partial(_combine_kernel, tm=tm),
        grid=(n // tm,),
        in_specs=[pl.BlockSpec((1, 1, 2 * tm), lambda i: (i, 0, 0), memory_space=pltpu.SMEM),
                  pl.BlockSpec((tm, LANES), lambda i: (i, 0)),
                  pl.BlockSpec(memory_space=pl.ANY),
                  pl.BlockSpec((tm, d), lambda i: (i, 0)),
                  pl.BlockSpec((1, 1, d), lambda i: (i // tiles_per_sample, 0, 0)),
                  pl.BlockSpec((1, d), lambda i: (0, 0))],
        out_specs=pl.BlockSpec((tm, d), lambda i: (i, 0)),
        out_shape=jax.ShapeDtypeStruct((n, d), F32),
        scratch_shapes=[pltpu.VMEM((tm, d), F32), pltpu.VMEM((tm, d), F32),
                        pltpu.SemaphoreType.DMA(())],
        compiler_params=_params(("arbitrary",)),
        name="moe_combine",
    )(dest, routed, ys, h1, g2, fng)


def _moe(a2, gates, sel, wg, wu, wd, h1, g2, fng, *, tm):
    b, l, d = h1.shape
    n = b * l
    n_rows = 2 * n + N_EXPERTS * MOE_TILE
    n_tiles = n_rows // MOE_TILE
    routed, tile_expert, group_ends = _route(sel.reshape(n, LANES), gates.reshape(n, LANES),
                                             tr=tm, tile_rows=MOE_TILE, n_tiles=n_tiles)
    dest = routed[:, 0:2].astype(jnp.int32).reshape(n // tm, tm, 2)
    dest = dest.transpose(0, 2, 1).reshape(n // tm, 1, 2 * tm)
    xs = _dispatch(group_ends[0, :N_EXPERTS], dest, a2.reshape(n, d), n_rows, tm=tm, tile_rows=MOE_TILE)
    ys = _experts(tile_expert[0, :n_tiles], xs, wg, wu, wd, tile_rows=MOE_TILE)
    out = _combine(dest, routed, ys, h1.reshape(n, d), g2, fng, tm=tm, tiles_per_sample=l // tm)
    return out.reshape(b, l, d)


def _rope_tables(seq_len, dim):
    t = jnp.arange(seq_len, dtype=jnp.int32)
    row = (t // GRID_W).astype(F32)
    col = (t % GRID_W).astype(F32)
    quarter = dim // 4
    inv = ROPE_BASE ** (-jnp.arange(quarter, dtype=F32) / quarter)
    ar = row[:, None] * inv[None, :]
    ac = col[:, None] * inv[None, :]
    cos = jnp.concatenate([jnp.cos(ar), jnp.cos(ar), jnp.cos(ac), jnp.cos(ac)], axis=-1)
    sin = jnp.concatenate([jnp.sin(ar), jnp.sin(ar), jnp.sin(ac), jnp.sin(ac)], axis=-1)
    reps = LANES // dim
    return jnp.tile(cos, (1, reps)), jnp.tile(sin, (1, reps))


def kernel(x, c, ctx, c_ctx, norm1_g, norm2_g, mod_w, mod_b, even_w_in, pool_w, pool_scale,
           lambda_q1, lambda_k1, lambda_q2, lambda_k2, even_w_out, ffn_w_gate, ffn_w_up,
           ffn_w_down, odd_w_in, q_norm_g, kv_norm_g, w_uq, w_ukv, odd_w_out, router_w,
           moe_w_gate, moe_w_up, moe_w_down, final_norm_g):
    b, l, d = x.shape
    n_ctx = ctx.shape[1]
    tm = min(ROW_TILE, l)
    tkv = min(KV_TILE, l)
    tq = min(DIFF_Q_TILE, l)
    cos, sin = _rope_tables(l, DIFF_HEAD_DIM)
    cos_c, sin_c = cos[:n_ctx], sin[:n_ctx]

    cond = jnp.zeros((8, d), F32).at[:b].set(c).at[b].set(c_ctx)
    mod = _modulation(cond, mod_w, mod_b)
    mod = mod.reshape(mod.shape[0], 8, N_MOD, d)

    def mod_vecs(layer):
        lat = [mod[layer, :b, k][:, None, :] for k in range(N_MOD)]
        cx = [mod[layer, b:b + 1, k][:, None, :] for k in range(N_MOD)]
        return lat, cx

    (sh1, sc1, g1, sh2, sc2, g2), (csh1, csc1, cg1, csh2, csc2, cg2) = mod_vecs(0)
    lam_init = 0.8 - 0.6 * math.exp(-0.3 * 0)
    w_in0 = even_w_in[0].astype(BF16)
    n1g = norm1_g[0][None, :]
    n2g = norm2_g[0][None, :]
    p_lat, q1t_lat, q2t_lat, k_lat, vt_lat = _pre0(x, n1g, sh1, sc1, w_in0, cos, sin,
                                                   use_rope=True, tm=tkv)
    p_ctx, q1t_ctx, q2t_ctx, k_ctx, vt_ctx = _pre0(ctx, n1g, csh1, csc1, w_in0, cos_c, sin_c,
                                                   use_rope=False, tm=n_ctx)
    lq = jnp.stack([lambda_q1[0], lambda_q2[0]])
    lk = jnp.stack([lambda_k1[0], lambda_k2[0]])
    diff_kern = functools.partial(_diff_attn_kernel, lam_init=lam_init)
    attn_lat = _attention(diff_kern, (q1t_lat, q2t_lat), (k_ctx, vt_ctx), (k_lat, vt_lat),
                          2 * DIFF_HEAD_DIM, extra=(lq, lk), tq=tq,
                          n_sub=min(DIFF_Q_SWEEPS, l // tq), name="diff_attn_lat")
    attn_ctx = _attention(diff_kern, (q1t_ctx, q2t_ctx), (k_ctx, vt_ctx), None,
                          2 * DIFF_HEAD_DIM, extra=(lq, lk), tq=n_ctx, n_sub=1, name="diff_attn_ctx")
    ng = len(POOL_WINDOWS)
    poolw_bd = jnp.zeros((POOL_WIDTH, POOL_WIDTH), F32)
    for g in range(ng):
        sl = slice(g * POOL_GROUP, (g + 1) * POOL_GROUP)
        poolw_bd = poolw_bd.at[sl, sl].set(pool_w[0, g])
    poolw_bd = poolw_bd.astype(BF16)
    pscale = pool_scale[0][None, :]
    w_out0 = even_w_out[0].astype(BF16)
    wg0, wu0, wd0 = ffn_w_gate[0].astype(BF16), ffn_w_up[0].astype(BF16), ffn_w_down[0].astype(BF16)
    h_lat = _post0(attn_lat, p_lat, poolw_bd, pscale, w_out0, x, g1, n2g, sh2, sc2, g2,
                   wg0, wu0, wd0, tm=tm)
    h_ctx = _post0(attn_ctx, p_ctx, poolw_bd, pscale, w_out0, ctx, cg1, n2g, csh2, csc2, cg2,
                   wg0, wu0, wd0, tm=n_ctx)

    (sh1, sc1, g1, sh2, sc2, g2), (csh1, csc1, _, _, _, _) = mod_vecs(1)
    n1g = norm1_g[1][None, :]
    n2g = norm2_g[1][None, :]
    q_rank, kv_rank = w_uq.shape[1], w_ukv.shape[1]
    n_heads = w_uq.shape[2] // (MLA_NOPE + MLA_ROPE)
    w_in1 = jnp.pad(odd_w_in[0], ((0, 0), (0, LANES - MLA_ROPE))).astype(BF16)
    wuq = w_uq[0].reshape(q_rank, n_heads, MLA_NOPE + MLA_ROPE)
    wuq = jnp.pad(wuq, ((0, 0), (0, 0), (0, MLA_QK_PAD - MLA_NOPE - MLA_ROPE)))
    wuq = wuq.reshape(q_rank, n_heads * MLA_QK_PAD).astype(BF16)
    wukv = w_ukv[0].reshape(kv_rank, n_heads, MLA_NOPE + MLA_V)
    wuk = wukv[:, :, :MLA_NOPE].reshape(kv_rank, n_heads * MLA_NOPE).astype(BF16)
    wuv = wukv[:, :, MLA_NOPE:].reshape(kv_rank, n_heads * MLA_V).astype(BF16)
    qg = q_norm_g[0][None, :]
    kvg = kv_norm_g[0][None, :]
    q1t, k1, vt1 = _pre1(h_lat, n1g, sh1, sc1, w_in1, qg, kvg, wuq, wuk, wuv, cos, sin,
                         use_rope=True, want_q=True, tm=tkv)
    k1c, vt1c = _pre1(h_ctx, n1g, csh1, csc1, w_in1, qg, kvg, wuq, wuk, wuv, cos_c, sin_c,
                      use_rope=False, want_q=False, tm=n_ctx)
    tq1 = min(MLA_Q_TILE, l)
    o1 = _attention(_mla_attn_kernel, (q1t,), (k1c, vt1c), (k1, vt1), MLA_QK_PAD,
                    tq=tq1, n_sub=min(MLA_Q_SWEEPS, l // tq1), name="mla_attn")
    rw_pad = jnp.pad(router_w[0], ((0, 0), (0, LANES - N_EXPERTS)))
    rw_hi = rw_pad.astype(BF16)
    rw_lo = (rw_pad - rw_hi.astype(F32)).astype(BF16)
    h1, a2, gates, sel = _post1(o1, odd_w_out[0].astype(BF16), h_lat, g1, n2g, sh2, sc2,
                                rw_hi, rw_lo, tm=tm)
    return _moe(a2, gates, sel, moe_w_gate[0].astype(BF16), moe_w_up[0].astype(BF16),
                moe_w_down[0].astype(BF16), h1, g2, final_norm_g[None, :], tm=tm)
```

```python
import functools
import math

import jax
import jax.numpy as jnp
from jax import lax
from jax.experimental import pallas as pl
from jax.experimental.pallas import tpu as pltpu

F32 = jnp.float32
BF16 = jnp.bfloat16

EPS = 1e-6
GRID_W = 64
ROPE_BASE = 10000.0
N_MOD = 6
POOL_WINDOWS = (2, 4, 8, 16)
POOL_GROUP = 64
POOL_WIDTH = POOL_GROUP * len(POOL_WINDOWS)
POOL_HALO = max(POOL_WINDOWS) // 2
DIFF_HEAD_DIM = 64
DIFF_V_DIM = 128
V_ONES_ROWS = 16
V_AUG = DIFF_V_DIM + V_ONES_ROWS
MLA_NOPE = 128
MLA_ROPE = 64
MLA_V = 128
MLA_QK_PAD = 256
N_EXPERTS = 8
ROPE_QUARTER = 16
LOG2E = math.log2(math.e)

LANES = 128
ROW_TILE = 512
KV_TILE = 512
DIFF_Q_TILE = 512
MLA_Q_TILE = 1024
DIFF_Q_SWEEPS = 8
MLA_Q_SWEEPS = 4
PAIR_UNROLL = 5
MOE_TILE = 512
DMA_ISSUE_UNROLL = 8
FFN_CHUNKS = (512, 512, 512, 512, 512, 256)
VMEM_LIMIT = 56 * 1024 * 1024


def _params(sem):
    return pltpu.CompilerParams(dimension_semantics=sem, vmem_limit_bytes=VMEM_LIMIT)


def _dot(a, b):
    return jnp.dot(a, b, preferred_element_type=F32)


def _rms(x):
    return x * lax.rsqrt(jnp.mean(x * x, axis=-1, keepdims=True) + EPS)


def _silu(x):
    return x * (1.0 / (1.0 + jnp.exp(-x)))


def _rope(x, cos, sin):
    lane = lax.broadcasted_iota(jnp.int32, x.shape, 1)
    even = (lane // ROPE_QUARTER) % 2 == 0
    rot = jnp.where(even, -pltpu.roll(x, LANES - ROPE_QUARTER, 1), pltpu.roll(x, ROPE_QUARTER, 1))
    return x * cos + rot * sin


def _store_vt(vt_ref, v, n_heads):
    tm = v.shape[0]
    ones = jnp.ones((V_ONES_ROWS, tm), BF16)
    for h in range(n_heads):
        r0 = h * V_AUG
        vt_ref[0, 0, r0:r0 + DIFF_V_DIM, :] = v[:, h * DIFF_V_DIM:(h + 1) * DIFF_V_DIM].T.astype(BF16)
        vt_ref[0, 0, r0 + DIFF_V_DIM:r0 + V_AUG, :] = ones


def _mod_kernel(cond_ref, w_ref, b_ref, o_ref):
    s = _silu(cond_ref[...])
    o_ref[0] = jnp.dot(s, w_ref[0], preferred_element_type=F32,
                       precision=lax.Precision.HIGHEST) + b_ref[0]


def _modulation(cond, mod_w, mod_b):
    depth, d, n = mod_w.shape
    tn = n // 4
    return pl.pallas_call(
        _mod_kernel,
        grid=(depth, n // tn),
        in_specs=[pl.BlockSpec((8, d), lambda i, j: (0, 0)),
                  pl.BlockSpec((1, d, tn), lambda i, j: (i, 0, j)),
                  pl.BlockSpec((1, 1, tn), lambda i, j: (i, 0, j))],
        out_specs=pl.BlockSpec((1, 8, tn), lambda i, j: (i, 0, j)),
        out_shape=jax.ShapeDtypeStruct((depth, 8, n), F32),
        compiler_params=_params(("arbitrary", "arbitrary")),
        name="modulation",
    )(cond, mod_w, mod_b.reshape(depth, 1, n))


def _bvec_spec(arr, d):
    if arr.shape[0] == 1:
        return pl.BlockSpec((1, 1, d), lambda b, i: (0, 0, 0))
    return pl.BlockSpec((1, 1, d), lambda b, i: (b, 0, 0))


def _norm_mod(h, g, sh, sc):
    a = _rms(h) * g
    return a * (1.0 + sc) + sh


def _pre0_kernel(h_ref, g_ref, sh_ref, sc_ref, w_ref, cos_ref, sin_ref,
                 p_ref, q1t_ref, q2t_ref, k_ref, vt_ref, *, use_rope, n_heads):
    a = _norm_mod(h_ref[0], g_ref[...], sh_ref[0], sc_ref[0])
    z = _dot(a.astype(BF16), w_ref[...])
    p_ref[0] = z[:, :POOL_WIDTH]
    head_w = 2 * DIFF_HEAD_DIM
    qk_w = n_heads * head_w
    q_scale = DIFF_HEAD_DIM ** -0.5 * LOG2E
    cos = cos_ref[...]
    sin = sin_ref[...]
    for h in range(n_heads):
        xq = z[:, POOL_WIDTH + h * head_w:POOL_WIDTH + (h + 1) * head_w]
        xk = z[:, POOL_WIDTH + qk_w + h * head_w:POOL_WIDTH + qk_w + (h + 1) * head_w]
        if use_rope:
            xq = _rope(xq, cos, sin)
            xk = _rope(xk, cos, sin)
        xqt = (xq * q_scale).T
        row = lax.broadcasted_iota(jnp.int32, xqt.shape, 0)
        q1t_ref[0, 0, h * head_w:(h + 1) * head_w, :] = jnp.where(row < DIFF_HEAD_DIM, xqt, 0.0).astype(BF16)
        q2t_ref[0, 0, h * head_w:(h + 1) * head_w, :] = jnp.where(row >= DIFF_HEAD_DIM, xqt, 0.0).astype(BF16)
        k_ref[0, :, h * head_w:(h + 1) * head_w] = xk.astype(BF16)
    _store_vt(vt_ref, z[:, POOL_WIDTH + 2 * qk_w:], n_heads)


def _pre0(h, g, sh, sc, w_in, cos, sin, *, use_rope, tm):
    b, l, d = h.shape
    n_in = w_in.shape[1]
    v_w = (n_in - POOL_WIDTH) // 3
    n_heads = v_w // DIFF_V_DIM
    kern = functools.partial(_pre0_kernel, use_rope=use_rope, n_heads=n_heads)
    return pl.pallas_call(
        kern,
        grid=(b, l // tm),
        in_specs=[pl.BlockSpec((1, tm, d), lambda b_, i: (b_, i, 0)),
                  pl.BlockSpec((1, d), lambda b_, i: (0, 0)),
                  _bvec_spec(sh, d), _bvec_spec(sc, d),
                  pl.BlockSpec((d, n_in), lambda b_, i: (0, 0)),
                  pl.BlockSpec((tm, LANES), lambda b_, i: (i, 0)),
                  pl.BlockSpec((tm, LANES), lambda b_, i: (i, 0))],
        out_specs=[pl.BlockSpec((1, tm, POOL_WIDTH), lambda b_, i: (b_, i, 0)),
                   pl.BlockSpec((1, 1, v_w, tm), lambda b_, i: (b_, i, 0, 0)),
                   pl.BlockSpec((1, 1, v_w, tm), lambda b_, i: (b_, i, 0, 0)),
                   pl.BlockSpec((1, tm, v_w), lambda b_, i: (b_, i, 0)),
                   pl.BlockSpec((1, 1, n_heads * V_AUG, tm), lambda b_, i: (b_, i, 0, 0))],
        out_shape=[jax.ShapeDtypeStruct((b, l, POOL_WIDTH), F32),
                   jax.ShapeDtypeStruct((b, l // tm, v_w, tm), BF16),
                   jax.ShapeDtypeStruct((b, l // tm, v_w, tm), BF16),
                   jax.ShapeDtypeStruct((b, l, v_w), BF16),
                   jax.ShapeDtypeStruct((b, l // tm, n_heads * V_AUG, tm), BF16)],
        compiler_params=_params(("parallel", "parallel")),
        name="pre0_rope" if use_rope else "pre0_ctx",
    )(h, g, sh, sc, w_in, cos, sin)


def _attn_core(qt_refs, j, tq, kc_ref, vc_ref, lat_refs, acc_refs, sbuf_refs):
    n = len(qt_refs)
    sub = tq // qt_refs[0].shape[3]
    qts = [jnp.concatenate([r[0, j * sub + u] for u in range(sub)], axis=1) for r in qt_refs]
    ctx_bufs = sbuf_refs[:n]
    lat_bufs = sbuf_refs[n:]
    ms = tuple(jnp.full((1, tq), -jnp.inf, F32) for _ in range(n))
    for i in range(n):
        acc_refs[i][...] = jnp.zeros_like(acc_refs[i])

    def scores(k, bufs):
        cmax = []
        for i in range(n):
            s = _dot(k, qts[i])
            bufs[i][...] = s
            cmax.append(jnp.max(s, axis=0, keepdims=True))
        return tuple(cmax)

    def softmax_pv(vt, bufs, cmax, ms):
        new_m = []
        for i in range(n):
            m_new = jnp.maximum(ms[i], cmax[i])
            alpha = jnp.exp2(ms[i] - m_new)
            p = jnp.exp2(bufs[i][...] - m_new)
            acc_refs[i][...] = alpha * acc_refs[i][...] + _dot(vt, p.astype(BF16))
            new_m.append(m_new)
        return tuple(new_m)

    if lat_refs is None:
        softmax_pv(vc_ref[0, 0], ctx_bufs, scores(kc_ref[0], ctx_bufs), ms)
        return
    kl_ref, vl_ref = lat_refs
    nblk, tk = vl_ref.shape[1], vl_ref.shape[3]
    assert nblk >= 2 and nblk % 2 == 0

    def qk(t, which):
        start = t * tk if isinstance(t, int) else pl.multiple_of(t * tk, tk)
        return scores(kl_ref[0, pl.ds(start, tk), :], lat_bufs[which::2])

    def pair(jj, carry):
        cmax_a, ms = carry
        t = 2 * jj
        cmax_b = qk(t + 1, 1)
        ms = softmax_pv(vl_ref[0, t], lat_bufs[0::2], cmax_a, ms)
        cmax_a = qk(t + 2, 0)
        ms = softmax_pv(vl_ref[0, t + 1], lat_bufs[1::2], cmax_b, ms)
        return cmax_a, ms

    cmax_a = qk(0, 0)
    n_pairs = nblk // 2 - 1
    cmax_a, ms = lax.fori_loop(0, n_pairs, pair, (cmax_a, ms),
                               unroll=max(1, min(PAIR_UNROLL, n_pairs)))
    cmax_b = qk(nblk - 1, 1)
    ms = softmax_pv(vl_ref[0, nblk - 2], lat_bufs[0::2], cmax_a, ms)
    cmax_c = scores(kc_ref[0], ctx_bufs)
    ms = softmax_pv(vl_ref[0, nblk - 1], lat_bufs[1::2], cmax_b, ms)
    softmax_pv(vc_ref[0, 0], ctx_bufs, cmax_c, ms)


def _normalised(acc_ref):
    acc = acc_ref[...]
    return acc[:DIFF_V_DIM] * (1.0 / acc[DIFF_V_DIM:DIFF_V_DIM + 1])


def _for_each_query_tile(o_ref, tq, tile_fn):
    n_tiles = o_ref.shape[1] // tq

    def body(j, carry):
        rows = pl.ds(pl.multiple_of(j * tq, tq), tq)
        o_ref[0, rows, :] = tile_fn(j)
        return carry

    if n_tiles == 1:
        o_ref[0] = tile_fn(0)
    else:
        lax.fori_loop(0, n_tiles, body, 0)


def _diff_attn_kernel(lq_ref, lk_ref, q1t_ref, q2t_ref, kc_ref, vc_ref, *rest, has_lat, lam_init, tq):
    lat_refs = rest[:2] if has_lat else None
    rest = rest[2:] if has_lat else rest
    o_ref, acc1_ref, acc2_ref = rest[:3]
    e = jnp.exp(jnp.sum(lq_ref[...] * lk_ref[...], axis=-1, keepdims=True))
    lam = e[0:1] - e[1:2] + lam_init

    def tile(j):
        _attn_core((q1t_ref, q2t_ref), j, tq, kc_ref, vc_ref, lat_refs, (acc1_ref, acc2_ref), rest[3:])
        o = _normalised(acc1_ref) - lam * _normalised(acc2_ref)
        ms = jnp.mean(o * o, axis=0, keepdims=True)
        y = o * (lax.rsqrt(ms + EPS) * (1.0 - lam_init))
        return y.T.astype(BF16)

    _for_each_query_tile(o_ref, tq, tile)


def _mla_attn_kernel(qt_ref, kc_ref, vc_ref, *rest, has_lat, tq):
    lat_refs = rest[:2] if has_lat else None
    rest = rest[2:] if has_lat else rest
    o_ref, acc_ref = rest[:2]

    def tile(j):
        _attn_core((qt_ref,), j, tq, kc_ref, vc_ref, lat_refs, (acc_ref,), rest[2:])
        return _normalised(acc_ref).T.astype(BF16)

    _for_each_query_tile(o_ref, tq, tile)


def _attention(kern, qts, ctx_kv, lat_kv, qk_w, extra=(), *, tq, n_sub, name):
    n_maps = len(qts)
    b, nqb, hw, tmq = qts[0].shape
    n_heads = hw // qk_w
    lq = nqb * tmq
    rows = n_sub * tq
    in_specs = [pl.BlockSpec(x.shape, lambda b_, h, i: (0, 0)) for x in extra]
    in_specs += [pl.BlockSpec((1, rows // tmq, qk_w, tmq), lambda b_, h, i: (b_, i, h, 0)) for _ in qts]
    args = list(extra) + list(qts)
    scratch = [pltpu.VMEM((V_AUG, tq), F32) for _ in range(n_maps)]
    scratch += [pltpu.VMEM((ctx_kv[0].shape[1], tq), F32) for _ in range(n_maps)]
    for kv in (ctx_kv, lat_kv):
        if kv is None:
            continue
        k, vt = kv
        nblk, tk = vt.shape[1], vt.shape[3]
        in_specs.append(pl.BlockSpec((1, k.shape[1], qk_w), lambda b_, h, i: (b_, 0, h)))
        in_specs.append(pl.BlockSpec((1, nblk, V_AUG, tk), lambda b_, h, i: (b_, 0, h, 0)))
        args += [k, vt]
    if lat_kv is not None:
        scratch += [pltpu.VMEM((lat_kv[1].shape[3], tq), F32) for _ in range(2 * n_maps)]
    return pl.pallas_call(
        functools.partial(kern, has_lat=lat_kv is not None, tq=tq),
        grid=(b, n_heads, lq // rows),
        in_specs=in_specs,
        out_specs=pl.BlockSpec((1, rows, DIFF_V_DIM), lambda b_, h, i: (b_, i, h)),
        out_shape=jax.ShapeDtypeStruct((b, lq, n_heads * DIFF_V_DIM), BF16),
        scratch_shapes=scratch,
        compiler_params=_params(("parallel", "parallel", "arbitrary")),
        name=name,
    )(*args)


def _post0_kernel(attn_ref, p_ref, pprev_ref, pnext_ref, poolw_ref, pscale_ref, wout_ref,
                  h_ref, g1_ref, n2g_ref, sh2_ref, sc2_ref, g2_ref, wg_ref, wu_ref, wd_ref,
                  o_ref, *, seq_len):
    i = pl.program_id(1)
    u = p_ref[0]
    tm = u.shape[0]
    ext = jnp.concatenate([pprev_ref[0], u, pnext_ref[0]], axis=0)
    row = i * tm - POOL_HALO + lax.broadcasted_iota(jnp.int32, (tm + 2 * POOL_HALO, 1), 0)
    ext = jnp.where((row >= 0) & (row < seq_len), ext, 0.0)

    def shifted(j):
        return ext[POOL_HALO + j:POOL_HALO + j + tm]

    t = i * tm + lax.broadcasted_iota(jnp.int32, (tm, 1), 0)
    lane = lax.broadcasted_iota(jnp.int32, (tm, POOL_WIDTH), 1)
    win_sum = shifted(-1) + shifted(0)
    pooled = jnp.zeros_like(u)
    for g, w in enumerate(POOL_WINDOWS):
        half = w // 2
        if g > 0:
            prev_half = POOL_WINDOWS[g - 1] // 2
            for j in range(prev_half, half):
                win_sum = win_sum + shifted(-j - 1) + shifted(j)
        cnt = (jnp.minimum(t + half, seq_len) - jnp.maximum(t - half, 0)).astype(F32)
        in_group = (lane >= g * POOL_GROUP) & (lane < (g + 1) * POOL_GROUP)
        pooled = jnp.where(in_group, win_sum / cnt, pooled)
    pooled = pooled - u
    y_pool = _dot(pooled.astype(BF16), poolw_ref[...]) * pscale_ref[...]
    y = _dot(y_pool.astype(BF16), wout_ref[:POOL_WIDTH, :]) + _dot(attn_ref[0], wout_ref[POOL_WIDTH:, :])
    h1 = h_ref[0] + g1_ref[0] * y
    a2 = _norm_mod(h1, n2g_ref[...], sh2_ref[0], sc2_ref[0]).astype(BF16)
    o_ref[0] = h1 + g2_ref[0] * _swiglu_tile(a2, wg_ref, wu_ref, wd_ref)


def _post0(attn, p, poolw_bd, pscale, w_out, h, g1, n2g, sh2, sc2, g2, wg, wu, wd, *, tm):
    b, l, d = h.shape
    hb = tm // POOL_HALO
    nhb = l // POOL_HALO
    assert sum(FFN_CHUNKS) == wg.shape[1]
    return pl.pallas_call(
        functools.partial(_post0_kernel, seq_len=l),
        grid=(b, l // tm),
        in_specs=[pl.BlockSpec((1, tm, attn.shape[2]), lambda b_, i: (b_, i, 0)),
                  pl.BlockSpec((1, tm, POOL_WIDTH), lambda b_, i: (b_, i, 0)),
                  pl.BlockSpec((1, POOL_HALO, POOL_WIDTH),
                               lambda b_, i: (b_, jnp.maximum(i * hb - 1, 0), 0)),
                  pl.BlockSpec((1, POOL_HALO, POOL_WIDTH),
                               lambda b_, i: (b_, jnp.minimum((i + 1) * hb, nhb - 1), 0)),
                  pl.BlockSpec(poolw_bd.shape, lambda b_, i: (0, 0)),
                  pl.BlockSpec((1, POOL_WIDTH), lambda b_, i: (0, 0)),
                  pl.BlockSpec(w_out.shape, lambda b_, i: (0, 0)),
                  pl.BlockSpec((1, tm, d), lambda b_, i: (b_, i, 0)),
                  _bvec_spec(g1, d),
                  pl.BlockSpec((1, d), lambda b_, i: (0, 0)),
                  _bvec_spec(sh2, d), _bvec_spec(sc2, d), _bvec_spec(g2, d),
                  pl.BlockSpec(wg.shape, lambda b_, i: (0, 0)),
                  pl.BlockSpec(wu.shape, lambda b_, i: (0, 0)),
                  pl.BlockSpec(wd.shape, lambda b_, i: (0, 0))],
        out_specs=pl.BlockSpec((1, tm, d), lambda b_, i: (b_, i, 0)),
        out_shape=jax.ShapeDtypeStruct((b, l, d), F32),
        compiler_params=_params(("parallel", "parallel")),
        name="post0_ffn",
    )(attn, p, p, p, poolw_bd, pscale, w_out, h, g1, n2g, sh2, sc2, g2, wg, wu, wd)


def _swiglu_tile(x, wg_ref, wu_ref, wd_ref, lead=()):
    acc = None
    c0 = 0
    for cw in FFN_CHUNKS:
        hg = _dot(x, wg_ref[lead + (slice(None), slice(c0, c0 + cw))])
        hu = _dot(x, wu_ref[lead + (slice(None), slice(c0, c0 + cw))])
        act = (_silu(hg) * hu).astype(BF16)
        part = _dot(act, wd_ref[lead + (slice(c0, c0 + cw), slice(None))])
        acc = part if acc is None else acc + part
        c0 += cw
    return acc


def _pre1_kernel(h_ref, g_ref, sh_ref, sc_ref, win_ref, qg_ref, kvg_ref, wuq_ref, wuk_ref, wuv_ref,
                 cos_ref, sin_ref, *out_refs, use_rope, want_q, q_rank, kv_rank, n_heads):
    a = _norm_mod(h_ref[0], g_ref[...], sh_ref[0], sc_ref[0])
    z = _dot(a.astype(BF16), win_ref[...])
    cos = cos_ref[...]
    sin = sin_ref[...]
    if want_q:
        qt_ref, k_ref, vt_ref = out_refs
        cq = _rms(z[:, :q_rank]) * qg_ref[...]
        qf = _dot(cq.astype(BF16), wuq_ref[...])
        q_scale = (MLA_NOPE + MLA_ROPE) ** -0.5 * LOG2E
        for h in range(n_heads):
            c0 = h * MLA_QK_PAD
            qn = qf[:, c0:c0 + MLA_NOPE] * q_scale
            qr = _rope(qf[:, c0 + MLA_NOPE:c0 + MLA_QK_PAD], cos, sin) * q_scale
            qt_ref[0, 0, c0:c0 + MLA_NOPE, :] = qn.T.astype(BF16)
            qt_ref[0, 0, c0 + MLA_NOPE:c0 + MLA_QK_PAD, :] = qr.T.astype(BF16)
    else:
        k_ref, vt_ref = out_refs
    ckv = (_rms(z[:, q_rank:q_rank + kv_rank]) * kvg_ref[...]).astype(BF16)
    kn = _dot(ckv, wuk_ref[...])
    vv = _dot(ckv, wuv_ref[...])
    kr = z[:, q_rank + kv_rank:]
    if use_rope:
        kr = _rope(kr, cos, sin)
    kr = kr.astype(BF16)
    for h in range(n_heads):
        c0 = h * MLA_QK_PAD
        k_ref[0, :, c0:c0 + MLA_NOPE] = kn[:, h * MLA_NOPE:(h + 1) * MLA_NOPE].astype(BF16)
        k_ref[0, :, c0 + MLA_NOPE:c0 + MLA_QK_PAD] = kr
    _store_vt(vt_ref, vv, n_heads)


def _pre1(h, g, sh, sc, w_in, qg, kvg, wuq, wuk, wuv, cos, sin, *, use_rope, want_q, tm):
    b, l, d = h.shape
    n_heads = wuk.shape[1] // MLA_NOPE
    q_rank, kv_rank = wuq.shape[0], wuk.shape[0]
    kern = functools.partial(_pre1_kernel, use_rope=use_rope, want_q=want_q,
                             q_rank=q_rank, kv_rank=kv_rank, n_heads=n_heads)
    full = lambda x: pl.BlockSpec(x.shape, lambda b_, i: (0,) * x.ndim)
    qk_w = n_heads * MLA_QK_PAD
    v_rows = n_heads * V_AUG
    out_specs = [pl.BlockSpec((1, tm, qk_w), lambda b_, i: (b_, i, 0)),
                 pl.BlockSpec((1, 1, v_rows, tm), lambda b_, i: (b_, i, 0, 0))]
    out_shape = [jax.ShapeDtypeStruct((b, l, qk_w), BF16),
                 jax.ShapeDtypeStruct((b, l // tm, v_rows, tm), BF16)]
    if want_q:
        out_specs = [pl.BlockSpec((1, 1, qk_w, tm), lambda b_, i: (b_, i, 0, 0))] + out_specs
        out_shape = [jax.ShapeDtypeStruct((b, l // tm, qk_w, tm), BF16)] + out_shape
    return pl.pallas_call(
        kern,
        grid=(b, l // tm),
        in_specs=[pl.BlockSpec((1, tm, d), lambda b_, i: (b_, i, 0)),
                  pl.BlockSpec((1, d), lambda b_, i: (0, 0)),
                  _bvec_spec(sh, d), _bvec_spec(sc, d),
                  full(w_in), full(qg), full(kvg), full(wuq), full(wuk), full(wuv),
                  pl.BlockSpec((tm, LANES), lambda b_, i: (i, 0)),
                  pl.BlockSpec((tm, LANES), lambda b_, i: (i, 0))],
        out_specs=out_specs,
        out_shape=out_shape,
        compiler_params=_params(("parallel", "parallel")),
        name="pre1_lat" if want_q else "pre1_ctx",
    )(h, g, sh, sc, w_in, qg, kvg, wuq, wuk, wuv, cos, sin)


def _post1_kernel(o_ref, wout_ref, h_ref, g1_ref, n2g_ref, sh2_ref, sc2_ref, rw_ref,
                  h1_ref, a2_ref, gates_ref, sel_ref):
    y = _dot(o_ref[0], wout_ref[...])
    h1 = h_ref[0] + g1_ref[0] * y
    h1_ref[0] = h1
    a2 = _norm_mod(h1, n2g_ref[...], sh2_ref[0], sc2_ref[0])
    a2_ref[0] = a2
    a_hi = a2.astype(BF16)
    a_lo = (a2 - a_hi.astype(F32)).astype(BF16)
    hi_both = _dot(a_hi, rw_ref[...])
    logits = hi_both[:, :LANES] + (_dot(a_lo, rw_ref[:, :LANES]) + hi_both[:, LANES:])
    lane = lax.broadcasted_iota(jnp.int32, logits.shape, 1)
    lg = jnp.where(lane < N_EXPERTS, logits, -jnp.inf)
    m1 = jnp.max(lg, axis=1, keepdims=True)
    i1 = jnp.min(jnp.where(lg == m1, lane, LANES), axis=1, keepdims=True)
    lg2 = jnp.where(lane == i1, -jnp.inf, lg)
    m2 = jnp.max(lg2, axis=1, keepdims=True)
    i2 = jnp.min(jnp.where(lg2 == m2, lane, LANES), axis=1, keepdims=True)
    p2 = jnp.exp(m2 - m1)
    w1 = 1.0 / (1.0 + p2)
    gates_ref[0] = jnp.where(lane == i1, w1, 0.0) + jnp.where(lane == i2, p2 * w1, 0.0)
    sel_ref[0] = jnp.where((lane == i1) | (lane == i2), 1.0, 0.0).astype(BF16)


def _post1(o, w_out, h, g1, n2g, sh2, sc2, rw_split, *, tm):
    b, l, d = h.shape
    return pl.pallas_call(
        _post1_kernel,
        grid=(b, l // tm),
        in_specs=[pl.BlockSpec((1, tm, o.shape[2]), lambda b_, i: (b_, i, 0)),
                  pl.BlockSpec(w_out.shape, lambda b_, i: (0, 0)),
                  pl.BlockSpec((1, tm, d), lambda b_, i: (b_, i, 0)),
                  _bvec_spec(g1, d),
                  pl.BlockSpec((1, d), lambda b_, i: (0, 0)),
                  _bvec_spec(sh2, d), _bvec_spec(sc2, d),
                  pl.BlockSpec(rw_split.shape, lambda b_, i: (0, 0))],
        out_specs=[pl.BlockSpec((1, tm, d), lambda b_, i: (b_, i, 0)),
                   pl.BlockSpec((1, tm, d), lambda b_, i: (b_, i, 0)),
                   pl.BlockSpec((1, tm, LANES), lambda b_, i: (b_, i, 0)),
                   pl.BlockSpec((1, tm, LANES), lambda b_, i: (b_, i, 0))],
        out_shape=[jax.ShapeDtypeStruct((b, l, d), F32),
                   jax.ShapeDtypeStruct((b, l, d), F32),
                   jax.ShapeDtypeStruct((b, l, LANES), F32),
                   jax.ShapeDtypeStruct((b, l, LANES), BF16)],
        compiler_params=_params(("parallel", "parallel")),
        name="post1_router",
    )(o, w_out, h, g1, n2g, sh2, sc2, rw_split)


def _route_kernel(sel_ref, gates_ref, r_ref, tile_ref, ends_ref, cnt_ref, off_ref, *, tile_rows):
    phase = pl.program_id(0)
    i = pl.program_id(1)
    sel = sel_ref[...]
    tr = sel.shape[0]
    col_count = jnp.sum(sel.astype(F32), axis=0, keepdims=True)
    lane = lax.broadcasted_iota(jnp.int32, (1, LANES), 1)

    @pl.when((phase == 0) & (i == 0))
    def _():
        cnt_ref[...] = jnp.zeros_like(cnt_ref)

    @pl.when(phase == 0)
    def _():
        cnt_ref[...] += col_count

    @pl.when((phase == 0) & (i == pl.num_programs(1) - 1))
    def _():
        padded = jnp.ceil(cnt_ref[...] / tile_rows) * tile_rows
        incl = padded
        for sh in (1, 2, 4):
            incl = incl + jnp.where(lane >= sh, pltpu.roll(incl, sh, 1), 0.0)
        off_ref[...] = incl - padded
        ends_ref[...] = incl.astype(jnp.int32)
        tile_start = (lax.broadcasted_iota(jnp.int32, tile_ref.shape, 1) * tile_rows).astype(F32)
        tile_expert = jnp.zeros(tile_ref.shape, jnp.int32)
        for e in range(N_EXPERTS):
            end_e = jnp.sum(jnp.where(lane == e, incl, 0.0), axis=1, keepdims=True)
            tile_expert = tile_expert + (tile_start >= end_e).astype(jnp.int32)
        tile_ref[...] = tile_expert
        cnt_ref[...] = jnp.zeros_like(cnt_ref)

    @pl.when(phase == 1)
    def _():
        rr = lax.broadcasted_iota(jnp.int32, (tr, tr), 0)
        cc = lax.broadcasted_iota(jnp.int32, (tr, tr), 1)
        earlier = jnp.where(cc < rr, 1.0, 0.0).astype(BF16)
        rank = _dot(earlier, sel) + cnt_ref[...]
        cnt_ref[...] += col_count
        pos = off_ref[...] + rank
        chosen = sel > 0
        lo = jnp.min(jnp.where(chosen, pos, 3e38), axis=1, keepdims=True)
        hi = jnp.max(jnp.where(chosen, pos, -1.0), axis=1, keepdims=True)
        g = gates_ref[...]
        w_lo = jnp.sum(jnp.where(chosen & (pos == lo), g, 0.0), axis=1, keepdims=True)
        w_hi = jnp.sum(jnp.where(chosen & (pos == hi), g, 0.0), axis=1, keepdims=True)
        lane_t = lax.broadcasted_iota(jnp.int32, (tr, LANES), 1)
        r_ref[...] = jnp.where(lane_t == 0, lo, jnp.where(lane_t == 1, hi,
                               jnp.where(lane_t == 2, w_lo, jnp.where(lane_t == 3, w_hi, 0.0))))


def _route(sel, gates, *, tr, tile_rows, n_tiles):
    n = sel.shape[0]
    n_tiles_pad = -(-n_tiles // LANES) * LANES
    return pl.pallas_call(
        functools.partial(_route_kernel, tile_rows=tile_rows),
        grid=(2, n // tr),
        in_specs=[pl.BlockSpec((tr, LANES), lambda ph, i: (i, 0)),
                  pl.BlockSpec((tr, LANES), lambda ph, i: (i, 0))],
        out_specs=[pl.BlockSpec((tr, LANES), lambda ph, i: (i * ph, 0)),
                   pl.BlockSpec((1, n_tiles_pad), lambda ph, i: (0, 0)),
                   pl.BlockSpec((1, LANES), lambda ph, i: (0, 0))],
        out_shape=[jax.ShapeDtypeStruct((n, LANES), F32),
                   jax.ShapeDtypeStruct((1, n_tiles_pad), jnp.int32),
                   jax.ShapeDtypeStruct((1, LANES), jnp.int32)],
        scratch_shapes=[pltpu.VMEM((1, LANES), F32), pltpu.VMEM((1, LANES), F32)],
        compiler_params=_params(("arbitrary", "arbitrary")),
        name="moe_route",
    )(sel, gates)


def _dispatch_kernel(ends_ref, dest_ref, x_ref, xs_hbm, zero_ref, sem, *, tm, tile_rows):
    n_tiles = xs_hbm.shape[0] // tile_rows

    def zero_tile(start):
        return pltpu.make_async_copy(zero_ref, xs_hbm.at[pl.ds(start, tile_rows)], sem)

    def group_tail(e):
        return zero_tile(pl.multiple_of(ends_ref[e] - tile_rows, tile_rows))

    @pl.when(pl.program_id(0) == 0)
    def _():
        zero_ref[...] = jnp.zeros_like(zero_ref)
        last_end = ends_ref[N_EXPERTS - 1]
        for e in range(N_EXPERTS):
            pl.when(ends_ref[e] >= tile_rows)(lambda e=e: group_tail(e).start())
        for j in range(n_tiles - N_EXPERTS, n_tiles):
            pl.when(j * tile_rows >= last_end)(lambda j=j: zero_tile(j * tile_rows).start())
        for e in range(N_EXPERTS):
            pl.when(ends_ref[e] >= tile_rows)(lambda e=e: group_tail(e).wait())
        for j in range(n_tiles - N_EXPERTS, n_tiles):
            pl.when(j * tile_rows >= last_end)(lambda j=j: zero_tile(j * tile_rows).wait())

    def row_copy(src_row, dst_row):
        return pltpu.make_async_copy(x_ref.at[pl.ds(src_row, 1)], xs_hbm.at[pl.ds(dst_row, 1)], sem)

    def issue(r, carry):
        row_copy(r, dest_ref[0, 0, r]).start()
        row_copy(r, dest_ref[0, 0, tm + r]).start()
        return carry

    lax.fori_loop(0, tm, issue, 0, unroll=DMA_ISSUE_UNROLL)
    pltpu.make_async_copy(x_ref, xs_hbm.at[pl.ds(0, tm)], sem).wait()
    pltpu.make_async_copy(x_ref, xs_hbm.at[pl.ds(0, tm)], sem).wait()


def _dispatch(group_ends, dest, x, n_rows, *, tm, tile_rows):
    n, d = x.shape
    return pl.pallas_call(
        functools.partial(_dispatch_kernel, tm=tm, tile_rows=tile_rows),
        grid_spec=pltpu.PrefetchScalarGridSpec(
            num_scalar_prefetch=1,
            grid=(n // tm,),
            in_specs=[pl.BlockSpec((1, 1, 2 * tm), lambda i, ends: (i, 0, 0), memory_space=pltpu.SMEM),
                      pl.BlockSpec((tm, d), lambda i, ends: (i, 0))],
            out_specs=pl.BlockSpec(memory_space=pl.ANY),
            scratch_shapes=[pltpu.VMEM((tile_rows, d), x.dtype), pltpu.SemaphoreType.DMA(())]),
        out_shape=jax.ShapeDtypeStruct((n_rows, d), x.dtype),
        compiler_params=pltpu.CompilerParams(dimension_semantics=("arbitrary",),
                                             has_side_effects=True),
        name="moe_dispatch",
    )(group_ends, dest, x)


def _experts_kernel(te_ref, x_ref, wg_ref, wu_ref, wd_ref, o_ref):
    valid = te_ref[pl.program_id(0)] < N_EXPERTS

    @pl.when(valid)
    def _():
        o_ref[...] = _swiglu_tile(x_ref[...].astype(BF16), wg_ref, wu_ref, wd_ref, lead=(0,))

    @pl.when(jnp.logical_not(valid))
    def _():
        o_ref[...] = jnp.zeros_like(o_ref)


def _experts(tile_expert, xs, wg, wu, wd, *, tile_rows):
    n_rows, d = xs.shape
    ne, _, f = wg.shape
    w_idx = lambda j, te: (jnp.minimum(te[j], ne - 1), 0, 0)
    return pl.pallas_call(
        _experts_kernel,
        grid_spec=pltpu.PrefetchScalarGridSpec(
            num_scalar_prefetch=1,
            grid=(n_rows // tile_rows,),
            in_specs=[pl.BlockSpec((tile_rows, d), lambda j, te: (j, 0)),
                      pl.BlockSpec((1, d, f), w_idx),
                      pl.BlockSpec((1, d, f), w_idx),
                      pl.BlockSpec((1, f, d), w_idx)],
            out_specs=pl.BlockSpec((tile_rows, d), lambda j, te: (j, 0))),
        out_shape=jax.ShapeDtypeStruct((n_rows, d), F32),
        compiler_params=_params(("arbitrary",)),
        name="moe_experts",
    )(tile_expert, xs, wg, wu, wd)


def _combine_kernel(dest_ref, routed_ref, ys_hbm, h1_ref, g2_ref, fng_ref, o_ref, ylo_ref, yhi_ref,
                    sem, *, tm):
    def issue(r, carry):
        pltpu.make_async_copy(ys_hbm.at[pl.ds(dest_ref[0, 0, r], 1)],
                              ylo_ref.at[pl.ds(r, 1)], sem).start()
        pltpu.make_async_copy(ys_hbm.at[pl.ds(dest_ref[0, 0, tm + r], 1)],
                              yhi_ref.at[pl.ds(r, 1)], sem).start()
        return carry

    lax.fori_loop(0, tm, issue, 0, unroll=DMA_ISSUE_UNROLL)
    pltpu.make_async_copy(ys_hbm.at[pl.ds(0, tm)], ylo_ref, sem).wait()
    pltpu.make_async_copy(ys_hbm.at[pl.ds(0, tm)], yhi_ref, sem).wait()
    routed = routed_ref[...]
    y = routed[:, 2:3] * ylo_ref[...] + routed[:, 3:4] * yhi_ref[...]
    out = h1_ref[...] + g2_ref[0] * y
    o_ref[...] = _rms(out) * fng_ref[...]


def _combine(dest, routed, ys, h1, g2, fng, *, tm, tiles_per_sample):
    n, d = h1.shape
    return pl.pallas_call(
        functools.partial(_combine_kernel, tm=tm),
        grid=(n // tm,),
        in_specs=[pl.BlockSpec((1, 1, 2 * tm), lambda i: (i, 0, 0), memory_space=pltpu.SMEM),
                  pl.BlockSpec((tm, LANES), lambda i: (i, 0)),
                  pl.BlockSpec(memory_space=pl.ANY),
                  pl.BlockSpec((tm, d), lambda i: (i, 0)),
                  pl.BlockSpec((1, 1, d), lambda i: (i // tiles_per_sample, 0, 0)),
                  pl.BlockSpec((1, d), lambda i: (0, 0))],
        out_specs=pl.BlockSpec((tm, d), lambda i: (i, 0)),
        out_shape=jax.ShapeDtypeStruct((n, d), F32),
        scratch_shapes=[pltpu.VMEM((tm, d), F32), pltpu.VMEM((tm, d), F32),
                        pltpu.SemaphoreType.DMA(())],
        compiler_params=_params(("arbitrary",)),
        name="moe_combine",
    )(dest, routed, ys, h1, g2, fng)


def _moe(a2, gates, sel, wg, wu, wd, h1, g2, fng, *, tm):
    b, l, d = h1.shape
    n = b * l
    n_rows = 2 * n + N_EXPERTS * MOE_TILE
    n_tiles = n_rows // MOE_TILE
    routed, tile_expert, group_ends = _route(sel.reshape(n, LANES), gates.reshape(n, LANES),
                                             tr=tm, tile_rows=MOE_TILE, n_tiles=n_tiles)
    dest = routed[:, 0:2].astype(jnp.int32).reshape(n // tm, tm, 2)
    dest = dest.transpose(0, 2, 1).reshape(n // tm, 1, 2 * tm)
    xs = _dispatch(group_ends[0, :N_EXPERTS], dest, a2.reshape(n, d), n_rows, tm=tm, tile_rows=MOE_TILE)
    ys = _experts(tile_expert[0, :n_tiles], xs, wg, wu, wd, tile_rows=MOE_TILE)
    out = _combine(dest, routed, ys, h1.reshape(n, d), g2, fng, tm=tm, tiles_per_sample=l // tm)
    return out.reshape(b, l, d)


def _rope_tables(seq_len, dim):
    t = jnp.arange(seq_len, dtype=jnp.int32)
    row = (t // GRID_W).astype(F32)
    col = (t % GRID_W).astype(F32)
    quarter = dim // 4
    inv = ROPE_BASE ** (-jnp.arange(quarter, dtype=F32) / quarter)
    ar = row[:, None] * inv[None, :]
    ac = col[:, None] * inv[None, :]
    cos = jnp.concatenate([jnp.cos(ar), jnp.cos(ar), jnp.cos(ac), jnp.cos(ac)], axis=-1)
    sin = jnp.concatenate([jnp.sin(ar), jnp.sin(ar), jnp.sin(ac), jnp.sin(ac)], axis=-1)
    reps = LANES // dim
    return jnp.tile(cos, (1, reps)), jnp.tile(sin, (1, reps))


def kernel(x, c, ctx, c_ctx, norm1_g, norm2_g, mod_w, mod_b, even_w_in, pool_w, pool_scale,
           lambda_q1, lambda_k1, lambda_q2, lambda_k2, even_w_out, ffn_w_gate, ffn_w_up,
           ffn_w_down, odd_w_in, q_norm_g, kv_norm_g, w_uq, w_ukv, odd_w_out, router_w,
           moe_w_gate, moe_w_up, moe_w_down, final_norm_g):
    b, l, d = x.shape
    n_ctx = ctx.shape[1]
    tm = min(ROW_TILE, l)
    tkv = min(KV_TILE, l)
    tq = min(DIFF_Q_TILE, l)
    cos, sin = _rope_tables(l, DIFF_HEAD_DIM)
    cos_c, sin_c = cos[:n_ctx], sin[:n_ctx]

    cond = jnp.zeros((8, d), F32).at[:b].set(c).at[b].set(c_ctx)
    mod = _modulation(cond, mod_w, mod_b)
    mod = mod.reshape(mod.shape[0], 8, N_MOD, d)

    def mod_vecs(layer):
        lat = [mod[layer, :b, k][:, None, :] for k in range(N_MOD)]
        cx = [mod[layer, b:b + 1, k][:, None, :] for k in range(N_MOD)]
        return lat, cx

    (sh1, sc1, g1, sh2, sc2, g2), (csh1, csc1, cg1, csh2, csc2, cg2) = mod_vecs(0)
    lam_init = 0.8 - 0.6 * math.exp(-0.3 * 0)
    w_in0 = even_w_in[0].astype(BF16)
    n1g = norm1_g[0][None, :]
    n2g = norm2_g[0][None, :]
    p_lat, q1t_lat, q2t_lat, k_lat, vt_lat = _pre0(x, n1g, sh1, sc1, w_in0, cos, sin,
                                                   use_rope=True, tm=tkv)
    p_ctx, q1t_ctx, q2t_ctx, k_ctx, vt_ctx = _pre0(ctx, n1g, csh1, csc1, w_in0, cos_c, sin_c,
                                                   use_rope=False, tm=n_ctx)
    lq = jnp.stack([lambda_q1[0], lambda_q2[0]])
    lk = jnp.stack([lambda_k1[0], lambda_k2[0]])
    diff_kern = functools.partial(_diff_attn_kernel, lam_init=lam_init)
    attn_lat = _attention(diff_kern, (q1t_lat, q2t_lat), (k_ctx, vt_ctx), (k_lat, vt_lat),
                          2 * DIFF_HEAD_DIM, extra=(lq, lk), tq=tq,
                          n_sub=min(DIFF_Q_SWEEPS, l // tq), name="diff_attn_lat")
    attn_ctx = _attention(diff_kern, (q1t_ctx, q2t_ctx), (k_ctx, vt_ctx), None,
                          2 * DIFF_HEAD_DIM, extra=(lq, lk), tq=n_ctx, n_sub=1, name="diff_attn_ctx")
    ng = len(POOL_WINDOWS)
    poolw_bd = jnp.zeros((POOL_WIDTH, POOL_WIDTH), F32)
    for g in range(ng):
        sl = slice(g * POOL_GROUP, (g + 1) * POOL_GROUP)
        poolw_bd = poolw_bd.at[sl, sl].set(pool_w[0, g])
    poolw_bd = poolw_bd.astype(BF16)
    pscale = pool_scale[0][None, :]
    w_out0 = even_w_out[0].astype(BF16)
    wg0, wu0, wd0 = ffn_w_gate[0].astype(BF16), ffn_w_up[0].astype(BF16), ffn_w_down[0].astype(BF16)
    h_lat = _post0(attn_lat, p_lat, poolw_bd, pscale, w_out0, x, g1, n2g, sh2, sc2, g2,
                   wg0, wu0, wd0, tm=tm)
    h_ctx = _post0(attn_ctx, p_ctx, poolw_bd, pscale, w_out0, ctx, cg1, n2g, csh2, csc2, cg2,
                   wg0, wu0, wd0, tm=n_ctx)

    (sh1, sc1, g1, sh2, sc2, g2), (csh1, csc1, _, _, _, _) = mod_vecs(1)
    n1g = norm1_g[1][None, :]
    n2g = norm2_g[1][None, :]
    q_rank, kv_rank = w_uq.shape[1], w_ukv.shape[1]
    n_heads = w_uq.shape[2] // (MLA_NOPE + MLA_ROPE)
    w_in1 = jnp.pad(odd_w_in[0], ((0, 0), (0, LANES - MLA_ROPE))).astype(BF16)
    wuq = w_uq[0].reshape(q_rank, n_heads, MLA_NOPE + MLA_ROPE)
    wuq = jnp.pad(wuq, ((0, 0), (0, 0), (0, MLA_QK_PAD - MLA_NOPE - MLA_ROPE)))
    wuq = wuq.reshape(q_rank, n_heads * MLA_QK_PAD).astype(BF16)
    wukv = w_ukv[0].reshape(kv_rank, n_heads, MLA_NOPE + MLA_V)
    wuk = wukv[:, :, :MLA_NOPE].reshape(kv_rank, n_heads * MLA_NOPE).astype(BF16)
    wuv = wukv[:, :, MLA_NOPE:].reshape(kv_rank, n_heads * MLA_V).astype(BF16)
    qg = q_norm_g[0][None, :]
    kvg = kv_norm_g[0][None, :]
    q1t, k1, vt1 = _pre1(h_lat, n1g, sh1, sc1, w_in1, qg, kvg, wuq, wuk, wuv, cos, sin,
                         use_rope=True, want_q=True, tm=tkv)
    k1c, vt1c = _pre1(h_ctx, n1g, csh1, csc1, w_in1, qg, kvg, wuq, wuk, wuv, cos_c, sin_c,
                      use_rope=False, want_q=False, tm=n_ctx)
    tq1 = min(MLA_Q_TILE, l)
    o1 = _attention(_mla_attn_kernel, (q1t,), (k1c, vt1c), (k1, vt1), MLA_QK_PAD,
                    tq=tq1, n_sub=min(MLA_Q_SWEEPS, l // tq1), name="mla_attn")
    rw_pad = jnp.pad(router_w[0], ((0, 0), (0, LANES - N_EXPERTS)))
    rw_hi = rw_pad.astype(BF16)
    rw_lo = (rw_pad - rw_hi.astype(F32)).astype(BF16)
    rw_split = jnp.concatenate([rw_hi, rw_lo], axis=1)
    h1, a2, gates, sel = _post1(o1, odd_w_out[0].astype(BF16), h_lat, g1, n2g, sh2, sc2,
                                rw_split, tm=tm)
    return _moe(a2, gates, sel, moe_w_gate[0].astype(BF16), moe_w_up[0].astype(BF16),
                moe_w_down[0].astype(BF16), h1, g2, final_norm_g[None, :], tm=tm)
```

```python
import functools
import math

import jax
import jax.numpy as jnp
from jax import lax
from jax.experimental import pallas as pl
from jax.experimental.pallas import tpu as pltpu

F32 = jnp.float32
BF16 = jnp.bfloat16

EPS = 1e-6
GRID_W = 64
ROPE_BASE = 10000.0
N_MOD = 6
POOL_WINDOWS = (2, 4, 8, 16)
POOL_GROUP = 64
POOL_WIDTH = POOL_GROUP * len(POOL_WINDOWS)
POOL_HALO = max(POOL_WINDOWS) // 2
DIFF_HEAD_DIM = 64
DIFF_V_DIM = 128
V_ONES_ROWS = 16
V_AUG = DIFF_V_DIM + V_ONES_ROWS
MLA_NOPE = 128
MLA_ROPE = 64
MLA_V = 128
MLA_QK_PAD = 256
N_EXPERTS = 8
ROPE_QUARTER = 16
LOG2E = math.log2(math.e)

LANES = 128
ROW_TILE = 512
KV_TILE = 512
DIFF_Q_TILE = 512
MLA_Q_TILE = 1024
DIFF_Q_SWEEPS = 8
MLA_Q_SWEEPS = 4
PAIR_UNROLL = 5
MOE_TILE = 512
DMA_ISSUE_UNROLL = 8
FFN_CHUNKS = (512, 512, 512, 512, 512, 256)
VMEM_LIMIT = 56 * 1024 * 1024


def _params(sem):
    return pltpu.CompilerParams(dimension_semantics=sem, vmem_limit_bytes=VMEM_LIMIT)


def _dot(a, b):
    return jnp.dot(a, b, preferred_element_type=F32)


def _rms(x):
    return x * lax.rsqrt(jnp.mean(x * x, axis=-1, keepdims=True) + EPS)


def _silu(x):
    return x * (1.0 / (1.0 + jnp.exp(-x)))


def _rope(x, cos, sin):
    lane = lax.broadcasted_iota(jnp.int32, x.shape, 1)
    even = (lane // ROPE_QUARTER) % 2 == 0
    rot = jnp.where(even, -pltpu.roll(x, LANES - ROPE_QUARTER, 1), pltpu.roll(x, ROPE_QUARTER, 1))
    return x * cos + rot * sin


def _dot_nt(a, b):
    return lax.dot_general(a, b, (((1,), (1,)), ((), ())), preferred_element_type=F32)


def _rope_t(x, cos_t, sin_t):
    q = ROPE_QUARTER
    rot = jnp.concatenate([-x[q:2 * q], x[:q], -x[3 * q:], x[2 * q:3 * q]], axis=0)
    return x * cos_t + rot * sin_t


def _store_vt(vt_ref, vt, n_heads):
    ones = jnp.ones((V_ONES_ROWS, vt.shape[1]), BF16)
    for h in range(n_heads):
        r0 = h * V_AUG
        vt_ref[0, 0, r0:r0 + DIFF_V_DIM, :] = vt[h * DIFF_V_DIM:(h + 1) * DIFF_V_DIM].astype(BF16)
        vt_ref[0, 0, r0 + DIFF_V_DIM:r0 + V_AUG, :] = ones


def _mod_kernel(cond_ref, w_ref, b_ref, o_ref):
    s = _silu(cond_ref[...])
    o_ref[0] = jnp.dot(s, w_ref[0], preferred_element_type=F32,
                       precision=lax.Precision.HIGHEST) + b_ref[0]


def _modulation(cond, mod_w, mod_b):
    depth, d, n = mod_w.shape
    tn = n // 4
    return pl.pallas_call(
        _mod_kernel,
        grid=(depth, n // tn),
        in_specs=[pl.BlockSpec((8, d), lambda i, j: (0, 0)),
                  pl.BlockSpec((1, d, tn), lambda i, j: (i, 0, j)),
                  pl.BlockSpec((1, 1, tn), lambda i, j: (i, 0, j))],
        out_specs=pl.BlockSpec((1, 8, tn), lambda i, j: (i, 0, j)),
        out_shape=jax.ShapeDtypeStruct((depth, 8, n), F32),
        compiler_params=_params(("arbitrary", "arbitrary")),
        name="modulation",
    )(cond, mod_w, mod_b.reshape(depth, 1, n))


def _bvec_spec(arr, d):
    if arr.shape[0] == 1:
        return pl.BlockSpec((1, 1, d), lambda b, i: (0, 0, 0))
    return pl.BlockSpec((1, 1, d), lambda b, i: (b, 0, 0))


def _norm_mod(h, g, sh, sc):
    a = _rms(h) * g
    return a * (1.0 + sc) + sh


def _pre0_kernel(h_ref, g_ref, sh_ref, sc_ref, wn_ref, wt_ref, cos_ref, sin_ref, cost_ref, sint_ref,
                 p_ref, q1t_ref, q2t_ref, k_ref, vt_ref, *, use_rope, n_heads):
    a = _norm_mod(h_ref[0], g_ref[...], sh_ref[0], sc_ref[0]).astype(BF16)
    z = _dot(a, wn_ref[...])
    zt = _dot_nt(wt_ref[...], a)
    p_ref[0] = z[:, :POOL_WIDTH]
    head_w = 2 * DIFF_HEAD_DIM
    qk_w = n_heads * head_w
    q_scale = DIFF_HEAD_DIM ** -0.5 * LOG2E
    cos, sin = cos_ref[...], sin_ref[...]
    cos_t, sin_t = cost_ref[...], sint_ref[...]
    for h in range(n_heads):
        xk = z[:, POOL_WIDTH + h * head_w:POOL_WIDTH + (h + 1) * head_w]
        xq1 = zt[h * head_w:h * head_w + DIFF_HEAD_DIM]
        xq2 = zt[h * head_w + DIFF_HEAD_DIM:(h + 1) * head_w]
        if use_rope:
            xk = _rope(xk, cos, sin)
            xq1 = _rope_t(xq1, cos_t, sin_t)
            xq2 = _rope_t(xq2, cos_t, sin_t)
        k_ref[0, :, h * head_w:(h + 1) * head_w] = xk.astype(BF16)
        xq1 = (xq1 * q_scale).astype(BF16)
        xq2 = (xq2 * q_scale).astype(BF16)
        zero = jnp.zeros_like(xq1)
        q1t_ref[0, 0, h * head_w:(h + 1) * head_w, :] = jnp.concatenate([xq1, zero], axis=0)
        q2t_ref[0, 0, h * head_w:(h + 1) * head_w, :] = jnp.concatenate([zero, xq2], axis=0)
    _store_vt(vt_ref, zt[qk_w:], n_heads)


def _pre0(h, g, sh, sc, w_nat, w_t, cos, sin, cos_t, sin_t, *, use_rope, tm):
    b, l, d = h.shape
    v_w = w_t.shape[0] // 2
    n_heads = v_w // DIFF_V_DIM
    kern = functools.partial(_pre0_kernel, use_rope=use_rope, n_heads=n_heads)
    return pl.pallas_call(
        kern,
        grid=(b, l // tm),
        in_specs=[pl.BlockSpec((1, tm, d), lambda b_, i: (b_, i, 0)),
                  pl.BlockSpec((1, d), lambda b_, i: (0, 0)),
                  _bvec_spec(sh, d), _bvec_spec(sc, d),
                  pl.BlockSpec(w_nat.shape, lambda b_, i: (0, 0)),
                  pl.BlockSpec(w_t.shape, lambda b_, i: (0, 0)),
                  pl.BlockSpec((tm, LANES), lambda b_, i: (i, 0)),
                  pl.BlockSpec((tm, LANES), lambda b_, i: (i, 0)),
                  pl.BlockSpec((DIFF_HEAD_DIM, tm), lambda b_, i: (0, i)),
                  pl.BlockSpec((DIFF_HEAD_DIM, tm), lambda b_, i: (0, i))],
        out_specs=[pl.BlockSpec((1, tm, POOL_WIDTH), lambda b_, i: (b_, i, 0)),
                   pl.BlockSpec((1, 1, v_w, tm), lambda b_, i: (b_, i, 0, 0)),
                   pl.BlockSpec((1, 1, v_w, tm), lambda b_, i: (b_, i, 0, 0)),
                   pl.BlockSpec((1, tm, v_w), lambda b_, i: (b_, i, 0)),
                   pl.BlockSpec((1, 1, n_heads * V_AUG, tm), lambda b_, i: (b_, i, 0, 0))],
        out_shape=[jax.ShapeDtypeStruct((b, l, POOL_WIDTH), F32),
                   jax.ShapeDtypeStruct((b, l // tm, v_w, tm), BF16),
                   jax.ShapeDtypeStruct((b, l // tm, v_w, tm), BF16),
                   jax.ShapeDtypeStruct((b, l, v_w), BF16),
                   jax.ShapeDtypeStruct((b, l // tm, n_heads * V_AUG, tm), BF16)],
        compiler_params=_params(("parallel", "parallel")),
        name="pre0_rope" if use_rope else "pre0_ctx",
    )(h, g, sh, sc, w_nat, w_t, cos, sin, cos_t, sin_t)


def _attn_core(qt_refs, j, tq, kc_ref, vc_ref, lat_refs, acc_refs, sbuf_refs):
    n = len(qt_refs)
    sub = tq // qt_refs[0].shape[3]
    qts = [jnp.concatenate([r[0, j * sub + u] for u in range(sub)], axis=1) for r in qt_refs]
    ctx_bufs = sbuf_refs[:n]
    lat_bufs = sbuf_refs[n:]
    ms = tuple(jnp.full((1, tq), -jnp.inf, F32) for _ in range(n))
    for i in range(n):
        acc_refs[i][...] = jnp.zeros_like(acc_refs[i])

    def scores(k, bufs):
        cmax = []
        for i in range(n):
            s = _dot(k, qts[i])
            bufs[i][...] = s
            cmax.append(jnp.max(s, axis=0, keepdims=True))
        return tuple(cmax)

    def softmax_pv(vt, bufs, cmax, ms):
        new_m = []
        for i in range(n):
            m_new = jnp.maximum(ms[i], cmax[i])
            alpha = jnp.exp2(ms[i] - m_new)
            p = jnp.exp2(bufs[i][...] - m_new)
            acc_refs[i][...] = alpha * acc_refs[i][...] + _dot(vt, p.astype(BF16))
            new_m.append(m_new)
        return tuple(new_m)

    if lat_refs is None:
        softmax_pv(vc_ref[0, 0], ctx_bufs, scores(kc_ref[0], ctx_bufs), ms)
        return
    kl_ref, vl_ref = lat_refs
    nblk, tk = vl_ref.shape[1], vl_ref.shape[3]
    assert nblk >= 2 and nblk % 2 == 0

    def qk(t, which):
        start = t * tk if isinstance(t, int) else pl.multiple_of(t * tk, tk)
        return scores(kl_ref[0, pl.ds(start, tk), :], lat_bufs[which::2])

    def pair(jj, carry):
        cmax_a, ms = carry
        t = 2 * jj
        cmax_b = qk(t + 1, 1)
        ms = softmax_pv(vl_ref[0, t], lat_bufs[0::2], cmax_a, ms)
        cmax_a = qk(t + 2, 0)
        ms = softmax_pv(vl_ref[0, t + 1], lat_bufs[1::2], cmax_b, ms)
        return cmax_a, ms

    cmax_a = qk(0, 0)
    n_pairs = nblk // 2 - 1
    cmax_a, ms = lax.fori_loop(0, n_pairs, pair, (cmax_a, ms),
                               unroll=max(1, min(PAIR_UNROLL, n_pairs)))
    cmax_b = qk(nblk - 1, 1)
    ms = softmax_pv(vl_ref[0, nblk - 2], lat_bufs[0::2], cmax_a, ms)
    cmax_c = scores(kc_ref[0], ctx_bufs)
    ms = softmax_pv(vl_ref[0, nblk - 1], lat_bufs[1::2], cmax_b, ms)
    softmax_pv(vc_ref[0, 0], ctx_bufs, cmax_c, ms)


def _normalised(acc_ref):
    acc = acc_ref[...]
    return acc[:DIFF_V_DIM] * (1.0 / acc[DIFF_V_DIM:DIFF_V_DIM + 1])


def _for_each_query_tile(o_ref, tq, tile_fn):
    n_tiles = o_ref.shape[1] // tq

    def body(j, carry):
        rows = pl.ds(pl.multiple_of(j * tq, tq), tq)
        o_ref[0, rows, :] = tile_fn(j)
        return carry

    if n_tiles == 1:
        o_ref[0] = tile_fn(0)
    else:
        lax.fori_loop(0, n_tiles, body, 0)


def _diff_attn_kernel(lq_ref, lk_ref, q1t_ref, q2t_ref, kc_ref, vc_ref, *rest, has_lat, lam_init, tq):
    lat_refs = rest[:2] if has_lat else None
    rest = rest[2:] if has_lat else rest
    o_ref, acc1_ref, acc2_ref = rest[:3]
    e = jnp.exp(jnp.sum(lq_ref[...] * lk_ref[...], axis=-1, keepdims=True))
    lam = e[0:1] - e[1:2] + lam_init

    def tile(j):
        _attn_core((q1t_ref, q2t_ref), j, tq, kc_ref, vc_ref, lat_refs, (acc1_ref, acc2_ref), rest[3:])
        o = _normalised(acc1_ref) - lam * _normalised(acc2_ref)
        ms = jnp.mean(o * o, axis=0, keepdims=True)
        y = o * (lax.rsqrt(ms + EPS) * (1.0 - lam_init))
        return y.T.astype(BF16)

    _for_each_query_tile(o_ref, tq, tile)


def _mla_attn_kernel(qt_ref, kc_ref, vc_ref, *rest, has_lat, tq):
    lat_refs = rest[:2] if has_lat else None
    rest = rest[2:] if has_lat else rest
    o_ref, acc_ref = rest[:2]

    def tile(j):
        _attn_core((qt_ref,), j, tq, kc_ref, vc_ref, lat_refs, (acc_ref,), rest[2:])
        return _normalised(acc_ref).T.astype(BF16)

    _for_each_query_tile(o_ref, tq, tile)


def _attention(kern, qts, ctx_kv, lat_kv, qk_w, extra=(), *, tq, n_sub, name):
    n_maps = len(qts)
    b, nqb, hw, tmq = qts[0].shape
    n_heads = hw // qk_w
    lq = nqb * tmq
    rows = n_sub * tq
    in_specs = [pl.BlockSpec(x.shape, lambda b_, h, i: (0, 0)) for x in extra]
    in_specs += [pl.BlockSpec((1, rows // tmq, qk_w, tmq), lambda b_, h, i: (b_, i, h, 0)) for _ in qts]
    args = list(extra) + list(qts)
    scratch = [pltpu.VMEM((V_AUG, tq), F32) for _ in range(n_maps)]
    scratch += [pltpu.VMEM((ctx_kv[0].shape[1], tq), F32) for _ in range(n_maps)]
    for kv in (ctx_kv, lat_kv):
        if kv is None:
            continue
        k, vt = kv
        nblk, tk = vt.shape[1], vt.shape[3]
        in_specs.append(pl.BlockSpec((1, k.shape[1], qk_w), lambda b_, h, i: (b_, 0, h)))
        in_specs.append(pl.BlockSpec((1, nblk, V_AUG, tk), lambda b_, h, i: (b_, 0, h, 0)))
        args += [k, vt]
    if lat_kv is not None:
        scratch += [pltpu.VMEM((lat_kv[1].shape[3], tq), F32) for _ in range(2 * n_maps)]
    return pl.pallas_call(
        functools.partial(kern, has_lat=lat_kv is not None, tq=tq),
        grid=(b, n_heads, lq // rows),
        in_specs=in_specs,
        out_specs=pl.BlockSpec((1, rows, DIFF_V_DIM), lambda b_, h, i: (b_, i, h)),
        out_shape=jax.ShapeDtypeStruct((b, lq, n_heads * DIFF_V_DIM), BF16),
        scratch_shapes=scratch,
        compiler_params=_params(("parallel", "parallel", "arbitrary")),
        name=name,
    )(*args)


def _post0_kernel(attn_ref, p_ref, pprev_ref, pnext_ref, poolw_ref, pscale_ref, wout_ref,
                  h_ref, g1_ref, n2g_ref, sh2_ref, sc2_ref, g2_ref, wg_ref, wu_ref, wd_ref,
                  o_ref, *, seq_len):
    i = pl.program_id(1)
    u = p_ref[0]
    tm = u.shape[0]
    ext = jnp.concatenate([pprev_ref[0], u, pnext_ref[0]], axis=0)
    row = i * tm - POOL_HALO + lax.broadcasted_iota(jnp.int32, (tm + 2 * POOL_HALO, 1), 0)
    ext = jnp.where((row >= 0) & (row < seq_len), ext, 0.0)

    def shifted(j):
        return ext[POOL_HALO + j:POOL_HALO + j + tm]

    t = i * tm + lax.broadcasted_iota(jnp.int32, (tm, 1), 0)
    lane = lax.broadcasted_iota(jnp.int32, (tm, POOL_WIDTH), 1)
    win_sum = shifted(-1) + shifted(0)
    pooled = jnp.zeros_like(u)
    for g, w in enumerate(POOL_WINDOWS):
        half = w // 2
        if g > 0:
            prev_half = POOL_WINDOWS[g - 1] // 2
            for j in range(prev_half, half):
                win_sum = win_sum + shifted(-j - 1) + shifted(j)
        cnt = (jnp.minimum(t + half, seq_len) - jnp.maximum(t - half, 0)).astype(F32)
        in_group = (lane >= g * POOL_GROUP) & (lane < (g + 1) * POOL_GROUP)
        pooled = jnp.where(in_group, win_sum / cnt, pooled)
    pooled = pooled - u
    y_pool = _dot(pooled.astype(BF16), poolw_ref[...]) * pscale_ref[...]
    y = _dot(y_pool.astype(BF16), wout_ref[:POOL_WIDTH, :]) + _dot(attn_ref[0], wout_ref[POOL_WIDTH:, :])
    h1 = h_ref[0] + g1_ref[0] * y
    a2 = _norm_mod(h1, n2g_ref[...], sh2_ref[0], sc2_ref[0]).astype(BF16)
    o_ref[0] = h1 + g2_ref[0] * _swiglu_tile(a2, wg_ref, wu_ref, wd_ref)


def _post0(attn, p, poolw_bd, pscale, w_out, h, g1, n2g, sh2, sc2, g2, wg, wu, wd, *, tm):
    b, l, d = h.shape
    hb = tm // POOL_HALO
    nhb = l // POOL_HALO
    assert sum(FFN_CHUNKS) == wg.shape[1]
    return pl.pallas_call(
        functools.partial(_post0_kernel, seq_len=l),
        grid=(b, l // tm),
        in_specs=[pl.BlockSpec((1, tm, attn.shape[2]), lambda b_, i: (b_, i, 0)),
                  pl.BlockSpec((1, tm, POOL_WIDTH), lambda b_, i: (b_, i, 0)),
                  pl.BlockSpec((1, POOL_HALO, POOL_WIDTH),
                               lambda b_, i: (b_, jnp.maximum(i * hb - 1, 0), 0)),
                  pl.BlockSpec((1, POOL_HALO, POOL_WIDTH),
                               lambda b_, i: (b_, jnp.minimum((i + 1) * hb, nhb - 1), 0)),
                  pl.BlockSpec(poolw_bd.shape, lambda b_, i: (0, 0)),
                  pl.BlockSpec((1, POOL_WIDTH), lambda b_, i: (0, 0)),
                  pl.BlockSpec(w_out.shape, lambda b_, i: (0, 0)),
                  pl.BlockSpec((1, tm, d), lambda b_, i: (b_, i, 0)),
                  _bvec_spec(g1, d),
                  pl.BlockSpec((1, d), lambda b_, i: (0, 0)),
                  _bvec_spec(sh2, d), _bvec_spec(sc2, d), _bvec_spec(g2, d),
                  pl.BlockSpec(wg.shape, lambda b_, i: (0, 0)),
                  pl.BlockSpec(wu.shape, lambda b_, i: (0, 0)),
                  pl.BlockSpec(wd.shape, lambda b_, i: (0, 0))],
        out_specs=pl.BlockSpec((1, tm, d), lambda b_, i: (b_, i, 0)),
        out_shape=jax.ShapeDtypeStruct((b, l, d), F32),
        compiler_params=_params(("parallel", "parallel")),
        name="post0_ffn",
    )(attn, p, p, p, poolw_bd, pscale, w_out, h, g1, n2g, sh2, sc2, g2, wg, wu, wd)


def _swiglu_tile(x, wg_ref, wu_ref, wd_ref, lead=()):
    acc = None
    c0 = 0
    for cw in FFN_CHUNKS:
        hg = _dot(x, wg_ref[lead + (slice(None), slice(c0, c0 + cw))])
        hu = _dot(x, wu_ref[lead + (slice(None), slice(c0, c0 + cw))])
        act = (_silu(hg) * hu).astype(BF16)
        part = _dot(act, wd_ref[lead + (slice(c0, c0 + cw), slice(None))])
        acc = part if acc is None else acc + part
        c0 += cw
    return acc


def _pre1_kernel(h_ref, g_ref, sh_ref, sc_ref, win_ref, qg_ref, kvg_ref, wuqt_ref, wuk_ref, wuvt_ref,
                 cos_ref, sin_ref, cost_ref, sint_ref, *out_refs,
                 use_rope, want_q, q_rank, kv_rank, n_heads):
    a = _norm_mod(h_ref[0], g_ref[...], sh_ref[0], sc_ref[0])
    z = _dot(a.astype(BF16), win_ref[...])
    cos = cos_ref[...]
    sin = sin_ref[...]
    if want_q:
        qt_ref, k_ref, vt_ref = out_refs
        cq = (_rms(z[:, :q_rank]) * qg_ref[...]).astype(BF16)
        qft = _dot_nt(wuqt_ref[...], cq)
        q_scale = (MLA_NOPE + MLA_ROPE) ** -0.5 * LOG2E
        rope_end = MLA_NOPE + MLA_ROPE
        for h in range(n_heads):
            c0 = h * MLA_QK_PAD
            qr = _rope_t(qft[c0 + MLA_NOPE:c0 + rope_end], cost_ref[...], sint_ref[...])
            qt_ref[0, 0, c0:c0 + MLA_NOPE, :] = (qft[c0:c0 + MLA_NOPE] * q_scale).astype(BF16)
            qt_ref[0, 0, c0 + MLA_NOPE:c0 + rope_end, :] = (qr * q_scale).astype(BF16)
            qt_ref[0, 0, c0 + rope_end:c0 + MLA_QK_PAD, :] = qft[c0 + rope_end:c0 + MLA_QK_PAD].astype(BF16)
    else:
        k_ref, vt_ref = out_refs
    ckv = (_rms(z[:, q_rank:q_rank + kv_rank]) * kvg_ref[...]).astype(BF16)
    kn = _dot(ckv, wuk_ref[...])
    vv = _dot_nt(wuvt_ref[...], ckv)
    kr = z[:, q_rank + kv_rank:]
    if use_rope:
        kr = _rope(kr, cos, sin)
    kr = kr.astype(BF16)
    for h in range(n_heads):
        c0 = h * MLA_QK_PAD
        k_ref[0, :, c0:c0 + MLA_NOPE] = kn[:, h * MLA_NOPE:(h + 1) * MLA_NOPE].astype(BF16)
        k_ref[0, :, c0 + MLA_NOPE:c0 + MLA_QK_PAD] = kr
    _store_vt(vt_ref, vv, n_heads)


def _pre1(h, g, sh, sc, w_in, qg, kvg, wuq_t, wuk, wuv_t, cos, sin, cos_t, sin_t, *,
          use_rope, want_q, tm):
    b, l, d = h.shape
    n_heads = wuk.shape[1] // MLA_NOPE
    q_rank, kv_rank = wuq_t.shape[1], wuk.shape[0]
    kern = functools.partial(_pre1_kernel, use_rope=use_rope, want_q=want_q,
                             q_rank=q_rank, kv_rank=kv_rank, n_heads=n_heads)
    full = lambda x: pl.BlockSpec(x.shape, lambda b_, i: (0,) * x.ndim)
    qk_w = n_heads * MLA_QK_PAD
    v_rows = n_heads * V_AUG
    out_specs = [pl.BlockSpec((1, tm, qk_w), lambda b_, i: (b_, i, 0)),
                 pl.BlockSpec((1, 1, v_rows, tm), lambda b_, i: (b_, i, 0, 0))]
    out_shape = [jax.ShapeDtypeStruct((b, l, qk_w), BF16),
                 jax.ShapeDtypeStruct((b, l // tm, v_rows, tm), BF16)]
    if want_q:
        out_specs = [pl.BlockSpec((1, 1, qk_w, tm), lambda b_, i: (b_, i, 0, 0))] + out_specs
        out_shape = [jax.ShapeDtypeStruct((b, l // tm, qk_w, tm), BF16)] + out_shape
    return pl.pallas_call(
        kern,
        grid=(b, l // tm),
        in_specs=[pl.BlockSpec((1, tm, d), lambda b_, i: (b_, i, 0)),
                  pl.BlockSpec((1, d), lambda b_, i: (0, 0)),
                  _bvec_spec(sh, d), _bvec_spec(sc, d),
                  full(w_in), full(qg), full(kvg), full(wuq_t), full(wuk), full(wuv_t),
                  pl.BlockSpec((tm, LANES), lambda b_, i: (i, 0)),
                  pl.BlockSpec((tm, LANES), lambda b_, i: (i, 0)),
                  pl.BlockSpec((MLA_ROPE, tm), lambda b_, i: (0, i)),
                  pl.BlockSpec((MLA_ROPE, tm), lambda b_, i: (0, i))],
        out_specs=out_specs,
        out_shape=out_shape,
        compiler_params=_params(("parallel", "parallel")),
        name="pre1_lat" if want_q else "pre1_ctx",
    )(h, g, sh, sc, w_in, qg, kvg, wuq_t, wuk, wuv_t, cos, sin, cos_t, sin_t)


def _post1_kernel(o_ref, wout_ref, h_ref, g1_ref, n2g_ref, sh2_ref, sc2_ref, rw_ref,
                  h1_ref, a2_ref, gates_ref, sel_ref):
    y = _dot(o_ref[0], wout_ref[...])
    h1 = h_ref[0] + g1_ref[0] * y
    h1_ref[0] = h1
    a2 = _norm_mod(h1, n2g_ref[...], sh2_ref[0], sc2_ref[0])
    a2_ref[0] = a2
    a_hi = a2.astype(BF16)
    a_lo = (a2 - a_hi.astype(F32)).astype(BF16)
    hi_both = _dot(a_hi, rw_ref[...])
    logits = hi_both[:, :LANES] + (_dot(a_lo, rw_ref[:, :LANES]) + hi_both[:, LANES:])
    lane = lax.broadcasted_iota(jnp.int32, logits.shape, 1)
    lg = jnp.where(lane < N_EXPERTS, logits, -jnp.inf)
    m1 = jnp.max(lg, axis=1, keepdims=True)
    i1 = jnp.min(jnp.where(lg == m1, lane, LANES), axis=1, keepdims=True)
    lg2 = jnp.where(lane == i1, -jnp.inf, lg)
    m2 = jnp.max(lg2, axis=1, keepdims=True)
    i2 = jnp.min(jnp.where(lg2 == m2, lane, LANES), axis=1, keepdims=True)
    p2 = jnp.exp(m2 - m1)
    w1 = 1.0 / (1.0 + p2)
    gates_ref[0] = jnp.where(lane == i1, w1, 0.0) + jnp.where(lane == i2, p2 * w1, 0.0)
    sel_ref[0] = jnp.where((lane == i1) | (lane == i2), 1.0, 0.0).astype(BF16)


def _post1(o, w_out, h, g1, n2g, sh2, sc2, rw_split, *, tm):
    b, l, d = h.shape
    return pl.pallas_call(
        _post1_kernel,
        grid=(b, l // tm),
        in_specs=[pl.BlockSpec((1, tm, o.shape[2]), lambda b_, i: (b_, i, 0)),
                  pl.BlockSpec(w_out.shape, lambda b_, i: (0, 0)),
                  pl.BlockSpec((1, tm, d), lambda b_, i: (b_, i, 0)),
                  _bvec_spec(g1, d),
                  pl.BlockSpec((1, d), lambda b_, i: (0, 0)),
                  _bvec_spec(sh2, d), _bvec_spec(sc2, d),
                  pl.BlockSpec(rw_split.shape, lambda b_, i: (0, 0))],
        out_specs=[pl.BlockSpec((1, tm, d), lambda b_, i: (b_, i, 0)),
                   pl.BlockSpec((1, tm, d), lambda b_, i: (b_, i, 0)),
                   pl.BlockSpec((1, tm, LANES), lambda b_, i: (b_, i, 0)),
                   pl.BlockSpec((1, tm, LANES), lambda b_, i: (b_, i, 0))],
        out_shape=[jax.ShapeDtypeStruct((b, l, d), F32),
                   jax.ShapeDtypeStruct((b, l, d), F32),
                   jax.ShapeDtypeStruct((b, l, LANES), F32),
                   jax.ShapeDtypeStruct((b, l, LANES), BF16)],
        compiler_params=_params(("parallel", "parallel")),
        name="post1_router",
    )(o, w_out, h, g1, n2g, sh2, sc2, rw_split)


def _route_kernel(sel_ref, gates_ref, r_ref, tile_ref, ends_ref, cnt_ref, off_ref, *, tile_rows):
    phase = pl.program_id(0)
    i = pl.program_id(1)
    sel = sel_ref[...]
    tr = sel.shape[0]
    col_count = jnp.sum(sel.astype(F32), axis=0, keepdims=True)
    lane = lax.broadcasted_iota(jnp.int32, (1, LANES), 1)

    @pl.when((phase == 0) & (i == 0))
    def _():
        cnt_ref[...] = jnp.zeros_like(cnt_ref)

    @pl.when(phase == 0)
    def _():
        cnt_ref[...] += col_count

    @pl.when((phase == 0) & (i == pl.num_programs(1) - 1))
    def _():
        padded = jnp.ceil(cnt_ref[...] / tile_rows) * tile_rows
        incl = padded
        for sh in (1, 2, 4):
            incl = incl + jnp.where(lane >= sh, pltpu.roll(incl, sh, 1), 0.0)
        off_ref[...] = incl - padded
        ends_ref[...] = incl.astype(jnp.int32)
        tile_start = (lax.broadcasted_iota(jnp.int32, tile_ref.shape, 1) * tile_rows).astype(F32)
        tile_expert = jnp.zeros(tile_ref.shape, jnp.int32)
        for e in range(N_EXPERTS):
            end_e = jnp.sum(jnp.where(lane == e, incl, 0.0), axis=1, keepdims=True)
            tile_expert = tile_expert + (tile_start >= end_e).astype(jnp.int32)
        tile_ref[...] = tile_expert
        cnt_ref[...] = jnp.zeros_like(cnt_ref)

    @pl.when(phase == 1)
    def _():
        rr = lax.broadcasted_iota(jnp.int32, (tr, tr), 0)
        cc = lax.broadcasted_iota(jnp.int32, (tr, tr), 1)
        earlier = jnp.where(cc < rr, 1.0, 0.0).astype(BF16)
        rank = _dot(earlier, sel) + cnt_ref[...]
        cnt_ref[...] += col_count
        pos = off_ref[...] + rank
        chosen = sel > 0
        lo = jnp.min(jnp.where(chosen, pos, 3e38), axis=1, keepdims=True)
        hi = jnp.max(jnp.where(chosen, pos, -1.0), axis=1, keepdims=True)
        g = gates_ref[...]
        w_lo = jnp.sum(jnp.where(chosen & (pos == lo), g, 0.0), axis=1, keepdims=True)
        w_hi = jnp.sum(jnp.where(chosen & (pos == hi), g, 0.0), axis=1, keepdims=True)
        lane_t = lax.broadcasted_iota(jnp.int32, (tr, LANES), 1)
        r_ref[...] = jnp.where(lane_t == 0, lo, jnp.where(lane_t == 1, hi,
                               jnp.where(lane_t == 2, w_lo, jnp.where(lane_t == 3, w_hi, 0.0))))


def _route(sel, gates, *, tr, tile_rows, n_tiles):
    n = sel.shape[0]
    n_tiles_pad = -(-n_tiles // LANES) * LANES
    return pl.pallas_call(
        functools.partial(_route_kernel, tile_rows=tile_rows),
        grid=(2, n // tr),
        in_specs=[pl.BlockSpec((tr, LANES), lambda ph, i: (i, 0)),
                  pl.BlockSpec((tr, LANES), lambda ph, i: (i, 0))],
        out_specs=[pl.BlockSpec((tr, LANES), lambda ph, i: (i * ph, 0)),
                   pl.BlockSpec((1, n_tiles_pad), lambda ph, i: (0, 0)),
                   pl.BlockSpec((1, LANES), lambda ph, i: (0, 0))],
        out_shape=[jax.ShapeDtypeStruct((n, LANES), F32),
                   jax.ShapeDtypeStruct((1, n_tiles_pad), jnp.int32),
                   jax.ShapeDtypeStruct((1, LANES), jnp.int32)],
        scratch_shapes=[pltpu.VMEM((1, LANES), F32), pltpu.VMEM((1, LANES), F32)],
        compiler_params=_params(("arbitrary", "arbitrary")),
        name="moe_route",
    )(sel, gates)


def _dispatch_kernel(ends_ref, dest_ref, x_ref, xs_hbm, zero_ref, sem, *, tm, tile_rows):
    n_tiles = xs_hbm.shape[0] // tile_rows

    def zero_tile(start):
        return pltpu.make_async_copy(zero_ref, xs_hbm.at[pl.ds(start, tile_rows)], sem)

    def group_tail(e):
        return zero_tile(pl.multiple_of(ends_ref[e] - tile_rows, tile_rows))

    @pl.when(pl.program_id(0) == 0)
    def _():
        zero_ref[...] = jnp.zeros_like(zero_ref)
        last_end = ends_ref[N_EXPERTS - 1]
        for e in range(N_EXPERTS):
            pl.when(ends_ref[e] >= tile_rows)(lambda e=e: group_tail(e).start())
        for j in range(n_tiles - N_EXPERTS, n_tiles):
            pl.when(j * tile_rows >= last_end)(lambda j=j: zero_tile(j * tile_rows).start())
        for e in range(N_EXPERTS):
            pl.when(ends_ref[e] >= tile_rows)(lambda e=e: group_tail(e).wait())
        for j in range(n_tiles - N_EXPERTS, n_tiles):
            pl.when(j * tile_rows >= last_end)(lambda j=j: zero_tile(j * tile_rows).wait())

    def row_copy(src_row, dst_row):
        return pltpu.make_async_copy(x_ref.at[pl.ds(src_row, 1)], xs_hbm.at[pl.ds(dst_row, 1)], sem)

    def issue(r, carry):
        row_copy(r, dest_ref[0, 0, r]).start()
        row_copy(r, dest_ref[0, 0, tm + r]).start()
        return carry

    lax.fori_loop(0, tm, issue, 0, unroll=DMA_ISSUE_UNROLL)
    pltpu.make_async_copy(x_ref, xs_hbm.at[pl.ds(0, tm)], sem).wait()
    pltpu.make_async_copy(x_ref, xs_hbm.at[pl.ds(0, tm)], sem).wait()


def _dispatch(group_ends, dest, x, n_rows, *, tm, tile_rows):
    n, d = x.shape
    return pl.pallas_call(
        functools.partial(_dispatch_kernel, tm=tm, tile_rows=tile_rows),
        grid_spec=pltpu.PrefetchScalarGridSpec(
            num_scalar_prefetch=1,
            grid=(n // tm,),
            in_specs=[pl.BlockSpec((1, 1, 2 * tm), lambda i, ends: (i, 0, 0), memory_space=pltpu.SMEM),
                      pl.BlockSpec((tm, d), lambda i, ends: (i, 0))],
            out_specs=pl.BlockSpec(memory_space=pl.ANY),
            scratch_shapes=[pltpu.VMEM((tile_rows, d), x.dtype), pltpu.SemaphoreType.DMA(())]),
        out_shape=jax.ShapeDtypeStruct((n_rows, d), x.dtype),
        compiler_params=pltpu.CompilerParams(dimension_semantics=("arbitrary",),
                                             has_side_effects=True),
        name="moe_dispatch",
    )(group_ends, dest, x)


def _experts_kernel(te_ref, x_ref, wg_ref, wu_ref, wd_ref, o_ref):
    valid = te_ref[pl.program_id(0)] < N_EXPERTS

    @pl.when(valid)
    def _():
        o_ref[...] = _swiglu_tile(x_ref[...].astype(BF16), wg_ref, wu_ref, wd_ref, lead=(0,))

    @pl.when(jnp.logical_not(valid))
    def _():
        o_ref[...] = jnp.zeros_like(o_ref)


def _experts(tile_expert, xs, wg, wu, wd, *, tile_rows):
    n_rows, d = xs.shape
    ne, _, f = wg.shape
    w_idx = lambda j, te: (jnp.minimum(te[j], ne - 1), 0, 0)
    return pl.pallas_call(
        _experts_kernel,
        grid_spec=pltpu.PrefetchScalarGridSpec(
            num_scalar_prefetch=1,
            grid=(n_rows // tile_rows,),
            in_specs=[pl.BlockSpec((tile_rows, d), lambda j, te: (j, 0)),
                      pl.BlockSpec((1, d, f), w_idx),
                      pl.BlockSpec((1, d, f), w_idx),
                      pl.BlockSpec((1, f, d), w_idx)],
            out_specs=pl.BlockSpec((tile_rows, d), lambda j, te: (j, 0))),
        out_shape=jax.ShapeDtypeStruct((n_rows, d), F32),
        compiler_params=_params(("arbitrary",)),
        name="moe_experts",
    )(tile_expert, xs, wg, wu, wd)


def _combine_kernel(dest_ref, routed_ref, ys_hbm, h1_ref, g2_ref, fng_ref, o_ref, ylo_ref, yhi_ref,
                    sem, *, tm):
    def issue(r, carry):
        pltpu.make_async_copy(ys_hbm.at[pl.ds(dest_ref[0, 0, r], 1)],
                              ylo_ref.at[pl.ds(r, 1)], sem).start()
        pltpu.make_async_copy(ys_hbm.at[pl.ds(dest_ref[0, 0, tm + r], 1)],
                              yhi_ref.at[pl.ds(r, 1)], sem).start()
        return carry

    lax.fori_loop(0, tm, issue, 0, unroll=DMA_ISSUE_UNROLL)
    pltpu.make_async_copy(ys_hbm.at[pl.ds(0, tm)], ylo_ref, sem).wait()
    pltpu.make_async_copy(ys_hbm.at[pl.ds(0, tm)], yhi_ref, sem).wait()
    routed = routed_ref[...]
    y = routed[:, 2:3] * ylo_ref[...] + routed[:, 3:4] * yhi_ref[...]
    out = h1_ref[...] + g2_ref[0] * y
    o_ref[...] = _rms(out) * fng_ref[...]


def _combine(dest, routed, ys, h1, g2, fng, *, tm, tiles_per_sample):
    n, d = h1.shape
    return pl.pallas_call(
        functools.partial(_combine_kernel, tm=tm),
        grid=(n // tm,),
        in_specs=[pl.BlockSpec((1, 1, 2 * tm), lambda i: (i, 0, 0), memory_space=pltpu.SMEM),
                  pl.BlockSpec((tm, LANES), lambda i: (i, 0)),
                  pl.BlockSpec(memory_space=pl.ANY),
                  pl.BlockSpec((tm, d), lambda i: (i, 0)),
                  pl.BlockSpec((1, 1, d), lambda i: (i // tiles_per_sample, 0, 0)),
                  pl.BlockSpec((1, d), lambda i: (0, 0))],
        out_specs=pl.BlockSpec((tm, d), lambda i: (i, 0)),
        out_shape=jax.ShapeDtypeStruct((n, d), F32),
        scratch_shapes=[pltpu.VMEM((tm, d), F32), pltpu.VMEM((tm, d), F32),
                        pltpu.SemaphoreType.DMA(())],
        compiler_params=_params(("arbitrary",)),
        name="moe_combine",
    )(dest, routed, ys, h1, g2, fng)


def _moe(a2, gates, sel, wg, wu, wd, h1, g2, fng, *, tm):
    b, l, d = h1.shape
    n = b * l
    n_rows = 2 * n + N_EXPERTS * MOE_TILE
    n_tiles = n_rows // MOE_TILE
    routed, tile_expert, group_ends = _route(sel.reshape(n, LANES), gates.reshape(n, LANES),
                                             tr=tm, tile_rows=MOE_TILE, n_tiles=n_tiles)
    dest = routed[:, 0:2].astype(jnp.int32).reshape(n // tm, tm, 2)
    dest = dest.transpose(0, 2, 1).reshape(n // tm, 1, 2 * tm)
    xs = _dispatch(group_ends[0, :N_EXPERTS], dest, a2.reshape(n, d), n_rows, tm=tm, tile_rows=MOE_TILE)
    ys = _experts(tile_expert[0, :n_tiles], xs, wg, wu, wd, tile_rows=MOE_TILE)
    out = _combine(dest, routed, ys, h1.reshape(n, d), g2, fng, tm=tm, tiles_per_sample=l // tm)
    return out.reshape(b, l, d)


def _rope_tables(seq_len, dim):
    t = jnp.arange(seq_len, dtype=jnp.int32)
    row = (t // GRID_W).astype(F32)
    col = (t % GRID_W).astype(F32)
    quarter = dim // 4
    inv = ROPE_BASE ** (-jnp.arange(quarter, dtype=F32) / quarter)
    ar = row[:, None] * inv[None, :]
    ac = col[:, None] * inv[None, :]
    cos = jnp.concatenate([jnp.cos(ar), jnp.cos(ar), jnp.cos(ac), jnp.cos(ac)], axis=-1)
    sin = jnp.concatenate([jnp.sin(ar), jnp.sin(ar), jnp.sin(ac), jnp.sin(ac)], axis=-1)
    reps = LANES // dim
    return jnp.tile(cos, (1, reps)), jnp.tile(sin, (1, reps))


def kernel(x, c, ctx, c_ctx, norm1_g, norm2_g, mod_w, mod_b, even_w_in, pool_w, pool_scale,
           lambda_q1, lambda_k1, lambda_q2, lambda_k2, even_w_out, ffn_w_gate, ffn_w_up,
           ffn_w_down, odd_w_in, q_norm_g, kv_norm_g, w_uq, w_ukv, odd_w_out, router_w,
           moe_w_gate, moe_w_up, moe_w_down, final_norm_g):
    b, l, d = x.shape
    n_ctx = ctx.shape[1]
    tm = min(ROW_TILE, l)
    tkv = min(KV_TILE, l)
    tq = min(DIFF_Q_TILE, l)
    cos, sin = _rope_tables(l, DIFF_HEAD_DIM)
    cos_t, sin_t = cos[:, :DIFF_HEAD_DIM].T, sin[:, :DIFF_HEAD_DIM].T
    cos_c, sin_c, cos_tc, sin_tc = cos[:n_ctx], sin[:n_ctx], cos_t[:, :n_ctx], sin_t[:, :n_ctx]

    cond = jnp.zeros((8, d), F32).at[:b].set(c).at[b].set(c_ctx)
    mod = _modulation(cond, mod_w, mod_b)
    mod = mod.reshape(mod.shape[0], 8, N_MOD, d)

    def mod_vecs(layer):
        lat = [mod[layer, :b, k][:, None, :] for k in range(N_MOD)]
        cx = [mod[layer, b:b + 1, k][:, None, :] for k in range(N_MOD)]
        return lat, cx

    (sh1, sc1, g1, sh2, sc2, g2), (csh1, csc1, cg1, csh2, csc2, cg2) = mod_vecs(0)
    lam_init = 0.8 - 0.6 * math.exp(-0.3 * 0)
    w_in0 = even_w_in[0]
    qk_w0 = (w_in0.shape[1] - POOL_WIDTH) // 3
    k0, v0 = POOL_WIDTH + qk_w0, POOL_WIDTH + 2 * qk_w0
    w_nat0 = jnp.concatenate([w_in0[:, :POOL_WIDTH], w_in0[:, k0:v0]], axis=1).astype(BF16)
    w_t0 = jnp.concatenate([w_in0[:, POOL_WIDTH:k0], w_in0[:, v0:]], axis=1).T.astype(BF16)
    n1g = norm1_g[0][None, :]
    n2g = norm2_g[0][None, :]
    p_lat, q1t_lat, q2t_lat, k_lat, vt_lat = _pre0(x, n1g, sh1, sc1, w_nat0, w_t0, cos, sin,
                                                   cos_t, sin_t, use_rope=True, tm=tkv)
    p_ctx, q1t_ctx, q2t_ctx, k_ctx, vt_ctx = _pre0(ctx, n1g, csh1, csc1, w_nat0, w_t0, cos_c, sin_c,
                                                   cos_tc, sin_tc, use_rope=False, tm=n_ctx)
    lq = jnp.stack([lambda_q1[0], lambda_q2[0]])
    lk = jnp.stack([lambda_k1[0], lambda_k2[0]])
    diff_kern = functools.partial(_diff_attn_kernel, lam_init=lam_init)
    attn_lat = _attention(diff_kern, (q1t_lat, q2t_lat), (k_ctx, vt_ctx), (k_lat, vt_lat),
                          2 * DIFF_HEAD_DIM, extra=(lq, lk), tq=tq,
                          n_sub=min(DIFF_Q_SWEEPS, l // tq), name="diff_attn_lat")
    attn_ctx = _attention(diff_kern, (q1t_ctx, q2t_ctx), (k_ctx, vt_ctx), None,
                          2 * DIFF_HEAD_DIM, extra=(lq, lk), tq=n_ctx, n_sub=1, name="diff_attn_ctx")
    ng = len(POOL_WINDOWS)
    poolw_bd = jnp.zeros((POOL_WIDTH, POOL_WIDTH), F32)
    for g in range(ng):
        sl = slice(g * POOL_GROUP, (g + 1) * POOL_GROUP)
        poolw_bd = poolw_bd.at[sl, sl].set(pool_w[0, g])
    poolw_bd = poolw_bd.astype(BF16)
    pscale = pool_scale[0][None, :]
    w_out0 = even_w_out[0].astype(BF16)
    wg0, wu0, wd0 = ffn_w_gate[0].astype(BF16), ffn_w_up[0].astype(BF16), ffn_w_down[0].astype(BF16)
    h_lat = _post0(attn_lat, p_lat, poolw_bd, pscale, w_out0, x, g1, n2g, sh2, sc2, g2,
                   wg0, wu0, wd0, tm=tm)
    h_ctx = _post0(attn_ctx, p_ctx, poolw_bd, pscale, w_out0, ctx, cg1, n2g, csh2, csc2, cg2,
                   wg0, wu0, wd0, tm=n_ctx)

    (sh1, sc1, g1, sh2, sc2, g2), (csh1, csc1, _, _, _, _) = mod_vecs(1)
    n1g = norm1_g[1][None, :]
    n2g = norm2_g[1][None, :]
    q_rank, kv_rank = w_uq.shape[1], w_ukv.shape[1]
    n_heads = w_uq.shape[2] // (MLA_NOPE + MLA_ROPE)
    w_in1 = jnp.pad(odd_w_in[0], ((0, 0), (0, LANES - MLA_ROPE))).astype(BF16)
    wuq = w_uq[0].reshape(q_rank, n_heads, MLA_NOPE + MLA_ROPE)
    wuq = jnp.pad(wuq, ((0, 0), (0, 0), (0, MLA_QK_PAD - MLA_NOPE - MLA_ROPE)))
    wuq_t = wuq.reshape(q_rank, n_heads * MLA_QK_PAD).T.astype(BF16)
    wukv = w_ukv[0].reshape(kv_rank, n_heads, MLA_NOPE + MLA_V)
    wuk = wukv[:, :, :MLA_NOPE].reshape(kv_rank, n_heads * MLA_NOPE).astype(BF16)
    wuv_t = wukv[:, :, MLA_NOPE:].reshape(kv_rank, n_heads * MLA_V).T.astype(BF16)
    qg = q_norm_g[0][None, :]
    kvg = kv_norm_g[0][None, :]
    q1t, k1, vt1 = _pre1(h_lat, n1g, sh1, sc1, w_in1, qg, kvg, wuq_t, wuk, wuv_t, cos, sin,
                         cos_t, sin_t, use_rope=True, want_q=True, tm=tkv)
    k1c, vt1c = _pre1(h_ctx, n1g, csh1, csc1, w_in1, qg, kvg, wuq_t, wuk, wuv_t, cos_c, sin_c,
                      cos_tc, sin_tc, use_rope=False, want_q=False, tm=n_ctx)
    tq1 = min(MLA_Q_TILE, l)
    o1 = _attention(_mla_attn_kernel, (q1t,), (k1c, vt1c), (k1, vt1), MLA_QK_PAD,
                    tq=tq1, n_sub=min(MLA_Q_SWEEPS, l // tq1), name="mla_attn")
    rw_pad = jnp.pad(router_w[0], ((0, 0), (0, LANES - N_EXPERTS)))
    rw_hi = rw_pad.astype(BF16)
    rw_lo = (rw_pad - rw_hi.astype(F32)).astype(BF16)
    rw_split = jnp.concatenate([rw_hi, rw_lo], axis=1)
    h1, a2, gates, sel = _post1(o1, odd_w_out[0].astype(BF16), h_lat, g1, n2g, sh2, sc2,
                                rw_split, tm=tm)
    return _moe(a2, gates, sel, moe_w_gate[0].astype(BF16), moe_w_up[0].astype(BF16),
                moe_w_down[0].astype(BF16), h1, g2, final_norm_g[None, :], tm=tm)
```

```python
import functools
import math

import jax
import jax.numpy as jnp
from jax import lax
from jax.experimental import pallas as pl
from jax.experimental.pallas import tpu as pltpu

F32 = jnp.float32
BF16 = jnp.bfloat16

EPS = 1e-6
GRID_W = 64
ROPE_BASE = 10000.0
N_MOD = 6
POOL_WINDOWS = (2, 4, 8, 16)
POOL_GROUP = 64
POOL_WIDTH = POOL_GROUP * len(POOL_WINDOWS)
POOL_HALO = max(POOL_WINDOWS) // 2
DIFF_HEAD_DIM = 64
DIFF_V_DIM = 128
V_ONES_ROWS = 16
V_AUG = DIFF_V_DIM + V_ONES_ROWS
MLA_NOPE = 128
MLA_ROPE = 64
MLA_V = 128
MLA_QK_PAD = 256
N_EXPERTS = 8
ROPE_QUARTER = 16
LOG2E = math.log2(math.e)

LANES = 128
ROW_TILE = 512
KV_TILE = 512
DIFF_Q_TILE = 512
MLA_Q_TILE = 1024
DIFF_Q_SWEEPS = 8
MLA_Q_SWEEPS = 4
PAIR_UNROLL = 5
MOE_TILE = 512
MOE_COPY_TILE = 1024
DMA_ISSUE_UNROLL = 8
FFN_CHUNKS = (512, 512, 512, 512, 512, 256)
VMEM_LIMIT = 56 * 1024 * 1024


def _params(sem):
    return pltpu.CompilerParams(dimension_semantics=sem, vmem_limit_bytes=VMEM_LIMIT)


def _dot(a, b):
    return jnp.dot(a, b, preferred_element_type=F32)


def _rms(x):
    return x * lax.rsqrt(jnp.mean(x * x, axis=-1, keepdims=True) + EPS)


def _silu(x):
    return x * (1.0 / (1.0 + jnp.exp(-x)))


def _rope(x, cos, sin):
    lane = lax.broadcasted_iota(jnp.int32, x.shape, 1)
    even = (lane // ROPE_QUARTER) % 2 == 0
    rot = jnp.where(even, -pltpu.roll(x, LANES - ROPE_QUARTER, 1), pltpu.roll(x, ROPE_QUARTER, 1))
    return x * cos + rot * sin


def _dot_nt(a, b):
    return lax.dot_general(a, b, (((1,), (1,)), ((), ())), preferred_element_type=F32)


def _rope_t(x, cos_t, sin_t):
    q = ROPE_QUARTER
    rot = jnp.concatenate([-x[q:2 * q], x[:q], -x[3 * q:], x[2 * q:3 * q]], axis=0)
    return x * cos_t + rot * sin_t


def _store_vt(vt_ref, vt, n_heads):
    ones = jnp.ones((V_ONES_ROWS, vt.shape[1]), BF16)
    for h in range(n_heads):
        r0 = h * V_AUG
        vt_ref[0, 0, r0:r0 + DIFF_V_DIM, :] = vt[h * DIFF_V_DIM:(h + 1) * DIFF_V_DIM].astype(BF16)
        vt_ref[0, 0, r0 + DIFF_V_DIM:r0 + V_AUG, :] = ones


def _mod_kernel(cond_ref, w_ref, b_ref, o_ref):
    s = _silu(cond_ref[...])
    o_ref[0] = jnp.dot(s, w_ref[0], preferred_element_type=F32,
                       precision=lax.Precision.HIGHEST) + b_ref[0]


def _modulation(cond, mod_w, mod_b):
    depth, d, n = mod_w.shape
    tn = n // 4
    return pl.pallas_call(
        _mod_kernel,
        grid=(depth, n // tn),
        in_specs=[pl.BlockSpec((8, d), lambda i, j: (0, 0)),
                  pl.BlockSpec((1, d, tn), lambda i, j: (i, 0, j)),
                  pl.BlockSpec((1, 1, tn), lambda i, j: (i, 0, j))],
        out_specs=pl.BlockSpec((1, 8, tn), lambda i, j: (i, 0, j)),
        out_shape=jax.ShapeDtypeStruct((depth, 8, n), F32),
        compiler_params=_params(("arbitrary", "arbitrary")),
        name="modulation",
    )(cond, mod_w, mod_b.reshape(depth, 1, n))


def _bvec_spec(arr, d):
    if arr.shape[0] == 1:
        return pl.BlockSpec((1, 1, d), lambda b, i: (0, 0, 0))
    return pl.BlockSpec((1, 1, d), lambda b, i: (b, 0, 0))


def _norm_mod(h, g, sh, sc):
    a = _rms(h) * g
    return a * (1.0 + sc) + sh


def _pre0_kernel(h_ref, g_ref, sh_ref, sc_ref, wn_ref, wt_ref, cos_ref, sin_ref, cost_ref, sint_ref,
                 p_ref, q1t_ref, q2t_ref, k_ref, vt_ref, *, use_rope, n_heads):
    a = _norm_mod(h_ref[0], g_ref[...], sh_ref[0], sc_ref[0]).astype(BF16)
    z = _dot(a, wn_ref[...])
    zt = _dot_nt(wt_ref[...], a)
    p_ref[0] = z[:, :POOL_WIDTH]
    head_w = 2 * DIFF_HEAD_DIM
    qk_w = n_heads * head_w
    q_scale = DIFF_HEAD_DIM ** -0.5 * LOG2E
    cos, sin = cos_ref[...], sin_ref[...]
    cos_t, sin_t = cost_ref[...], sint_ref[...]
    for h in range(n_heads):
        xk = z[:, POOL_WIDTH + h * head_w:POOL_WIDTH + (h + 1) * head_w]
        xq1 = zt[h * head_w:h * head_w + DIFF_HEAD_DIM]
        xq2 = zt[h * head_w + DIFF_HEAD_DIM:(h + 1) * head_w]
        if use_rope:
            xk = _rope(xk, cos, sin)
            xq1 = _rope_t(xq1, cos_t, sin_t)
            xq2 = _rope_t(xq2, cos_t, sin_t)
        k_ref[0, :, h * head_w:(h + 1) * head_w] = xk.astype(BF16)
        xq1 = (xq1 * q_scale).astype(BF16)
        xq2 = (xq2 * q_scale).astype(BF16)
        zero = jnp.zeros_like(xq1)
        q1t_ref[0, 0, h * head_w:(h + 1) * head_w, :] = jnp.concatenate([xq1, zero], axis=0)
        q2t_ref[0, 0, h * head_w:(h + 1) * head_w, :] = jnp.concatenate([zero, xq2], axis=0)
    _store_vt(vt_ref, zt[qk_w:], n_heads)


def _pre0(h, g, sh, sc, w_nat, w_t, cos, sin, cos_t, sin_t, *, use_rope, tm):
    b, l, d = h.shape
    v_w = w_t.shape[0] // 2
    n_heads = v_w // DIFF_V_DIM
    kern = functools.partial(_pre0_kernel, use_rope=use_rope, n_heads=n_heads)
    return pl.pallas_call(
        kern,
        grid=(b, l // tm),
        in_specs=[pl.BlockSpec((1, tm, d), lambda b_, i: (b_, i, 0)),
                  pl.BlockSpec((1, d), lambda b_, i: (0, 0)),
                  _bvec_spec(sh, d), _bvec_spec(sc, d),
                  pl.BlockSpec(w_nat.shape, lambda b_, i: (0, 0)),
                  pl.BlockSpec(w_t.shape, lambda b_, i: (0, 0)),
                  pl.BlockSpec((tm, LANES), lambda b_, i: (i, 0)),
                  pl.BlockSpec((tm, LANES), lambda b_, i: (i, 0)),
                  pl.BlockSpec((DIFF_HEAD_DIM, tm), lambda b_, i: (0, i)),
                  pl.BlockSpec((DIFF_HEAD_DIM, tm), lambda b_, i: (0, i))],
        out_specs=[pl.BlockSpec((1, tm, POOL_WIDTH), lambda b_, i: (b_, i, 0)),
                   pl.BlockSpec((1, 1, v_w, tm), lambda b_, i: (b_, i, 0, 0)),
                   pl.BlockSpec((1, 1, v_w, tm), lambda b_, i: (b_, i, 0, 0)),
                   pl.BlockSpec((1, tm, v_w), lambda b_, i: (b_, i, 0)),
                   pl.BlockSpec((1, 1, n_heads * V_AUG, tm), lambda b_, i: (b_, i, 0, 0))],
        out_shape=[jax.ShapeDtypeStruct((b, l, POOL_WIDTH), F32),
                   jax.ShapeDtypeStruct((b, l // tm, v_w, tm), BF16),
                   jax.ShapeDtypeStruct((b, l // tm, v_w, tm), BF16),
                   jax.ShapeDtypeStruct((b, l, v_w), BF16),
                   jax.ShapeDtypeStruct((b, l // tm, n_heads * V_AUG, tm), BF16)],
        compiler_params=_params(("parallel", "parallel")),
        name="pre0_rope" if use_rope else "pre0_ctx",
    )(h, g, sh, sc, w_nat, w_t, cos, sin, cos_t, sin_t)


def _attn_core(qt_refs, j, tq, kc_ref, vc_ref, lat_refs, acc_refs, sbuf_refs):
    n = len(qt_refs)
    sub = tq // qt_refs[0].shape[3]
    qts = [jnp.concatenate([r[0, j * sub + u] for u in range(sub)], axis=1) for r in qt_refs]
    ctx_bufs = sbuf_refs[:n]
    lat_bufs = sbuf_refs[n:]
    ms = tuple(jnp.full((1, tq), -jnp.inf, F32) for _ in range(n))
    for i in range(n):
        acc_refs[i][...] = jnp.zeros_like(acc_refs[i])

    def scores(k, bufs):
        cmax = []
        for i in range(n):
            s = _dot(k, qts[i])
            bufs[i][...] = s
            cmax.append(jnp.max(s, axis=0, keepdims=True))
        return tuple(cmax)

    def softmax_pv(vt, bufs, cmax, ms):
        new_m = []
        for i in range(n):
            m_new = jnp.maximum(ms[i], cmax[i])
            alpha = jnp.exp2(ms[i] - m_new)
            p = jnp.exp2(bufs[i][...] - m_new)
            acc_refs[i][...] = alpha * acc_refs[i][...] + _dot(vt, p.astype(BF16))
            new_m.append(m_new)
        return tuple(new_m)

    if lat_refs is None:
        softmax_pv(vc_ref[0, 0], ctx_bufs, scores(kc_ref[0], ctx_bufs), ms)
        return
    kl_ref, vl_ref = lat_refs
    nblk, tk = vl_ref.shape[1], vl_ref.shape[3]
    assert nblk >= 2 and nblk % 2 == 0

    def qk(t, which):
        start = t * tk if isinstance(t, int) else pl.multiple_of(t * tk, tk)
        return scores(kl_ref[0, pl.ds(start, tk), :], lat_bufs[which::2])

    def pair(jj, carry):
        cmax_a, ms = carry
        t = 2 * jj
        cmax_b = qk(t + 1, 1)
        ms = softmax_pv(vl_ref[0, t], lat_bufs[0::2], cmax_a, ms)
        cmax_a = qk(t + 2, 0)
        ms = softmax_pv(vl_ref[0, t + 1], lat_bufs[1::2], cmax_b, ms)
        return cmax_a, ms

    cmax_a = qk(0, 0)
    n_pairs = nblk // 2 - 1
    cmax_a, ms = lax.fori_loop(0, n_pairs, pair, (cmax_a, ms),
                               unroll=max(1, min(PAIR_UNROLL, n_pairs)))
    cmax_b = qk(nblk - 1, 1)
    ms = softmax_pv(vl_ref[0, nblk - 2], lat_bufs[0::2], cmax_a, ms)
    cmax_c = scores(kc_ref[0], ctx_bufs)
    ms = softmax_pv(vl_ref[0, nblk - 1], lat_bufs[1::2], cmax_b, ms)
    softmax_pv(vc_ref[0, 0], ctx_bufs, cmax_c, ms)


def _normalised(acc_ref):
    acc = acc_ref[...]
    return acc[:DIFF_V_DIM] * (1.0 / acc[DIFF_V_DIM:DIFF_V_DIM + 1])


def _for_each_query_tile(o_ref, tq, tile_fn):
    n_tiles = o_ref.shape[1] // tq

    def body(j, carry):
        rows = pl.ds(pl.multiple_of(j * tq, tq), tq)
        o_ref[0, rows, :] = tile_fn(j)
        return carry

    if n_tiles == 1:
        o_ref[0] = tile_fn(0)
    else:
        lax.fori_loop(0, n_tiles, body, 0)


def _diff_attn_kernel(lq_ref, lk_ref, q1t_ref, q2t_ref, kc_ref, vc_ref, *rest, has_lat, lam_init, tq):
    lat_refs = rest[:2] if has_lat else None
    rest = rest[2:] if has_lat else rest
    o_ref, acc1_ref, acc2_ref = rest[:3]
    e = jnp.exp(jnp.sum(lq_ref[...] * lk_ref[...], axis=-1, keepdims=True))
    lam = e[0:1] - e[1:2] + lam_init

    def tile(j):
        _attn_core((q1t_ref, q2t_ref), j, tq, kc_ref, vc_ref, lat_refs, (acc1_ref, acc2_ref), rest[3:])
        o = _normalised(acc1_ref) - lam * _normalised(acc2_ref)
        ms = jnp.mean(o * o, axis=0, keepdims=True)
        y = o * (lax.rsqrt(ms + EPS) * (1.0 - lam_init))
        return y.T.astype(BF16)

    _for_each_query_tile(o_ref, tq, tile)


def _mla_attn_kernel(qt_ref, kc_ref, vc_ref, *rest, has_lat, tq):
    lat_refs = rest[:2] if has_lat else None
    rest = rest[2:] if has_lat else rest
    o_ref, acc_ref = rest[:2]

    def tile(j):
        _attn_core((qt_ref,), j, tq, kc_ref, vc_ref, lat_refs, (acc_ref,), rest[2:])
        return _normalised(acc_ref).T.astype(BF16)

    _for_each_query_tile(o_ref, tq, tile)


def _attention(kern, qts, ctx_kv, lat_kv, qk_w, extra=(), *, tq, n_sub, name):
    n_maps = len(qts)
    b, nqb, hw, tmq = qts[0].shape
    n_heads = hw // qk_w
    lq = nqb * tmq
    rows = n_sub * tq
    in_specs = [pl.BlockSpec(x.shape, lambda b_, h, i: (0, 0)) for x in extra]
    in_specs += [pl.BlockSpec((1, rows // tmq, qk_w, tmq), lambda b_, h, i: (b_, i, h, 0)) for _ in qts]
    args = list(extra) + list(qts)
    scratch = [pltpu.VMEM((V_AUG, tq), F32) for _ in range(n_maps)]
    scratch += [pltpu.VMEM((ctx_kv[0].shape[1], tq), F32) for _ in range(n_maps)]
    for kv in (ctx_kv, lat_kv):
        if kv is None:
            continue
        k, vt = kv
        nblk, tk = vt.shape[1], vt.shape[3]
        in_specs.append(pl.BlockSpec((1, k.shape[1], qk_w), lambda b_, h, i: (b_, 0, h)))
        in_specs.append(pl.BlockSpec((1, nblk, V_AUG, tk), lambda b_, h, i: (b_, 0, h, 0)))
        args += [k, vt]
    if lat_kv is not None:
        scratch += [pltpu.VMEM((lat_kv[1].shape[3], tq), F32) for _ in range(2 * n_maps)]
    return pl.pallas_call(
        functools.partial(kern, has_lat=lat_kv is not None, tq=tq),
        grid=(b, n_heads, lq // rows),
        in_specs=in_specs,
        out_specs=pl.BlockSpec((1, rows, DIFF_V_DIM), lambda b_, h, i: (b_, i, h)),
        out_shape=jax.ShapeDtypeStruct((b, lq, n_heads * DIFF_V_DIM), BF16),
        scratch_shapes=scratch,
        compiler_params=_params(("parallel", "parallel", "arbitrary")),
        name=name,
    )(*args)


def _post0_kernel(attn_ref, p_ref, pprev_ref, pnext_ref, poolw_ref, pscale_ref, wout_ref,
                  h_ref, g1_ref, n2g_ref, sh2_ref, sc2_ref, g2_ref, wg_ref, wu_ref, wd_ref,
                  o_ref, *, seq_len):
    i = pl.program_id(1)
    u = p_ref[0]
    tm = u.shape[0]
    ext = jnp.concatenate([pprev_ref[0], u, pnext_ref[0]], axis=0)
    row = i * tm - POOL_HALO + lax.broadcasted_iota(jnp.int32, (tm + 2 * POOL_HALO, 1), 0)
    ext = jnp.where((row >= 0) & (row < seq_len), ext, 0.0)

    def shifted(j):
        return ext[POOL_HALO + j:POOL_HALO + j + tm]

    t = i * tm + lax.broadcasted_iota(jnp.int32, (tm, 1), 0)
    lane = lax.broadcasted_iota(jnp.int32, (tm, POOL_WIDTH), 1)
    win_sum = shifted(-1) + shifted(0)
    pooled = jnp.zeros_like(u)
    for g, w in enumerate(POOL_WINDOWS):
        half = w // 2
        if g > 0:
            prev_half = POOL_WINDOWS[g - 1] // 2
            for j in range(prev_half, half):
                win_sum = win_sum + shifted(-j - 1) + shifted(j)
        cnt = (jnp.minimum(t + half, seq_len) - jnp.maximum(t - half, 0)).astype(F32)
        in_group = (lane >= g * POOL_GROUP) & (lane < (g + 1) * POOL_GROUP)
        pooled = jnp.where(in_group, win_sum / cnt, pooled)
    pooled = pooled - u
    y_pool = _dot(pooled.astype(BF16), poolw_ref[...]) * pscale_ref[...]
    y = _dot(y_pool.astype(BF16), wout_ref[:POOL_WIDTH, :]) + _dot(attn_ref[0], wout_ref[POOL_WIDTH:, :])
    h1 = h_ref[0] + g1_ref[0] * y
    a2 = _norm_mod(h1, n2g_ref[...], sh2_ref[0], sc2_ref[0]).astype(BF16)
    o_ref[0] = h1 + g2_ref[0] * _swiglu_tile(a2, wg_ref, wu_ref, wd_ref)


def _post0(attn, p, poolw_bd, pscale, w_out, h, g1, n2g, sh2, sc2, g2, wg, wu, wd, *, tm):
    b, l, d = h.shape
    hb = tm // POOL_HALO
    nhb = l // POOL_HALO
    assert sum(FFN_CHUNKS) == wg.shape[1]
    return pl.pallas_call(
        functools.partial(_post0_kernel, seq_len=l),
        grid=(b, l // tm),
        in_specs=[pl.BlockSpec((1, tm, attn.shape[2]), lambda b_, i: (b_, i, 0)),
                  pl.BlockSpec((1, tm, POOL_WIDTH), lambda b_, i: (b_, i, 0)),
                  pl.BlockSpec((1, POOL_HALO, POOL_WIDTH),
                               lambda b_, i: (b_, jnp.maximum(i * hb - 1, 0), 0)),
                  pl.BlockSpec((1, POOL_HALO, POOL_WIDTH),
                               lambda b_, i: (b_, jnp.minimum((i + 1) * hb, nhb - 1), 0)),
                  pl.BlockSpec(poolw_bd.shape, lambda b_, i: (0, 0)),
                  pl.BlockSpec((1, POOL_WIDTH), lambda b_, i: (0, 0)),
                  pl.BlockSpec(w_out.shape, lambda b_, i: (0, 0)),
                  pl.BlockSpec((1, tm, d), lambda b_, i: (b_, i, 0)),
                  _bvec_spec(g1, d),
                  pl.BlockSpec((1, d), lambda b_, i: (0, 0)),
                  _bvec_spec(sh2, d), _bvec_spec(sc2, d), _bvec_spec(g2, d),
                  pl.BlockSpec(wg.shape, lambda b_, i: (0, 0)),
                  pl.BlockSpec(wu.shape, lambda b_, i: (0, 0)),
                  pl.BlockSpec(wd.shape, lambda b_, i: (0, 0))],
        out_specs=pl.BlockSpec((1, tm, d), lambda b_, i: (b_, i, 0)),
        out_shape=jax.ShapeDtypeStruct((b, l, d), F32),
        compiler_params=_params(("parallel", "parallel")),
        name="post0_ffn",
    )(attn, p, p, p, poolw_bd, pscale, w_out, h, g1, n2g, sh2, sc2, g2, wg, wu, wd)


def _swiglu_tile(x, wg_ref, wu_ref, wd_ref, lead=()):
    acc = None
    c0 = 0
    for cw in FFN_CHUNKS:
        hg = _dot(x, wg_ref[lead + (slice(None), slice(c0, c0 + cw))])
        hu = _dot(x, wu_ref[lead + (slice(None), slice(c0, c0 + cw))])
        act = (_silu(hg) * hu).astype(BF16)
        part = _dot(act, wd_ref[lead + (slice(c0, c0 + cw), slice(None))])
        acc = part if acc is None else acc + part
        c0 += cw
    return acc


def _pre1_kernel(h_ref, g_ref, sh_ref, sc_ref, win_ref, qg_ref, kvg_ref, wuqt_ref, wuk_ref, wuvt_ref,
                 cos_ref, sin_ref, cost_ref, sint_ref, *out_refs,
                 use_rope, want_q, q_rank, kv_rank, n_heads):
    a = _norm_mod(h_ref[0], g_ref[...], sh_ref[0], sc_ref[0])
    z = _dot(a.astype(BF16), win_ref[...])
    cos = cos_ref[...]
    sin = sin_ref[...]
    if want_q:
        qt_ref, k_ref, vt_ref = out_refs
        cq = (_rms(z[:, :q_rank]) * qg_ref[...]).astype(BF16)
        qft = _dot_nt(wuqt_ref[...], cq)
        q_scale = (MLA_NOPE + MLA_ROPE) ** -0.5 * LOG2E
        rope_end = MLA_NOPE + MLA_ROPE
        for h in range(n_heads):
            c0 = h * MLA_QK_PAD
            qr = _rope_t(qft[c0 + MLA_NOPE:c0 + rope_end], cost_ref[...], sint_ref[...])
            qt_ref[0, 0, c0:c0 + MLA_NOPE, :] = (qft[c0:c0 + MLA_NOPE] * q_scale).astype(BF16)
            qt_ref[0, 0, c0 + MLA_NOPE:c0 + rope_end, :] = (qr * q_scale).astype(BF16)
            qt_ref[0, 0, c0 + rope_end:c0 + MLA_QK_PAD, :] = qft[c0 + rope_end:c0 + MLA_QK_PAD].astype(BF16)
    else:
        k_ref, vt_ref = out_refs
    ckv = (_rms(z[:, q_rank:q_rank + kv_rank]) * kvg_ref[...]).astype(BF16)
    kn = _dot(ckv, wuk_ref[...])
    vv = _dot_nt(wuvt_ref[...], ckv)
    kr = z[:, q_rank + kv_rank:]
    if use_rope:
        kr = _rope(kr, cos, sin)
    kr = kr.astype(BF16)
    for h in range(n_heads):
        c0 = h * MLA_QK_PAD
        k_ref[0, :, c0:c0 + MLA_NOPE] = kn[:, h * MLA_NOPE:(h + 1) * MLA_NOPE].astype(BF16)
        k_ref[0, :, c0 + MLA_NOPE:c0 + MLA_QK_PAD] = kr
    _store_vt(vt_ref, vv, n_heads)


def _pre1(h, g, sh, sc, w_in, qg, kvg, wuq_t, wuk, wuv_t, cos, sin, cos_t, sin_t, *,
          use_rope, want_q, tm):
    b, l, d = h.shape
    n_heads = wuk.shape[1] // MLA_NOPE
    q_rank, kv_rank = wuq_t.shape[1], wuk.shape[0]
    kern = functools.partial(_pre1_kernel, use_rope=use_rope, want_q=want_q,
                             q_rank=q_rank, kv_rank=kv_rank, n_heads=n_heads)
    full = lambda x: pl.BlockSpec(x.shape, lambda b_, i: (0,) * x.ndim)
    qk_w = n_heads * MLA_QK_PAD
    v_rows = n_heads * V_AUG
    out_specs = [pl.BlockSpec((1, tm, qk_w), lambda b_, i: (b_, i, 0)),
                 pl.BlockSpec((1, 1, v_rows, tm), lambda b_, i: (b_, i, 0, 0))]
    out_shape = [jax.ShapeDtypeStruct((b, l, qk_w), BF16),
                 jax.ShapeDtypeStruct((b, l // tm, v_rows, tm), BF16)]
    if want_q:
        out_specs = [pl.BlockSpec((1, 1, qk_w, tm), lambda b_, i: (b_, i, 0, 0))] + out_specs
        out_shape = [jax.ShapeDtypeStruct((b, l // tm, qk_w, tm), BF16)] + out_shape
    return pl.pallas_call(
        kern,
        grid=(b, l // tm),
        in_specs=[pl.BlockSpec((1, tm, d), lambda b_, i: (b_, i, 0)),
                  pl.BlockSpec((1, d), lambda b_, i: (0, 0)),
                  _bvec_spec(sh, d), _bvec_spec(sc, d),
                  full(w_in), full(qg), full(kvg), full(wuq_t), full(wuk), full(wuv_t),
                  pl.BlockSpec((tm, LANES), lambda b_, i: (i, 0)),
                  pl.BlockSpec((tm, LANES), lambda b_, i: (i, 0)),
                  pl.BlockSpec((MLA_ROPE, tm), lambda b_, i: (0, i)),
                  pl.BlockSpec((MLA_ROPE, tm), lambda b_, i: (0, i))],
        out_specs=out_specs,
        out_shape=out_shape,
        compiler_params=_params(("parallel", "parallel")),
        name="pre1_lat" if want_q else "pre1_ctx",
    )(h, g, sh, sc, w_in, qg, kvg, wuq_t, wuk, wuv_t, cos, sin, cos_t, sin_t)


def _post1_kernel(o_ref, wout_ref, h_ref, g1_ref, n2g_ref, sh2_ref, sc2_ref, rw_ref,
                  h1_ref, a2_ref, gates_ref, sel_ref):
    y = _dot(o_ref[0], wout_ref[...])
    h1 = h_ref[0] + g1_ref[0] * y
    h1_ref[0] = h1
    a2 = _norm_mod(h1, n2g_ref[...], sh2_ref[0], sc2_ref[0])
    a2_ref[0] = a2
    a_hi = a2.astype(BF16)
    a_lo = (a2 - a_hi.astype(F32)).astype(BF16)
    hi_both = _dot(a_hi, rw_ref[...])
    logits = hi_both[:, :LANES] + (_dot(a_lo, rw_ref[:, :LANES]) + hi_both[:, LANES:])
    lane = lax.broadcasted_iota(jnp.int32, logits.shape, 1)
    lg = jnp.where(lane < N_EXPERTS, logits, -jnp.inf)
    m1 = jnp.max(lg, axis=1, keepdims=True)
    i1 = jnp.min(jnp.where(lg == m1, lane, LANES), axis=1, keepdims=True)
    lg2 = jnp.where(lane == i1, -jnp.inf, lg)
    m2 = jnp.max(lg2, axis=1, keepdims=True)
    i2 = jnp.min(jnp.where(lg2 == m2, lane, LANES), axis=1, keepdims=True)
    p2 = jnp.exp(m2 - m1)
    w1 = 1.0 / (1.0 + p2)
    gates_ref[0] = jnp.where(lane == i1, w1, 0.0) + jnp.where(lane == i2, p2 * w1, 0.0)
    sel_ref[0] = jnp.where((lane == i1) | (lane == i2), 1.0, 0.0).astype(BF16)


def _post1(o, w_out, h, g1, n2g, sh2, sc2, rw_split, *, tm):
    b, l, d = h.shape
    return pl.pallas_call(
        _post1_kernel,
        grid=(b, l // tm),
        in_specs=[pl.BlockSpec((1, tm, o.shape[2]), lambda b_, i: (b_, i, 0)),
                  pl.BlockSpec(w_out.shape, lambda b_, i: (0, 0)),
                  pl.BlockSpec((1, tm, d), lambda b_, i: (b_, i, 0)),
                  _bvec_spec(g1, d),
                  pl.BlockSpec((1, d), lambda b_, i: (0, 0)),
                  _bvec_spec(sh2, d), _bvec_spec(sc2, d),
                  pl.BlockSpec(rw_split.shape, lambda b_, i: (0, 0))],
        out_specs=[pl.BlockSpec((1, tm, d), lambda b_, i: (b_, i, 0)),
                   pl.BlockSpec((1, tm, d), lambda b_, i: (b_, i, 0)),
                   pl.BlockSpec((1, tm, LANES), lambda b_, i: (b_, i, 0)),
                   pl.BlockSpec((1, tm, LANES), lambda b_, i: (b_, i, 0))],
        out_shape=[jax.ShapeDtypeStruct((b, l, d), F32),
                   jax.ShapeDtypeStruct((b, l, d), F32),
                   jax.ShapeDtypeStruct((b, l, LANES), F32),
                   jax.ShapeDtypeStruct((b, l, LANES), BF16)],
        compiler_params=_params(("parallel", "parallel")),
        name="post1_router",
    )(o, w_out, h, g1, n2g, sh2, sc2, rw_split)


def _route_kernel(sel_ref, gates_ref, r_ref, tile_ref, ends_ref, cnt_ref, off_ref, *, tile_rows):
    phase = pl.program_id(0)
    i = pl.program_id(1)
    sel = sel_ref[...]
    tr = sel.shape[0]
    col_count = jnp.sum(sel.astype(F32), axis=0, keepdims=True)
    lane = lax.broadcasted_iota(jnp.int32, (1, LANES), 1)

    @pl.when((phase == 0) & (i == 0))
    def _():
        cnt_ref[...] = jnp.zeros_like(cnt_ref)

    @pl.when(phase == 0)
    def _():
        cnt_ref[...] += col_count

    @pl.when((phase == 0) & (i == pl.num_programs(1) - 1))
    def _():
        padded = jnp.ceil(cnt_ref[...] / tile_rows) * tile_rows
        incl = padded
        for sh in (1, 2, 4):
            incl = incl + jnp.where(lane >= sh, pltpu.roll(incl, sh, 1), 0.0)
        off_ref[...] = incl - padded
        ends_ref[...] = incl.astype(jnp.int32)
        tile_start = (lax.broadcasted_iota(jnp.int32, tile_ref.shape, 1) * tile_rows).astype(F32)
        tile_expert = jnp.zeros(tile_ref.shape, jnp.int32)
        for e in range(N_EXPERTS):
            end_e = jnp.sum(jnp.where(lane == e, incl, 0.0), axis=1, keepdims=True)
            tile_expert = tile_expert + (tile_start >= end_e).astype(jnp.int32)
        tile_ref[...] = tile_expert
        cnt_ref[...] = jnp.zeros_like(cnt_ref)

    @pl.when(phase == 1)
    def _():
        rr = lax.broadcasted_iota(jnp.int32, (tr, tr), 0)
        cc = lax.broadcasted_iota(jnp.int32, (tr, tr), 1)
        earlier = jnp.where(cc < rr, 1.0, 0.0).astype(BF16)
        rank = _dot(earlier, sel) + cnt_ref[...]
        cnt_ref[...] += col_count
        pos = off_ref[...] + rank
        chosen = sel > 0
        lo = jnp.min(jnp.where(chosen, pos, 3e38), axis=1, keepdims=True)
        hi = jnp.max(jnp.where(chosen, pos, -1.0), axis=1, keepdims=True)
        g = gates_ref[...]
        w_lo = jnp.sum(jnp.where(chosen & (pos == lo), g, 0.0), axis=1, keepdims=True)
        w_hi = jnp.sum(jnp.where(chosen & (pos == hi), g, 0.0), axis=1, keepdims=True)
        lane_t = lax.broadcasted_iota(jnp.int32, (tr, LANES), 1)
        r_ref[...] = jnp.where(lane_t == 0, lo, jnp.where(lane_t == 1, hi,
                               jnp.where(lane_t == 2, w_lo, jnp.where(lane_t == 3, w_hi, 0.0))))


def _route(sel, gates, *, tr, tile_rows, n_tiles):
    n = sel.shape[0]
    n_tiles_pad = -(-n_tiles // LANES) * LANES
    return pl.pallas_call(
        functools.partial(_route_kernel, tile_rows=tile_rows),
        grid=(2, n // tr),
        in_specs=[pl.BlockSpec((tr, LANES), lambda ph, i: (i, 0)),
                  pl.BlockSpec((tr, LANES), lambda ph, i: (i, 0))],
        out_specs=[pl.BlockSpec((tr, LANES), lambda ph, i: (i * ph, 0)),
                   pl.BlockSpec((1, n_tiles_pad), lambda ph, i: (0, 0)),
                   pl.BlockSpec((1, LANES), lambda ph, i: (0, 0))],
        out_shape=[jax.ShapeDtypeStruct((n, LANES), F32),
                   jax.ShapeDtypeStruct((1, n_tiles_pad), jnp.int32),
                   jax.ShapeDtypeStruct((1, LANES), jnp.int32)],
        scratch_shapes=[pltpu.VMEM((1, LANES), F32), pltpu.VMEM((1, LANES), F32)],
        compiler_params=_params(("arbitrary", "arbitrary")),
        name="moe_route",
    )(sel, gates)


def _dispatch_kernel(ends_ref, dest_ref, x_ref, xs_hbm, zero_ref, sem, *, tm, tile_rows):
    n_tiles = xs_hbm.shape[0] // tile_rows

    def zero_tile(start):
        return pltpu.make_async_copy(zero_ref, xs_hbm.at[pl.ds(start, tile_rows)], sem)

    def group_tail(e):
        return zero_tile(pl.multiple_of(ends_ref[e] - tile_rows, tile_rows))

    @pl.when(pl.program_id(0) == 0)
    def _():
        zero_ref[...] = jnp.zeros_like(zero_ref)
        last_end = ends_ref[N_EXPERTS - 1]
        for e in range(N_EXPERTS):
            pl.when(ends_ref[e] >= tile_rows)(lambda e=e: group_tail(e).start())
        for j in range(n_tiles - N_EXPERTS, n_tiles):
            pl.when(j * tile_rows >= last_end)(lambda j=j: zero_tile(j * tile_rows).start())
        for e in range(N_EXPERTS):
            pl.when(ends_ref[e] >= tile_rows)(lambda e=e: group_tail(e).wait())
        for j in range(n_tiles - N_EXPERTS, n_tiles):
            pl.when(j * tile_rows >= last_end)(lambda j=j: zero_tile(j * tile_rows).wait())

    def row_copy(src_row, dst_row):
        return pltpu.make_async_copy(x_ref.at[pl.ds(src_row, 1)], xs_hbm.at[pl.ds(dst_row, 1)], sem)

    def issue(r, carry):
        row_copy(r, dest_ref[0, 0, r]).start()
        row_copy(r, dest_ref[0, 0, tm + r]).start()
        return carry

    lax.fori_loop(0, tm, issue, 0, unroll=DMA_ISSUE_UNROLL)
    pltpu.make_async_copy(x_ref, xs_hbm.at[pl.ds(0, tm)], sem).wait()
    pltpu.make_async_copy(x_ref, xs_hbm.at[pl.ds(0, tm)], sem).wait()


def _dispatch(group_ends, dest, x, n_rows, *, tm, tile_rows):
    n, d = x.shape
    return pl.pallas_call(
        functools.partial(_dispatch_kernel, tm=tm, tile_rows=tile_rows),
        grid_spec=pltpu.PrefetchScalarGridSpec(
            num_scalar_prefetch=1,
            grid=(n // tm,),
            in_specs=[pl.BlockSpec((1, 1, 2 * tm), lambda i, ends: (i, 0, 0), memory_space=pltpu.SMEM),
                      pl.BlockSpec((tm, d), lambda i, ends: (i, 0))],
            out_specs=pl.BlockSpec(memory_space=pl.ANY),
            scratch_shapes=[pltpu.VMEM((tile_rows, d), x.dtype), pltpu.SemaphoreType.DMA(())]),
        out_shape=jax.ShapeDtypeStruct((n_rows, d), x.dtype),
        compiler_params=pltpu.CompilerParams(dimension_semantics=("arbitrary",),
                                             has_side_effects=True),
        name="moe_dispatch",
    )(group_ends, dest, x)


def _experts_kernel(te_ref, x_ref, wg_ref, wu_ref, wd_ref, o_ref):
    valid = te_ref[pl.program_id(0)] < N_EXPERTS

    @pl.when(valid)
    def _():
        o_ref[...] = _swiglu_tile(x_ref[...].astype(BF16), wg_ref, wu_ref, wd_ref, lead=(0,))

    @pl.when(jnp.logical_not(valid))
    def _():
        o_ref[...] = jnp.zeros_like(o_ref)


def _experts(tile_expert, xs, wg, wu, wd, *, tile_rows):
    n_rows, d = xs.shape
    ne, _, f = wg.shape
    w_idx = lambda j, te: (jnp.minimum(te[j], ne - 1), 0, 0)
    return pl.pallas_call(
        _experts_kernel,
        grid_spec=pltpu.PrefetchScalarGridSpec(
            num_scalar_prefetch=1,
            grid=(n_rows // tile_rows,),
            in_specs=[pl.BlockSpec((tile_rows, d), lambda j, te: (j, 0)),
                      pl.BlockSpec((1, d, f), w_idx),
                      pl.BlockSpec((1, d, f), w_idx),
                      pl.BlockSpec((1, f, d), w_idx)],
            out_specs=pl.BlockSpec((tile_rows, d), lambda j, te: (j, 0))),
        out_shape=jax.ShapeDtypeStruct((n_rows, d), F32),
        compiler_params=_params(("arbitrary",)),
        name="moe_experts",
    )(tile_expert, xs, wg, wu, wd)


def _combine_kernel(dest_ref, routed_ref, ys_hbm, h1_ref, g2_ref, fng_ref, o_ref, ylo_ref, yhi_ref,
                    sem, *, tm):
    def issue(r, carry):
        pltpu.make_async_copy(ys_hbm.at[pl.ds(dest_ref[0, 0, r], 1)],
                              ylo_ref.at[pl.ds(r, 1)], sem).start()
        pltpu.make_async_copy(ys_hbm.at[pl.ds(dest_ref[0, 0, tm + r], 1)],
                              yhi_ref.at[pl.ds(r, 1)], sem).start()
        return carry

    lax.fori_loop(0, tm, issue, 0, unroll=DMA_ISSUE_UNROLL)
    pltpu.make_async_copy(ys_hbm.at[pl.ds(0, tm)], ylo_ref, sem).wait()
    pltpu.make_async_copy(ys_hbm.at[pl.ds(0, tm)], yhi_ref, sem).wait()
    routed = routed_ref[...]
    y = routed[:, 2:3] * ylo_ref[...] + routed[:, 3:4] * yhi_ref[...]
    out = h1_ref[...] + g2_ref[0] * y
    o_ref[...] = _rms(out) * fng_ref[...]


def _combine(dest, routed, ys, h1, g2, fng, *, tm, tiles_per_sample):
    n, d = h1.shape
    return pl.pallas_call(
        functools.partial(_combine_kernel, tm=tm),
        grid=(n // tm,),
        in_specs=[pl.BlockSpec((1, 1, 2 * tm), lambda i: (i, 0, 0), memory_space=pltpu.SMEM),
                  pl.BlockSpec((tm, LANES), lambda i: (i, 0)),
                  pl.BlockSpec(memory_space=pl.ANY),
                  pl.BlockSpec((tm, d), lambda i: (i, 0)),
                  pl.BlockSpec((1, 1, d), lambda i: (i // tiles_per_sample, 0, 0)),
                  pl.BlockSpec((1, d), lambda i: (0, 0))],
        out_specs=pl.BlockSpec((tm, d), lambda i: (i, 0)),
        out_shape=jax.ShapeDtypeStruct((n, d), F32),
        scratch_shapes=[pltpu.VMEM((tm, d), F32), pltpu.VMEM((tm, d), F32),
                        pltpu.SemaphoreType.DMA(())],
        compiler_params=_params(("arbitrary",)),
        name="moe_combine",
    )(dest, routed, ys, h1, g2, fng)


def _moe(a2, gates, sel, wg, wu, wd, h1, g2, fng, *, tm):
    b, l, d = h1.shape
    n = b * l
    n_rows = 2 * n + N_EXPERTS * MOE_TILE
    n_tiles = n_rows // MOE_TILE
    routed, tile_expert, group_ends = _route(sel.reshape(n, LANES), gates.reshape(n, LANES),
                                             tr=tm, tile_rows=MOE_TILE, n_tiles=n_tiles)
    tc = min(MOE_COPY_TILE, l)
    dest = routed[:, 0:2].astype(jnp.int32).reshape(n // tc, tc, 2)
    dest = dest.transpose(0, 2, 1).reshape(n // tc, 1, 2 * tc)
    xs = _dispatch(group_ends[0, :N_EXPERTS], dest, a2.reshape(n, d), n_rows, tm=tc, tile_rows=MOE_TILE)
    ys = _experts(tile_expert[0, :n_tiles], xs, wg, wu, wd, tile_rows=MOE_TILE)
    out = _combine(dest, routed, ys, h1.reshape(n, d), g2, fng, tm=tc, tiles_per_sample=l // tc)
    return out.reshape(b, l, d)


def _rope_tables(seq_len, dim):
    t = jnp.arange(seq_len, dtype=jnp.int32)
    row = (t // GRID_W).astype(F32)
    col = (t % GRID_W).astype(F32)
    quarter = dim // 4
    inv = ROPE_BASE ** (-jnp.arange(quarter, dtype=F32) / quarter)
    ar = row[:, None] * inv[None, :]
    ac = col[:, None] * inv[None, :]
    cos = jnp.concatenate([jnp.cos(ar), jnp.cos(ar), jnp.cos(ac), jnp.cos(ac)], axis=-1)
    sin = jnp.concatenate([jnp.sin(ar), jnp.sin(ar), jnp.sin(ac), jnp.sin(ac)], axis=-1)
    reps = LANES // dim
    return jnp.tile(cos, (1, reps)), jnp.tile(sin, (1, reps))


def kernel(x, c, ctx, c_ctx, norm1_g, norm2_g, mod_w, mod_b, even_w_in, pool_w, pool_scale,
           lambda_q1, lambda_k1, lambda_q2, lambda_k2, even_w_out, ffn_w_gate, ffn_w_up,
           ffn_w_down, odd_w_in, q_norm_g, kv_norm_g, w_uq, w_ukv, odd_w_out, router_w,
           moe_w_gate, moe_w_up, moe_w_down, final_norm_g):
    b, l, d = x.shape
    n_ctx = ctx.shape[1]
    tm = min(ROW_TILE, l)
    tkv = min(KV_TILE, l)
    tq = min(DIFF_Q_TILE, l)
    cos, sin = _rope_tables(l, DIFF_HEAD_DIM)
    cos_t, sin_t = cos[:, :DIFF_HEAD_DIM].T, sin[:, :DIFF_HEAD_DIM].T
    cos_c, sin_c, cos_tc, sin_tc = cos[:n_ctx], sin[:n_ctx], cos_t[:, :n_ctx], sin_t[:, :n_ctx]

    cond = jnp.zeros((8, d), F32).at[:b].set(c).at[b].set(c_ctx)
    mod = _modulation(cond, mod_w, mod_b)
    mod = mod.reshape(mod.shape[0], 8, N_MOD, d)

    def mod_vecs(layer):
        lat = [mod[layer, :b, k][:, None, :] for k in range(N_MOD)]
        cx = [mod[layer, b:b + 1, k][:, None, :] for k in range(N_MOD)]
        return lat, cx

    (sh1, sc1, g1, sh2, sc2, g2), (csh1, csc1, cg1, csh2, csc2, cg2) = mod_vecs(0)
    lam_init = 0.8 - 0.6 * math.exp(-0.3 * 0)
    w_in0 = even_w_in[0]
    qk_w0 = (w_in0.shape[1] - POOL_WIDTH) // 3
    k0, v0 = POOL_WIDTH + qk_w0, POOL_WIDTH + 2 * qk_w0
    w_nat0 = jnp.concatenate([w_in0[:, :POOL_WIDTH], w_in0[:, k0:v0]], axis=1).astype(BF16)
    w_t0 = jnp.concatenate([w_in0[:, POOL_WIDTH:k0], w_in0[:, v0:]], axis=1).T.astype(BF16)
    n1g = norm1_g[0][None, :]
    n2g = norm2_g[0][None, :]
    p_lat, q1t_lat, q2t_lat, k_lat, vt_lat = _pre0(x, n1g, sh1, sc1, w_nat0, w_t0, cos, sin,
                                                   cos_t, sin_t, use_rope=True, tm=tkv)
    p_ctx, q1t_ctx, q2t_ctx, k_ctx, vt_ctx = _pre0(ctx, n1g, csh1, csc1, w_nat0, w_t0, cos_c, sin_c,
                                                   cos_tc, sin_tc, use_rope=False, tm=n_ctx)
    lq = jnp.stack([lambda_q1[0], lambda_q2[0]])
    lk = jnp.stack([lambda_k1[0], lambda_k2[0]])
    diff_kern = functools.partial(_diff_attn_kernel, lam_init=lam_init)
    attn_lat = _attention(diff_kern, (q1t_lat, q2t_lat), (k_ctx, vt_ctx), (k_lat, vt_lat),
                          2 * DIFF_HEAD_DIM, extra=(lq, lk), tq=tq,
                          n_sub=min(DIFF_Q_SWEEPS, l // tq), name="diff_attn_lat")
    attn_ctx = _attention(diff_kern, (q1t_ctx, q2t_ctx), (k_ctx, vt_ctx), None,
                          2 * DIFF_HEAD_DIM, extra=(lq, lk), tq=n_ctx, n_sub=1, name="diff_attn_ctx")
    ng = len(POOL_WINDOWS)
    poolw_bd = jnp.zeros((POOL_WIDTH, POOL_WIDTH), F32)
    for g in range(ng):
        sl = slice(g * POOL_GROUP, (g + 1) * POOL_GROUP)
        poolw_bd = poolw_bd.at[sl, sl].set(pool_w[0, g])
    poolw_bd = poolw_bd.astype(BF16)
    pscale = pool_scale[0][None, :]
    w_out0 = even_w_out[0].astype(BF16)
    wg0, wu0, wd0 = ffn_w_gate[0].astype(BF16), ffn_w_up[0].astype(BF16), ffn_w_down[0].astype(BF16)
    h_lat = _post0(attn_lat, p_lat, poolw_bd, pscale, w_out0, x, g1, n2g, sh2, sc2, g2,
                   wg0, wu0, wd0, tm=tm)
    h_ctx = _post0(attn_ctx, p_ctx, poolw_bd, pscale, w_out0, ctx, cg1, n2g, csh2, csc2, cg2,
                   wg0, wu0, wd0, tm=n_ctx)

    (sh1, sc1, g1, sh2, sc2, g2), (csh1, csc1, _, _, _, _) = mod_vecs(1)
    n1g = norm1_g[1][None, :]
    n2g = norm2_g[1][None, :]
    q_rank, kv_rank = w_uq.shape[1], w_ukv.shape[1]
    n_heads = w_uq.shape[2] // (MLA_NOPE + MLA_ROPE)
    w_in1 = jnp.pad(odd_w_in[0], ((0, 0), (0, LANES - MLA_ROPE))).astype(BF16)
    wuq = w_uq[0].reshape(q_rank, n_heads, MLA_NOPE + MLA_ROPE)
    wuq = jnp.pad(wuq, ((0, 0), (0, 0), (0, MLA_QK_PAD - MLA_NOPE - MLA_ROPE)))
    wuq_t = wuq.reshape(q_rank, n_heads * MLA_QK_PAD).T.astype(BF16)
    wukv = w_ukv[0].reshape(kv_rank, n_heads, MLA_NOPE + MLA_V)
    wuk = wukv[:, :, :MLA_NOPE].reshape(kv_rank, n_heads * MLA_NOPE).astype(BF16)
    wuv_t = wukv[:, :, MLA_NOPE:].reshape(kv_rank, n_heads * MLA_V).T.astype(BF16)
    qg = q_norm_g[0][None, :]
    kvg = kv_norm_g[0][None, :]
    q1t, k1, vt1 = _pre1(h_lat, n1g, sh1, sc1, w_in1, qg, kvg, wuq_t, wuk, wuv_t, cos, sin,
                         cos_t, sin_t, use_rope=True, want_q=True, tm=tkv)
    k1c, vt1c = _pre1(h_ctx, n1g, csh1, csc1, w_in1, qg, kvg, wuq_t, wuk, wuv_t, cos_c, sin_c,
                      cos_tc, sin_tc, use_rope=False, want_q=False, tm=n_ctx)
    tq1 = min(MLA_Q_TILE, l)
    o1 = _attention(_mla_attn_kernel, (q1t,), (k1c, vt1c), (k1, vt1), MLA_QK_PAD,
                    tq=tq1, n_sub=min(MLA_Q_SWEEPS, l // tq1), name="mla_attn")
    rw_pad = jnp.pad(router_w[0], ((0, 0), (0, LANES - N_EXPERTS)))
    rw_hi = rw_pad.astype(BF16)
    rw_lo = (rw_pad - rw_hi.astype(F32)).astype(BF16)
    rw_split = jnp.concatenate([rw_hi, rw_lo], axis=1)
    h1, a2, gates, sel = _post1(o1, odd_w_out[0].astype(BF16), h_lat, g1, n2g, sh2, sc2,
                                rw_split, tm=tm)
    return _moe(a2, gates, sel, moe_w_gate[0].astype(BF16), moe_w_up[0].astype(BF16),
                moe_w_down[0].astype(BF16), h1, g2, final_norm_g[None, :], tm=tm)
```

```python
import functools
import math

import jax
import jax.numpy as jnp
from jax import lax
from jax.experimental import pallas as pl
from jax.experimental.pallas import tpu as pltpu

F32 = jnp.float32
BF16 = jnp.bfloat16

EPS = 1e-6
GRID_W = 64
ROPE_BASE = 10000.0
N_MOD = 6
POOL_WINDOWS = (2, 4, 8, 16)
POOL_GROUP = 64
POOL_WIDTH = POOL_GROUP * len(POOL_WINDOWS)
POOL_HALO = max(POOL_WINDOWS) // 2
DIFF_HEAD_DIM = 64
DIFF_V_DIM = 128
V_ONES_ROWS = 16
V_AUG = DIFF_V_DIM + V_ONES_ROWS
MLA_NOPE = 128
MLA_ROPE = 64
MLA_V = 128
MLA_QK_PAD = 256
N_EXPERTS = 8
ROPE_QUARTER = 16
LOG2E = math.log2(math.e)

LANES = 128
ROW_TILE = 512
KV_TILE = 512
DIFF_Q_TILE = 512
MLA_Q_TILE = 1024
DIFF_Q_SWEEPS = 8
MLA_Q_SWEEPS = 4
PAIR_UNROLL = 5
MOE_TILE = 512
MOE_COPY_TILE = 1024
DMA_ISSUE_UNROLL = 8
FFN_CHUNKS = (512, 512, 512, 512, 512, 256)
VMEM_LIMIT = 56 * 1024 * 1024


def _params(sem):
    return pltpu.CompilerParams(dimension_semantics=sem, vmem_limit_bytes=VMEM_LIMIT)


def _dot(a, b):
    return jnp.dot(a, b, preferred_element_type=F32)


def _rms(x):
    return x * lax.rsqrt(jnp.mean(x * x, axis=-1, keepdims=True) + EPS)


def _silu(x):
    return x * (1.0 / (1.0 + jnp.exp(-x)))


def _rope(x, cos, sin):
    lane = lax.broadcasted_iota(jnp.int32, x.shape, 1)
    even = (lane // ROPE_QUARTER) % 2 == 0
    rot = jnp.where(even, -pltpu.roll(x, LANES - ROPE_QUARTER, 1), pltpu.roll(x, ROPE_QUARTER, 1))
    return x * cos + rot * sin


def _dot_nt(a, b):
    return lax.dot_general(a, b, (((1,), (1,)), ((), ())), preferred_element_type=F32)


def _rope_t(x, cos_t, sin_t):
    q = ROPE_QUARTER
    rot = jnp.concatenate([-x[q:2 * q], x[:q], -x[3 * q:], x[2 * q:3 * q]], axis=0)
    return x * cos_t + rot * sin_t


def _store_vt(vt_ref, vt, n_heads):
    ones = jnp.ones((V_ONES_ROWS, vt.shape[1]), BF16)
    for h in range(n_heads):
        r0 = h * V_AUG
        vt_ref[0, 0, r0:r0 + DIFF_V_DIM, :] = vt[h * DIFF_V_DIM:(h + 1) * DIFF_V_DIM].astype(BF16)
        vt_ref[0, 0, r0 + DIFF_V_DIM:r0 + V_AUG, :] = ones


def _mod_kernel(cond_ref, w_ref, b_ref, o_ref):
    s = _silu(cond_ref[...])
    o_ref[0] = jnp.dot(s, w_ref[0], preferred_element_type=F32,
                       precision=lax.Precision.HIGHEST) + b_ref[0]


def _modulation(cond, mod_w, mod_b):
    depth, d, n = mod_w.shape
    tn = n // 4
    return pl.pallas_call(
        _mod_kernel,
        grid=(depth, n // tn),
        in_specs=[pl.BlockSpec((8, d), lambda i, j: (0, 0)),
                  pl.BlockSpec((1, d, tn), lambda i, j: (i, 0, j)),
                  pl.BlockSpec((1, 1, tn), lambda i, j: (i, 0, j))],
        out_specs=pl.BlockSpec((1, 8, tn), lambda i, j: (i, 0, j)),
        out_shape=jax.ShapeDtypeStruct((depth, 8, n), F32),
        compiler_params=_params(("arbitrary", "arbitrary")),
        name="modulation",
    )(cond, mod_w, mod_b.reshape(depth, 1, n))


def _bvec_spec(arr, d):
    if arr.shape[0] == 1:
        return pl.BlockSpec((1, 1, d), lambda b, i: (0, 0, 0))
    return pl.BlockSpec((1, 1, d), lambda b, i: (b, 0, 0))


def _norm_mod(h, g, sh, sc):
    a = _rms(h) * g
    return a * (1.0 + sc) + sh


def _pre0_kernel(h_ref, g_ref, sh_ref, sc_ref, wn_ref, wt_ref, cos_ref, sin_ref, cost_ref, sint_ref,
                 p_ref, q1t_ref, q2t_ref, k_ref, vt_ref, *, use_rope, n_heads):
    a = _norm_mod(h_ref[0], g_ref[...], sh_ref[0], sc_ref[0]).astype(BF16)
    z = _dot(a, wn_ref[...])
    zt = _dot_nt(wt_ref[...], a)
    p_ref[0] = z[:, :POOL_WIDTH]
    head_w = 2 * DIFF_HEAD_DIM
    qk_w = n_heads * head_w
    q_scale = DIFF_HEAD_DIM ** -0.5 * LOG2E
    cos, sin = cos_ref[...], sin_ref[...]
    cos_t, sin_t = cost_ref[...], sint_ref[...]
    for h in range(n_heads):
        xk = z[:, POOL_WIDTH + h * head_w:POOL_WIDTH + (h + 1) * head_w]
        xq1 = zt[h * head_w:h * head_w + DIFF_HEAD_DIM]
        xq2 = zt[h * head_w + DIFF_HEAD_DIM:(h + 1) * head_w]
        if use_rope:
            xk = _rope(xk, cos, sin)
            xq1 = _rope_t(xq1, cos_t, sin_t)
            xq2 = _rope_t(xq2, cos_t, sin_t)
        k_ref[0, :, h * head_w:(h + 1) * head_w] = xk.astype(BF16)
        xq1 = (xq1 * q_scale).astype(BF16)
        xq2 = (xq2 * q_scale).astype(BF16)
        zero = jnp.zeros_like(xq1)
        q1t_ref[0, 0, h * head_w:(h + 1) * head_w, :] = jnp.concatenate([xq1, zero], axis=0)
        q2t_ref[0, 0, h * head_w:(h + 1) * head_w, :] = jnp.concatenate([zero, xq2], axis=0)
    _store_vt(vt_ref, zt[qk_w:], n_heads)


def _pre0(h, g, sh, sc, w_nat, w_t, cos, sin, cos_t, sin_t, *, use_rope, tm):
    b, l, d = h.shape
    v_w = w_t.shape[0] // 2
    n_heads = v_w // DIFF_V_DIM
    kern = functools.partial(_pre0_kernel, use_rope=use_rope, n_heads=n_heads)
    return pl.pallas_call(
        kern,
        grid=(b, l // tm),
        in_specs=[pl.BlockSpec((1, tm, d), lambda b_, i: (b_, i, 0)),
                  pl.BlockSpec((1, d), lambda b_, i: (0, 0)),
                  _bvec_spec(sh, d), _bvec_spec(sc, d),
                  pl.BlockSpec(w_nat.shape, lambda b_, i: (0, 0)),
                  pl.BlockSpec(w_t.shape, lambda b_, i: (0, 0)),
                  pl.BlockSpec((tm, LANES), lambda b_, i: (i, 0)),
                  pl.BlockSpec((tm, LANES), lambda b_, i: (i, 0)),
                  pl.BlockSpec((DIFF_HEAD_DIM, tm), lambda b_, i: (0, i)),
                  pl.BlockSpec((DIFF_HEAD_DIM, tm), lambda b_, i: (0, i))],
        out_specs=[pl.BlockSpec((1, tm, POOL_WIDTH), lambda b_, i: (b_, i, 0)),
                   pl.BlockSpec((1, 1, v_w, tm), lambda b_, i: (b_, i, 0, 0)),
                   pl.BlockSpec((1, 1, v_w, tm), lambda b_, i: (b_, i, 0, 0)),
                   pl.BlockSpec((1, tm, v_w), lambda b_, i: (b_, i, 0)),
                   pl.BlockSpec((1, 1, n_heads * V_AUG, tm), lambda b_, i: (b_, i, 0, 0))],
        out_shape=[jax.ShapeDtypeStruct((b, l, POOL_WIDTH), F32),
                   jax.ShapeDtypeStruct((b, l // tm, v_w, tm), BF16),
                   jax.ShapeDtypeStruct((b, l // tm, v_w, tm), BF16),
                   jax.ShapeDtypeStruct((b, l, v_w), BF16),
                   jax.ShapeDtypeStruct((b, l // tm, n_heads * V_AUG, tm), BF16)],
        compiler_params=_params(("parallel", "parallel")),
        name="pre0_rope" if use_rope else "pre0_ctx",
    )(h, g, sh, sc, w_nat, w_t, cos, sin, cos_t, sin_t)


def _attn_core(qt_refs, j, tq, kc_ref, vc_ref, lat_refs, acc_refs, sbuf_refs):
    n = len(qt_refs)
    sub = tq // qt_refs[0].shape[3]
    qts = [jnp.concatenate([r[0, j * sub + u] for u in range(sub)], axis=1) for r in qt_refs]
    ctx_bufs = sbuf_refs[:n]
    lat_bufs = sbuf_refs[n:]
    ms = tuple(jnp.full((1, tq), -jnp.inf, F32) for _ in range(n))
    for i in range(n):
        acc_refs[i][...] = jnp.zeros_like(acc_refs[i])

    def scores(k, bufs):
        cmax = []
        for i in range(n):
            s = _dot(k, qts[i])
            bufs[i][...] = s
            cmax.append(jnp.max(s, axis=0, keepdims=True))
        return tuple(cmax)

    def softmax_pv(vt, bufs, cmax, ms):
        new_m = []
        for i in range(n):
            m_new = jnp.maximum(ms[i], cmax[i])
            alpha = jnp.exp2(ms[i] - m_new)
            p = jnp.exp2(bufs[i][...] - m_new)
            acc_refs[i][...] = alpha * acc_refs[i][...] + _dot(vt, p.astype(BF16))
            new_m.append(m_new)
        return tuple(new_m)

    if lat_refs is None:
        softmax_pv(vc_ref[0, 0], ctx_bufs, scores(kc_ref[0], ctx_bufs), ms)
        return
    kl_ref, vl_ref = lat_refs
    nblk, tk = vl_ref.shape[1], vl_ref.shape[3]
    assert nblk >= 2 and nblk % 2 == 0

    def qk(t, which):
        start = t * tk if isinstance(t, int) else pl.multiple_of(t * tk, tk)
        return scores(kl_ref[0, pl.ds(start, tk), :], lat_bufs[which::2])

    def pair(jj, carry):
        cmax_a, ms = carry
        t = 2 * jj
        cmax_b = qk(t + 1, 1)
        ms = softmax_pv(vl_ref[0, t], lat_bufs[0::2], cmax_a, ms)
        cmax_a = qk(t + 2, 0)
        ms = softmax_pv(vl_ref[0, t + 1], lat_bufs[1::2], cmax_b, ms)
        return cmax_a, ms

    cmax_a = qk(0, 0)
    n_pairs = nblk // 2 - 1
    cmax_a, ms = lax.fori_loop(0, n_pairs, pair, (cmax_a, ms),
                               unroll=max(1, min(PAIR_UNROLL, n_pairs)))
    cmax_b = qk(nblk - 1, 1)
    ms = softmax_pv(vl_ref[0, nblk - 2], lat_bufs[0::2], cmax_a, ms)
    cmax_c = scores(kc_ref[0], ctx_bufs)
    ms = softmax_pv(vl_ref[0, nblk - 1], lat_bufs[1::2], cmax_b, ms)
    softmax_pv(vc_ref[0, 0], ctx_bufs, cmax_c, ms)


def _normalised(acc_ref):
    acc = acc_ref[...]
    return acc[:DIFF_V_DIM] * (1.0 / acc[DIFF_V_DIM:DIFF_V_DIM + 1])


def _for_each_query_tile(o_ref, tq, tile_fn):
    n_tiles = o_ref.shape[1] // tq

    def body(j, carry):
        rows = pl.ds(pl.multiple_of(j * tq, tq), tq)
        o_ref[0, rows, :] = tile_fn(j)
        return carry

    if n_tiles == 1:
        o_ref[0] = tile_fn(0)
    else:
        lax.fori_loop(0, n_tiles, body, 0)


def _diff_attn_kernel(lq_ref, lk_ref, q1t_ref, q2t_ref, kc_ref, vc_ref, *rest, has_lat, lam_init, tq):
    lat_refs = rest[:2] if has_lat else None
    rest = rest[2:] if has_lat else rest
    o_ref, acc1_ref, acc2_ref = rest[:3]
    e = jnp.exp(jnp.sum(lq_ref[...] * lk_ref[...], axis=-1, keepdims=True))
    lam = e[0:1] - e[1:2] + lam_init

    def tile(j):
        _attn_core((q1t_ref, q2t_ref), j, tq, kc_ref, vc_ref, lat_refs, (acc1_ref, acc2_ref), rest[3:])
        o = _normalised(acc1_ref) - lam * _normalised(acc2_ref)
        ms = jnp.mean(o * o, axis=0, keepdims=True)
        y = o * (lax.rsqrt(ms + EPS) * (1.0 - lam_init))
        return y.T.astype(BF16)

    _for_each_query_tile(o_ref, tq, tile)


def _mla_attn_kernel(qt_ref, kc_ref, vc_ref, *rest, has_lat, tq):
    lat_refs = rest[:2] if has_lat else None
    rest = rest[2:] if has_lat else rest
    o_ref, acc_ref = rest[:2]

    def tile(j):
        _attn_core((qt_ref,), j, tq, kc_ref, vc_ref, lat_refs, (acc_ref,), rest[2:])
        return _normalised(acc_ref).T.astype(BF16)

    _for_each_query_tile(o_ref, tq, tile)


def _attention(kern, qts, ctx_kv, lat_kv, qk_w, extra=(), *, tq, n_sub, name):
    n_maps = len(qts)
    b, nqb, hw, tmq = qts[0].shape
    n_heads = hw // qk_w
    lq = nqb * tmq
    rows = n_sub * tq
    in_specs = [pl.BlockSpec(x.shape, lambda b_, h, i: (0, 0)) for x in extra]
    in_specs += [pl.BlockSpec((1, rows // tmq, qk_w, tmq), lambda b_, h, i: (b_, i, h, 0)) for _ in qts]
    args = list(extra) + list(qts)
    scratch = [pltpu.VMEM((V_AUG, tq), F32) for _ in range(n_maps)]
    scratch += [pltpu.VMEM((ctx_kv[0].shape[1], tq), F32) for _ in range(n_maps)]
    for kv in (ctx_kv, lat_kv):
        if kv is None:
            continue
        k, vt = kv
        nblk, tk = vt.shape[1], vt.shape[3]
        in_specs.append(pl.BlockSpec((1, k.shape[1], qk_w), lambda b_, h, i: (b_, 0, h)))
        in_specs.append(pl.BlockSpec((1, nblk, V_AUG, tk), lambda b_, h, i: (b_, 0, h, 0)))
        args += [k, vt]
    if lat_kv is not None:
        scratch += [pltpu.VMEM((lat_kv[1].shape[3], tq), F32) for _ in range(2 * n_maps)]
    return pl.pallas_call(
        functools.partial(kern, has_lat=lat_kv is not None, tq=tq),
        grid=(b, n_heads, lq // rows),
        in_specs=in_specs,
        out_specs=pl.BlockSpec((1, rows, DIFF_V_DIM), lambda b_, h, i: (b_, i, h)),
        out_shape=jax.ShapeDtypeStruct((b, lq, n_heads * DIFF_V_DIM), BF16),
        scratch_shapes=scratch,
        compiler_params=_params(("parallel", "parallel", "arbitrary")),
        name=name,
    )(*args)


def _post0_kernel(attn_ref, p_ref, pprev_ref, pnext_ref, poolw_ref, pscale_ref, wout_ref,
                  h_ref, g1_ref, n2g_ref, sh2_ref, sc2_ref, g2_ref, wg_ref, wu_ref, wd_ref,
                  o_ref, *, seq_len):
    i = pl.program_id(1)
    u = p_ref[0]
    tm = u.shape[0]
    ext = jnp.concatenate([pprev_ref[0], u, pnext_ref[0]], axis=0)
    row = i * tm - POOL_HALO + lax.broadcasted_iota(jnp.int32, (tm + 2 * POOL_HALO, 1), 0)
    ext = jnp.where((row >= 0) & (row < seq_len), ext, 0.0)

    def shifted(j):
        return ext[POOL_HALO + j:POOL_HALO + j + tm]

    t = i * tm + lax.broadcasted_iota(jnp.int32, (tm, 1), 0)
    lane = lax.broadcasted_iota(jnp.int32, (tm, POOL_WIDTH), 1)
    win_sum = shifted(-1) + shifted(0)
    pooled = jnp.zeros_like(u)
    for g, w in enumerate(POOL_WINDOWS):
        half = w // 2
        if g > 0:
            prev_half = POOL_WINDOWS[g - 1] // 2
            for j in range(prev_half, half):
                win_sum = win_sum + shifted(-j - 1) + shifted(j)
        cnt = (jnp.minimum(t + half, seq_len) - jnp.maximum(t - half, 0)).astype(F32)
        in_group = (lane >= g * POOL_GROUP) & (lane < (g + 1) * POOL_GROUP)
        pooled = jnp.where(in_group, win_sum / cnt, pooled)
    pooled = pooled - u
    y_pool = _dot(pooled.astype(BF16), poolw_ref[...]) * pscale_ref[...]
    y = _dot(y_pool.astype(BF16), wout_ref[:POOL_WIDTH, :]) + _dot(attn_ref[0], wout_ref[POOL_WIDTH:, :])
    h1 = h_ref[0] + g1_ref[0] * y
    a2 = _norm_mod(h1, n2g_ref[...], sh2_ref[0], sc2_ref[0]).astype(BF16)
    o_ref[0] = h1 + g2_ref[0] * _swiglu_tile(a2, wg_ref, wu_ref, wd_ref)


def _post0(attn, p, poolw_bd, pscale, w_out, h, g1, n2g, sh2, sc2, g2, wg, wu, wd, *, tm):
    b, l, d = h.shape
    hb = tm // POOL_HALO
    nhb = l // POOL_HALO
    assert sum(FFN_CHUNKS) == wg.shape[1]
    return pl.pallas_call(
        functools.partial(_post0_kernel, seq_len=l),
        grid=(b, l // tm),
        in_specs=[pl.BlockSpec((1, tm, attn.shape[2]), lambda b_, i: (b_, i, 0)),
                  pl.BlockSpec((1, tm, POOL_WIDTH), lambda b_, i: (b_, i, 0)),
                  pl.BlockSpec((1, POOL_HALO, POOL_WIDTH),
                               lambda b_, i: (b_, jnp.maximum(i * hb - 1, 0), 0)),
                  pl.BlockSpec((1, POOL_HALO, POOL_WIDTH),
                               lambda b_, i: (b_, jnp.minimum((i + 1) * hb, nhb - 1), 0)),
                  pl.BlockSpec(poolw_bd.shape, lambda b_, i: (0, 0)),
                  pl.BlockSpec((1, POOL_WIDTH), lambda b_, i: (0, 0)),
                  pl.BlockSpec(w_out.shape, lambda b_, i: (0, 0)),
                  pl.BlockSpec((1, tm, d), lambda b_, i: (b_, i, 0)),
                  _bvec_spec(g1, d),
                  pl.BlockSpec((1, d), lambda b_, i: (0, 0)),
                  _bvec_spec(sh2, d), _bvec_spec(sc2, d), _bvec_spec(g2, d),
                  pl.BlockSpec(wg.shape, lambda b_, i: (0, 0)),
                  pl.BlockSpec(wu.shape, lambda b_, i: (0, 0)),
                  pl.BlockSpec(wd.shape, lambda b_, i: (0, 0))],
        out_specs=pl.BlockSpec((1, tm, d), lambda b_, i: (b_, i, 0)),
        out_shape=jax.ShapeDtypeStruct((b, l, d), F32),
        compiler_params=_params(("parallel", "parallel")),
        name="post0_ffn",
    )(attn, p, p, p, poolw_bd, pscale, w_out, h, g1, n2g, sh2, sc2, g2, wg, wu, wd)


def _swiglu_tile(x, wg_ref, wu_ref, wd_ref, lead=()):
    acc = None
    c0 = 0
    for cw in FFN_CHUNKS:
        hg = _dot(x, wg_ref[lead + (slice(None), slice(c0, c0 + cw))])
        hu = _dot(x, wu_ref[lead + (slice(None), slice(c0, c0 + cw))])
        act = (_silu(hg) * hu).astype(BF16)
        part = _dot(act, wd_ref[lead + (slice(c0, c0 + cw), slice(None))])
        acc = part if acc is None else acc + part
        c0 += cw
    return acc


def _pre1_kernel(h_ref, g_ref, sh_ref, sc_ref, win_ref, qg_ref, kvg_ref, wuqt_ref, wuk_ref, wuvt_ref,
                 cos_ref, sin_ref, cost_ref, sint_ref, *out_refs,
                 use_rope, want_q, q_rank, kv_rank, n_heads):
    a = _norm_mod(h_ref[0], g_ref[...], sh_ref[0], sc_ref[0])
    z = _dot(a.astype(BF16), win_ref[...])
    cos = cos_ref[...]
    sin = sin_ref[...]
    if want_q:
        qt_ref, k_ref, vt_ref = out_refs
        cq = (_rms(z[:, :q_rank]) * qg_ref[...]).astype(BF16)
        qft = _dot_nt(wuqt_ref[...], cq)
        q_scale = (MLA_NOPE + MLA_ROPE) ** -0.5 * LOG2E
        rope_end = MLA_NOPE + MLA_ROPE
        for h in range(n_heads):
            c0 = h * MLA_QK_PAD
            qr = _rope_t(qft[c0 + MLA_NOPE:c0 + rope_end], cost_ref[...], sint_ref[...])
            qt_ref[0, 0, c0:c0 + MLA_NOPE, :] = (qft[c0:c0 + MLA_NOPE] * q_scale).astype(BF16)
            qt_ref[0, 0, c0 + MLA_NOPE:c0 + rope_end, :] = (qr * q_scale).astype(BF16)
            qt_ref[0, 0, c0 + rope_end:c0 + MLA_QK_PAD, :] = qft[c0 + rope_end:c0 + MLA_QK_PAD].astype(BF16)
    else:
        k_ref, vt_ref = out_refs
    ckv = (_rms(z[:, q_rank:q_rank + kv_rank]) * kvg_ref[...]).astype(BF16)
    kn = _dot(ckv, wuk_ref[...])
    vv = _dot_nt(wuvt_ref[...], ckv)
    kr = z[:, q_rank + kv_rank:]
    if use_rope:
        kr = _rope(kr, cos, sin)
    kr = kr.astype(BF16)
    for h in range(n_heads):
        c0 = h * MLA_QK_PAD
        k_ref[0, :, c0:c0 + MLA_NOPE] = kn[:, h * MLA_NOPE:(h + 1) * MLA_NOPE].astype(BF16)
        k_ref[0, :, c0 + MLA_NOPE:c0 + MLA_QK_PAD] = kr
    _store_vt(vt_ref, vv, n_heads)


def _pre1(h, g, sh, sc, w_in, qg, kvg, wuq_t, wuk, wuv_t, cos, sin, cos_t, sin_t, *,
          use_rope, want_q, tm):
    b, l, d = h.shape
    n_heads = wuk.shape[1] // MLA_NOPE
    q_rank, kv_rank = wuq_t.shape[1], wuk.shape[0]
    kern = functools.partial(_pre1_kernel, use_rope=use_rope, want_q=want_q,
                             q_rank=q_rank, kv_rank=kv_rank, n_heads=n_heads)
    full = lambda x: pl.BlockSpec(x.shape, lambda b_, i: (0,) * x.ndim)
    qk_w = n_heads * MLA_QK_PAD
    v_rows = n_heads * V_AUG
    out_specs = [pl.BlockSpec((1, tm, qk_w), lambda b_, i: (b_, i, 0)),
                 pl.BlockSpec((1, 1, v_rows, tm), lambda b_, i: (b_, i, 0, 0))]
    out_shape = [jax.ShapeDtypeStruct((b, l, qk_w), BF16),
                 jax.ShapeDtypeStruct((b, l // tm, v_rows, tm), BF16)]
    if want_q:
        out_specs = [pl.BlockSpec((1, 1, qk_w, tm), lambda b_, i: (b_, i, 0, 0))] + out_specs
        out_shape = [jax.ShapeDtypeStruct((b, l // tm, qk_w, tm), BF16)] + out_shape
    return pl.pallas_call(
        kern,
        grid=(b, l // tm),
        in_specs=[pl.BlockSpec((1, tm, d), lambda b_, i: (b_, i, 0)),
                  pl.BlockSpec((1, d), lambda b_, i: (0, 0)),
                  _bvec_spec(sh, d), _bvec_spec(sc, d),
                  full(w_in), full(qg), full(kvg), full(wuq_t), full(wuk), full(wuv_t),
                  pl.BlockSpec((tm, LANES), lambda b_, i: (i, 0)),
                  pl.BlockSpec((tm, LANES), lambda b_, i: (i, 0)),
                  pl.BlockSpec((MLA_ROPE, tm), lambda b_, i: (0, i)),
                  pl.BlockSpec((MLA_ROPE, tm), lambda b_, i: (0, i))],
        out_specs=out_specs,
        out_shape=out_shape,
        compiler_params=_params(("parallel", "parallel")),
        name="pre1_lat" if want_q else "pre1_ctx",
    )(h, g, sh, sc, w_in, qg, kvg, wuq_t, wuk, wuv_t, cos, sin, cos_t, sin_t)


def _post1_kernel(o_ref, wout_ref, h_ref, g1_ref, n2g_ref, sh2_ref, sc2_ref, rw_ref,
                  h1_ref, a2_ref, gates_ref, sel_ref):
    y = _dot(o_ref[0], wout_ref[...])
    h1 = h_ref[0] + g1_ref[0] * y
    h1_ref[0] = h1
    a2 = _norm_mod(h1, n2g_ref[...], sh2_ref[0], sc2_ref[0])
    a2_ref[0] = a2
    a_hi = a2.astype(BF16)
    a_lo = (a2 - a_hi.astype(F32)).astype(BF16)
    hi_both = _dot(a_hi, rw_ref[...])
    logits = hi_both[:, :LANES] + (_dot(a_lo, rw_ref[:, :LANES]) + hi_both[:, LANES:])
    lane = lax.broadcasted_iota(jnp.int32, logits.shape, 1)
    lg = jnp.where(lane < N_EXPERTS, logits, -jnp.inf)
    m1 = jnp.max(lg, axis=1, keepdims=True)
    i1 = jnp.min(jnp.where(lg == m1, lane, LANES), axis=1, keepdims=True)
    lg2 = jnp.where(lane == i1, -jnp.inf, lg)
    m2 = jnp.max(lg2, axis=1, keepdims=True)
    i2 = jnp.min(jnp.where(lg2 == m2, lane, LANES), axis=1, keepdims=True)
    p2 = jnp.exp(m2 - m1)
    w1 = 1.0 / (1.0 + p2)
    gates_ref[0] = jnp.where(lane == i1, w1, 0.0) + jnp.where(lane == i2, p2 * w1, 0.0)
    sel_ref[0] = jnp.where((lane == i1) | (lane == i2), 1.0, 0.0).astype(BF16)


def _post1(o, w_out, h, g1, n2g, sh2, sc2, rw_split, *, tm):
    b, l, d = h.shape
    return pl.pallas_call(
        _post1_kernel,
        grid=(b, l // tm),
        in_specs=[pl.BlockSpec((1, tm, o.shape[2]), lambda b_, i: (b_, i, 0)),
                  pl.BlockSpec(w_out.shape, lambda b_, i: (0, 0)),
                  pl.BlockSpec((1, tm, d), lambda b_, i: (b_, i, 0)),
                  _bvec_spec(g1, d),
                  pl.BlockSpec((1, d), lambda b_, i: (0, 0)),
                  _bvec_spec(sh2, d), _bvec_spec(sc2, d),
                  pl.BlockSpec(rw_split.shape, lambda b_, i: (0, 0))],
        out_specs=[pl.BlockSpec((1, tm, d), lambda b_, i: (b_, i, 0)),
                   pl.BlockSpec((1, tm, d), lambda b_, i: (b_, i, 0)),
                   pl.BlockSpec((1, tm, LANES), lambda b_, i: (b_, i, 0)),
                   pl.BlockSpec((1, tm, LANES), lambda b_, i: (b_, i, 0))],
        out_shape=[jax.ShapeDtypeStruct((b, l, d), F32),
                   jax.ShapeDtypeStruct((b, l, d), F32),
                   jax.ShapeDtypeStruct((b, l, LANES), F32),
                   jax.ShapeDtypeStruct((b, l, LANES), BF16)],
        compiler_params=_params(("parallel", "parallel")),
        name="post1_router",
    )(o, w_out, h, g1, n2g, sh2, sc2, rw_split)


def _route_kernel(sel_ref, gates_ref, r_ref, tile_ref, ends_ref, cnt_ref, off_ref, *, tile_rows):
    phase = pl.program_id(0)
    i = pl.program_id(1)
    sel = sel_ref[...]
    tr = sel.shape[0]
    col_count = jnp.sum(sel.astype(F32), axis=0, keepdims=True)
    lane = lax.broadcasted_iota(jnp.int32, (1, LANES), 1)

    @pl.when((phase == 0) & (i == 0))
    def _():
        cnt_ref[...] = jnp.zeros_like(cnt_ref)

    @pl.when(phase == 0)
    def _():
        cnt_ref[...] += col_count

    @pl.when((phase == 0) & (i == pl.num_programs(1) - 1))
    def _():
        padded = jnp.ceil(cnt_ref[...] / tile_rows) * tile_rows
        incl = padded
        for sh in (1, 2, 4):
            incl = incl + jnp.where(lane >= sh, pltpu.roll(incl, sh, 1), 0.0)
        off_ref[...] = incl - padded
        ends_ref[...] = incl.astype(jnp.int32)
        tile_start = (lax.broadcasted_iota(jnp.int32, tile_ref.shape, 1) * tile_rows).astype(F32)
        tile_expert = jnp.zeros(tile_ref.shape, jnp.int32)
        for e in range(N_EXPERTS):
            end_e = jnp.sum(jnp.where(lane == e, incl, 0.0), axis=1, keepdims=True)
            tile_expert = tile_expert + (tile_start >= end_e).astype(jnp.int32)
        tile_ref[...] = tile_expert
        cnt_ref[...] = jnp.zeros_like(cnt_ref)

    @pl.when(phase == 1)
    def _():
        rr = lax.broadcasted_iota(jnp.int32, (tr, tr), 0)
        cc = lax.broadcasted_iota(jnp.int32, (tr, tr), 1)
        earlier = jnp.where(cc < rr, 1.0, 0.0).astype(BF16)
        rank = _dot(earlier, sel) + cnt_ref[...]
        cnt_ref[...] += col_count
        pos = off_ref[...] + rank
        chosen = sel > 0
        lo = jnp.min(jnp.where(chosen, pos, 3e38), axis=1, keepdims=True)
        hi = jnp.max(jnp.where(chosen, pos, -1.0), axis=1, keepdims=True)
        g = gates_ref[...]
        w_lo = jnp.sum(jnp.where(chosen & (pos == lo), g, 0.0), axis=1, keepdims=True)
        w_hi = jnp.sum(jnp.where(chosen & (pos == hi), g, 0.0), axis=1, keepdims=True)
        lane_t = lax.broadcasted_iota(jnp.int32, (tr, LANES), 1)
        r_ref[...] = jnp.where(lane_t == 0, lo, jnp.where(lane_t == 1, hi,
                               jnp.where(lane_t == 2, w_lo, jnp.where(lane_t == 3, w_hi, 0.0))))


def _route(sel, gates, *, tr, tile_rows, n_tiles):
    n = sel.shape[0]
    n_tiles_pad = -(-n_tiles // LANES) * LANES
    return pl.pallas_call(
        functools.partial(_route_kernel, tile_rows=tile_rows),
        grid=(2, n // tr),
        in_specs=[pl.BlockSpec((tr, LANES), lambda ph, i: (i, 0)),
                  pl.BlockSpec((tr, LANES), lambda ph, i: (i, 0))],
        out_specs=[pl.BlockSpec((tr, LANES), lambda ph, i: (i * ph, 0)),
                   pl.BlockSpec((1, n_tiles_pad), lambda ph, i: (0, 0)),
                   pl.BlockSpec((1, LANES), lambda ph, i: (0, 0))],
        out_shape=[jax.ShapeDtypeStruct((n, LANES), F32),
                   jax.ShapeDtypeStruct((1, n_tiles_pad), jnp.int32),
                   jax.ShapeDtypeStruct((1, LANES), jnp.int32)],
        scratch_shapes=[pltpu.VMEM((1, LANES), F32), pltpu.VMEM((1, LANES), F32)],
        compiler_params=_params(("arbitrary", "arbitrary")),
        name="moe_route",
    )(sel, gates)


def _dispatch_kernel(ends_ref, dest_ref, x_ref, xs_hbm, zero_ref, sem, *, tm, tile_rows):
    n_tiles = xs_hbm.shape[0] // tile_rows

    def zero_tile(start):
        return pltpu.make_async_copy(zero_ref, xs_hbm.at[pl.ds(start, tile_rows)], sem)

    def group_tail(e):
        return zero_tile(pl.multiple_of(ends_ref[e] - tile_rows, tile_rows))

    @pl.when(pl.program_id(0) == 0)
    def _():
        zero_ref[...] = jnp.zeros_like(zero_ref)
        last_end = ends_ref[N_EXPERTS - 1]
        for e in range(N_EXPERTS):
            pl.when(ends_ref[e] >= tile_rows)(lambda e=e: group_tail(e).start())
        for j in range(n_tiles - N_EXPERTS, n_tiles):
            pl.when(j * tile_rows >= last_end)(lambda j=j: zero_tile(j * tile_rows).start())
        for e in range(N_EXPERTS):
            pl.when(ends_ref[e] >= tile_rows)(lambda e=e: group_tail(e).wait())
        for j in range(n_tiles - N_EXPERTS, n_tiles):
            pl.when(j * tile_rows >= last_end)(lambda j=j: zero_tile(j * tile_rows).wait())

    def row_copy(src_row, dst_row):
        return pltpu.make_async_copy(x_ref.at[pl.ds(src_row, 1)], xs_hbm.at[pl.ds(dst_row, 1)], sem)

    def issue(r, carry):
        row_copy(r, dest_ref[0, 0, r]).start()
        row_copy(r, dest_ref[0, 0, tm + r]).start()
        return carry

    lax.fori_loop(0, tm, issue, 0, unroll=DMA_ISSUE_UNROLL)
    pltpu.make_async_copy(x_ref, xs_hbm.at[pl.ds(0, tm)], sem).wait()
    pltpu.make_async_copy(x_ref, xs_hbm.at[pl.ds(0, tm)], sem).wait()


def _dispatch(group_ends, dest, x, n_rows, *, tm, tile_rows):
    n, d = x.shape
    return pl.pallas_call(
        functools.partial(_dispatch_kernel, tm=tm, tile_rows=tile_rows),
        grid_spec=pltpu.PrefetchScalarGridSpec(
            num_scalar_prefetch=1,
            grid=(n // tm,),
            in_specs=[pl.BlockSpec((1, 1, 2 * tm), lambda i, ends: (i, 0, 0), memory_space=pltpu.SMEM),
                      pl.BlockSpec((tm, d), lambda i, ends: (i, 0))],
            out_specs=pl.BlockSpec(memory_space=pl.ANY),
            scratch_shapes=[pltpu.VMEM((tile_rows, d), x.dtype), pltpu.SemaphoreType.DMA(())]),
        out_shape=jax.ShapeDtypeStruct((n_rows, d), x.dtype),
        compiler_params=pltpu.CompilerParams(dimension_semantics=("arbitrary",),
                                             has_side_effects=True),
        name="moe_dispatch",
    )(group_ends, dest, x)


def _experts_kernel(te_ref, x_ref, wg_ref, wu_ref, wd_ref, o_ref):
    valid = te_ref[pl.program_id(0)] < N_EXPERTS

    @pl.when(valid)
    def _():
        o_ref[...] = _swiglu_tile(x_ref[...].astype(BF16), wg_ref, wu_ref, wd_ref, lead=(0,))

    @pl.when(jnp.logical_not(valid))
    def _():
        o_ref[...] = jnp.zeros_like(o_ref)


def _experts(tile_expert, xs, wg, wu, wd, *, tile_rows):
    n_rows, d = xs.shape
    ne, _, f = wg.shape
    w_idx = lambda j, te: (jnp.minimum(te[j], ne - 1), 0, 0)
    return pl.pallas_call(
        _experts_kernel,
        grid_spec=pltpu.PrefetchScalarGridSpec(
            num_scalar_prefetch=1,
            grid=(n_rows // tile_rows,),
            in_specs=[pl.BlockSpec((tile_rows, d), lambda j, te: (j, 0)),
                      pl.BlockSpec((1, d, f), w_idx),
                      pl.BlockSpec((1, d, f), w_idx),
                      pl.BlockSpec((1, f, d), w_idx)],
            out_specs=pl.BlockSpec((tile_rows, d), lambda j, te: (j, 0))),
        out_shape=jax.ShapeDtypeStruct((n_rows, d), F32),
        compiler_params=_params(("arbitrary",)),
        name="moe_experts",
    )(tile_expert, xs, wg, wu, wd)


def _combine_kernel(dest_ref, dest_next_ref, routed_ref, ys_hbm, h1_ref, g2_ref, fng_ref, o_ref,
                    ylo_ref, yhi_ref, sems, *, tm):
    i = pl.program_id(0)
    slot = i % 2

    def gather(idx_ref, s):
        def issue(r, carry):
            pltpu.make_async_copy(ys_hbm.at[pl.ds(idx_ref[0, 0, r], 1)],
                                  ylo_ref.at[s, pl.ds(r, 1)], sems.at[s]).start()
            pltpu.make_async_copy(ys_hbm.at[pl.ds(idx_ref[0, 0, tm + r], 1)],
                                  yhi_ref.at[s, pl.ds(r, 1)], sems.at[s]).start()
            return carry
        lax.fori_loop(0, tm, issue, 0, unroll=DMA_ISSUE_UNROLL)

    @pl.when(i == 0)
    def _():
        gather(dest_ref, 0)

    @pl.when(i + 1 < pl.num_programs(0))
    def _():
        gather(dest_next_ref, 1 - slot)

    pltpu.make_async_copy(ys_hbm.at[pl.ds(0, tm)], ylo_ref.at[slot], sems.at[slot]).wait()
    pltpu.make_async_copy(ys_hbm.at[pl.ds(0, tm)], yhi_ref.at[slot], sems.at[slot]).wait()
    routed = routed_ref[...]
    y = routed[:, 2:3] * ylo_ref[slot] + routed[:, 3:4] * yhi_ref[slot]
    out = h1_ref[...] + g2_ref[0] * y
    o_ref[...] = _rms(out) * fng_ref[...]


def _combine(dest, routed, ys, h1, g2, fng, *, tm, tiles_per_sample):
    n, d = h1.shape
    last = n // tm - 1
    return pl.pallas_call(
        functools.partial(_combine_kernel, tm=tm),
        grid=(n // tm,),
        in_specs=[pl.BlockSpec((1, 1, 2 * tm), lambda i: (i, 0, 0), memory_space=pltpu.SMEM),
                  pl.BlockSpec((1, 1, 2 * tm), lambda i: (jnp.minimum(i + 1, last), 0, 0),
                               memory_space=pltpu.SMEM),
                  pl.BlockSpec((tm, LANES), lambda i: (i, 0)),
                  pl.BlockSpec(memory_space=pl.ANY),
                  pl.BlockSpec((tm, d), lambda i: (i, 0)),
                  pl.BlockSpec((1, 1, d), lambda i: (i // tiles_per_sample, 0, 0)),
                  pl.BlockSpec((1, d), lambda i: (0, 0))],
        out_specs=pl.BlockSpec((tm, d), lambda i: (i, 0)),
        out_shape=jax.ShapeDtypeStruct((n, d), F32),
        scratch_shapes=[pltpu.VMEM((2, tm, d), F32), pltpu.VMEM((2, tm, d), F32),
                        pltpu.SemaphoreType.DMA((2,))],
        compiler_params=_params(("arbitrary",)),
        name="moe_combine",
    )(dest, dest, routed, ys, h1, g2, fng)


def _moe(a2, gates, sel, wg, wu, wd, h1, g2, fng, *, tm):
    b, l, d = h1.shape
    n = b * l
    n_rows = 2 * n + N_EXPERTS * MOE_TILE
    n_tiles = n_rows // MOE_TILE
    routed, tile_expert, group_ends = _route(sel.reshape(n, LANES), gates.reshape(n, LANES),
                                             tr=tm, tile_rows=MOE_TILE, n_tiles=n_tiles)
    tc = min(MOE_COPY_TILE, l)
    dest = routed[:, 0:2].astype(jnp.int32).reshape(n // tc, tc, 2)
    dest = dest.transpose(0, 2, 1).reshape(n // tc, 1, 2 * tc)
    xs = _dispatch(group_ends[0, :N_EXPERTS], dest, a2.reshape(n, d), n_rows, tm=tc, tile_rows=MOE_TILE)
    ys = _experts(tile_expert[0, :n_tiles], xs, wg, wu, wd, tile_rows=MOE_TILE)
    out = _combine(dest, routed, ys, h1.reshape(n, d), g2, fng, tm=tc, tiles_per_sample=l // tc)
    return out.reshape(b, l, d)


def _rope_tables(seq_len, dim):
    t = jnp.arange(seq_len, dtype=jnp.int32)
    row = (t // GRID_W).astype(F32)
    col = (t % GRID_W).astype(F32)
    quarter = dim // 4
    inv = ROPE_BASE ** (-jnp.arange(quarter, dtype=F32) / quarter)
    ar = row[:, None] * inv[None, :]
    ac = col[:, None] * inv[None, :]
    cos = jnp.concatenate([jnp.cos(ar), jnp.cos(ar), jnp.cos(ac), jnp.cos(ac)], axis=-1)
    sin = jnp.concatenate([jnp.sin(ar), jnp.sin(ar), jnp.sin(ac), jnp.sin(ac)], axis=-1)
    reps = LANES // dim
    return jnp.tile(cos, (1, reps)), jnp.tile(sin, (1, reps))


def kernel(x, c, ctx, c_ctx, norm1_g, norm2_g, mod_w, mod_b, even_w_in, pool_w, pool_scale,
           lambda_q1, lambda_k1, lambda_q2, lambda_k2, even_w_out, ffn_w_gate, ffn_w_up,
           ffn_w_down, odd_w_in, q_norm_g, kv_norm_g, w_uq, w_ukv, odd_w_out, router_w,
           moe_w_gate, moe_w_up, moe_w_down, final_norm_g):
    b, l, d = x.shape
    n_ctx = ctx.shape[1]
    tm = min(ROW_TILE, l)
    tkv = min(KV_TILE, l)
    tq = min(DIFF_Q_TILE, l)
    cos, sin = _rope_tables(l, DIFF_HEAD_DIM)
    cos_t, sin_t = cos[:, :DIFF_HEAD_DIM].T, sin[:, :DIFF_HEAD_DIM].T
    cos_c, sin_c, cos_tc, sin_tc = cos[:n_ctx], sin[:n_ctx], cos_t[:, :n_ctx], sin_t[:, :n_ctx]

    cond = jnp.zeros((8, d), F32).at[:b].set(c).at[b].set(c_ctx)
    mod = _modulation(cond, mod_w, mod_b)
    mod = mod.reshape(mod.shape[0], 8, N_MOD, d)

    def mod_vecs(layer):
        lat = [mod[layer, :b, k][:, None, :] for k in range(N_MOD)]
        cx = [mod[layer, b:b + 1, k][:, None, :] for k in range(N_MOD)]
        return lat, cx

    (sh1, sc1, g1, sh2, sc2, g2), (csh1, csc1, cg1, csh2, csc2, cg2) = mod_vecs(0)
    lam_init = 0.8 - 0.6 * math.exp(-0.3 * 0)
    w_in0 = even_w_in[0]
    qk_w0 = (w_in0.shape[1] - POOL_WIDTH) // 3
    k0, v0 = POOL_WIDTH + qk_w0, POOL_WIDTH + 2 * qk_w0
    w_nat0 = jnp.concatenate([w_in0[:, :POOL_WIDTH], w_in0[:, k0:v0]], axis=1).astype(BF16)
    w_t0 = jnp.concatenate([w_in0[:, POOL_WIDTH:k0], w_in0[:, v0:]], axis=1).T.astype(BF16)
    n1g = norm1_g[0][None, :]
    n2g = norm2_g[0][None, :]
    p_lat, q1t_lat, q2t_lat, k_lat, vt_lat = _pre0(x, n1g, sh1, sc1, w_nat0, w_t0, cos, sin,
                                                   cos_t, sin_t, use_rope=True, tm=tkv)
    p_ctx, q1t_ctx, q2t_ctx, k_ctx, vt_ctx = _pre0(ctx, n1g, csh1, csc1, w_nat0, w_t0, cos_c, sin_c,
                                                   cos_tc, sin_tc, use_rope=False, tm=n_ctx)
    lq = jnp.stack([lambda_q1[0], lambda_q2[0]])
    lk = jnp.stack([lambda_k1[0], lambda_k2[0]])
    diff_kern = functools.partial(_diff_attn_kernel, lam_init=lam_init)
    attn_lat = _attention(diff_kern, (q1t_lat, q2t_lat), (k_ctx, vt_ctx), (k_lat, vt_lat),
                          2 * DIFF_HEAD_DIM, extra=(lq, lk), tq=tq,
                          n_sub=min(DIFF_Q_SWEEPS, l // tq), name="diff_attn_lat")
    attn_ctx = _attention(diff_kern, (q1t_ctx, q2t_ctx), (k_ctx, vt_ctx), None,
                          2 * DIFF_HEAD_DIM, extra=(lq, lk), tq=n_ctx, n_sub=1, name="diff_attn_ctx")
    ng = len(POOL_WINDOWS)
    poolw_bd = jnp.zeros((POOL_WIDTH, POOL_WIDTH), F32)
    for g in range(ng):
        sl = slice(g * POOL_GROUP, (g + 1) * POOL_GROUP)
        poolw_bd = poolw_bd.at[sl, sl].set(pool_w[0, g])
    poolw_bd = poolw_bd.astype(BF16)
    pscale = pool_scale[0][None, :]
    w_out0 = even_w_out[0].astype(BF16)
    wg0, wu0, wd0 = ffn_w_gate[0].astype(BF16), ffn_w_up[0].astype(BF16), ffn_w_down[0].astype(BF16)
    h_lat = _post0(attn_lat, p_lat, poolw_bd, pscale, w_out0, x, g1, n2g, sh2, sc2, g2,
                   wg0, wu0, wd0, tm=tm)
    h_ctx = _post0(attn_ctx, p_ctx, poolw_bd, pscale, w_out0, ctx, cg1, n2g, csh2, csc2, cg2,
                   wg0, wu0, wd0, tm=n_ctx)

    (sh1, sc1, g1, sh2, sc2, g2), (csh1, csc1, _, _, _, _) = mod_vecs(1)
    n1g = norm1_g[1][None, :]
    n2g = norm2_g[1][None, :]
    q_rank, kv_rank = w_uq.shape[1], w_ukv.shape[1]
    n_heads = w_uq.shape[2] // (MLA_NOPE + MLA_ROPE)
    w_in1 = jnp.pad(odd_w_in[0], ((0, 0), (0, LANES - MLA_ROPE))).astype(BF16)
    wuq = w_uq[0].reshape(q_rank, n_heads, MLA_NOPE + MLA_ROPE)
    wuq = jnp.pad(wuq, ((0, 0), (0, 0), (0, MLA_QK_PAD - MLA_NOPE - MLA_ROPE)))
    wuq_t = wuq.reshape(q_rank, n_heads * MLA_QK_PAD).T.astype(BF16)
    wukv = w_ukv[0].reshape(kv_rank, n_heads, MLA_NOPE + MLA_V)
    wuk = wukv[:, :, :MLA_NOPE].reshape(kv_rank, n_heads * MLA_NOPE).astype(BF16)
    wuv_t = wukv[:, :, MLA_NOPE:].reshape(kv_rank, n_heads * MLA_V).T.astype(BF16)
    qg = q_norm_g[0][None, :]
    kvg = kv_norm_g[0][None, :]
    q1t, k1, vt1 = _pre1(h_lat, n1g, sh1, sc1, w_in1, qg, kvg, wuq_t, wuk, wuv_t, cos, sin,
                         cos_t, sin_t, use_rope=True, want_q=True, tm=tkv)
    k1c, vt1c = _pre1(h_ctx, n1g, csh1, csc1, w_in1, qg, kvg, wuq_t, wuk, wuv_t, cos_c, sin_c,
                      cos_tc, sin_tc, use_rope=False, want_q=False, tm=n_ctx)
    tq1 = min(MLA_Q_TILE, l)
    o1 = _attention(_mla_attn_kernel, (q1t,), (k1c, vt1c), (k1, vt1), MLA_QK_PAD,
                    tq=tq1, n_sub=min(MLA_Q_SWEEPS, l // tq1), name="mla_attn")
    rw_pad = jnp.pad(router_w[0], ((0, 0), (0, LANES - N_EXPERTS)))
    rw_hi = rw_pad.astype(BF16)
    rw_lo = (rw_pad - rw_hi.astype(F32)).astype(BF16)
    rw_split = jnp.concatenate([rw_hi, rw_lo], axis=1)
    h1, a2, gates, sel = _post1(o1, odd_w_out[0].astype(BF16), h_lat, g1, n2g, sh2, sc2,
                                rw_split, tm=tm)
    return _moe(a2, gates, sel, moe_w_gate[0].astype(BF16), moe_w_up[0].astype(BF16),
                moe_w_down[0].astype(BF16), h1, g2, final_norm_g[None, :], tm=tm)
```
